```python
import math
import jax, jax.numpy as jnp
from jax import lax
import numpy as np

D_MODEL = 1024
BATCH = 16
SEQ = 256
DEPTH = 2
DEC_BATCH = 4
DEC_SEQ = 1024
PAST_LEN = 256

GRID_W = 64
EPS = 1e-6
MLA_HEADS = 8
MLA_NOPE = 64
MLA_ROPE = 32
MLA_QK = MLA_NOPE + MLA_ROPE
MLA_V = 64
MLA_Q_RANK = 384
MLA_KV_RANK = 256
MLA_WIDTH = MLA_HEADS * MLA_V
ROPE_THETA = 10000.0
ATTN_BLOCK = 128
GDN_HEADS = 4
GDN_DK = 128
GDN_DV = 128
GDN_KW = GDN_HEADS * GDN_DK
GDN_VW = GDN_HEADS * GDN_DV
GDN_CHUNK = 64
CONV_W = 5
CM_GROUPS = 4
CM_CHUNK = 128
CM_WIDTH = 512
CM_GW = CM_WIDTH // CM_GROUPS
N_BRANCH = 3
BRANCH_W = 512
SPLIT_SIZES = (MLA_Q_RANK, MLA_KV_RANK, MLA_ROPE, MLA_WIDTH,
               2 * GDN_KW + GDN_VW, 2 * GDN_HEADS, 2 * GDN_HEADS, GDN_VW,
               CM_WIDTH, CM_WIDTH, CM_WIDTH, N_BRANCH * D_MODEL)
D_IN = sum(SPLIT_SIZES)

kernel_name = "hybrid_mla_gdn_chunkmlp_prefix_diffusion_step"


def _rms(x, g):
    xf = x.astype(jnp.float32)
    y = xf * lax.rsqrt(jnp.mean(xf * xf, axis=-1, keepdims=True) + EPS)
    return (y * g.astype(jnp.float32)).astype(x.dtype)


def _l2n(x):
    xf = x.astype(jnp.float32)
    return xf * lax.rsqrt(jnp.sum(xf * xf, axis=-1, keepdims=True) + EPS)


def _axial_rope(x, rows):
    row = jnp.repeat(jnp.arange(rows, dtype=jnp.float32), GRID_W)
    col = jnp.tile(jnp.arange(GRID_W, dtype=jnp.float32), rows)
    half = MLA_ROPE // 2
    nf = half // 2
    inv = ROPE_THETA ** (-jnp.arange(nf, dtype=jnp.float32) / nf)
    xf = x.astype(jnp.float32)

    def rot(xa, pos):
        ang = pos[:, None] * inv[None, :]
        cos = jnp.cos(ang)[:, None, :]
        sin = jnp.sin(ang)[:, None, :]
        x1, x2 = xa[..., :nf], xa[..., nf:]
        return jnp.concatenate([x1 * cos - x2 * sin, x1 * sin + x2 * cos], axis=-1)

    return jnp.concatenate([rot(xf[..., :half], row), rot(xf[..., half:], col)], axis=-1).astype(x.dtype)


def _attend(q, k, v):
    B, Tq, H, d = q.shape
    nb = Tq // ATTN_BLOCK
    qb = jnp.moveaxis(q.reshape(B, nb, ATTN_BLOCK, H, d), 1, 0)
    scale = 1.0 / math.sqrt(d)

    def one(qi):
        s = jnp.einsum('bqhd,bkhd->bhqk', qi, k).astype(jnp.float32) * scale
        pr = jax.nn.softmax(s, axis=-1)
        return jnp.einsum('bhqk,bkhd->bqhd', pr.astype(v.dtype), v)

    o = lax.map(one, qb)
    return jnp.moveaxis(o, 0, 1).reshape(B, Tq, H, v.shape[-1])


def _mla_kv(ckv_n, krope, p):
    B, T, _ = ckv_n.shape
    kv = (ckv_n @ p['w_ukv']).reshape(B, T, MLA_HEADS, MLA_NOPE + MLA_V)
    kr = jnp.broadcast_to(krope[:, :, None, :], (B, T, MLA_HEADS, MLA_ROPE))
    k = _rms(jnp.concatenate([kv[..., :MLA_NOPE], kr], axis=-1), p['k_norm'])
    return k, kv[..., MLA_NOPE:]


def _dwconv(x, w):
    return lax.conv_general_dilated(x, w[:, None, :], window_strides=(1,),
                                    padding=[(CONV_W // 2, CONV_W // 2)],
                                    dimension_numbers=('NWC', 'WIO', 'NWC'),
                                    feature_group_count=x.shape[-1])


def _gdn_chunk(q, k, v, g, beta, s0):
    B, T, H, dk = q.shape
    C = GDN_CHUNK
    N = T // C

    def blk(a):
        a = a.reshape((B, N, C, H) + a.shape[3:])
        return jnp.moveaxis(jnp.moveaxis(a, 1, 0), 3, 2)

    qc, kc, vc = blk(q * (dk ** -0.5)), blk(k), blk(v)
    gc = jnp.cumsum(blk(g), axis=-1)
    bc = blk(beta)
    kb = kc * bc[..., None]
    vb = vc * bc[..., None]
    incl = jnp.tril(jnp.ones((C, C), bool))
    strict = jnp.tril(jnp.ones((C, C), bool), -1)
    diff = gc[..., :, None] - gc[..., None, :]
    decay = jnp.where(incl, jnp.exp(jnp.where(incl, diff, 0.0)), 0.0)
    A = jnp.where(strict, jnp.einsum('nbhid,nbhjd->nbhij', kb, kc) * decay, 0.0)
    eye = jnp.eye(C, dtype=A.dtype)
    Tinv = lax.linalg.triangular_solve(A + eye, jnp.broadcast_to(eye, A.shape),
                                       left_side=True, lower=True, unit_diagonal=True)
    u = Tinv @ vb
    w = Tinv @ (kb * jnp.exp(gc)[..., None])

    def step(S, xs):
        qi, ki, ui, wi, gi, di = xs
        v_new = ui - wi @ S
        att = jnp.einsum('bhid,bhjd->bhij', qi, ki) * di
        o = (qi * jnp.exp(gi)[..., None]) @ S + att @ v_new
        gl = gi[..., -1:]
        S = S * jnp.exp(gl)[..., None] + jnp.einsum('bhcd,bhce->bhde', ki * jnp.exp(gl - gi)[..., None], v_new)
        return S, o

    S, o = lax.scan(step, s0, (qc, kc, u, w, gc, decay))
    o = jnp.moveaxis(jnp.moveaxis(o, 2, 3), 0, 1).reshape(B, T, H, v.shape[-1])
    return o, S


def _chunk_mlp(u, v, p):
    B, T, _ = u.shape
    u = jax.nn.gelu(u)
    vf = jax.nn.gelu(v).astype(jnp.float32)
    mu = jnp.mean(vf, axis=-1, keepdims=True)
    var = jnp.mean(jnp.square(vf - mu), axis=-1, keepdims=True)
    vn = ((vf - mu) * lax.rsqrt(var + EPS) * p['cm_ln_g'] + p['cm_ln_b']).astype(u.dtype)
    vr = vn.reshape(B, T // CM_CHUNK, CM_CHUNK, CM_GROUPS, CM_GW)
    sv = jnp.einsum('gpq,bnqgc->bnpgc', p['w_s'], vr) + jnp.swapaxes(p['b_s'], 0, 1)[:, :, None]
    return u * sv.reshape(B, T, CM_WIDTH)


def _layer(x, mod, p, cache=None):
    B, T, _ = x.shape
    f32 = jnp.float32
    shift, scale, gate = jnp.split(mod, 3, axis=-1)
    h = _rms(x, p['norm_g']) * (1 + scale) + shift
    proj = h @ p['w_in']
    split_pts = np.cumsum(SPLIT_SIZES)[:-1].tolist()
    (cq, ckv, krope, z_a, qkv_b, ga, gb, z_b, cu, cv, z_c, gl) = jnp.split(proj, split_pts, axis=-1)

    q = (_rms(cq, p['q_a_norm']) @ p['w_uq']).reshape(B, T, MLA_HEADS, MLA_QK)
    q = _rms(q, p['q_norm'])
    ckv_n = _rms(ckv, p['kv_a_norm'])
    k, v = _mla_kv(ckv_n, krope, p)
    if cache is None:
        o_a = _attend(q, k, v)
        s0 = jnp.zeros((B, 2, GDN_HEADS, GDN_DK, GDN_DV), f32)
    else:
        c_ckv, c_krope, s0 = cache
        rows = T // GRID_W
        q = jnp.concatenate([q[..., :MLA_NOPE], _axial_rope(q[..., MLA_NOPE:], rows)], axis=-1)
        k = jnp.concatenate([k[..., :MLA_NOPE], _axial_rope(k[..., MLA_NOPE:], rows)], axis=-1)
        kc, vc = _mla_kv(c_ckv, c_krope, p)
        o_a = _attend(q, jnp.concatenate([kc, k], axis=1), jnp.concatenate([vc, v], axis=1))
    o_a = o_a.reshape(B, T, MLA_WIDTH)

    qkv = jax.nn.silu(_dwconv(qkv_b, p['conv_w']))
    gq, gk, gv = jnp.split(qkv, [GDN_KW, 2 * GDN_KW], axis=-1)
    gq = _l2n(gq.reshape(B, T, GDN_HEADS, GDN_DK))
    gk = _l2n(gk.reshape(B, T, GDN_HEADS, GDN_DK))
    gv = gv.reshape(B, T, GDN_HEADS, GDN_DV).astype(f32)
    beta = jax.nn.sigmoid(gb.astype(f32)).reshape(B, T, 2, GDN_HEADS)
    g = -jnp.exp(p['a_log'].astype(f32)) * jax.nn.softplus(
        ga.astype(f32).reshape(B, T, 2, GDN_HEADS) + p['dt_bias'].astype(f32))
    s0 = s0.astype(f32)
    o_f, s_f = _gdn_chunk(gq, gk, gv, g[:, :, 0], beta[:, :, 0], s0[:, 0])
    fl = lambda t: jnp.flip(t, axis=1)
    o_bw, s_b = _gdn_chunk(fl(gq), fl(gk), fl(gv), fl(g[:, :, 1]), fl(beta[:, :, 1]), s0[:, 1])
    o_b = _rms(o_f + fl(o_bw), p['gdn_onorm']).astype(x.dtype).reshape(B, T, GDN_VW)

    o_c = _chunk_mlp(cu, cv, p)

    br = jnp.stack([o_a * jax.nn.silu(z_a), o_b * jax.nn.silu(z_b), o_c * jax.nn.silu(z_c)], axis=2)
    yb = jnp.einsum('btnw,nwd->btnd', br, p['w_branch'])
    gates = jax.nn.sigmoid(gl.reshape(B, T, N_BRANCH, D_MODEL))
    y = jnp.sum(gates * yb, axis=2) @ p['w_o']
    x = x + gate * y
    if cache is None:
        return x, (ckv_n, krope, jnp.stack([s_f, s_b], axis=1))
    return x, None


def setup_inputs(seed: int = 0) -> dict:
    key = jax.random.key(seed)
    ks = jax.random.split(key, 32)
    f32 = jnp.float32
    L = DEPTH

    def nrm(k, shape, s):
        return s * jax.random.normal(k, shape, f32)

    dt = jnp.exp(jax.random.uniform(ks[20], (L, 2, GDN_HEADS), f32, math.log(1e-3), math.log(1e-1)))
    return {
        'x_prompt': nrm(ks[0], (BATCH, SEQ, D_MODEL), 1.0),
        'x_sample': nrm(ks[1], (DEC_BATCH, DEC_SEQ, D_MODEL), 1.0),
        'cache_ckv': nrm(ks[2], (DEC_BATCH, DEPTH, PAST_LEN, MLA_KV_RANK), 1.0),
        'cache_krope': nrm(ks[3], (DEC_BATCH, DEPTH, PAST_LEN, MLA_ROPE), 1.0),
        'state_gdn': nrm(ks[4], (DEC_BATCH, DEPTH, 2, GDN_HEADS, GDN_DK, GDN_DV), 0.3),
        'c': nrm(ks[5], (DEC_BATCH, D_MODEL), 1.0),
        'c_ctx': nrm(ks[6], (D_MODEL,), 1.0),
        'norm_g': 1.0 + nrm(ks[7], (L, D_MODEL), 0.02),
        'w_mod': nrm(ks[8], (L, D_MODEL, 3 * D_MODEL), 0.5 * D_MODEL ** -0.5),
        'b_mod': nrm(ks[9], (L, 3 * D_MODEL), 0.02),
        'w_in': nrm(ks[10], (L, D_MODEL, D_IN), D_MODEL ** -0.5),
        'q_a_norm': 1.0 + nrm(ks[11], (L, MLA_Q_RANK), 0.02),
        'w_uq': nrm(ks[12], (L, MLA_Q_RANK, MLA_HEADS * MLA_QK), MLA_Q_RANK ** -0.5),
        'kv_a_norm': 1.0 + nrm(ks[13], (L, MLA_KV_RANK), 0.02),
        'w_ukv': nrm(ks[14], (L, MLA_KV_RANK, MLA_HEADS * (MLA_NOPE + MLA_V)), MLA_KV_RANK ** -0.5),
        'q_norm': 1.0 + nrm(ks[15], (L, MLA_QK), 0.02),
        'k_norm': 1.0 + nrm(ks[16], (L, MLA_QK), 0.02),
        'conv_w': nrm(ks[17], (L, CONV_W, 2 * GDN_KW + GDN_VW), CONV_W ** -0.5),
        'a_log': jnp.log(jax.random.uniform(ks[18], (L, 2, GDN_HEADS), f32, 1.0, 16.0)),
        'dt_bias': dt + jnp.log(-jnp.expm1(-dt)),
        'gdn_onorm': 1.0 + nrm(ks[19], (L, GDN_DV), 0.02),
        'cm_ln_g': 1.0 + nrm(ks[21], (L, CM_WIDTH), 0.02),
        'cm_ln_b': nrm(ks[22], (L, CM_WIDTH), 0.02),
        'w_s': nrm(ks[23], (L, CM_GROUPS, CM_CHUNK, CM_CHUNK), CM_CHUNK ** -0.5),
        'b_s': 1.0 + nrm(ks[24], (L, CM_GROUPS, CM_CHUNK), 0.02),
        'w_branch': nrm(ks[25], (L, N_BRANCH, BRANCH_W, D_MODEL), BRANCH_W ** -0.5),
        'w_o': nrm(ks[26], (L, D_MODEL, D_MODEL), D_MODEL ** -0.5),
    }


def reference(x_prompt, x_sample, cache_ckv, cache_krope, state_gdn, c, c_ctx, norm_g, w_mod, b_mod,
              w_in, q_a_norm, w_uq, kv_a_norm, w_ukv, q_norm, k_norm, conv_w, a_log, dt_bias,
              gdn_onorm, cm_ln_g, cm_ln_b, w_s, b_s, w_branch, w_o):
    yp = x_prompt
    ys = x_sample
    ckvs, kropes, states = [], [], []
    for l in range(DEPTH):
        p = {'norm_g': norm_g[l], 'w_in': w_in[l], 'q_a_norm': q_a_norm[l], 'w_uq': w_uq[l],
             'kv_a_norm': kv_a_norm[l], 'w_ukv': w_ukv[l], 'q_norm': q_norm[l], 'k_norm': k_norm[l],
             'conv_w': conv_w[l], 'a_log': a_log[l], 'dt_bias': dt_bias[l], 'gdn_onorm': gdn_onorm[l],
             'cm_ln_g': cm_ln_g[l], 'cm_ln_b': cm_ln_b[l], 'w_s': w_s[l], 'b_s': b_s[l],
             'w_branch': w_branch[l], 'w_o': w_o[l]}
        mod_ctx = (jax.nn.silu(c_ctx) @ w_mod[l] + b_mod[l])[None, None, :]
        mod_lat = (jax.nn.silu(c) @ w_mod[l] + b_mod[l])[:, None, :]
        yp, (ckv_l, kr_l, s_l) = _layer(yp, mod_ctx, p)
        ys, _ = _layer(ys, mod_lat, p, (cache_ckv[:, l], cache_krope[:, l], state_gdn[:, l]))
        ckvs.append(ckv_l)
        kropes.append(kr_l)
        states.append(s_l)
    new_ckv = jnp.stack(ckvs, axis=1)
    new_krope = jnp.stack(kropes, axis=1)
    new_state = jnp.stack(states, axis=1)
    return (yp, ys, new_ckv, new_krope, new_state)
```

```python
import functools
import math

import jax
import jax.numpy as jnp
from jax import lax
from jax.experimental import pallas as pl
from jax.experimental.pallas import tpu as pltpu

D_MODEL = 1024
DEPTH = 2
GRID_W = 64
EPS = 1e-6
MLA_HEADS = 8
MLA_NOPE = 64
MLA_ROPE = 32
MLA_QK = MLA_NOPE + MLA_ROPE
MLA_V = 64
MLA_Q_RANK = 384
MLA_KV_RANK = 256
MLA_WIDTH = MLA_HEADS * MLA_V
ROPE_THETA = 10000.0
GDN_HEADS = 4
GDN_DK = 128
GDN_DV = 128
GDN_KW = GDN_HEADS * GDN_DK
GDN_VW = GDN_HEADS * GDN_DV
GDN_CHUNK = 64
CONV_W = 5
CM_GROUPS = 4
CM_CHUNK = 128
CM_WIDTH = 512
N_BRANCH = 3
BRANCH_W = 512
SPLIT_SIZES = (MLA_Q_RANK, MLA_KV_RANK, MLA_ROPE, MLA_WIDTH,
               2 * GDN_KW + GDN_VW, 2 * GDN_HEADS, 2 * GDN_HEADS, GDN_VW,
               CM_WIDTH, CM_WIDTH, CM_WIDTH, N_BRANCH * D_MODEL)

LANES = 128
HEAD_SLOT = LANES
ROPE_LANE0 = MLA_NOPE

OFF_GL = 0
OFF_QKV = OFF_GL + N_BRANCH * D_MODEL
OFF_CQS = OFF_QKV + 2 * GDN_KW + GDN_VW
OFF_SMALL = OFF_CQS + MLA_Q_RANK
OFF_ZA = OFF_CQS + 512
OFF_ZB = OFF_ZA + 512
OFF_CU = OFF_ZB + 512
OFF_CV = OFF_CU + 512
OFF_ZC = OFF_CV + 512
OFF_CKV = OFF_ZC + 512
PROJ_W = OFF_CKV + MLA_KV_RANK

VMEM_LIMIT = 56 * 1024 * 1024

f32 = jnp.float32
bf16 = jnp.bfloat16


def _cparams(n_axes):
    return pltpu.CompilerParams(dimension_semantics=("arbitrary",) * n_axes,
                                vmem_limit_bytes=VMEM_LIMIT)


def _dot(a, b):
    return jnp.dot(a, b, preferred_element_type=f32)


def _dot_nt(a, b):
    return lax.dot_general(a, b, (((1,), (1,)), ((), ())), preferred_element_type=f32)


def _dot_tn(a, b):
    return lax.dot_general(a, b, (((0,), (0,)), ((), ())), preferred_element_type=f32)


def _rms_rows(x, g, n=None):
    n = x.shape[-1] if n is None else n
    ms = jnp.sum(x * x, axis=-1, keepdims=True) * (1.0 / n)
    return x * lax.rsqrt(ms + EPS) * g


def _mod_kernel(c_ref, w_ref, b_ref, o_ref):
    a = c_ref[...]
    a = (a * jax.nn.sigmoid(a)).astype(bf16)
    o_ref[0] = _dot(a, w_ref[0].astype(bf16)) + b_ref[0]


def _modulation(c8, w_mod, b_mod):
    tn = 512
    return pl.pallas_call(
        _mod_kernel,
        grid=(DEPTH, 3 * D_MODEL // tn),
        in_specs=[pl.BlockSpec((8, D_MODEL), lambda l, n: (0, 0)),
                  pl.BlockSpec((1, D_MODEL, tn), lambda l, n: (l, 0, n)),
                  pl.BlockSpec((1, 1, tn), lambda l, n: (l, 0, n))],
        out_specs=pl.BlockSpec((1, 8, tn), lambda l, n: (l, 0, n)),
        out_shape=jax.ShapeDtypeStruct((DEPTH, 8, 3 * D_MODEL), f32),
        compiler_params=_cparams(2),
        name="modulation",
    )(c8, w_mod, b_mod.reshape(DEPTH, 1, 3 * D_MODEL))


def _inproj_kernel(x_ref, mod_ref, g_ref, w_ref, o_ref):
    x = x_ref[...]
    m = mod_ref[0]
    shift = m[:, :D_MODEL]
    scale = m[:, D_MODEL:2 * D_MODEL]
    h = (_rms_rows(x, g_ref[0]) * (1.0 + scale) + shift).astype(bf16)
    for a in range(0, PROJ_W, 512):
        b = min(a + 512, PROJ_W)
        o_ref[:, a:b] = _dot(h, w_ref[:, a:b])


def _mod_row(latent, tm, seq):
    if latent:
        return lambda t: ((t * tm) // seq, 0, 0)
    return lambda t: (4, 0, 0)


def _inproj(x, mod_l, norm_g, w_in_p, l, latent, seq):
    ntok = x.shape[0]
    tm = 256
    return pl.pallas_call(
        _inproj_kernel,
        grid=(ntok // tm,),
        in_specs=[pl.BlockSpec((tm, D_MODEL), lambda t: (t, 0)),
                  pl.BlockSpec((1, 1, 3 * D_MODEL), _mod_row(latent, tm, seq)),
                  pl.BlockSpec((1, 1, D_MODEL), lambda t: (l, 0, 0)),
                  pl.BlockSpec((None, D_MODEL, PROJ_W), lambda t: (l, 0, 0),
                               pipeline_mode=pl.Buffered(1))],
        out_specs=pl.BlockSpec((tm, PROJ_W), lambda t: (t, 0)),
        out_shape=jax.ShapeDtypeStruct((ntok, PROJ_W), f32),
        compiler_params=_cparams(1),
        name="inproj",
    )(x, mod_l, norm_g, w_in_p)


def _rope(x, cos_t, sin_lo, sin_hi):
    return x * cos_t + pltpu.roll(x, LANES - 8, 1) * sin_lo + pltpu.roll(x, 8, 1) * sin_hi


def _build_kv(ckvn_b, kr, wk_ref, wv_ref, knorm, rope, k_s, v_s, r0):
    n = ckvn_b.shape[0]
    kfull = _dot(ckvn_b, wk_ref[...])
    v_s[r0:r0 + n, :] = _dot(ckvn_b, wv_ref[...]).astype(bf16)
    for h in range(MLA_HEADS):
        sl = slice(h * HEAD_SLOT, (h + 1) * HEAD_SLOT)
        kh = _rms_rows(kfull[:, sl] + kr, knorm, n=MLA_QK)
        if rope is not None:
            kh = _rope(kh, *rope)
        k_s[r0:r0 + n, sl] = kh.astype(bf16)


def _rope_lane_mask(shape):
    lane = lax.broadcasted_iota(jnp.int32, shape, 1)
    return (lane >= ROPE_LANE0) & (lane < ROPE_LANE0 + MLA_ROPE)


def _attend_block(qa_b, wuq_ref, qnorm, rope, k_s, v_s, o_ref):
    tq = qa_b.shape[0]
    qfull = _dot(qa_b, wuq_ref[...])
    scale = 1.0 / math.sqrt(MLA_QK)
    lane = lax.broadcasted_iota(jnp.int32, (tq, LANES), 1)
    for hp in range(MLA_HEADS // 2):
        vp = v_s[:, hp * LANES:(hp + 1) * LANES]
        outs = []
        for h in (2 * hp, 2 * hp + 1):
            sl = slice(h * HEAD_SLOT, (h + 1) * HEAD_SLOT)
            qh = _rms_rows(qfull[:, sl], qnorm, n=MLA_QK)
            if rope is not None:
                qh = _rope(qh, *rope)
            s = _dot_nt(qh.astype(bf16), k_s[:, sl]) * scale
            p = jnp.exp(s - jnp.max(s, axis=-1, keepdims=True))
            den = jnp.sum(p, axis=-1, keepdims=True)
            outs.append(_dot(p.astype(bf16), vp) / den)
        o_ref[:, hp * LANES:(hp + 1) * LANES] = jnp.where(lane < MLA_V, outs[0], outs[1])


def _attn_ctx_kernel(cqs_ref, ckv_ref, qan_ref, wuq_ref, kvn_ref, wk_ref, wv_ref, qn_ref, kn_ref,
                     o_ref, ckvn_ref, kr_ref, k_s, v_s):
    small = cqs_ref[:, MLA_Q_RANK:]
    kr = jnp.where(_rope_lane_mask(small.shape), small, 0.0)
    kr_ref[...] = small[:, ROPE_LANE0:ROPE_LANE0 + MLA_ROPE]
    ckvn = _rms_rows(ckv_ref[...], kvn_ref[0])
    ckvn_ref[...] = ckvn
    _build_kv(ckvn.astype(bf16), kr, wk_ref, wv_ref, kn_ref[0], None, k_s, v_s, 0)
    qa = _rms_rows(cqs_ref[:, :MLA_Q_RANK], qan_ref[0]).astype(bf16)
    _attend_block(qa, wuq_ref, qn_ref[0], None, k_s, v_s, o_ref)


def _attn_lat_kernel(cqs_ref, ckv_ref, cckv_ref, ckr_ref, cos_ref, slo_ref, shi_ref,
                     qan_ref, wuq_ref, kvn_ref, wk_ref, wv_ref, qn_ref, kn_ref,
                     o_ref, k_s, v_s, *, seq, past, tq):
    qi = pl.program_id(1)
    rb = 256

    @pl.when(qi == 0)
    def _():
        _build_kv(cckv_ref[...].astype(bf16), ckr_ref[...], wk_ref, wv_ref, kn_ref[0], None, k_s, v_s, 0)
        for r in range(seq // rb):
            rs = slice(r * rb, (r + 1) * rb)
            small = cqs_ref[rs, MLA_Q_RANK:]
            kr = jnp.where(_rope_lane_mask(small.shape), small, 0.0)
            ckvn = _rms_rows(ckv_ref[rs, :], kvn_ref[0])
            rope = (cos_ref[rs, :], slo_ref[rs, :], shi_ref[rs, :])
            _build_kv(ckvn.astype(bf16), kr, wk_ref, wv_ref, kn_ref[0], rope, k_s, v_s, past + r * rb)

    rows = pl.ds(pl.multiple_of(qi * tq, tq), tq)
    qa = _rms_rows(cqs_ref[rows, :MLA_Q_RANK], qan_ref[0]).astype(bf16)
    rope = (cos_ref[rows, :], slo_ref[rows, :], shi_ref[rows, :])
    _attend_block(qa, wuq_ref, qn_ref[0], rope, k_s, v_s, o_ref)


def _attn_weight_specs(l, nidx):
    z = (0,) * (nidx - 1)

    def const(*idx):
        return lambda *g: idx

    return [pl.BlockSpec((1, 1, MLA_Q_RANK), const(l, 0, 0)),
            pl.BlockSpec((None, MLA_Q_RANK, MLA_HEADS * HEAD_SLOT), const(l, 0, 0)),
            pl.BlockSpec((1, 1, MLA_KV_RANK), const(l, 0, 0)),
            pl.BlockSpec((None, MLA_KV_RANK, MLA_HEADS * HEAD_SLOT), const(l, 0, 0)),
            pl.BlockSpec((None, MLA_KV_RANK, MLA_WIDTH), const(l, 0, 0)),
            pl.BlockSpec((1, 1, HEAD_SLOT), const(l, 0, 0)),
            pl.BlockSpec((1, 1, HEAD_SLOT), const(l, 0, 0))]


def _attn_ctx(proj, wts, l, batch, seq):
    ntok = batch * seq
    return pl.pallas_call(
        _attn_ctx_kernel,
        grid=(batch,),
        in_specs=[pl.BlockSpec((seq, 512), lambda b: (b, OFF_CQS // 512)),
                  pl.BlockSpec((seq, MLA_KV_RANK), lambda b: (b, OFF_CKV // MLA_KV_RANK))]
        + _attn_weight_specs(l, 1),
        out_specs=[pl.BlockSpec((seq, MLA_WIDTH), lambda b: (b, 0)),
                   pl.BlockSpec((seq, MLA_KV_RANK), lambda b: (b, 0)),
                   pl.BlockSpec((seq, MLA_ROPE), lambda b: (b, 0))],
        out_shape=[jax.ShapeDtypeStruct((ntok, MLA_WIDTH), f32),
                   jax.ShapeDtypeStruct((ntok, MLA_KV_RANK), f32),
                   jax.ShapeDtypeStruct((ntok, MLA_ROPE), f32)],
        scratch_shapes=[pltpu.VMEM((seq, MLA_HEADS * HEAD_SLOT), bf16),
                        pltpu.VMEM((seq, MLA_WIDTH), bf16)],
        compiler_params=_cparams(1),
        name="attn_ctx",
    )(proj, proj, *wts)


def _attn_lat(proj, cache_ckv, cache_kr_p, rope_tabs, wts, l, batch, seq, past):
    ntok = batch * seq
    tq = 256
    nq = seq // tq
    kern = functools.partial(_attn_lat_kernel, seq=seq, past=past, tq=tq)
    tab = pl.BlockSpec((seq, LANES), lambda b, q: (0, 0))
    return pl.pallas_call(
        kern,
        grid=(batch, nq),
        in_specs=[pl.BlockSpec((seq, 512), lambda b, q: (b, OFF_CQS // 512)),
                  pl.BlockSpec((seq, MLA_KV_RANK), lambda b, q: (b, OFF_CKV // MLA_KV_RANK)),
                  pl.BlockSpec((None, None, past, MLA_KV_RANK), lambda b, q: (b, l, 0, 0)),
                  pl.BlockSpec((None, None, past, LANES), lambda b, q: (b, l, 0, 0)),
                  tab, tab, tab]
        + _attn_weight_specs(l, 2),
        out_specs=pl.BlockSpec((tq, MLA_WIDTH), lambda b, q: (b * nq + q, 0)),
        out_shape=jax.ShapeDtypeStruct((ntok, MLA_WIDTH), f32),
        scratch_shapes=[pltpu.VMEM((past + seq, MLA_HEADS * HEAD_SLOT), bf16),
                        pltpu.VMEM((past + seq, MLA_WIDTH), bf16)],
        compiler_params=_cparams(2),
        name="attn_lat",
    )(proj, proj, cache_ckv, cache_kr_p, *rope_tabs, *wts)


def _split3(x):
    hi = x.astype(bf16)
    r1 = x - hi.astype(f32)
    mid = r1.astype(bf16)
    lo = (r1 - mid.astype(f32)).astype(bf16)
    return hi, mid, lo


def _tri_cumsum(tri_b, x):
    hi, mid, lo = _split3(x)
    return _dot(tri_b, hi) + _dot(tri_b, mid) + _dot(tri_b, lo)


def _unit_tri_inverse(a, eye, level):
    t = eye - jnp.where(level == 1, a, 0.0)
    for lv in range(2, GDN_CHUNK.bit_length()):
        r = _dot(t.astype(bf16), jnp.where(level == lv, a, 0.0).astype(bf16))
        t = t - _dot(r.astype(bf16), t.astype(bf16))
    return t


def _gdn_kernel(*refs, seq, has_state):
    if has_state:
        (qkv_ref, small_ref, cw_ref, alog_ref, dtb_ref, onorm_ref, s0_ref,
         o_ref, xpad, q_s, k_s, v_s, g_s, b_s, st_s) = refs
        sout_ref = None
    else:
        (qkv_ref, small_ref, cw_ref, alog_ref, dtb_ref, onorm_ref,
         o_ref, sout_ref, xpad, q_s, k_s, v_s, g_s, b_s, st_s) = refs
    C = GDN_CHUNK
    nchunk = seq // C
    H = GDN_HEADS
    width = 2 * GDN_KW + GDN_VW
    halo = 8

    xpad[0:halo, :] = jnp.zeros((halo, width), f32)
    xpad[halo + seq:, :] = jnp.zeros((halo, width), f32)
    xpad[halo:halo + seq, :] = qkv_ref[...]
    o_ref[...] = jnp.zeros((seq, GDN_VW), f32)
    if has_state:
        for d in range(2):
            for h in range(H):
                st_s[d * H + h] = s0_ref[d, h]
    else:
        st_s[...] = jnp.zeros((2 * H, GDN_DK, GDN_DV), f32)

    neg_a = -jnp.exp(alog_ref[0])
    dtb = dtb_ref[0]

    def prep(c, carry):
        r0 = pl.multiple_of(c * C, C)
        for j in range(width // LANES):
            ls = slice(j * LANES, (j + 1) * LANES)
            xe = xpad[pl.ds(r0, C + 2 * halo), ls]
            w = cw_ref[:, ls]
            y = xe[halo - 2:halo - 2 + C] * w[0:1]
            for tap in range(1, CONV_W):
                y = y + xe[halo - 2 + tap:halo - 2 + tap + C] * w[tap:tap + 1]
            y = y * jax.nn.sigmoid(y)
            if j < 2 * H:
                y = y * lax.rsqrt(jnp.sum(y * y, axis=-1, keepdims=True) + EPS)
            if j < H:
                q_s[pl.ds(r0, C), ls] = y * (GDN_DK ** -0.5)
            elif j < 2 * H:
                k_s[pl.ds(r0, C), (j - H) * LANES:(j - H + 1) * LANES] = y
            else:
                v_s[pl.ds(r0, C), (j - 2 * H) * LANES:(j - 2 * H + 1) * LANES] = y
        sm = small_ref[pl.ds(r0, C), :]
        z = sm + dtb
        g_s[pl.ds(r0, C), :] = neg_a * (jnp.maximum(z, 0.0) + jnp.log1p(jnp.exp(-jnp.abs(z))))
        b_s[pl.ds(r0, C), :] = pltpu.roll(jax.nn.sigmoid(sm), LANES - 2 * H, 1)
        return carry

    lax.fori_loop(0, nchunk, prep, 0)

    ri = lax.broadcasted_iota(jnp.int32, (C, C), 0)
    ci = lax.broadcasted_iota(jnp.int32, (C, C), 1)
    eye = (ri == ci).astype(f32)
    incl = (ri >= ci, ri <= ci)
    strict = (ri > ci, ri < ci)
    tri_b = (incl[0].astype(bf16), incl[1].astype(bf16))
    xor = ri ^ ci
    level = sum((xor >= (1 << b)).astype(jnp.int32) for b in range(GDN_CHUNK.bit_length() - 1))

    def step(i, carry):
        for d in range(2):
            c = i if d == 0 else nchunk - 1 - i
            r0 = pl.multiple_of(c * C, C)
            rows = pl.ds(r0, C)
            gc = _tri_cumsum(tri_b[d], g_s[rows, :])
            gct = gc.T
            bt = b_s[rows, :]
            last = C - 1 if d == 0 else 0
            for h in range(H):
                ch = d * H + h
                ls = slice(h * LANES, (h + 1) * LANES)
                k = k_s[rows, ls]
                q = q_s[rows, ls]
                v = v_s[rows, ls]
                gcc = jnp.broadcast_to(gc[:, ch:ch + 1], (C, LANES))
                btc = jnp.broadcast_to(bt[:, ch:ch + 1], (C, LANES))
                glast = gcc[last:last + 1, :]
                e1 = jnp.exp(gcc)
                e3 = jnp.exp(glast - gcc)
                diff = gcc[:, :C] - jnp.broadcast_to(gct[ch:ch + 1, :], (C, C))
                dec = jnp.where(incl[d], jnp.exp(jnp.where(incl[d], diff, 0.0)), 0.0)
                kb = k.astype(bf16)
                kq = _dot_nt(jnp.concatenate([k, q], axis=0).astype(bf16), kb)
                a = jnp.where(strict[d], btc[:, :C] * kq[:C] * dec, 0.0)
                tinv = _unit_tri_inverse(a, eye, level)
                rhs = jnp.concatenate([v * btc, k * (btc * e1)], axis=1).astype(bf16)
                uw = _dot(tinv.astype(bf16), rhs)
                s_old = st_s[ch]
                ws_qs = _dot(jnp.concatenate([uw[:, LANES:], q * e1], axis=0).astype(bf16),
                             s_old.astype(bf16))
                v_new = uw[:, :LANES] - ws_qs[:C]
                vnb = v_new.astype(bf16)
                att = (kq[C:] * dec).astype(bf16)
                o = ws_qs[C:] + _dot(att, vnb)
                st_s[ch] = s_old * jnp.exp(glast) + _dot_tn((k * e3).astype(bf16), vnb)
                o_ref[rows, ls] += o
        return carry

    lax.fori_loop(0, nchunk, step, 0)

    onorm = onorm_ref[0]

    def fin(c, carry):
        rows = pl.ds(pl.multiple_of(c * C, C), C)
        for h in range(H):
            ls = slice(h * LANES, (h + 1) * LANES)
            o_ref[rows, ls] = _rms_rows(o_ref[rows, ls], onorm)
        return carry

    lax.fori_loop(0, nchunk, fin, 0)
    if sout_ref is not None:
        for d in range(2):
            for h in range(H):
                sout_ref[d, h] = st_s[d * H + h]


def _gdn(proj, conv_w, alog_p, dtb_p, onorm, state, l, batch, seq):
    ntok = batch * seq
    width = 2 * GDN_KW + GDN_VW
    has_state = state is not None
    kern = functools.partial(_gdn_kernel, seq=seq, has_state=has_state)
    in_specs = [pl.BlockSpec((seq, width), lambda b: (b, OFF_QKV // width)),
                pl.BlockSpec((seq, LANES), lambda b: (b, OFF_SMALL // LANES)),
                pl.BlockSpec((None, CONV_W, width), lambda b: (l, 0, 0)),
                pl.BlockSpec((1, 1, LANES), lambda b: (l, 0, 0)),
                pl.BlockSpec((1, 1, LANES), lambda b: (l, 0, 0)),
                pl.BlockSpec((1, 1, GDN_DV), lambda b: (l, 0, 0))]
    args = [proj, proj, conv_w, alog_p, dtb_p, onorm]
    o_spec = pl.BlockSpec((seq, GDN_VW), lambda b: (b, 0))
    o_shape = jax.ShapeDtypeStruct((ntok, GDN_VW), f32)
    st_block = (None, None, 2, GDN_HEADS, GDN_DK, GDN_DV)
    if has_state:
        in_specs.append(pl.BlockSpec(st_block, lambda b: (b, l, 0, 0, 0, 0)))
        args.append(state)
        out_specs, out_shape = o_spec, o_shape
    else:
        out_specs = [o_spec, pl.BlockSpec(st_block[1:], lambda b: (b, 0, 0, 0, 0))]
        out_shape = [o_shape, jax.ShapeDtypeStruct((batch, 2, GDN_HEADS, GDN_DK, GDN_DV), f32)]
    return pl.pallas_call(
        kern,
        grid=(batch,),
        in_specs=in_specs,
        out_specs=out_specs,
        out_shape=out_shape,
        scratch_shapes=[pltpu.VMEM((seq + 16, width), f32),
                        pltpu.VMEM((seq, GDN_KW), f32),
                        pltpu.VMEM((seq, GDN_KW), f32),
                        pltpu.VMEM((seq, GDN_VW), f32),
                        pltpu.VMEM((seq, LANES), f32),
                        pltpu.VMEM((seq, LANES), f32),
                        pltpu.VMEM((2 * GDN_HEADS, GDN_DK, GDN_DV), f32)],
        compiler_params=_cparams(1),
        name="gdn_lat" if has_state else "gdn_ctx",
    )(*args)


def _merge_kernel(x_ref, mod_ref, oa_ref, ob_ref, za_ref, zb_ref, cu_ref, cv_ref, zc_ref, gl_ref,
                  lng_ref, lnb_ref, ws_ref, bs_ref, wbr_ref, wo_ref, out_ref, sv_s):
    tm = x_ref.shape[0]
    u = jax.nn.gelu(cu_ref[...])
    vf = jax.nn.gelu(cv_ref[...])
    mu = jnp.mean(vf, axis=-1, keepdims=True)
    vc = vf - mu
    var = jnp.mean(vc * vc, axis=-1, keepdims=True)
    vn = (vc * lax.rsqrt(var + EPS) * lng_ref[0] + lnb_ref[0]).astype(bf16)
    for ck in range(tm // CM_CHUNK):
        rs = slice(ck * CM_CHUNK, (ck + 1) * CM_CHUNK)
        for g in range(CM_GROUPS):
            ls = slice(g * LANES, (g + 1) * LANES)
            sv_s[rs, ls] = _dot(ws_ref[g], vn[rs, ls]) + bs_ref[:, ls]
    o_c = u * sv_s[...]

    def silu(z):
        return z * jax.nn.sigmoid(z)

    brs = (oa_ref[...] * silu(za_ref[...]), ob_ref[...] * silu(zb_ref[...]), o_c * silu(zc_ref[...]))
    ysum = None
    for n in range(N_BRANCH):
        yb = _dot(brs[n].astype(bf16), wbr_ref[n])
        t = jax.nn.sigmoid(gl_ref[:, n * D_MODEL:(n + 1) * D_MODEL]) * yb
        ysum = t if ysum is None else ysum + t
    y = _dot(ysum.astype(bf16), wo_ref[...])
    gate = mod_ref[0][:, 2 * D_MODEL:]
    out_ref[...] = x_ref[...] + gate * y


def _merge(x, mod_l, proj, o_a, o_b, lng, lnb, ws_b, bs_full, wbr_b, wo_b, l, latent, seq):
    ntok = x.shape[0]
    tm = 256

    def col(off):
        return pl.BlockSpec((tm, 512), lambda t: (t, off // 512))

    def const(*idx):
        return lambda t: idx

    return pl.pallas_call(
        _merge_kernel,
        grid=(ntok // tm,),
        in_specs=[pl.BlockSpec((tm, D_MODEL), lambda t: (t, 0)),
                  pl.BlockSpec((1, 1, 3 * D_MODEL), _mod_row(latent, tm, seq)),
                  pl.BlockSpec((tm, 512), lambda t: (t, 0)),
                  pl.BlockSpec((tm, 512), lambda t: (t, 0)),
                  col(OFF_ZA), col(OFF_ZB), col(OFF_CU), col(OFF_CV), col(OFF_ZC),
                  pl.BlockSpec((tm, N_BRANCH * D_MODEL), lambda t: (t, 0)),
                  pl.BlockSpec((1, 1, CM_WIDTH), const(l, 0, 0)),
                  pl.BlockSpec((1, 1, CM_WIDTH), const(l, 0, 0)),
                  pl.BlockSpec((None, CM_GROUPS, CM_CHUNK, CM_CHUNK), const(l, 0, 0, 0)),
                  pl.BlockSpec((None, CM_CHUNK, CM_WIDTH), const(l, 0, 0)),
                  pl.BlockSpec((None, N_BRANCH, BRANCH_W, D_MODEL), const(l, 0, 0, 0)),
                  pl.BlockSpec((None, D_MODEL, D_MODEL), const(l, 0, 0))],
        out_specs=pl.BlockSpec((tm, D_MODEL), lambda t: (t, 0)),
        out_shape=jax.ShapeDtypeStruct((ntok, D_MODEL), f32),
        scratch_shapes=[pltpu.VMEM((tm, CM_WIDTH), f32)],
        compiler_params=_cparams(1),
        name="merge",
    )(x, mod_l, o_a, o_b, proj, proj, proj, proj, proj, proj, lng, lnb, ws_b, bs_full, wbr_b, wo_b)


def _prep_w_in(w_in):
    offs = [0]
    for s in SPLIT_SIZES:
        offs.append(offs[-1] + s)
    cq, ckv, krope, z_a, qkv, ga, gb, z_b, cu, cv, z_c, gl = [
        w_in[..., offs[i]:offs[i + 1]] for i in range(len(SPLIT_SIZES))]

    def zeros(n):
        return jnp.zeros(w_in.shape[:-1] + (n,), w_in.dtype)

    small = jnp.concatenate([ga, gb, zeros(ROPE_LANE0 - 4 * GDN_HEADS), krope,
                             zeros(LANES - ROPE_LANE0 - MLA_ROPE)], axis=-1)
    return jnp.concatenate([gl, qkv, cq, small, z_a, z_b, cu, cv, z_c, ckv], axis=-1).astype(bf16)


def _pad_last(x, n):
    return jnp.pad(x, [(0, 0)] * (x.ndim - 1) + [(0, n - x.shape[-1])])


def _rope_tables(seq):
    t = jnp.arange(seq, dtype=jnp.int32)
    row = (t // GRID_W).astype(f32)
    colp = (t % GRID_W).astype(f32)
    nf = MLA_ROPE // 4
    inv = ROPE_THETA ** (-jnp.arange(nf, dtype=f32) / nf)
    cos_t = [jnp.ones((seq, ROPE_LANE0), f32)]
    s_lo = [jnp.zeros((seq, ROPE_LANE0), f32)]
    s_hi = [jnp.zeros((seq, ROPE_LANE0), f32)]
    zero = jnp.zeros((seq, nf), f32)
    for pos in (row, colp):
        ang = pos[:, None] * inv[None, :]
        cs, sn = jnp.cos(ang), jnp.sin(ang)
        cos_t += [cs, cs]
        s_lo += [-sn, zero]
        s_hi += [zero, sn]
    tail = LANES - ROPE_LANE0 - MLA_ROPE
    cos_t.append(jnp.ones((seq, tail), f32))
    s_lo.append(jnp.zeros((seq, tail), f32))
    s_hi.append(jnp.zeros((seq, tail), f32))
    return tuple(jnp.concatenate(p, axis=1) for p in (cos_t, s_lo, s_hi))


def kernel(x_prompt, x_sample, cache_ckv, cache_krope, state_gdn, c, c_ctx, norm_g, w_mod, b_mod, w_in, q_a_norm, w_uq, kv_a_norm, w_ukv, q_norm, k_norm, conv_w, a_log, dt_bias, gdn_onorm, cm_ln_g, cm_ln_b, w_s, b_s, w_branch, w_o):
    L = DEPTH
    batch, seq, _ = x_prompt.shape
    dbatch, dseq, _ = x_sample.shape
    past = cache_ckv.shape[2]

    w_in_p = _prep_w_in(w_in)
    wuq_p = _pad_last(w_uq.reshape(L, MLA_Q_RANK, MLA_HEADS, MLA_QK), HEAD_SLOT)
    wuq_p = wuq_p.reshape(L, MLA_Q_RANK, MLA_HEADS * HEAD_SLOT).astype(bf16)
    wukv = w_ukv.reshape(L, MLA_KV_RANK, MLA_HEADS, MLA_NOPE + MLA_V)
    wk_p = _pad_last(wukv[..., :MLA_NOPE], HEAD_SLOT).reshape(L, MLA_KV_RANK, MLA_HEADS * HEAD_SLOT).astype(bf16)
    wv_p = wukv[..., MLA_NOPE:].reshape(L, MLA_KV_RANK, MLA_WIDTH).astype(bf16)
    qn_p = _pad_last(q_norm, HEAD_SLOT).reshape(L, 1, HEAD_SLOT)
    kn_p = _pad_last(k_norm, HEAD_SLOT).reshape(L, 1, HEAD_SLOT)
    attn_w = (q_a_norm.reshape(L, 1, MLA_Q_RANK), wuq_p, kv_a_norm.reshape(L, 1, MLA_KV_RANK),
              wk_p, wv_p, qn_p, kn_p)
    cache_kr_p = jnp.pad(cache_krope, [(0, 0)] * 3 + [(ROPE_LANE0, LANES - ROPE_LANE0 - MLA_ROPE)])
    rope_tabs = _rope_tables(dseq)
    alog_p = _pad_last(a_log.reshape(L, 1, 2 * GDN_HEADS), LANES)
    dtb_p = _pad_last(dt_bias.reshape(L, 1, 2 * GDN_HEADS), LANES)
    onorm = gdn_onorm.reshape(L, 1, GDN_DV)
    lng = cm_ln_g.reshape(L, 1, CM_WIDTH)
    lnb = cm_ln_b.reshape(L, 1, CM_WIDTH)
    ws_b = w_s.astype(bf16)
    bs_full = jnp.repeat(jnp.swapaxes(b_s, 1, 2), CM_WIDTH // CM_GROUPS, axis=2)
    wbr_b = w_branch.astype(bf16)
    wo_b = w_o.astype(bf16)
    norm_g3 = norm_g.reshape(L, 1, D_MODEL)

    c8 = jnp.concatenate([c, c_ctx[None, :], jnp.zeros((8 - dbatch - 1, D_MODEL), f32)], axis=0)
    mod = _modulation(c8, w_mod, b_mod)

    yp = x_prompt.reshape(batch * seq, D_MODEL)
    ys = x_sample.reshape(dbatch * dseq, D_MODEL)
    ckvs, kropes, states = [], [], []
    for l in range(L):
        mod_l = mod[l].reshape(8, 1, 3 * D_MODEL)
        proj = _inproj(yp, mod_l, norm_g3, w_in_p, l, False, seq)
        o_a, ckvn, kr = _attn_ctx(proj, attn_w, l, batch, seq)
        o_b, s_new = _gdn(proj, conv_w, alog_p, dtb_p, onorm, None, l, batch, seq)
        yp = _merge(yp, mod_l, proj, o_a, o_b, lng, lnb, ws_b, bs_full, wbr_b, wo_b, l, False, seq)
        ckvs.append(ckvn.reshape(batch, seq, MLA_KV_RANK))
        kropes.append(kr.reshape(batch, seq, MLA_ROPE))
        states.append(s_new)
        proj = _inproj(ys, mod_l, norm_g3, w_in_p, l, True, dseq)
        o_a = _attn_lat(proj, cache_ckv, cache_kr_p, rope_tabs, attn_w, l, dbatch, dseq, past)
        o_b = _gdn(proj, conv_w, alog_p, dtb_p, onorm, state_gdn, l, dbatch, dseq)
        ys = _merge(ys, mod_l, proj, o_a, o_b, lng, lnb, ws_b, bs_full, wbr_b, wo_b, l, True, dseq)
    return (yp.reshape(batch, seq, D_MODEL), ys.reshape(dbatch, dseq, D_MODEL),
            jnp.stack(ckvs, axis=1), jnp.stack(kropes, axis=1), jnp.stack(states, axis=1))
```

```python
import functools
import math

import jax
import jax.numpy as jnp
from jax import lax
from jax.experimental import pallas as pl
from jax.experimental.pallas import tpu as pltpu

D_MODEL = 1024
DEPTH = 2
GRID_W = 64
EPS = 1e-6
MLA_HEADS = 8
MLA_NOPE = 64
MLA_ROPE = 32
MLA_QK = MLA_NOPE + MLA_ROPE
MLA_V = 64
MLA_Q_RANK = 384
MLA_KV_RANK = 256
MLA_WIDTH = MLA_HEADS * MLA_V
ROPE_THETA = 10000.0
GDN_HEADS = 4
GDN_DK = 128
GDN_DV = 128
GDN_KW = GDN_HEADS * GDN_DK
GDN_VW = GDN_HEADS * GDN_DV
GDN_CHUNK = 64
CONV_W = 5
GDN_SOLVE_CHUNKS = 2
CM_GROUPS = 4
CM_CHUNK = 128
CM_WIDTH = 512
N_BRANCH = 3
BRANCH_W = 512
SPLIT_SIZES = (MLA_Q_RANK, MLA_KV_RANK, MLA_ROPE, MLA_WIDTH,
               2 * GDN_KW + GDN_VW, 2 * GDN_HEADS, 2 * GDN_HEADS, GDN_VW,
               CM_WIDTH, CM_WIDTH, CM_WIDTH, N_BRANCH * D_MODEL)

LANES = 128
HEAD_SLOT = LANES
ROPE_LANE0 = MLA_NOPE

OFF_GL = 0
OFF_QKV = OFF_GL + N_BRANCH * D_MODEL
OFF_CQS = OFF_QKV + 2 * GDN_KW + GDN_VW
OFF_SMALL = OFF_CQS + MLA_Q_RANK
OFF_ZA = OFF_CQS + 512
OFF_ZB = OFF_ZA + 512
OFF_CU = OFF_ZB + 512
OFF_CV = OFF_CU + 512
OFF_ZC = OFF_CV + 512
OFF_CKV = OFF_ZC + 512
PROJ_W = OFF_CKV + MLA_KV_RANK

VMEM_LIMIT = 56 * 1024 * 1024

f32 = jnp.float32
bf16 = jnp.bfloat16


def _cparams(n_axes):
    return pltpu.CompilerParams(dimension_semantics=("arbitrary",) * n_axes,
                                vmem_limit_bytes=VMEM_LIMIT)


def _dot(a, b):
    return jnp.dot(a, b, preferred_element_type=f32)


def _dot_nt(a, b):
    return lax.dot_general(a, b, (((1,), (1,)), ((), ())), preferred_element_type=f32)


def _dot_tn(a, b):
    return lax.dot_general(a, b, (((0,), (0,)), ((), ())), preferred_element_type=f32)


def _rms_rows(x, g, n=None):
    n = x.shape[-1] if n is None else n
    ms = jnp.sum(x * x, axis=-1, keepdims=True) * (1.0 / n)
    return x * lax.rsqrt(ms + EPS) * g


def _mod_kernel(c_ref, w_ref, b_ref, o_ref):
    a = c_ref[...]
    a = (a * jax.nn.sigmoid(a)).astype(bf16)
    o_ref[0] = _dot(a, w_ref[0].astype(bf16)) + b_ref[0]


def _modulation(c8, w_mod, b_mod):
    tn = 512
    return pl.pallas_call(
        _mod_kernel,
        grid=(DEPTH, 3 * D_MODEL // tn),
        in_specs=[pl.BlockSpec((8, D_MODEL), lambda l, n: (0, 0)),
                  pl.BlockSpec((1, D_MODEL, tn), lambda l, n: (l, 0, n)),
                  pl.BlockSpec((1, 1, tn), lambda l, n: (l, 0, n))],
        out_specs=pl.BlockSpec((1, 8, tn), lambda l, n: (l, 0, n)),
        out_shape=jax.ShapeDtypeStruct((DEPTH, 8, 3 * D_MODEL), f32),
        compiler_params=_cparams(2),
        name="modulation",
    )(c8, w_mod, b_mod.reshape(DEPTH, 1, 3 * D_MODEL))


def _inproj_kernel(x_ref, mod_ref, g_ref, w_ref, o_ref):
    x = x_ref[...]
    m = mod_ref[0]
    shift = m[:, :D_MODEL]
    scale = m[:, D_MODEL:2 * D_MODEL]
    h = (_rms_rows(x, g_ref[0]) * (1.0 + scale) + shift).astype(bf16)
    for a in range(0, PROJ_W, 512):
        b = min(a + 512, PROJ_W)
        o_ref[:, a:b] = _dot(h, w_ref[:, a:b])


def _mod_row(latent, tm, seq):
    if latent:
        return lambda t: ((t * tm) // seq, 0, 0)
    return lambda t: (4, 0, 0)


def _inproj(x, mod_l, norm_g, w_in_p, l, latent, seq):
    ntok = x.shape[0]
    tm = 256
    return pl.pallas_call(
        _inproj_kernel,
        grid=(ntok // tm,),
        in_specs=[pl.BlockSpec((tm, D_MODEL), lambda t: (t, 0)),
                  pl.BlockSpec((1, 1, 3 * D_MODEL), _mod_row(latent, tm, seq)),
                  pl.BlockSpec((1, 1, D_MODEL), lambda t: (l, 0, 0)),
                  pl.BlockSpec((None, D_MODEL, PROJ_W), lambda t: (l, 0, 0),
                               pipeline_mode=pl.Buffered(1))],
        out_specs=pl.BlockSpec((tm, PROJ_W), lambda t: (t, 0)),
        out_shape=jax.ShapeDtypeStruct((ntok, PROJ_W), f32),
        compiler_params=_cparams(1),
        name="inproj",
    )(x, mod_l, norm_g, w_in_p)


def _rope(x, cos_t, sin_lo, sin_hi):
    return x * cos_t + pltpu.roll(x, LANES - 8, 1) * sin_lo + pltpu.roll(x, 8, 1) * sin_hi


def _build_kv(ckvn_b, kr, wk_ref, wv_ref, knorm, rope, k_s, v_s, r0):
    n = ckvn_b.shape[0]
    kfull = _dot(ckvn_b, wk_ref[...])
    v_s[r0:r0 + n, :] = _dot(ckvn_b, wv_ref[...]).astype(bf16)
    for h in range(MLA_HEADS):
        sl = slice(h * HEAD_SLOT, (h + 1) * HEAD_SLOT)
        kh = _rms_rows(kfull[:, sl] + kr, knorm, n=MLA_QK)
        if rope is not None:
            kh = _rope(kh, *rope)
        k_s[r0:r0 + n, sl] = kh.astype(bf16)


def _rope_lane_mask(shape):
    lane = lax.broadcasted_iota(jnp.int32, shape, 1)
    return (lane >= ROPE_LANE0) & (lane < ROPE_LANE0 + MLA_ROPE)


def _attend_block(qa_b, wuq_ref, qnorm, rope, k_s, v_s, o_ref):
    tq = qa_b.shape[0]
    qfull = _dot(qa_b, wuq_ref[...])
    scale = 1.0 / math.sqrt(MLA_QK)
    lane = lax.broadcasted_iota(jnp.int32, (tq, LANES), 1)
    for hp in range(MLA_HEADS // 2):
        vp = v_s[:, hp * LANES:(hp + 1) * LANES]
        outs = []
        for h in (2 * hp, 2 * hp + 1):
            sl = slice(h * HEAD_SLOT, (h + 1) * HEAD_SLOT)
            qh = _rms_rows(qfull[:, sl], qnorm, n=MLA_QK)
            if rope is not None:
                qh = _rope(qh, *rope)
            s = _dot_nt(qh.astype(bf16), k_s[:, sl]) * scale
            p = jnp.exp(s - jnp.max(s, axis=-1, keepdims=True))
            den = jnp.sum(p, axis=-1, keepdims=True)
            outs.append(_dot(p.astype(bf16), vp) / den)
        o_ref[:, hp * LANES:(hp + 1) * LANES] = jnp.where(lane < MLA_V, outs[0], outs[1])


def _attn_ctx_kernel(cqs_ref, ckv_ref, qan_ref, wuq_ref, kvn_ref, wk_ref, wv_ref, qn_ref, kn_ref,
                     o_ref, ckvn_ref, kr_ref, k_s, v_s):
    small = cqs_ref[:, MLA_Q_RANK:]
    kr = jnp.where(_rope_lane_mask(small.shape), small, 0.0)
    kr_ref[...] = small[:, ROPE_LANE0:ROPE_LANE0 + MLA_ROPE]
    ckvn = _rms_rows(ckv_ref[...], kvn_ref[0])
    ckvn_ref[...] = ckvn
    _build_kv(ckvn.astype(bf16), kr, wk_ref, wv_ref, kn_ref[0], None, k_s, v_s, 0)
    qa = _rms_rows(cqs_ref[:, :MLA_Q_RANK], qan_ref[0]).astype(bf16)
    _attend_block(qa, wuq_ref, qn_ref[0], None, k_s, v_s, o_ref)


def _attn_lat_kernel(cqs_ref, ckv_ref, cckv_ref, ckr_ref, cos_ref, slo_ref, shi_ref,
                     qan_ref, wuq_ref, kvn_ref, wk_ref, wv_ref, qn_ref, kn_ref,
                     o_ref, k_s, v_s, *, seq, past, tq):
    qi = pl.program_id(1)
    rb = 256

    @pl.when(qi == 0)
    def _():
        _build_kv(cckv_ref[...].astype(bf16), ckr_ref[...], wk_ref, wv_ref, kn_ref[0], None, k_s, v_s, 0)
        for r in range(seq // rb):
            rs = slice(r * rb, (r + 1) * rb)
            small = cqs_ref[rs, MLA_Q_RANK:]
            kr = jnp.where(_rope_lane_mask(small.shape), small, 0.0)
            ckvn = _rms_rows(ckv_ref[rs, :], kvn_ref[0])
            rope = (cos_ref[rs, :], slo_ref[rs, :], shi_ref[rs, :])
            _build_kv(ckvn.astype(bf16), kr, wk_ref, wv_ref, kn_ref[0], rope, k_s, v_s, past + r * rb)

    rows = pl.ds(pl.multiple_of(qi * tq, tq), tq)
    qa = _rms_rows(cqs_ref[rows, :MLA_Q_RANK], qan_ref[0]).astype(bf16)
    rope = (cos_ref[rows, :], slo_ref[rows, :], shi_ref[rows, :])
    _attend_block(qa, wuq_ref, qn_ref[0], rope, k_s, v_s, o_ref)


def _attn_weight_specs(l, nidx):
    z = (0,) * (nidx - 1)

    def const(*idx):
        return lambda *g: idx

    return [pl.BlockSpec((1, 1, MLA_Q_RANK), const(l, 0, 0)),
            pl.BlockSpec((None, MLA_Q_RANK, MLA_HEADS * HEAD_SLOT), const(l, 0, 0)),
            pl.BlockSpec((1, 1, MLA_KV_RANK), const(l, 0, 0)),
            pl.BlockSpec((None, MLA_KV_RANK, MLA_HEADS * HEAD_SLOT), const(l, 0, 0)),
            pl.BlockSpec((None, MLA_KV_RANK, MLA_WIDTH), const(l, 0, 0)),
            pl.BlockSpec((1, 1, HEAD_SLOT), const(l, 0, 0)),
            pl.BlockSpec((1, 1, HEAD_SLOT), const(l, 0, 0))]


def _attn_ctx(proj, wts, l, batch, seq):
    ntok = batch * seq
    return pl.pallas_call(
        _attn_ctx_kernel,
        grid=(batch,),
        in_specs=[pl.BlockSpec((seq, 512), lambda b: (b, OFF_CQS // 512)),
                  pl.BlockSpec((seq, MLA_KV_RANK), lambda b: (b, OFF_CKV // MLA_KV_RANK))]
        + _attn_weight_specs(l, 1),
        out_specs=[pl.BlockSpec((seq, MLA_WIDTH), lambda b: (b, 0)),
                   pl.BlockSpec((seq, MLA_KV_RANK), lambda b: (b, 0)),
                   pl.BlockSpec((seq, MLA_ROPE), lambda b: (b, 0))],
        out_shape=[jax.ShapeDtypeStruct((ntok, MLA_WIDTH), f32),
                   jax.ShapeDtypeStruct((ntok, MLA_KV_RANK), f32),
                   jax.ShapeDtypeStruct((ntok, MLA_ROPE), f32)],
        scratch_shapes=[pltpu.VMEM((seq, MLA_HEADS * HEAD_SLOT), bf16),
                        pltpu.VMEM((seq, MLA_WIDTH), bf16)],
        compiler_params=_cparams(1),
        name="attn_ctx",
    )(proj, proj, *wts)


def _attn_lat(proj, cache_ckv, cache_kr_p, rope_tabs, wts, l, batch, seq, past):
    ntok = batch * seq
    tq = 256
    nq = seq // tq
    kern = functools.partial(_attn_lat_kernel, seq=seq, past=past, tq=tq)
    tab = pl.BlockSpec((seq, LANES), lambda b, q: (0, 0))
    return pl.pallas_call(
        kern,
        grid=(batch, nq),
        in_specs=[pl.BlockSpec((seq, 512), lambda b, q: (b, OFF_CQS // 512)),
                  pl.BlockSpec((seq, MLA_KV_RANK), lambda b, q: (b, OFF_CKV // MLA_KV_RANK)),
                  pl.BlockSpec((None, None, past, MLA_KV_RANK), lambda b, q: (b, l, 0, 0)),
                  pl.BlockSpec((None, None, past, LANES), lambda b, q: (b, l, 0, 0)),
                  tab, tab, tab]
        + _attn_weight_specs(l, 2),
        out_specs=pl.BlockSpec((tq, MLA_WIDTH), lambda b, q: (b * nq + q, 0)),
        out_shape=jax.ShapeDtypeStruct((ntok, MLA_WIDTH), f32),
        scratch_shapes=[pltpu.VMEM((past + seq, MLA_HEADS * HEAD_SLOT), bf16),
                        pltpu.VMEM((past + seq, MLA_WIDTH), bf16)],
        compiler_params=_cparams(2),
        name="attn_lat",
    )(proj, proj, cache_ckv, cache_kr_p, *rope_tabs, *wts)


def _split3(x):
    hi = x.astype(bf16)
    r1 = x - hi.astype(f32)
    mid = r1.astype(bf16)
    lo = (r1 - mid.astype(f32)).astype(bf16)
    return hi, mid, lo


def _tri_cumsum(tri_b, x):
    hi, mid, lo = _split3(x)
    return _dot(tri_b, hi) + _dot(tri_b, mid) + _dot(tri_b, lo)


def _lane_bcast(x, c):
    return jnp.broadcast_to(x[:, c:c + 1], (x.shape[0], LANES))


def _gdn_kernel(*refs, seq, has_state):
    if has_state:
        (qkv_ref, small_ref, cw_ref, alog_ref, dtb_ref, onorm_ref, s0_ref,
         o_ref, xpad, q_s, k_s, v_s, g_s, b_s, st_s, wq_s, ak_s, u_s, el_s, rhs_s) = refs
        sout_ref = None
    else:
        (qkv_ref, small_ref, cw_ref, alog_ref, dtb_ref, onorm_ref,
         o_ref, sout_ref, xpad, q_s, k_s, v_s, g_s, b_s, st_s, wq_s, ak_s, u_s, el_s, rhs_s) = refs
    C = GDN_CHUNK
    nchunk = seq // C
    H = GDN_HEADS
    width = 2 * GDN_KW + GDN_VW
    halo = 8

    xpad[0:halo, :] = jnp.zeros((halo, width), f32)
    xpad[halo + seq:, :] = jnp.zeros((halo, width), f32)
    xpad[halo:halo + seq, :] = qkv_ref[...]
    o_ref[...] = jnp.zeros((seq, GDN_VW), f32)
    if has_state:
        for d in range(2):
            for h in range(H):
                st_s[d * H + h] = s0_ref[d, h]
    else:
        st_s[...] = jnp.zeros((2 * H, GDN_DK, GDN_DV), f32)

    neg_a = -jnp.exp(alog_ref[0])
    dtb = dtb_ref[0]

    def prep(c, carry):
        r0 = pl.multiple_of(c * C, C)
        for j in range(width // LANES):
            ls = slice(j * LANES, (j + 1) * LANES)
            xe = xpad[pl.ds(r0, C + 2 * halo), ls]
            w = cw_ref[:, ls]
            y = xe[halo - 2:halo - 2 + C] * w[0:1]
            for tap in range(1, CONV_W):
                y = y + xe[halo - 2 + tap:halo - 2 + tap + C] * w[tap:tap + 1]
            y = y * jax.nn.sigmoid(y)
            if j < 2 * H:
                y = y * lax.rsqrt(jnp.sum(y * y, axis=-1, keepdims=True) + EPS)
            if j < H:
                q_s[pl.ds(r0, C), ls] = y * (GDN_DK ** -0.5)
            elif j < 2 * H:
                k_s[pl.ds(r0, C), (j - H) * LANES:(j - H + 1) * LANES] = y
            else:
                v_s[pl.ds(r0, C), (j - 2 * H) * LANES:(j - 2 * H + 1) * LANES] = y
        sm = small_ref[pl.ds(r0, C), :]
        z = sm + dtb
        g_s[pl.ds(r0, C), :] = neg_a * (jnp.maximum(z, 0.0) + jnp.log1p(jnp.exp(-jnp.abs(z))))
        b_s[pl.ds(r0, C), :] = pltpu.roll(jax.nn.sigmoid(sm), LANES - 2 * H, 1)
        return carry

    lax.fori_loop(0, nchunk, prep, 0)

    ri = lax.broadcasted_iota(jnp.int32, (C, LANES), 0)
    cl = lax.broadcasted_iota(jnp.int32, (C, LANES), 1)
    fwd = cl < C
    cj = cl & (C - 1)
    eye2 = (ri == cj).astype(f32)
    incl2 = (fwd & (ri >= cj)) | (~fwd & (ri <= cj))
    strict2 = (fwd & (ri > cj)) | (~fwd & (ri < cj))
    xor = ri ^ cj
    level2 = sum((xor >= (1 << b)).astype(jnp.int32) for b in range(C.bit_length() - 1))
    lvl_top = jnp.where(fwd, level2, 0)
    lvl_bot = jnp.where(fwd, 0, level2)
    r2 = lax.broadcasted_iota(jnp.int32, (2 * C, C), 0)
    c2 = lax.broadcasted_iota(jnp.int32, (2 * C, C), 1)
    tri2 = (((r2 < C) & (r2 >= c2)) | ((r2 >= C) & (r2 - C <= c2))).astype(bf16)
    zrhs = jnp.zeros((C, 2 * LANES), bf16)
    zvn = jnp.zeros((C, LANES), bf16)
    fwd_row = fwd[0:1, :]
    cpi = GDN_SOLVE_CHUNKS

    def block_diag(x):
        return jnp.concatenate([jnp.where(fwd, x, 0.0), jnp.where(fwd, 0.0, x)], axis=0).astype(bf16)

    def solve_phase(i, carry):
        chains = []
        for cc in range(cpi):
            c = i * cpi + cc
            rows = pl.ds(pl.multiple_of(c * C, C), C)
            g2 = _tri_cumsum(tri2, g_s[rows, :])
            g2t = g2.T
            bt = b_s[rows, :]
            for h in range(H):
                chains.append((cc, c, h, rows, g2, g2t, bt))

        a2s, t2s = [], []
        for (cc, c, h, rows, g2, g2t, bt) in chains:
            ls = slice(h * LANES, (h + 1) * LANES)
            k = k_s[rows, ls]
            q = q_s[rows, ls]
            v = v_s[rows, ls]
            kq = _dot_nt(jnp.concatenate([k, q], axis=0).astype(bf16),
                         jnp.concatenate([k, k], axis=0).astype(bf16))
            gcc_f = _lane_bcast(g2[:C], h)
            gcc_b = _lane_bcast(g2[C:], H + h)
            btc_f = _lane_bcast(bt, h)
            btc_b = _lane_bcast(bt, H + h)
            grow = jnp.where(fwd_row, g2t[h:h + 1, :], g2t[H + h:H + h + 1, :])
            diff = jnp.where(fwd, gcc_f, gcc_b) - grow
            dec = jnp.where(incl2, jnp.exp(jnp.where(incl2, diff, 0.0)), 0.0)
            a2 = jnp.where(strict2, jnp.where(fwd, btc_f, btc_b) * kq[:C] * dec, 0.0)
            a2s.append(a2)
            t2s.append(eye2 - jnp.where(level2 == 1, a2, 0.0))
            glast_f = gcc_f[C - 1:C, :]
            glast_b = gcc_b[0:1, :]
            e1_f = jnp.exp(gcc_f)
            e1_b = jnp.exp(gcc_b)
            ket = jnp.concatenate([k * jnp.exp(glast_f - gcc_f), k * jnp.exp(glast_b - gcc_b)], axis=0).T
            ak_s[pl.ds(pl.multiple_of((c * H + h) * 3 * C, 3 * C), 3 * C), :] = jnp.concatenate(
                [kq[C:] * dec, ket], axis=0).astype(bf16)
            for d, (btc, e1, glast) in enumerate(((btc_f, e1_f, glast_f), (btc_b, e1_b, glast_b))):
                ch = d * H + h
                j = (cc * H + h) * 2 + d
                rhs_s[j * C:(j + 1) * C, :] = jnp.concatenate([v * btc, k * (btc * e1)], axis=1).astype(bf16)
                wq_s[pl.ds(pl.multiple_of((c * 2 * H + ch) * 2 * C + C, C), C), :] = (q * e1).astype(bf16)
                el_s[pl.ds(pl.multiple_of((c * 2 * H + ch) * 8, 8), 8), :] = jnp.broadcast_to(
                    jnp.exp(glast), (8, LANES))

        for lv in range(2, C.bit_length()):
            rs = []
            for a2, t2 in zip(a2s, t2s):
                abd = jnp.concatenate([jnp.where(lvl_top == lv, a2, 0.0),
                                       jnp.where(lvl_bot == lv, a2, 0.0)], axis=0).astype(bf16)
                rs.append(_dot(t2.astype(bf16), abd))
            t2s = [t2 - _dot(r.astype(bf16), block_diag(t2)) for r, t2 in zip(rs, t2s)]

        for (cc, c, h, rows, g2, g2t, bt), t2 in zip(chains, t2s):
            t2b = t2.astype(bf16)
            for d in range(2):
                ch = d * H + h
                j = (cc * H + h) * 2 + d
                rhs = rhs_s[j * C:(j + 1) * C, :]
                rhs = jnp.concatenate([rhs, zrhs] if d == 0 else [zrhs, rhs], axis=0)
                uw = _dot(t2b, rhs)
                u_s[pl.ds(pl.multiple_of((c * 2 * H + ch) * C, C), C), :] = uw[:, :LANES]
                wq_s[pl.ds(pl.multiple_of((c * 2 * H + ch) * 2 * C, C), C), :] = uw[:, LANES:].astype(bf16)
        return carry

    lax.fori_loop(0, nchunk // cpi, solve_phase, 0)

    def scan_phase(i, carry):
        cs = [i if ch < H else nchunk - 1 - i for ch in range(2 * H)]
        s_old = [st_s[ch] for ch in range(2 * H)]
        r1 = [_dot(wq_s[pl.ds(pl.multiple_of((cs[ch] * 2 * H + ch) * 2 * C, 2 * C), 2 * C), :],
                   s_old[ch].astype(bf16)) for ch in range(2 * H)]
        r2s = []
        for ch in range(2 * H):
            u = u_s[pl.ds(pl.multiple_of((cs[ch] * 2 * H + ch) * C, C), C), :]
            vnb = (u - r1[ch][:C]).astype(bf16)
            rhs = jnp.concatenate([vnb, zvn] if ch < H else [zvn, vnb], axis=0)
            ak = ak_s[pl.ds(pl.multiple_of((cs[ch] * H + ch % H) * 3 * C, 3 * C), 3 * C), :]
            r2s.append(_dot(ak, rhs))
        for ch in range(2 * H):
            el = el_s[pl.ds(pl.multiple_of((cs[ch] * 2 * H + ch) * 8, 8), 8), :][0:1, :]
            st_s[ch] = s_old[ch] * el + r2s[ch][C:]
            rows = pl.ds(pl.multiple_of(cs[ch] * C, C), C)
            ls = slice((ch % H) * LANES, (ch % H + 1) * LANES)
            o_ref[rows, ls] += r1[ch][C:] + r2s[ch][:C]
        return carry

    lax.fori_loop(0, nchunk, scan_phase, 0)

    onorm = onorm_ref[0]

    def fin(c, carry):
        rows = pl.ds(pl.multiple_of(c * C, C), C)
        for h in range(H):
            ls = slice(h * LANES, (h + 1) * LANES)
            o_ref[rows, ls] = _rms_rows(o_ref[rows, ls], onorm)
        return carry

    lax.fori_loop(0, nchunk, fin, 0)
    if sout_ref is not None:
        for d in range(2):
            for h in range(H):
                sout_ref[d, h] = st_s[d * H + h]


def _gdn(proj, conv_w, alog_p, dtb_p, onorm, state, l, batch, seq):
    ntok = batch * seq
    width = 2 * GDN_KW + GDN_VW
    has_state = state is not None
    nchunk = seq // GDN_CHUNK
    kern = functools.partial(_gdn_kernel, seq=seq, has_state=has_state)
    in_specs = [pl.BlockSpec((seq, width), lambda b: (b, OFF_QKV // width)),
                pl.BlockSpec((seq, LANES), lambda b: (b, OFF_SMALL // LANES)),
                pl.BlockSpec((None, CONV_W, width), lambda b: (l, 0, 0)),
                pl.BlockSpec((1, 1, LANES), lambda b: (l, 0, 0)),
                pl.BlockSpec((1, 1, LANES), lambda b: (l, 0, 0)),
                pl.BlockSpec((1, 1, GDN_DV), lambda b: (l, 0, 0))]
    args = [proj, proj, conv_w, alog_p, dtb_p, onorm]
    o_spec = pl.BlockSpec((seq, GDN_VW), lambda b: (b, 0))
    o_shape = jax.ShapeDtypeStruct((ntok, GDN_VW), f32)
    st_block = (None, None, 2, GDN_HEADS, GDN_DK, GDN_DV)
    if has_state:
        in_specs.append(pl.BlockSpec(st_block, lambda b: (b, l, 0, 0, 0, 0)))
        args.append(state)
        out_specs, out_shape = o_spec, o_shape
    else:
        out_specs = [o_spec, pl.BlockSpec(st_block[1:], lambda b: (b, 0, 0, 0, 0))]
        out_shape = [o_shape, jax.ShapeDtypeStruct((batch, 2, GDN_HEADS, GDN_DK, GDN_DV), f32)]
    return pl.pallas_call(
        kern,
        grid=(batch,),
        in_specs=in_specs,
        out_specs=out_specs,
        out_shape=out_shape,
        scratch_shapes=[pltpu.VMEM((seq + 16, width), f32),
                        pltpu.VMEM((seq, GDN_KW), f32),
                        pltpu.VMEM((seq, GDN_KW), f32),
                        pltpu.VMEM((seq, GDN_VW), f32),
                        pltpu.VMEM((seq, LANES), f32),
                        pltpu.VMEM((seq, LANES), f32),
                        pltpu.VMEM((2 * GDN_HEADS, GDN_DK, GDN_DV), f32),
                        pltpu.VMEM((nchunk * 2 * GDN_HEADS * 2 * GDN_CHUNK, LANES), bf16),
                        pltpu.VMEM((nchunk * GDN_HEADS * 3 * GDN_CHUNK, LANES), bf16),
                        pltpu.VMEM((nchunk * 2 * GDN_HEADS * GDN_CHUNK, LANES), f32),
                        pltpu.VMEM((nchunk * 2 * GDN_HEADS * 8, LANES), f32),
                        pltpu.VMEM((GDN_SOLVE_CHUNKS * 2 * GDN_HEADS * GDN_CHUNK, 2 * LANES), bf16)],
        compiler_params=_cparams(1),
        name="gdn_lat" if has_state else "gdn_ctx",
    )(*args)


def _merge_kernel(x_ref, mod_ref, oa_ref, ob_ref, za_ref, zb_ref, cu_ref, cv_ref, zc_ref, gl_ref,
                  lng_ref, lnb_ref, ws_ref, bs_ref, wbr_ref, wo_ref, out_ref, sv_s):
    tm = x_ref.shape[0]
    u = jax.nn.gelu(cu_ref[...])
    vf = jax.nn.gelu(cv_ref[...])
    mu = jnp.mean(vf, axis=-1, keepdims=True)
    vc = vf - mu
    var = jnp.mean(vc * vc, axis=-1, keepdims=True)
    vn = (vc * lax.rsqrt(var + EPS) * lng_ref[0] + lnb_ref[0]).astype(bf16)
    for ck in range(tm // CM_CHUNK):
        rs = slice(ck * CM_CHUNK, (ck + 1) * CM_CHUNK)
        for g in range(CM_GROUPS):
            ls = slice(g * LANES, (g + 1) * LANES)
            sv_s[rs, ls] = _dot(ws_ref[g], vn[rs, ls]) + bs_ref[:, ls]
    o_c = u * sv_s[...]

    def silu(z):
        return z * jax.nn.sigmoid(z)

    brs = (oa_ref[...] * silu(za_ref[...]), ob_ref[...] * silu(zb_ref[...]), o_c * silu(zc_ref[...]))
    ysum = None
    for n in range(N_BRANCH):
        yb = _dot(brs[n].astype(bf16), wbr_ref[n])
        t = jax.nn.sigmoid(gl_ref[:, n * D_MODEL:(n + 1) * D_MODEL]) * yb
        ysum = t if ysum is None else ysum + t
    y = _dot(ysum.astype(bf16), wo_ref[...])
    gate = mod_ref[0][:, 2 * D_MODEL:]
    out_ref[...] = x_ref[...] + gate * y


def _merge(x, mod_l, proj, o_a, o_b, lng, lnb, ws_b, bs_full, wbr_b, wo_b, l, latent, seq):
    ntok = x.shape[0]
    tm = 256

    def col(off):
        return pl.BlockSpec((tm, 512), lambda t: (t, off // 512))

    def const(*idx):
        return lambda t: idx

    return pl.pallas_call(
        _merge_kernel,
        grid=(ntok // tm,),
        in_specs=[pl.BlockSpec((tm, D_MODEL), lambda t: (t, 0)),
                  pl.BlockSpec((1, 1, 3 * D_MODEL), _mod_row(latent, tm, seq)),
                  pl.BlockSpec((tm, 512), lambda t: (t, 0)),
                  pl.BlockSpec((tm, 512), lambda t: (t, 0)),
                  col(OFF_ZA), col(OFF_ZB), col(OFF_CU), col(OFF_CV), col(OFF_ZC),
                  pl.BlockSpec((tm, N_BRANCH * D_MODEL), lambda t: (t, 0)),
                  pl.BlockSpec((1, 1, CM_WIDTH), const(l, 0, 0)),
                  pl.BlockSpec((1, 1, CM_WIDTH), const(l, 0, 0)),
                  pl.BlockSpec((None, CM_GROUPS, CM_CHUNK, CM_CHUNK), const(l, 0, 0, 0)),
                  pl.BlockSpec((None, CM_CHUNK, CM_WIDTH), const(l, 0, 0)),
                  pl.BlockSpec((None, N_BRANCH, BRANCH_W, D_MODEL), const(l, 0, 0, 0)),
                  pl.BlockSpec((None, D_MODEL, D_MODEL), const(l, 0, 0))],
        out_specs=pl.BlockSpec((tm, D_MODEL), lambda t: (t, 0)),
        out_shape=jax.ShapeDtypeStruct((ntok, D_MODEL), f32),
        scratch_shapes=[pltpu.VMEM((tm, CM_WIDTH), f32)],
        compiler_params=_cparams(1),
        name="merge",
    )(x, mod_l, o_a, o_b, proj, proj, proj, proj, proj, proj, lng, lnb, ws_b, bs_full, wbr_b, wo_b)


def _prep_w_in(w_in):
    offs = [0]
    for s in SPLIT_SIZES:
        offs.append(offs[-1] + s)
    cq, ckv, krope, z_a, qkv, ga, gb, z_b, cu, cv, z_c, gl = [
        w_in[..., offs[i]:offs[i + 1]] for i in range(len(SPLIT_SIZES))]

    def zeros(n):
        return jnp.zeros(w_in.shape[:-1] + (n,), w_in.dtype)

    small = jnp.concatenate([ga, gb, zeros(ROPE_LANE0 - 4 * GDN_HEADS), krope,
                             zeros(LANES - ROPE_LANE0 - MLA_ROPE)], axis=-1)
    return jnp.concatenate([gl, qkv, cq, small, z_a, z_b, cu, cv, z_c, ckv], axis=-1).astype(bf16)


def _pad_last(x, n):
    return jnp.pad(x, [(0, 0)] * (x.ndim - 1) + [(0, n - x.shape[-1])])


def _rope_tables(seq):
    t = jnp.arange(seq, dtype=jnp.int32)
    row = (t // GRID_W).astype(f32)
    colp = (t % GRID_W).astype(f32)
    nf = MLA_ROPE // 4
    inv = ROPE_THETA ** (-jnp.arange(nf, dtype=f32) / nf)
    cos_t = [jnp.ones((seq, ROPE_LANE0), f32)]
    s_lo = [jnp.zeros((seq, ROPE_LANE0), f32)]
    s_hi = [jnp.zeros((seq, ROPE_LANE0), f32)]
    zero = jnp.zeros((seq, nf), f32)
    for pos in (row, colp):
        ang = pos[:, None] * inv[None, :]
        cs, sn = jnp.cos(ang), jnp.sin(ang)
        cos_t += [cs, cs]
        s_lo += [-sn, zero]
        s_hi += [zero, sn]
    tail = LANES - ROPE_LANE0 - MLA_ROPE
    cos_t.append(jnp.ones((seq, tail), f32))
    s_lo.append(jnp.zeros((seq, tail), f32))
    s_hi.append(jnp.zeros((seq, tail), f32))
    return tuple(jnp.concatenate(p, axis=1) for p in (cos_t, s_lo, s_hi))


def kernel(x_prompt, x_sample, cache_ckv, cache_krope, state_gdn, c, c_ctx, norm_g, w_mod, b_mod, w_in, q_a_norm, w_uq, kv_a_norm, w_ukv, q_norm, k_norm, conv_w, a_log, dt_bias, gdn_onorm, cm_ln_g, cm_ln_b, w_s, b_s, w_branch, w_o):
    L = DEPTH
    batch, seq, _ = x_prompt.shape
    dbatch, dseq, _ = x_sample.shape
    past = cache_ckv.shape[2]

    w_in_p = _prep_w_in(w_in)
    wuq_p = _pad_last(w_uq.reshape(L, MLA_Q_RANK, MLA_HEADS, MLA_QK), HEAD_SLOT)
    wuq_p = wuq_p.reshape(L, MLA_Q_RANK, MLA_HEADS * HEAD_SLOT).astype(bf16)
    wukv = w_ukv.reshape(L, MLA_KV_RANK, MLA_HEADS, MLA_NOPE + MLA_V)
    wk_p = _pad_last(wukv[..., :MLA_NOPE], HEAD_SLOT).reshape(L, MLA_KV_RANK, MLA_HEADS * HEAD_SLOT).astype(bf16)
    wv_p = wukv[..., MLA_NOPE:].reshape(L, MLA_KV_RANK, MLA_WIDTH).astype(bf16)
    qn_p = _pad_last(q_norm, HEAD_SLOT).reshape(L, 1, HEAD_SLOT)
    kn_p = _pad_last(k_norm, HEAD_SLOT).reshape(L, 1, HEAD_SLOT)
    attn_w = (q_a_norm.reshape(L, 1, MLA_Q_RANK), wuq_p, kv_a_norm.reshape(L, 1, MLA_KV_RANK),
              wk_p, wv_p, qn_p, kn_p)
    cache_kr_p = jnp.pad(cache_krope, [(0, 0)] * 3 + [(ROPE_LANE0, LANES - ROPE_LANE0 - MLA_ROPE)])
    rope_tabs = _rope_tables(dseq)
    alog_p = _pad_last(a_log.reshape(L, 1, 2 * GDN_HEADS), LANES)
    dtb_p = _pad_last(dt_bias.reshape(L, 1, 2 * GDN_HEADS), LANES)
    onorm = gdn_onorm.reshape(L, 1, GDN_DV)
    lng = cm_ln_g.reshape(L, 1, CM_WIDTH)
    lnb = cm_ln_b.reshape(L, 1, CM_WIDTH)
    ws_b = w_s.astype(bf16)
    bs_full = jnp.repeat(jnp.swapaxes(b_s, 1, 2), CM_WIDTH // CM_GROUPS, axis=2)
    wbr_b = w_branch.astype(bf16)
    wo_b = w_o.astype(bf16)
    norm_g3 = norm_g.reshape(L, 1, D_MODEL)

    c8 = jnp.concatenate([c, c_ctx[None, :], jnp.zeros((8 - dbatch - 1, D_MODEL), f32)], axis=0)
    mod = _modulation(c8, w_mod, b_mod)

    yp = x_prompt.reshape(batch * seq, D_MODEL)
    ys = x_sample.reshape(dbatch * dseq, D_MODEL)
    ckvs, kropes, states = [], [], []
    for l in range(L):
        mod_l = mod[l].reshape(8, 1, 3 * D_MODEL)
        proj = _inproj(yp, mod_l, norm_g3, w_in_p, l, False, seq)
        o_a, ckvn, kr = _attn_ctx(proj, attn_w, l, batch, seq)
        o_b, s_new = _gdn(proj, conv_w, alog_p, dtb_p, onorm, None, l, batch, seq)
        yp = _merge(yp, mod_l, proj, o_a, o_b, lng, lnb, ws_b, bs_full, wbr_b, wo_b, l, False, seq)
        ckvs.append(ckvn.reshape(batch, seq, MLA_KV_RANK))
        kropes.append(kr.reshape(batch, seq, MLA_ROPE))
        states.append(s_new)
        proj = _inproj(ys, mod_l, norm_g3, w_in_p, l, True, dseq)
        o_a = _attn_lat(proj, cache_ckv, cache_kr_p, rope_tabs, attn_w, l, dbatch, dseq, past)
        o_b = _gdn(proj, conv_w, alog_p, dtb_p, onorm, state_gdn, l, dbatch, dseq)
        ys = _merge(ys, mod_l, proj, o_a, o_b, lng, lnb, ws_b, bs_full, wbr_b, wo_b, l, True, dseq)
    return (yp.reshape(batch, seq, D_MODEL), ys.reshape(dbatch, dseq, D_MODEL),
            jnp.stack(ckvs, axis=1), jnp.stack(kropes, axis=1), jnp.stack(states, axis=1))
```

```python
import functools
import math

import numpy as np

import jax
import jax.numpy as jnp
from jax import lax
from jax.experimental import pallas as pl
from jax.experimental.pallas import tpu as pltpu

D_MODEL = 1024
DEPTH = 2
GRID_W = 64
EPS = 1e-6
MLA_HEADS = 8
MLA_NOPE = 64
MLA_ROPE = 32
MLA_QK = MLA_NOPE + MLA_ROPE
MLA_V = 64
MLA_Q_RANK = 384
MLA_KV_RANK = 256
MLA_WIDTH = MLA_HEADS * MLA_V
ROPE_THETA = 10000.0
GDN_HEADS = 4
GDN_DK = 128
GDN_DV = 128
GDN_KW = GDN_HEADS * GDN_DK
GDN_VW = GDN_HEADS * GDN_DV
GDN_CHUNK = 64
CONV_W = 5
GDN_SOLVE_CHUNKS = 4
CM_GROUPS = 4
CM_CHUNK = 128
CM_WIDTH = 512
N_BRANCH = 3
BRANCH_W = 512
SPLIT_SIZES = (MLA_Q_RANK, MLA_KV_RANK, MLA_ROPE, MLA_WIDTH,
               2 * GDN_KW + GDN_VW, 2 * GDN_HEADS, 2 * GDN_HEADS, GDN_VW,
               CM_WIDTH, CM_WIDTH, CM_WIDTH, N_BRANCH * D_MODEL)

LANES = 128
HEAD_SLOT = LANES
ROPE_LANE0 = MLA_NOPE

OFF_GL = 0
OFF_QKV = OFF_GL + N_BRANCH * D_MODEL
OFF_CQS = OFF_QKV + 2 * GDN_KW + GDN_VW
OFF_SMALL = OFF_CQS + MLA_Q_RANK
OFF_ZA = OFF_CQS + 512
OFF_ZB = OFF_ZA + 512
OFF_CU = OFF_ZB + 512
OFF_CV = OFF_CU + 512
OFF_ZC = OFF_CV + 512
OFF_CKV = OFF_ZC + 512
PROJ_W = OFF_CKV + MLA_KV_RANK

VMEM_LIMIT = 56 * 1024 * 1024

f32 = jnp.float32
bf16 = jnp.bfloat16


def _cparams(n_axes):
    return pltpu.CompilerParams(dimension_semantics=("arbitrary",) * n_axes,
                                vmem_limit_bytes=VMEM_LIMIT)


def _dot(a, b):
    return jnp.dot(a, b, preferred_element_type=f32)


def _dot_nt(a, b):
    return lax.dot_general(a, b, (((1,), (1,)), ((), ())), preferred_element_type=f32)


def _dot_tn(a, b):
    return lax.dot_general(a, b, (((0,), (0,)), ((), ())), preferred_element_type=f32)


def _rms_rows(x, g, n=None):
    n = x.shape[-1] if n is None else n
    ms = jnp.sum(x * x, axis=-1, keepdims=True) * (1.0 / n)
    return x * lax.rsqrt(ms + EPS) * g


def _mod_kernel(c_ref, w_ref, b_ref, o_ref):
    a = c_ref[...]
    a = (a * jax.nn.sigmoid(a)).astype(bf16)
    o_ref[0] = _dot(a, w_ref[0].astype(bf16)) + b_ref[0]


def _modulation(c8, w_mod, b_mod):
    tn = 512
    return pl.pallas_call(
        _mod_kernel,
        grid=(DEPTH, 3 * D_MODEL // tn),
        in_specs=[pl.BlockSpec((8, D_MODEL), lambda l, n: (0, 0)),
                  pl.BlockSpec((1, D_MODEL, tn), lambda l, n: (l, 0, n)),
                  pl.BlockSpec((1, 1, tn), lambda l, n: (l, 0, n))],
        out_specs=pl.BlockSpec((1, 8, tn), lambda l, n: (l, 0, n)),
        out_shape=jax.ShapeDtypeStruct((DEPTH, 8, 3 * D_MODEL), f32),
        compiler_params=_cparams(2),
        name="modulation",
    )(c8, w_mod, b_mod.reshape(DEPTH, 1, 3 * D_MODEL))


def _inproj_kernel(x_ref, mod_ref, g_ref, w_ref, o_ref):
    x = x_ref[...]
    m = mod_ref[0]
    shift = m[:, :D_MODEL]
    scale = m[:, D_MODEL:2 * D_MODEL]
    h = (_rms_rows(x, g_ref[0]) * (1.0 + scale) + shift).astype(bf16)
    for a in range(0, PROJ_W, 512):
        b = min(a + 512, PROJ_W)
        o_ref[:, a:b] = _dot(h, w_ref[:, a:b])


def _mod_row(latent, tm, seq):
    if latent:
        return lambda t: ((t * tm) // seq, 0, 0)
    return lambda t: (4, 0, 0)


def _inproj(x, mod_l, norm_g, w_in_p, l, latent, seq):
    ntok = x.shape[0]
    tm = 256
    return pl.pallas_call(
        _inproj_kernel,
        grid=(ntok // tm,),
        in_specs=[pl.BlockSpec((tm, D_MODEL), lambda t: (t, 0)),
                  pl.BlockSpec((1, 1, 3 * D_MODEL), _mod_row(latent, tm, seq)),
                  pl.BlockSpec((1, 1, D_MODEL), lambda t: (l, 0, 0)),
                  pl.BlockSpec((None, D_MODEL, PROJ_W), lambda t: (l, 0, 0),
                               pipeline_mode=pl.Buffered(1))],
        out_specs=pl.BlockSpec((tm, PROJ_W), lambda t: (t, 0)),
        out_shape=jax.ShapeDtypeStruct((ntok, PROJ_W), f32),
        compiler_params=_cparams(1),
        name="inproj",
    )(x, mod_l, norm_g, w_in_p)


def _rope(x, cos_t, sin_lo, sin_hi):
    return x * cos_t + pltpu.roll(x, LANES - 8, 1) * sin_lo + pltpu.roll(x, 8, 1) * sin_hi


def _build_kv(ckvn_b, kr, wk_ref, wv_ref, knorm, rope, k_s, v_s, r0):
    n = ckvn_b.shape[0]
    kfull = _dot(ckvn_b, wk_ref[...])
    v_s[r0:r0 + n, :] = _dot(ckvn_b, wv_ref[...]).astype(bf16)
    krg = kr * knorm
    if rope is not None:
        krg = _rope(krg, *rope)
    kr_ss = jnp.sum(kr * kr, axis=-1, keepdims=True)
    for h in range(MLA_HEADS):
        sl = slice(h * HEAD_SLOT, (h + 1) * HEAD_SLOT)
        kn = kfull[:, sl]
        ms = (jnp.sum(kn * kn, axis=-1, keepdims=True) + kr_ss) * (1.0 / MLA_QK)
        k_s[r0:r0 + n, sl] = ((kn * knorm + krg) * lax.rsqrt(ms + EPS)).astype(bf16)


def _rope_lane_mask(shape):
    lane = lax.broadcasted_iota(jnp.int32, shape, 1)
    return (lane >= ROPE_LANE0) & (lane < ROPE_LANE0 + MLA_ROPE)


def _attend_block(qa_b, wuq_ref, qnorm, rope, k_s, v_s, o_ref):
    tq = qa_b.shape[0]
    qfull = _dot(qa_b, wuq_ref[...])
    qgain = qnorm * (math.log2(math.e) / math.sqrt(MLA_QK))
    lane = lax.broadcasted_iota(jnp.int32, (tq, LANES), 1)

    def scores(h):
        sl = slice(h * HEAD_SLOT, (h + 1) * HEAD_SLOT)
        qh = _rms_rows(qfull[:, sl], qgain, n=MLA_QK)
        if rope is not None:
            qh = _rope(qh, *rope)
        return _dot_nt(qh.astype(bf16), k_s[:, sl])

    s_next = scores(0)
    outs = []
    for h in range(MLA_HEADS):
        s = s_next
        if h + 1 < MLA_HEADS:
            s_next = scores(h + 1)
        p = jnp.exp2(s - jnp.max(s, axis=-1, keepdims=True))
        den = jnp.sum(p, axis=-1, keepdims=True)
        hp = h // 2
        outs.append(_dot(p.astype(bf16), v_s[:, hp * LANES:(hp + 1) * LANES]) / den)
        if h % 2 == 1:
            o_ref[:, hp * LANES:(hp + 1) * LANES] = jnp.where(lane < MLA_V, outs[h - 1], outs[h])


def _attn_ctx_kernel(cqs_ref, ckv_ref, qan_ref, wuq_ref, kvn_ref, wk_ref, wv_ref, qn_ref, kn_ref,
                     o_ref, ckvn_ref, kr_ref, k_s, v_s):
    small = cqs_ref[:, MLA_Q_RANK:]
    kr = jnp.where(_rope_lane_mask(small.shape), small, 0.0)
    kr_ref[...] = small[:, ROPE_LANE0:ROPE_LANE0 + MLA_ROPE]
    ckvn = _rms_rows(ckv_ref[...], kvn_ref[0])
    ckvn_ref[...] = ckvn
    _build_kv(ckvn.astype(bf16), kr, wk_ref, wv_ref, kn_ref[0], None, k_s, v_s, 0)
    qa = _rms_rows(cqs_ref[:, :MLA_Q_RANK], qan_ref[0]).astype(bf16)
    _attend_block(qa, wuq_ref, qn_ref[0], None, k_s, v_s, o_ref)


def _attn_lat_kernel(cqs_ref, ckv_ref, cckv_ref, ckr_ref, cos_ref, slo_ref, shi_ref,
                     qan_ref, wuq_ref, kvn_ref, wk_ref, wv_ref, qn_ref, kn_ref,
                     o_ref, k_s, v_s, *, seq, past, tq):
    qi = pl.program_id(1)
    rb = 256

    @pl.when(qi == 0)
    def _():
        _build_kv(cckv_ref[...].astype(bf16), ckr_ref[...], wk_ref, wv_ref, kn_ref[0], None, k_s, v_s, 0)
        for r in range(seq // rb):
            rs = slice(r * rb, (r + 1) * rb)
            small = cqs_ref[rs, MLA_Q_RANK:]
            kr = jnp.where(_rope_lane_mask(small.shape), small, 0.0)
            ckvn = _rms_rows(ckv_ref[rs, :], kvn_ref[0])
            rope = (cos_ref[rs, :], slo_ref[rs, :], shi_ref[rs, :])
            _build_kv(ckvn.astype(bf16), kr, wk_ref, wv_ref, kn_ref[0], rope, k_s, v_s, past + r * rb)

    rows = pl.ds(pl.multiple_of(qi * tq, tq), tq)
    qa = _rms_rows(cqs_ref[rows, :MLA_Q_RANK], qan_ref[0]).astype(bf16)
    rope = (cos_ref[rows, :], slo_ref[rows, :], shi_ref[rows, :])
    _attend_block(qa, wuq_ref, qn_ref[0], rope, k_s, v_s, o_ref)


def _attn_weight_specs(l, nidx):
    z = (0,) * (nidx - 1)

    def const(*idx):
        return lambda *g: idx

    return [pl.BlockSpec((1, 1, MLA_Q_RANK), const(l, 0, 0)),
            pl.BlockSpec((None, MLA_Q_RANK, MLA_HEADS * HEAD_SLOT), const(l, 0, 0)),
            pl.BlockSpec((1, 1, MLA_KV_RANK), const(l, 0, 0)),
            pl.BlockSpec((None, MLA_KV_RANK, MLA_HEADS * HEAD_SLOT), const(l, 0, 0)),
            pl.BlockSpec((None, MLA_KV_RANK, MLA_WIDTH), const(l, 0, 0)),
            pl.BlockSpec((1, 1, HEAD_SLOT), const(l, 0, 0)),
            pl.BlockSpec((1, 1, HEAD_SLOT), const(l, 0, 0))]


def _attn_ctx(proj, wts, l, batch, seq):
    ntok = batch * seq
    return pl.pallas_call(
        _attn_ctx_kernel,
        grid=(batch,),
        in_specs=[pl.BlockSpec((seq, 512), lambda b: (b, OFF_CQS // 512)),
                  pl.BlockSpec((seq, MLA_KV_RANK), lambda b: (b, OFF_CKV // MLA_KV_RANK))]
        + _attn_weight_specs(l, 1),
        out_specs=[pl.BlockSpec((seq, MLA_WIDTH), lambda b: (b, 0)),
                   pl.BlockSpec((seq, MLA_KV_RANK), lambda b: (b, 0)),
                   pl.BlockSpec((seq, MLA_ROPE), lambda b: (b, 0))],
        out_shape=[jax.ShapeDtypeStruct((ntok, MLA_WIDTH), f32),
                   jax.ShapeDtypeStruct((ntok, MLA_KV_RANK), f32),
                   jax.ShapeDtypeStruct((ntok, MLA_ROPE), f32)],
        scratch_shapes=[pltpu.VMEM((seq, MLA_HEADS * HEAD_SLOT), bf16),
                        pltpu.VMEM((seq, MLA_WIDTH), bf16)],
        compiler_params=_cparams(1),
        name="attn_ctx",
    )(proj, proj, *wts)


def _attn_lat(proj, cache_ckv, cache_kr_p, rope_tabs, wts, l, batch, seq, past):
    ntok = batch * seq
    tq = 256
    nq = seq // tq
    kern = functools.partial(_attn_lat_kernel, seq=seq, past=past, tq=tq)
    tab = pl.BlockSpec((seq, LANES), lambda b, q: (0, 0))
    return pl.pallas_call(
        kern,
        grid=(batch, nq),
        in_specs=[pl.BlockSpec((seq, 512), lambda b, q: (b, OFF_CQS // 512)),
                  pl.BlockSpec((seq, MLA_KV_RANK), lambda b, q: (b, OFF_CKV // MLA_KV_RANK)),
                  pl.BlockSpec((None, None, past, MLA_KV_RANK), lambda b, q: (b, l, 0, 0)),
                  pl.BlockSpec((None, None, past, LANES), lambda b, q: (b, l, 0, 0)),
                  tab, tab, tab]
        + _attn_weight_specs(l, 2),
        out_specs=pl.BlockSpec((tq, MLA_WIDTH), lambda b, q: (b * nq + q, 0)),
        out_shape=jax.ShapeDtypeStruct((ntok, MLA_WIDTH), f32),
        scratch_shapes=[pltpu.VMEM((past + seq, MLA_HEADS * HEAD_SLOT), bf16),
                        pltpu.VMEM((past + seq, MLA_WIDTH), bf16)],
        compiler_params=_cparams(2),
        name="attn_lat",
    )(proj, proj, cache_ckv, cache_kr_p, *rope_tabs, *wts)


def _split3(x):
    hi = x.astype(bf16)
    r1 = x - hi.astype(f32)
    mid = r1.astype(bf16)
    lo = (r1 - mid.astype(f32)).astype(bf16)
    return hi, mid, lo


def _tri_cumsum(tri_b, x):
    hi, mid, lo = _split3(x)
    return _dot(tri_b, hi) + _dot(tri_b, mid) + _dot(tri_b, lo)


def _lane_bcast(x, c):
    return jnp.broadcast_to(x[:, c:c + 1], (x.shape[0], LANES))


def _gdn_kernel(*refs, seq, has_state):
    if has_state:
        (qkv_ref, small_ref, cw_ref, alog_ref, dtb_ref, onorm_ref, s0_ref,
         o_ref, xpad, qkv_s, g_s, b_s, st_s, wq_s, ak_s, u_s, el_s, rhs_s) = refs
        sout_ref = None
    else:
        (qkv_ref, small_ref, cw_ref, alog_ref, dtb_ref, onorm_ref,
         o_ref, sout_ref, xpad, qkv_s, g_s, b_s, st_s, wq_s, ak_s, u_s, el_s, rhs_s) = refs
    C = GDN_CHUNK
    nchunk = seq // C
    H = GDN_HEADS
    width = 2 * GDN_KW + GDN_VW
    halo = 8

    for j in range(width // LANES):
        xpad[j, 0:halo, :] = jnp.zeros((halo, LANES), f32)
        xpad[j, halo + seq:, :] = jnp.zeros((halo, LANES), f32)
        xpad[j, halo:halo + seq, :] = qkv_ref[:, j * LANES:(j + 1) * LANES]
    o_ref[...] = jnp.zeros((seq, GDN_VW), f32)
    if has_state:
        for d in range(2):
            for h in range(H):
                st_s[d * H + h] = s0_ref[d, h]
    else:
        st_s[...] = jnp.zeros((2 * H, GDN_DK, GDN_DV), f32)

    neg_a = -jnp.exp(alog_ref[0])
    dtb = dtb_ref[0]

    def conv_tile(j, l2norm):
        w = cw_ref[j]
        post = jnp.where(j < H, GDN_DK ** -0.5, 1.0)
        for c in range(nchunk):
            base = halo - CONV_W // 2 + c * C
            y = xpad[j, base:base + C, :] * w[0:1]
            for tap in range(1, CONV_W):
                y = y + xpad[j, base + tap:base + tap + C, :] * w[tap:tap + 1]
            y = y * jax.nn.sigmoid(y)
            if l2norm:
                y = y * (lax.rsqrt(jnp.sum(y * y, axis=-1, keepdims=True) + EPS) * post)
            qkv_s[j, c * C:(c + 1) * C, :] = y

    def conv_qk(j, carry):
        conv_tile(j, True)
        return carry

    def conv_v(j, carry):
        conv_tile(j, False)
        return carry

    tile_unroll = 2 if nchunk <= 4 else 1
    lax.fori_loop(0, 2 * H, conv_qk, 0, unroll=tile_unroll)
    lax.fori_loop(2 * H, 3 * H, conv_v, 0, unroll=tile_unroll)

    def prep(c, carry):
        r0 = pl.multiple_of(c * C, C)
        sm = small_ref[pl.ds(r0, C), :]
        z = sm + dtb
        g_s[pl.ds(r0, C), :] = neg_a * (jnp.maximum(z, 0.0) + jnp.log1p(jnp.exp(-jnp.abs(z))))
        b_s[pl.ds(r0, C), :] = pltpu.roll(jax.nn.sigmoid(sm), LANES - 2 * H, 1)
        return carry

    lax.fori_loop(0, nchunk, prep, 0, unroll=4)

    ri = lax.broadcasted_iota(jnp.int32, (C, LANES), 0)
    cl = lax.broadcasted_iota(jnp.int32, (C, LANES), 1)
    fwd = cl < C
    cj = cl & (C - 1)
    eye2 = (ri == cj).astype(f32)
    incl2 = (fwd & (ri >= cj)) | (~fwd & (ri <= cj))
    strict2 = (fwd & (ri > cj)) | (~fwd & (ri < cj))
    xor = ri ^ cj
    level2 = sum((xor >= (1 << b)).astype(jnp.int32) for b in range(C.bit_length() - 1))
    lvl_top = jnp.where(fwd, level2, 0)
    lvl_bot = jnp.where(fwd, 0, level2)
    r2 = lax.broadcasted_iota(jnp.int32, (2 * C, C), 0)
    c2 = lax.broadcasted_iota(jnp.int32, (2 * C, C), 1)
    tri2 = (((r2 < C) & (r2 >= c2)) | ((r2 >= C) & (r2 - C <= c2))).astype(bf16)
    zrhs = jnp.zeros((C, 2 * LANES), bf16)
    zvn = jnp.zeros((C, LANES), bf16)
    fwd_row = fwd[0:1, :]
    cpi = GDN_SOLVE_CHUNKS

    def block_diag(x):
        return jnp.concatenate([jnp.where(fwd, x, 0.0), jnp.where(fwd, 0.0, x)], axis=0).astype(bf16)

    def solve_phase(i, carry):
        chains = []
        for cc in range(cpi):
            c = i * cpi + cc
            rows = pl.ds(pl.multiple_of(c * C, C), C)
            g2 = _tri_cumsum(tri2, g_s[rows, :])
            g2t = g2.T
            bt = b_s[rows, :]
            for h in range(H):
                chains.append((cc, c, h, rows, g2, g2t, bt))

        a2s, t2s = [], []
        for (cc, c, h, rows, g2, g2t, bt) in chains:
            q = qkv_s[h, rows, :]
            k = qkv_s[H + h, rows, :]
            v = qkv_s[2 * H + h, rows, :]
            kq = _dot_nt(jnp.concatenate([k, q], axis=0).astype(bf16),
                         jnp.concatenate([k, k], axis=0).astype(bf16))
            gcc_f = _lane_bcast(g2[:C], h)
            gcc_b = _lane_bcast(g2[C:], H + h)
            btc_f = _lane_bcast(bt, h)
            btc_b = _lane_bcast(bt, H + h)
            grow = jnp.where(fwd_row, g2t[h:h + 1, :], g2t[H + h:H + h + 1, :])
            diff = jnp.where(fwd, gcc_f, gcc_b) - grow
            dec = jnp.where(incl2, jnp.exp(jnp.where(incl2, diff, 0.0)), 0.0)
            a2 = jnp.where(strict2, jnp.where(fwd, btc_f, btc_b) * kq[:C] * dec, 0.0)
            a2s.append(a2)
            t2s.append(eye2 - jnp.where(level2 == 1, a2, 0.0))
            glast_f = gcc_f[C - 1:C, :]
            glast_b = gcc_b[0:1, :]
            e1_f = jnp.exp(gcc_f)
            e1_b = jnp.exp(gcc_b)
            ket = jnp.concatenate([k * jnp.exp(glast_f - gcc_f), k * jnp.exp(glast_b - gcc_b)], axis=0).T
            ak_s[pl.ds(pl.multiple_of((c * H + h) * 3 * C, 3 * C), 3 * C), :] = jnp.concatenate(
                [kq[C:] * dec, ket], axis=0).astype(bf16)
            for d, (btc, e1, glast) in enumerate(((btc_f, e1_f, glast_f), (btc_b, e1_b, glast_b))):
                ch = d * H + h
                j = (cc * H + h) * 2 + d
                rhs_s[j * C:(j + 1) * C, :] = jnp.concatenate([v * btc, k * (btc * e1)], axis=1).astype(bf16)
                wq_s[pl.ds(pl.multiple_of((c * 2 * H + ch) * 2 * C + C, C), C), :] = (q * e1).astype(bf16)
                el_s[pl.ds(pl.multiple_of((c * 2 * H + ch) * 8, 8), 8), :] = jnp.broadcast_to(
                    jnp.exp(glast), (8, LANES))

        for lv in range(2, C.bit_length()):
            rs = []
            for a2, t2 in zip(a2s, t2s):
                abd = jnp.concatenate([jnp.where(lvl_top == lv, a2, 0.0),
                                       jnp.where(lvl_bot == lv, a2, 0.0)], axis=0).astype(bf16)
                rs.append(_dot(t2.astype(bf16), abd))
            t2s = [t2 - _dot(r.astype(bf16), block_diag(t2)) for r, t2 in zip(rs, t2s)]

        for (cc, c, h, rows, g2, g2t, bt), t2 in zip(chains, t2s):
            t2b = t2.astype(bf16)
            for d in range(2):
                ch = d * H + h
                j = (cc * H + h) * 2 + d
                rhs = rhs_s[j * C:(j + 1) * C, :]
                rhs = jnp.concatenate([rhs, zrhs] if d == 0 else [zrhs, rhs], axis=0)
                uw = _dot(t2b, rhs)
                u_s[pl.ds(pl.multiple_of((c * 2 * H + ch) * C, C), C), :] = uw[:, :LANES]
                wq_s[pl.ds(pl.multiple_of((c * 2 * H + ch) * 2 * C, C), C), :] = uw[:, LANES:].astype(bf16)
        return carry

    lax.fori_loop(0, nchunk // cpi, solve_phase, 0)

    def scan_phase(i, carry):
        cs = [i if ch < H else nchunk - 1 - i for ch in range(2 * H)]
        s_old = [st_s[ch] for ch in range(2 * H)]
        r1 = [_dot(wq_s[pl.ds(pl.multiple_of((cs[ch] * 2 * H + ch) * 2 * C, 2 * C), 2 * C), :],
                   s_old[ch].astype(bf16)) for ch in range(2 * H)]
        r2s = []
        for ch in range(2 * H):
            u = u_s[pl.ds(pl.multiple_of((cs[ch] * 2 * H + ch) * C, C), C), :]
            vnb = (u - r1[ch][:C]).astype(bf16)
            rhs = jnp.concatenate([vnb, zvn] if ch < H else [zvn, vnb], axis=0)
            ak = ak_s[pl.ds(pl.multiple_of((cs[ch] * H + ch % H) * 3 * C, 3 * C), 3 * C), :]
            r2s.append(_dot(ak, rhs))
        for ch in range(2 * H):
            el = el_s[pl.ds(pl.multiple_of((cs[ch] * 2 * H + ch) * 8, 8), 8), :][0:1, :]
            st_s[ch] = s_old[ch] * el + r2s[ch][C:]
            rows = pl.ds(pl.multiple_of(cs[ch] * C, C), C)
            ls = slice((ch % H) * LANES, (ch % H + 1) * LANES)
            o_ref[rows, ls] += r1[ch][C:] + r2s[ch][:C]
        return carry

    lax.fori_loop(0, nchunk, scan_phase, 0)

    onorm = onorm_ref[0]

    def fin(c, carry):
        rows = pl.ds(pl.multiple_of(c * C, C), C)
        for h in range(H):
            ls = slice(h * LANES, (h + 1) * LANES)
            o_ref[rows, ls] = _rms_rows(o_ref[rows, ls], onorm)
        return carry

    lax.fori_loop(0, nchunk, fin, 0, unroll=4)
    if sout_ref is not None:
        for d in range(2):
            for h in range(H):
                sout_ref[d, h] = st_s[d * H + h]


def _gdn(proj, conv_w, alog_p, dtb_p, onorm, state, l, batch, seq):
    ntok = batch * seq
    width = 2 * GDN_KW + GDN_VW
    has_state = state is not None
    nchunk = seq // GDN_CHUNK
    kern = functools.partial(_gdn_kernel, seq=seq, has_state=has_state)
    in_specs = [pl.BlockSpec((seq, width), lambda b: (b, OFF_QKV // width)),
                pl.BlockSpec((seq, LANES), lambda b: (b, OFF_SMALL // LANES)),
                pl.BlockSpec((None, width // LANES, CONV_W, LANES), lambda b: (l, 0, 0, 0)),
                pl.BlockSpec((1, 1, LANES), lambda b: (l, 0, 0)),
                pl.BlockSpec((1, 1, LANES), lambda b: (l, 0, 0)),
                pl.BlockSpec((1, 1, GDN_DV), lambda b: (l, 0, 0))]
    args = [proj, proj, conv_w, alog_p, dtb_p, onorm]
    o_spec = pl.BlockSpec((seq, GDN_VW), lambda b: (b, 0))
    o_shape = jax.ShapeDtypeStruct((ntok, GDN_VW), f32)
    st_block = (None, None, 2, GDN_HEADS, GDN_DK, GDN_DV)
    if has_state:
        in_specs.append(pl.BlockSpec(st_block, lambda b: (b, l, 0, 0, 0, 0)))
        args.append(state)
        out_specs, out_shape = o_spec, o_shape
    else:
        out_specs = [o_spec, pl.BlockSpec(st_block[1:], lambda b: (b, 0, 0, 0, 0))]
        out_shape = [o_shape, jax.ShapeDtypeStruct((batch, 2, GDN_HEADS, GDN_DK, GDN_DV), f32)]
    return pl.pallas_call(
        kern,
        grid=(batch,),
        in_specs=in_specs,
        out_specs=out_specs,
        out_shape=out_shape,
        scratch_shapes=[pltpu.VMEM((width // LANES, seq + 16, LANES), f32),
                        pltpu.VMEM((width // LANES, seq, LANES), f32),
                        pltpu.VMEM((seq, LANES), f32),
                        pltpu.VMEM((seq, LANES), f32),
                        pltpu.VMEM((2 * GDN_HEADS, GDN_DK, GDN_DV), f32),
                        pltpu.VMEM((nchunk * 2 * GDN_HEADS * 2 * GDN_CHUNK, LANES), bf16),
                        pltpu.VMEM((nchunk * GDN_HEADS * 3 * GDN_CHUNK, LANES), bf16),
                        pltpu.VMEM((nchunk * 2 * GDN_HEADS * GDN_CHUNK, LANES), f32),
                        pltpu.VMEM((nchunk * 2 * GDN_HEADS * 8, LANES), f32),
                        pltpu.VMEM((GDN_SOLVE_CHUNKS * 2 * GDN_HEADS * GDN_CHUNK, 2 * LANES), bf16)],
        compiler_params=_cparams(1),
        name="gdn_lat" if has_state else "gdn_ctx",
    )(*args)


def _merge_kernel(x_ref, mod_ref, oa_ref, ob_ref, za_ref, zb_ref, cu_ref, cv_ref, zc_ref, gl_ref,
                  lng_ref, lnb_ref, ws_ref, bs_ref, wbr_ref, wo_ref, out_ref, sv_s):
    tm = x_ref.shape[0]
    u = jax.nn.gelu(cu_ref[...])
    vf = jax.nn.gelu(cv_ref[...])
    mu = jnp.mean(vf, axis=-1, keepdims=True)
    vc = vf - mu
    var = jnp.mean(vc * vc, axis=-1, keepdims=True)
    vn = (vc * lax.rsqrt(var + EPS) * lng_ref[0] + lnb_ref[0]).astype(bf16)
    for ck in range(tm // CM_CHUNK):
        rs = slice(ck * CM_CHUNK, (ck + 1) * CM_CHUNK)
        for g in range(CM_GROUPS):
            ls = slice(g * LANES, (g + 1) * LANES)
            sv_s[rs, ls] = _dot(ws_ref[g], vn[rs, ls]) + bs_ref[:, ls]
    o_c = u * sv_s[...]

    def silu(z):
        return z * jax.nn.sigmoid(z)

    brs = (oa_ref[...] * silu(za_ref[...]), ob_ref[...] * silu(zb_ref[...]), o_c * silu(zc_ref[...]))
    ysum = None
    for n in range(N_BRANCH):
        yb = _dot(brs[n].astype(bf16), wbr_ref[n])
        t = jax.nn.sigmoid(gl_ref[:, n * D_MODEL:(n + 1) * D_MODEL]) * yb
        ysum = t if ysum is None else ysum + t
    y = _dot(ysum.astype(bf16), wo_ref[...])
    gate = mod_ref[0][:, 2 * D_MODEL:]
    out_ref[...] = x_ref[...] + gate * y


def _merge(x, mod_l, proj, o_a, o_b, lng, lnb, ws_b, bs_full, wbr_b, wo_b, l, latent, seq):
    ntok = x.shape[0]
    tm = 256

    def col(off):
        return pl.BlockSpec((tm, 512), lambda t: (t, off // 512))

    def const(*idx):
        return lambda t: idx

    return pl.pallas_call(
        _merge_kernel,
        grid=(ntok // tm,),
        in_specs=[pl.BlockSpec((tm, D_MODEL), lambda t: (t, 0)),
                  pl.BlockSpec((1, 1, 3 * D_MODEL), _mod_row(latent, tm, seq)),
                  pl.BlockSpec((tm, 512), lambda t: (t, 0)),
                  pl.BlockSpec((tm, 512), lambda t: (t, 0)),
                  col(OFF_ZA), col(OFF_ZB), col(OFF_CU), col(OFF_CV), col(OFF_ZC),
                  pl.BlockSpec((tm, N_BRANCH * D_MODEL), lambda t: (t, 0)),
                  pl.BlockSpec((1, 1, CM_WIDTH), const(l, 0, 0)),
                  pl.BlockSpec((1, 1, CM_WIDTH), const(l, 0, 0)),
                  pl.BlockSpec((None, CM_GROUPS, CM_CHUNK, CM_CHUNK), const(l, 0, 0, 0)),
                  pl.BlockSpec((None, CM_CHUNK, CM_WIDTH), const(l, 0, 0)),
                  pl.BlockSpec((None, N_BRANCH, BRANCH_W, D_MODEL), const(l, 0, 0, 0)),
                  pl.BlockSpec((None, D_MODEL, D_MODEL), const(l, 0, 0))],
        out_specs=pl.BlockSpec((tm, D_MODEL), lambda t: (t, 0)),
        out_shape=jax.ShapeDtypeStruct((ntok, D_MODEL), f32),
        scratch_shapes=[pltpu.VMEM((tm, CM_WIDTH), f32)],
        compiler_params=_cparams(1),
        name="merge",
    )(x, mod_l, o_a, o_b, proj, proj, proj, proj, proj, proj, lng, lnb, ws_b, bs_full, wbr_b, wo_b)


def _w_in_moves():
    offs = [0]
    for s in SPLIT_SIZES:
        offs.append(offs[-1] + s)
    cq, ckv, krope, z_a, qkv, ga, gb, z_b, cu, cv, z_c, gl = offs[:-1]
    moves = [(gl, N_BRANCH * D_MODEL, OFF_GL), (qkv, 2 * GDN_KW + GDN_VW, OFF_QKV), (cq, MLA_Q_RANK, OFF_CQS),
             (z_a, 512, OFF_ZA), (z_b, 512, OFF_ZB), (cu, 512, OFF_CU), (cv, 512, OFF_CV), (z_c, 512, OFF_ZC),
             (ckv, MLA_KV_RANK, OFF_CKV)]
    return moves, ga, krope


W_IN_COLS = sum(SPLIT_SIZES)
W_IN_FULL_TILES = W_IN_COLS // LANES


def _wprep_kernel(w_ref, tail_ref, o_ref):
    tr = w_ref.shape[0]
    lane = lax.broadcasted_iota(jnp.int32, (tr, LANES), 1)
    moves, ga, krope = _w_in_moves()
    rolled = {}

    def tile(t, m):
        if (t, m) not in rolled:
            x = tail_ref[...] if t == W_IN_FULL_TILES else w_ref[:, t * LANES:(t + 1) * LANES]
            rolled[(t, m)] = pltpu.roll(x, (LANES - m) % LANES, 1) if m else x
        return rolled[(t, m)]

    for (a, w, d) in moves:
        q, m = divmod(a, LANES)
        for j in range(w // LANES):
            y = tile(q + j, m)
            if m:
                y = jnp.where(lane < LANES - m, y, tile(q + j + 1, m))
            o_ref[:, d + j * LANES:d + (j + 1) * LANES] = y.astype(bf16)
    gq, gm = divmod(ga, LANES)
    kq_, km = divmod(krope, LANES)
    gates = tile(gq, gm)
    kr = pltpu.roll(tile(kq_, km), ROPE_LANE0, 1)
    small = jnp.where(lane < 4 * GDN_HEADS, gates,
                      jnp.where((lane >= ROPE_LANE0) & (lane < ROPE_LANE0 + MLA_ROPE), kr, 0.0))
    o_ref[:, OFF_SMALL:OFF_SMALL + LANES] = small.astype(bf16)


def _prep_w_in(w_in):
    L = w_in.shape[0]
    tr = 128
    tail = _pad_last(w_in[:, :, W_IN_FULL_TILES * LANES:], LANES)
    return pl.pallas_call(
        _wprep_kernel,
        grid=(L, D_MODEL // tr),
        in_specs=[pl.BlockSpec((None, tr, W_IN_COLS), lambda l, r: (l, r, 0)),
                  pl.BlockSpec((None, tr, LANES), lambda l, r: (l, r, 0))],
        out_specs=pl.BlockSpec((None, tr, PROJ_W), lambda l, r: (l, r, 0)),
        out_shape=jax.ShapeDtypeStruct((L, D_MODEL, PROJ_W), bf16),
        compiler_params=_cparams(2),
        name="wprep",
    )(w_in, tail)


def _pad_last(x, n):
    return jnp.pad(x, [(0, 0)] * (x.ndim - 1) + [(0, n - x.shape[-1])])


def _rope_tables(seq):
    t = np.arange(seq)
    row = (t // GRID_W).astype(np.float32)
    colp = (t % GRID_W).astype(np.float32)
    nf = MLA_ROPE // 4
    inv = (ROPE_THETA ** (-np.arange(nf, dtype=np.float32) / nf)).astype(np.float32)
    cos_t = np.ones((seq, LANES), np.float32)
    s_lo = np.zeros((seq, LANES), np.float32)
    s_hi = np.zeros((seq, LANES), np.float32)
    for i, pos in enumerate((row, colp)):
        ang = (pos[:, None] * inv[None, :]).astype(np.float32)
        cs, sn = np.cos(ang), np.sin(ang)
        lo = ROPE_LANE0 + 2 * nf * i
        cos_t[:, lo:lo + nf] = cs
        cos_t[:, lo + nf:lo + 2 * nf] = cs
        s_lo[:, lo:lo + nf] = -sn
        s_hi[:, lo + nf:lo + 2 * nf] = sn
    return tuple(jnp.asarray(p) for p in (cos_t, s_lo, s_hi))


def kernel(x_prompt, x_sample, cache_ckv, cache_krope, state_gdn, c, c_ctx, norm_g, w_mod, b_mod, w_in, q_a_norm, w_uq, kv_a_norm, w_ukv, q_norm, k_norm, conv_w, a_log, dt_bias, gdn_onorm, cm_ln_g, cm_ln_b, w_s, b_s, w_branch, w_o):
    L = DEPTH
    batch, seq, _ = x_prompt.shape
    dbatch, dseq, _ = x_sample.shape
    past = cache_ckv.shape[2]

    w_in_p = _prep_w_in(w_in)
    wuq_p = _pad_last(w_uq.reshape(L, MLA_Q_RANK, MLA_HEADS, MLA_QK), HEAD_SLOT)
    wuq_p = wuq_p.reshape(L, MLA_Q_RANK, MLA_HEADS * HEAD_SLOT).astype(bf16)
    wukv = w_ukv.reshape(L, MLA_KV_RANK, MLA_HEADS, MLA_NOPE + MLA_V)
    wk_p = _pad_last(wukv[..., :MLA_NOPE], HEAD_SLOT).reshape(L, MLA_KV_RANK, MLA_HEADS * HEAD_SLOT).astype(bf16)
    wv_p = wukv[..., MLA_NOPE:].reshape(L, MLA_KV_RANK, MLA_WIDTH).astype(bf16)
    qn_p = _pad_last(q_norm, HEAD_SLOT).reshape(L, 1, HEAD_SLOT)
    kn_p = _pad_last(k_norm, HEAD_SLOT).reshape(L, 1, HEAD_SLOT)
    attn_w = (q_a_norm.reshape(L, 1, MLA_Q_RANK), wuq_p, kv_a_norm.reshape(L, 1, MLA_KV_RANK),
              wk_p, wv_p, qn_p, kn_p)
    cache_kr_p = jnp.pad(cache_krope, [(0, 0)] * 3 + [(ROPE_LANE0, LANES - ROPE_LANE0 - MLA_ROPE)])
    rope_tabs = _rope_tables(dseq)
    alog_p = _pad_last(a_log.reshape(L, 1, 2 * GDN_HEADS), LANES)
    dtb_p = _pad_last(dt_bias.reshape(L, 1, 2 * GDN_HEADS), LANES)
    onorm = gdn_onorm.reshape(L, 1, GDN_DV)
    conv_w = jnp.swapaxes(conv_w.reshape(L, CONV_W, -1, LANES), 1, 2)
    lng =cm_ln_g.reshape(L, 1, CM_WIDTH)
    lnb = cm_ln_b.reshape(L, 1, CM_WIDTH)
    ws_b = w_s.astype(bf16)
    bs_full = jnp.repeat(jnp.swapaxes(b_s, 1, 2), CM_WIDTH // CM_GROUPS, axis=2)
    wbr_b = w_branch.astype(bf16)
    wo_b = w_o.astype(bf16)
    norm_g3 = norm_g.reshape(L, 1, D_MODEL)

    c8 = jnp.concatenate([c, c_ctx[None, :], jnp.zeros((8 - dbatch - 1, D_MODEL), f32)], axis=0)
    mod = _modulation(c8, w_mod, b_mod)

    yp = x_prompt.reshape(batch * seq, D_MODEL)
    ys = x_sample.reshape(dbatch * dseq, D_MODEL)
    ckvs, kropes, states = [], [], []
    for l in range(L):
        mod_l = mod[l].reshape(8, 1, 3 * D_MODEL)
        proj = _inproj(yp, mod_l, norm_g3, w_in_p, l, False, seq)
        o_a, ckvn, kr = _attn_ctx(proj, attn_w, l, batch, seq)
        o_b, s_new = _gdn(proj, conv_w, alog_p, dtb_p, onorm, None, l, batch, seq)
        yp = _merge(yp, mod_l, proj, o_a, o_b, lng, lnb, ws_b, bs_full, wbr_b, wo_b, l, False, seq)
        ckvs.append(ckvn.reshape(batch, seq, MLA_KV_RANK))
        kropes.append(kr.reshape(batch, seq, MLA_ROPE))
        states.append(s_new)
        proj = _inproj(ys, mod_l, norm_g3, w_in_p, l, True, dseq)
        o_a = _attn_lat(proj, cache_ckv, cache_kr_p, rope_tabs, attn_w, l, dbatch, dseq, past)
        o_b = _gdn(proj, conv_w, alog_p, dtb_p, onorm, state_gdn, l, dbatch, dseq)
        ys = _merge(ys, mod_l, proj, o_a, o_b, lng, lnb, ws_b, bs_full, wbr_b, wo_b, l, True, dseq)
    return (yp.reshape(batch, seq, D_MODEL), ys.reshape(dbatch, dseq, D_MODEL),
            jnp.stack(ckvs, axis=1), jnp.stack(kropes, axis=1), jnp.stack(states, axis=1))
```

```python
import functools
import math

import numpy as np

import jax
import jax.numpy as jnp
from jax import lax
from jax.experimental import pallas as pl
from jax.experimental.pallas import tpu as pltpu

D_MODEL = 1024
DEPTH = 2
GRID_W = 64
EPS = 1e-6
MLA_HEADS = 8
MLA_NOPE = 64
MLA_ROPE = 32
MLA_QK = MLA_NOPE + MLA_ROPE
MLA_V = 64
MLA_Q_RANK = 384
MLA_KV_RANK = 256
MLA_WIDTH = MLA_HEADS * MLA_V
ROPE_THETA = 10000.0
GDN_HEADS = 4
GDN_DK = 128
GDN_DV = 128
GDN_KW = GDN_HEADS * GDN_DK
GDN_VW = GDN_HEADS * GDN_DV
GDN_CHUNK = 64
CONV_W = 5
GDN_SOLVE_CHUNKS = 4
CM_GROUPS = 4
CM_CHUNK = 128
CM_WIDTH = 512
N_BRANCH = 3
BRANCH_W = 512
SPLIT_SIZES = (MLA_Q_RANK, MLA_KV_RANK, MLA_ROPE, MLA_WIDTH,
               2 * GDN_KW + GDN_VW, 2 * GDN_HEADS, 2 * GDN_HEADS, GDN_VW,
               CM_WIDTH, CM_WIDTH, CM_WIDTH, N_BRANCH * D_MODEL)

LANES = 128
SUBLANES = 8
HEAD_SLOT = LANES
ROPE_LANE0 = MLA_NOPE

OFF_GL = 0
OFF_QKV = OFF_GL + N_BRANCH * D_MODEL
OFF_CQS = OFF_QKV + 2 * GDN_KW + GDN_VW
OFF_SMALL = OFF_CQS + MLA_Q_RANK
OFF_ZA = OFF_CQS + 512
OFF_ZB = OFF_ZA + 512
OFF_CU = OFF_ZB + 512
OFF_CV = OFF_CU + 512
OFF_ZC = OFF_CV + 512
OFF_CKV = OFF_ZC + 512
PROJ_W = OFF_CKV + MLA_KV_RANK

VMEM_LIMIT = 56 * 1024 * 1024

f32 = jnp.float32
bf16 = jnp.bfloat16


def _cparams(n_axes):
    return pltpu.CompilerParams(dimension_semantics=("arbitrary",) * n_axes,
                                vmem_limit_bytes=VMEM_LIMIT)


def _dot(a, b):
    return jnp.dot(a, b, preferred_element_type=f32)


def _dot_nt(a, b):
    return lax.dot_general(a, b, (((1,), (1,)), ((), ())), preferred_element_type=f32)


def _dot_tn(a, b):
    return lax.dot_general(a, b, (((0,), (0,)), ((), ())), preferred_element_type=f32)


def _rms_rows(x, g, n=None):
    n = x.shape[-1] if n is None else n
    ms = jnp.sum(x * x, axis=-1, keepdims=True) * (1.0 / n)
    return x * lax.rsqrt(ms + EPS) * g


def _mod_kernel(c_ref, w_ref, b_ref, o_ref):
    a = c_ref[...]
    a = (a * jax.nn.sigmoid(a)).astype(bf16)
    o_ref[0] = _dot(a, w_ref[0].astype(bf16)) + b_ref[0]


def _modulation(c8, w_mod, b_mod):
    tn = 512
    return pl.pallas_call(
        _mod_kernel,
        grid=(DEPTH, 3 * D_MODEL // tn),
        in_specs=[pl.BlockSpec((8, D_MODEL), lambda l, n: (0, 0)),
                  pl.BlockSpec((1, D_MODEL, tn), lambda l, n: (l, 0, n)),
                  pl.BlockSpec((1, 1, tn), lambda l, n: (l, 0, n))],
        out_specs=pl.BlockSpec((1, 8, tn), lambda l, n: (l, 0, n)),
        out_shape=jax.ShapeDtypeStruct((DEPTH, 8, 3 * D_MODEL), f32),
        compiler_params=_cparams(2),
        name="modulation",
    )(c8, w_mod, b_mod.reshape(DEPTH, 1, 3 * D_MODEL))


def _inproj_kernel(x_ref, mod_ref, g_ref, w_ref, o_ref):
    x = x_ref[...]
    m = mod_ref[0]
    shift = m[:, :D_MODEL]
    scale = m[:, D_MODEL:2 * D_MODEL]
    h = (_rms_rows(x, g_ref[0]) * (1.0 + scale) + shift).astype(bf16)
    for a in range(0, PROJ_W, 512):
        b = min(a + 512, PROJ_W)
        o_ref[:, a:b] = _dot_nt(h, w_ref[a:b, :])


def _mod_row(latent, tm, seq):
    if latent:
        return lambda t: ((t * tm) // seq, 0, 0)
    return lambda t: (4, 0, 0)


def _inproj(x, mod_l, norm_g, w_in_p, l, latent, seq):
    ntok = x.shape[0]
    tm = 256
    return pl.pallas_call(
        _inproj_kernel,
        grid=(ntok // tm,),
        in_specs=[pl.BlockSpec((tm, D_MODEL), lambda t: (t, 0)),
                  pl.BlockSpec((1, 1, 3 * D_MODEL), _mod_row(latent, tm, seq)),
                  pl.BlockSpec((1, 1, D_MODEL), lambda t: (l, 0, 0)),
                  pl.BlockSpec((None, PROJ_W, D_MODEL), lambda t: (l, 0, 0),
                               pipeline_mode=pl.Buffered(1))],
        out_specs=pl.BlockSpec((tm, PROJ_W), lambda t: (t, 0)),
        out_shape=jax.ShapeDtypeStruct((ntok, PROJ_W), f32),
        compiler_params=_cparams(1),
        name="inproj",
    )(x, mod_l, norm_g, w_in_p)


def _rope(x, cos_t, sin_lo, sin_hi):
    return x * cos_t + pltpu.roll(x, LANES - 8, 1) * sin_lo + pltpu.roll(x, 8, 1) * sin_hi


def _build_kv(ckvn_b, kr, wk_ref, wv_ref, knorm, rope, k_s, v_s, r0):
    n = ckvn_b.shape[0]
    kfull = _dot(ckvn_b, wk_ref[...])
    v_s[r0:r0 + n, :] = _dot(ckvn_b, wv_ref[...]).astype(bf16)
    krg = kr * knorm
    if rope is not None:
        krg = _rope(krg, *rope)
    kr_ss = jnp.sum(kr * kr, axis=-1, keepdims=True)
    for h in range(MLA_HEADS):
        sl = slice(h * HEAD_SLOT, (h + 1) * HEAD_SLOT)
        kn = kfull[:, sl]
        ms = (jnp.sum(kn * kn, axis=-1, keepdims=True) + kr_ss) * (1.0 / MLA_QK)
        k_s[r0:r0 + n, sl] = ((kn * knorm + krg) * lax.rsqrt(ms + EPS)).astype(bf16)


def _rope_lane_mask(shape):
    lane = lax.broadcasted_iota(jnp.int32, shape, 1)
    return (lane >= ROPE_LANE0) & (lane < ROPE_LANE0 + MLA_ROPE)


def _attend_block(qa_b, wuq_ref, qnorm, rope, k_s, v_s, o_ref):
    tq = qa_b.shape[0]
    qfull = _dot(qa_b, wuq_ref[...])
    qgain = qnorm * (math.log2(math.e) / math.sqrt(MLA_QK))
    lane = lax.broadcasted_iota(jnp.int32, (tq, LANES), 1)

    def scores(h):
        sl = slice(h * HEAD_SLOT, (h + 1) * HEAD_SLOT)
        qh = _rms_rows(qfull[:, sl], qgain, n=MLA_QK)
        if rope is not None:
            qh = _rope(qh, *rope)
        return _dot_nt(qh.astype(bf16), k_s[:, sl])

    s_next = scores(0)
    outs = []
    for h in range(MLA_HEADS):
        s = s_next
        if h + 1 < MLA_HEADS:
            s_next = scores(h + 1)
        p = jnp.exp2(s - jnp.max(s, axis=-1, keepdims=True))
        den = jnp.sum(p, axis=-1, keepdims=True)
        hp = h // 2
        outs.append(_dot(p.astype(bf16), v_s[:, hp * LANES:(hp + 1) * LANES]) / den)
        if h % 2 == 1:
            o_ref[:, hp * LANES:(hp + 1) * LANES] = jnp.where(lane < MLA_V, outs[h - 1], outs[h])


def _attn_ctx_kernel(cqs_ref, ckv_ref, qan_ref, wuq_ref, kvn_ref, wk_ref, wv_ref, qn_ref, kn_ref,
                     o_ref, ckvn_ref, kr_ref, k_s, v_s):
    small = cqs_ref[:, MLA_Q_RANK:]
    kr = jnp.where(_rope_lane_mask(small.shape), small, 0.0)
    kr_ref[...] = small[:, ROPE_LANE0:ROPE_LANE0 + MLA_ROPE]
    ckvn = _rms_rows(ckv_ref[...], kvn_ref[0])
    ckvn_ref[...] = ckvn
    _build_kv(ckvn.astype(bf16), kr, wk_ref, wv_ref, kn_ref[0], None, k_s, v_s, 0)
    qa = _rms_rows(cqs_ref[:, :MLA_Q_RANK], qan_ref[0]).astype(bf16)
    _attend_block(qa, wuq_ref, qn_ref[0], None, k_s, v_s, o_ref)


def _attn_lat_kernel(cqs_ref, ckv_ref, cckv_ref, ckr_ref, cos_ref, slo_ref, shi_ref,
                     qan_ref, wuq_ref, kvn_ref, wk_ref, wv_ref, qn_ref, kn_ref,
                     o_ref, k_s, v_s, *, seq, past, tq):
    qi = pl.program_id(1)
    rb = 256

    @pl.when(qi == 0)
    def _():
        _build_kv(cckv_ref[...].astype(bf16), ckr_ref[...], wk_ref, wv_ref, kn_ref[0], None, k_s, v_s, 0)
        for r in range(seq // rb):
            rs = slice(r * rb, (r + 1) * rb)
            small = cqs_ref[rs, MLA_Q_RANK:]
            kr = jnp.where(_rope_lane_mask(small.shape), small, 0.0)
            ckvn = _rms_rows(ckv_ref[rs, :], kvn_ref[0])
            rope = (cos_ref[rs, :], slo_ref[rs, :], shi_ref[rs, :])
            _build_kv(ckvn.astype(bf16), kr, wk_ref, wv_ref, kn_ref[0], rope, k_s, v_s, past + r * rb)

    rows = pl.ds(pl.multiple_of(qi * tq, tq), tq)
    qa = _rms_rows(cqs_ref[rows, :MLA_Q_RANK], qan_ref[0]).astype(bf16)
    rope = (cos_ref[rows, :], slo_ref[rows, :], shi_ref[rows, :])
    _attend_block(qa, wuq_ref, qn_ref[0], rope, k_s, v_s, o_ref)


def _attn_weight_specs(l, nidx):
    z = (0,) * (nidx - 1)

    def const(*idx):
        return lambda *g: idx

    return [pl.BlockSpec((1, 1, MLA_Q_RANK), const(l, 0, 0)),
            pl.BlockSpec((None, MLA_Q_RANK, MLA_HEADS * HEAD_SLOT), const(l, 0, 0)),
            pl.BlockSpec((1, 1, MLA_KV_RANK), const(l, 0, 0)),
            pl.BlockSpec((None, MLA_KV_RANK, MLA_HEADS * HEAD_SLOT), const(l, 0, 0)),
            pl.BlockSpec((None, MLA_KV_RANK, MLA_WIDTH), const(l, 0, 0)),
            pl.BlockSpec((1, 1, HEAD_SLOT), const(l, 0, 0)),
            pl.BlockSpec((1, 1, HEAD_SLOT), const(l, 0, 0))]


def _attn_ctx(proj, wts, l, batch, seq):
    ntok = batch * seq
    return pl.pallas_call(
        _attn_ctx_kernel,
        grid=(batch,),
        in_specs=[pl.BlockSpec((seq, 512), lambda b: (b, OFF_CQS // 512)),
                  pl.BlockSpec((seq, MLA_KV_RANK), lambda b: (b, OFF_CKV // MLA_KV_RANK))]
        + _attn_weight_specs(l, 1),
        out_specs=[pl.BlockSpec((seq, MLA_WIDTH), lambda b: (b, 0)),
                   pl.BlockSpec((seq, MLA_KV_RANK), lambda b: (b, 0)),
                   pl.BlockSpec((seq, MLA_ROPE), lambda b: (b, 0))],
        out_shape=[jax.ShapeDtypeStruct((ntok, MLA_WIDTH), f32),
                   jax.ShapeDtypeStruct((ntok, MLA_KV_RANK), f32),
                   jax.ShapeDtypeStruct((ntok, MLA_ROPE), f32)],
        scratch_shapes=[pltpu.VMEM((seq, MLA_HEADS * HEAD_SLOT), bf16),
                        pltpu.VMEM((seq, MLA_WIDTH), bf16)],
        compiler_params=_cparams(1),
        name="attn_ctx",
    )(proj, proj, *wts)


def _attn_lat(proj, cache_ckv, cache_kr_p, rope_tabs, wts, l, batch, seq, past):
    ntok = batch * seq
    tq = 256
    nq = seq // tq
    kern = functools.partial(_attn_lat_kernel, seq=seq, past=past, tq=tq)
    tab = pl.BlockSpec((seq, LANES), lambda b, q: (0, 0))
    return pl.pallas_call(
        kern,
        grid=(batch, nq),
        in_specs=[pl.BlockSpec((seq, 512), lambda b, q: (b, OFF_CQS // 512)),
                  pl.BlockSpec((seq, MLA_KV_RANK), lambda b, q: (b, OFF_CKV // MLA_KV_RANK)),
                  pl.BlockSpec((None, None, past, MLA_KV_RANK), lambda b, q: (b, l, 0, 0)),
                  pl.BlockSpec((None, None, past, LANES), lambda b, q: (b, l, 0, 0)),
                  tab, tab, tab]
        + _attn_weight_specs(l, 2),
        out_specs=pl.BlockSpec((tq, MLA_WIDTH), lambda b, q: (b * nq + q, 0)),
        out_shape=jax.ShapeDtypeStruct((ntok, MLA_WIDTH), f32),
        scratch_shapes=[pltpu.VMEM((past + seq, MLA_HEADS * HEAD_SLOT), bf16),
                        pltpu.VMEM((past + seq, MLA_WIDTH), bf16)],
        compiler_params=_cparams(2),
        name="attn_lat",
    )(proj, proj, cache_ckv, cache_kr_p, *rope_tabs, *wts)


def _split3(x):
    hi = x.astype(bf16)
    r1 = x - hi.astype(f32)
    mid = r1.astype(bf16)
    lo = (r1 - mid.astype(f32)).astype(bf16)
    return hi, mid, lo


def _tri_cumsum(tri_b, x):
    hi, mid, lo = _split3(x)
    return _dot(tri_b, hi) + _dot(tri_b, mid) + _dot(tri_b, lo)


def _lane_bcast(x, c):
    return jnp.broadcast_to(x[:, c:c + 1], (x.shape[0], LANES))


def _gdn_kernel(*refs, seq, has_state):
    if has_state:
        (qkv_ref, small_ref, cw_ref, alog_ref, dtb_ref, onorm_ref, s0_ref,
         o_ref, xpad, qkv_s, g_s, b_s, st_s, wq_s, ak_s, u_s, el_s, rhs_s) = refs
        sout_ref = None
    else:
        (qkv_ref, small_ref, cw_ref, alog_ref, dtb_ref, onorm_ref,
         o_ref, sout_ref, xpad, qkv_s, g_s, b_s, st_s, wq_s, ak_s, u_s, el_s, rhs_s) = refs
    C = GDN_CHUNK
    nchunk = seq // C
    H = GDN_HEADS
    width = 2 * GDN_KW + GDN_VW
    halo = 8

    for j in range(width // LANES):
        xpad[j, 0:halo, :] = jnp.zeros((halo, LANES), f32)
        xpad[j, halo + seq:, :] = jnp.zeros((halo, LANES), f32)
        xpad[j, halo:halo + seq, :] = qkv_ref[:, j * LANES:(j + 1) * LANES]
    o_ref[...] = jnp.zeros((seq, GDN_VW), f32)
    if has_state:
        for d in range(2):
            for h in range(H):
                st_s[d * H + h] = s0_ref[d, h]
    else:
        st_s[...] = jnp.zeros((2 * H, GDN_DK, GDN_DV), f32)

    neg_a = -jnp.exp(alog_ref[0])
    dtb = dtb_ref[0]

    def conv_tile(j, l2norm):
        w = cw_ref[j]
        post = jnp.where(j < H, GDN_DK ** -0.5, 1.0)
        for c in range(nchunk):
            base = halo - CONV_W // 2 + c * C
            y = xpad[j, base:base + C, :] * w[0:1]
            for tap in range(1, CONV_W):
                y = y + xpad[j, base + tap:base + tap + C, :] * w[tap:tap + 1]
            y = y * jax.nn.sigmoid(y)
            if l2norm:
                y = y * (lax.rsqrt(jnp.sum(y * y, axis=-1, keepdims=True) + EPS) * post)
            qkv_s[j, c * C:(c + 1) * C, :] = y

    def conv_qk(j, carry):
        conv_tile(j, True)
        return carry

    def conv_v(j, carry):
        conv_tile(j, False)
        return carry

    tile_unroll = 2 if nchunk <= 4 else 1
    lax.fori_loop(0, 2 * H, conv_qk, 0, unroll=tile_unroll)
    lax.fori_loop(2 * H, 3 * H, conv_v, 0, unroll=tile_unroll)

    def prep(c, carry):
        r0 = pl.multiple_of(c * C, C)
        sm = small_ref[pl.ds(r0, C), :]
        z = sm + dtb
        g_s[pl.ds(r0, C), :] = neg_a * (jnp.maximum(z, 0.0) + jnp.log1p(jnp.exp(-jnp.abs(z))))
        b_s[pl.ds(r0, C), :] = pltpu.roll(jax.nn.sigmoid(sm), LANES - 2 * H, 1)
        return carry

    lax.fori_loop(0, nchunk, prep, 0, unroll=4)

    ri = lax.broadcasted_iota(jnp.int32, (C, LANES), 0)
    cl = lax.broadcasted_iota(jnp.int32, (C, LANES), 1)
    fwd = cl < C
    cj = cl & (C - 1)
    eye2 = (ri == cj).astype(f32)
    incl2 = (fwd & (ri >= cj)) | (~fwd & (ri <= cj))
    strict2 = (fwd & (ri > cj)) | (~fwd & (ri < cj))
    xor = ri ^ cj
    level2 = sum((xor >= (1 << b)).astype(jnp.int32) for b in range(C.bit_length() - 1))
    lvl_top = jnp.where(fwd, level2, 0)
    lvl_bot = jnp.where(fwd, 0, level2)
    r2 = lax.broadcasted_iota(jnp.int32, (2 * C, C), 0)
    c2 = lax.broadcasted_iota(jnp.int32, (2 * C, C), 1)
    tri2 = (((r2 < C) & (r2 >= c2)) | ((r2 >= C) & (r2 - C <= c2))).astype(bf16)
    zrhs = jnp.zeros((C, 2 * LANES), bf16)
    zvn = jnp.zeros((C, LANES), bf16)
    fwd_row = fwd[0:1, :]
    cpi = GDN_SOLVE_CHUNKS

    def block_diag(x):
        return jnp.concatenate([jnp.where(fwd, x, 0.0), jnp.where(fwd, 0.0, x)], axis=0).astype(bf16)

    def solve_phase(i, carry):
        chains = []
        for cc in range(cpi):
            c = i * cpi + cc
            rows = pl.ds(pl.multiple_of(c * C, C), C)
            g2 = _tri_cumsum(tri2, g_s[rows, :])
            g2t = g2.T
            bt = b_s[rows, :]
            for h in range(H):
                chains.append((cc, c, h, rows, g2, g2t, bt))

        a2s, t2s = [], []
        for (cc, c, h, rows, g2, g2t, bt) in chains:
            q = qkv_s[h, rows, :]
            k = qkv_s[H + h, rows, :]
            v = qkv_s[2 * H + h, rows, :]
            kq = _dot_nt(jnp.concatenate([k, q], axis=0).astype(bf16),
                         jnp.concatenate([k, k], axis=0).astype(bf16))
            gcc_f = _lane_bcast(g2[:C], h)
            gcc_b = _lane_bcast(g2[C:], H + h)
            btc_f = _lane_bcast(bt, h)
            btc_b = _lane_bcast(bt, H + h)
            grow = jnp.where(fwd_row, g2t[h:h + 1, :], g2t[H + h:H + h + 1, :])
            diff = jnp.where(fwd, gcc_f, gcc_b) - grow
            dec = jnp.where(incl2, jnp.exp(jnp.where(incl2, diff, 0.0)), 0.0)
            a2 = jnp.where(strict2, jnp.where(fwd, btc_f, btc_b) * kq[:C] * dec, 0.0)
            a2s.append(a2)
            t2s.append(eye2 - jnp.where(level2 == 1, a2, 0.0))
            glast_f = gcc_f[C - 1:C, :]
            glast_b = gcc_b[0:1, :]
            e1_f = jnp.exp(gcc_f)
            e1_b = jnp.exp(gcc_b)
            ket = jnp.concatenate([k * jnp.exp(glast_f - gcc_f), k * jnp.exp(glast_b - gcc_b)], axis=0).T
            ak_s[pl.ds(pl.multiple_of((c * H + h) * 3 * C, 3 * C), 3 * C), :] = jnp.concatenate(
                [kq[C:] * dec, ket], axis=0).astype(bf16)
            for d, (btc, e1, glast) in enumerate(((btc_f, e1_f, glast_f), (btc_b, e1_b, glast_b))):
                ch = d * H + h
                j = (cc * H + h) * 2 + d
                rhs_s[j * C:(j + 1) * C, :] = jnp.concatenate([v * btc, k * (btc * e1)], axis=1).astype(bf16)
                wq_s[pl.ds(pl.multiple_of((c * 2 * H + ch) * 2 * C + C, C), C), :] = (q * e1).astype(bf16)
                el_s[pl.ds(pl.multiple_of((c * 2 * H + ch) * 8, 8), 8), :] = jnp.broadcast_to(
                    jnp.exp(glast), (8, LANES))

        for lv in range(2, C.bit_length()):
            rs = []
            for a2, t2 in zip(a2s, t2s):
                abd = jnp.concatenate([jnp.where(lvl_top == lv, a2, 0.0),
                                       jnp.where(lvl_bot == lv, a2, 0.0)], axis=0).astype(bf16)
                rs.append(_dot(t2.astype(bf16), abd))
            t2s = [t2 - _dot(r.astype(bf16), block_diag(t2)) for r, t2 in zip(rs, t2s)]

        for (cc, c, h, rows, g2, g2t, bt), t2 in zip(chains, t2s):
            t2b = t2.astype(bf16)
            for d in range(2):
                ch = d * H + h
                j = (cc * H + h) * 2 + d
                rhs = rhs_s[j * C:(j + 1) * C, :]
                rhs = jnp.concatenate([rhs, zrhs] if d == 0 else [zrhs, rhs], axis=0)
                uw = _dot(t2b, rhs)
                u_s[pl.ds(pl.multiple_of((c * 2 * H + ch) * C, C), C), :] = uw[:, :LANES]
                wq_s[pl.ds(pl.multiple_of((c * 2 * H + ch) * 2 * C, C), C), :] = uw[:, LANES:].astype(bf16)
        return carry

    lax.fori_loop(0, nchunk // cpi, solve_phase, 0)

    def scan_phase(i, carry):
        cs = [i if ch < H else nchunk - 1 - i for ch in range(2 * H)]
        s_old = [st_s[ch] for ch in range(2 * H)]
        r1 = [_dot(wq_s[pl.ds(pl.multiple_of((cs[ch] * 2 * H + ch) * 2 * C, 2 * C), 2 * C), :],
                   s_old[ch].astype(bf16)) for ch in range(2 * H)]
        r2s = []
        for ch in range(2 * H):
            u = u_s[pl.ds(pl.multiple_of((cs[ch] * 2 * H + ch) * C, C), C), :]
            vnb = (u - r1[ch][:C]).astype(bf16)
            rhs = jnp.concatenate([vnb, zvn] if ch < H else [zvn, vnb], axis=0)
            ak = ak_s[pl.ds(pl.multiple_of((cs[ch] * H + ch % H) * 3 * C, 3 * C), 3 * C), :]
            r2s.append(_dot(ak, rhs))
        for ch in range(2 * H):
            el = el_s[pl.ds(pl.multiple_of((cs[ch] * 2 * H + ch) * 8, 8), 8), :][0:1, :]
            st_s[ch] = s_old[ch] * el + r2s[ch][C:]
            rows = pl.ds(pl.multiple_of(cs[ch] * C, C), C)
            ls = slice((ch % H) * LANES, (ch % H + 1) * LANES)
            o_ref[rows, ls] += r1[ch][C:] + r2s[ch][:C]
        return carry

    lax.fori_loop(0, nchunk, scan_phase, 0)

    onorm = onorm_ref[0]

    def fin(c, carry):
        rows = pl.ds(pl.multiple_of(c * C, C), C)
        for h in range(H):
            ls = slice(h * LANES, (h + 1) * LANES)
            o_ref[rows, ls] = _rms_rows(o_ref[rows, ls], onorm)
        return carry

    lax.fori_loop(0, nchunk, fin, 0, unroll=4)
    if sout_ref is not None:
        for d in range(2):
            for h in range(H):
                sout_ref[d, h] = st_s[d * H + h]


def _gdn(proj, conv_w, alog_p, dtb_p, onorm, state, l, batch, seq):
    ntok = batch * seq
    width = 2 * GDN_KW + GDN_VW
    has_state = state is not None
    nchunk = seq // GDN_CHUNK
    kern = functools.partial(_gdn_kernel, seq=seq, has_state=has_state)
    in_specs = [pl.BlockSpec((seq, width), lambda b: (b, OFF_QKV // width)),
                pl.BlockSpec((seq, LANES), lambda b: (b, OFF_SMALL // LANES)),
                pl.BlockSpec((None, width // LANES, CONV_W, LANES), lambda b: (l, 0, 0, 0)),
                pl.BlockSpec((1, 1, LANES), lambda b: (l, 0, 0)),
                pl.BlockSpec((1, 1, LANES), lambda b: (l, 0, 0)),
                pl.BlockSpec((1, 1, GDN_DV), lambda b: (l, 0, 0))]
    args = [proj, proj, conv_w, alog_p, dtb_p, onorm]
    o_spec = pl.BlockSpec((seq, GDN_VW), lambda b: (b, 0))
    o_shape = jax.ShapeDtypeStruct((ntok, GDN_VW), f32)
    st_block = (None, None, 2, GDN_HEADS, GDN_DK, GDN_DV)
    if has_state:
        in_specs.append(pl.BlockSpec(st_block, lambda b: (b, l, 0, 0, 0, 0)))
        args.append(state)
        out_specs, out_shape = o_spec, o_shape
    else:
        out_specs = [o_spec, pl.BlockSpec(st_block[1:], lambda b: (b, 0, 0, 0, 0))]
        out_shape = [o_shape, jax.ShapeDtypeStruct((batch, 2, GDN_HEADS, GDN_DK, GDN_DV), f32)]
    return pl.pallas_call(
        kern,
        grid=(batch,),
        in_specs=in_specs,
        out_specs=out_specs,
        out_shape=out_shape,
        scratch_shapes=[pltpu.VMEM((width // LANES, seq + 16, LANES), f32),
                        pltpu.VMEM((width // LANES, seq, LANES), f32),
                        pltpu.VMEM((seq, LANES), f32),
                        pltpu.VMEM((seq, LANES), f32),
                        pltpu.VMEM((2 * GDN_HEADS, GDN_DK, GDN_DV), f32),
                        pltpu.VMEM((nchunk * 2 * GDN_HEADS * 2 * GDN_CHUNK, LANES), bf16),
                        pltpu.VMEM((nchunk * GDN_HEADS * 3 * GDN_CHUNK, LANES), bf16),
                        pltpu.VMEM((nchunk * 2 * GDN_HEADS * GDN_CHUNK, LANES), f32),
                        pltpu.VMEM((nchunk * 2 * GDN_HEADS * 8, LANES), f32),
                        pltpu.VMEM((GDN_SOLVE_CHUNKS * 2 * GDN_HEADS * GDN_CHUNK, 2 * LANES), bf16)],
        compiler_params=_cparams(1),
        name="gdn_lat" if has_state else "gdn_ctx",
    )(*args)


def _merge_kernel(x_ref, mod_ref, oa_ref, ob_ref, za_ref, zb_ref, cu_ref, cv_ref, zc_ref, gl_ref,
                  lng_ref, lnb_ref, ws_ref, bs_ref, wbr_ref, wo_ref, out_ref, sv_s):
    tm = x_ref.shape[0]
    u = jax.nn.gelu(cu_ref[...])
    vf = jax.nn.gelu(cv_ref[...])
    mu = jnp.mean(vf, axis=-1, keepdims=True)
    vc = vf - mu
    var = jnp.mean(vc * vc, axis=-1, keepdims=True)
    vn = (vc * lax.rsqrt(var + EPS) * lng_ref[0] + lnb_ref[0]).astype(bf16)
    for ck in range(tm // CM_CHUNK):
        rs = slice(ck * CM_CHUNK, (ck + 1) * CM_CHUNK)
        for g in range(CM_GROUPS):
            ls = slice(g * LANES, (g + 1) * LANES)
            sv_s[rs, ls] = _dot(ws_ref[g], vn[rs, ls]) + bs_ref[:, ls]
    o_c = u * sv_s[...]

    def silu(z):
        return z * jax.nn.sigmoid(z)

    brs = (oa_ref[...] * silu(za_ref[...]), ob_ref[...] * silu(zb_ref[...]), o_c * silu(zc_ref[...]))
    ysum = None
    for n in range(N_BRANCH):
        yb = _dot(brs[n].astype(bf16), wbr_ref[n])
        t = jax.nn.sigmoid(gl_ref[:, n * D_MODEL:(n + 1) * D_MODEL]) * yb
        ysum = t if ysum is None else ysum + t
    y = _dot(ysum.astype(bf16), wo_ref[...])
    gate = mod_ref[0][:, 2 * D_MODEL:]
    out_ref[...] = x_ref[...] + gate * y


def _merge(x, mod_l, proj, o_a, o_b, lng, lnb, ws_b, bs_full, wbr_b, wo_b, l, latent, seq):
    ntok = x.shape[0]
    tm = 256

    def col(off):
        return pl.BlockSpec((tm, 512), lambda t: (t, off // 512))

    def const(*idx):
        return lambda t: idx

    return pl.pallas_call(
        _merge_kernel,
        grid=(ntok // tm,),
        in_specs=[pl.BlockSpec((tm, D_MODEL), lambda t: (t, 0)),
                  pl.BlockSpec((1, 1, 3 * D_MODEL), _mod_row(latent, tm, seq)),
                  pl.BlockSpec((tm, 512), lambda t: (t, 0)),
                  pl.BlockSpec((tm, 512), lambda t: (t, 0)),
                  col(OFF_ZA), col(OFF_ZB), col(OFF_CU), col(OFF_CV), col(OFF_ZC),
                  pl.BlockSpec((tm, N_BRANCH * D_MODEL), lambda t: (t, 0)),
                  pl.BlockSpec((1, 1, CM_WIDTH), const(l, 0, 0)),
                  pl.BlockSpec((1, 1, CM_WIDTH), const(l, 0, 0)),
                  pl.BlockSpec((None, CM_GROUPS, CM_CHUNK, CM_CHUNK), const(l, 0, 0, 0)),
                  pl.BlockSpec((None, CM_CHUNK, CM_WIDTH), const(l, 0, 0)),
                  pl.BlockSpec((None, N_BRANCH, BRANCH_W, D_MODEL), const(l, 0, 0, 0)),
                  pl.BlockSpec((None, D_MODEL, D_MODEL), const(l, 0, 0))],
        out_specs=pl.BlockSpec((tm, D_MODEL), lambda t: (t, 0)),
        out_shape=jax.ShapeDtypeStruct((ntok, D_MODEL), f32),
        scratch_shapes=[pltpu.VMEM((tm, CM_WIDTH), f32)],
        compiler_params=_cparams(1),
        name="merge",
    )(x, mod_l, o_a, o_b, proj, proj, proj, proj, proj, proj, lng, lnb, ws_b, bs_full, wbr_b, wo_b)


def _w_in_moves():
    offs = [0]
    for s in SPLIT_SIZES:
        offs.append(offs[-1] + s)
    cq, ckv, krope, z_a, qkv, ga, gb, z_b, cu, cv, z_c, gl = offs[:-1]
    moves = [(gl, N_BRANCH * D_MODEL, OFF_GL), (qkv, 2 * GDN_KW + GDN_VW, OFF_QKV), (cq, MLA_Q_RANK, OFF_CQS),
             (z_a, 512, OFF_ZA), (z_b, 512, OFF_ZB), (cu, 512, OFF_CU), (cv, 512, OFF_CV), (z_c, 512, OFF_ZC),
             (ckv, MLA_KV_RANK, OFF_CKV)]
    return moves, ga, krope


W_ROW_TILE = 256
W_SMALL_TILE = OFF_SMALL // W_ROW_TILE


def _w_row_offsets():
    moves, _, _ = _w_in_moves()
    offs = []
    for r in range(PROJ_W // W_ROW_TILE):
        d = r * W_ROW_TILE
        src = [a + d - dst for (a, w, dst) in moves if dst <= d < dst + w]
        offs.append(src[0])
    return np.asarray(offs, np.int32) // SUBLANES


def _wprep_kernel(offs_ref, a_ref, g_ref, k_ref, o_ref):
    t = pl.program_id(1)

    @pl.when(t != W_SMALL_TILE)
    def _():
        o_ref[...] = a_ref[...].astype(bf16)

    @pl.when(t == W_SMALL_TILE)
    def _():
        half = W_ROW_TILE - LANES
        ngate = 4 * GDN_HEADS
        o_ref[0:half, :] = a_ref[0:half, :].astype(bf16)
        o_ref[half:half + ngate, :] = g_ref[...].astype(bf16)
        o_ref[half + ngate:half + ROPE_LANE0, :] = jnp.zeros((ROPE_LANE0 - ngate, D_MODEL), bf16)
        o_ref[half + ROPE_LANE0:half + ROPE_LANE0 + MLA_ROPE, :] = k_ref[...].astype(bf16)
        o_ref[half + ROPE_LANE0 + MLA_ROPE:, :] = jnp.zeros((LANES - ROPE_LANE0 - MLA_ROPE, D_MODEL), bf16)


def _prep_w_in(w_in):
    L = w_in.shape[0]
    w_t = jnp.swapaxes(w_in, 1, 2)
    _, ga, krope = _w_in_moves()
    full = pl.Element(D_MODEL)
    grid_spec = pltpu.PrefetchScalarGridSpec(
        num_scalar_prefetch=1,
        grid=(L, PROJ_W // W_ROW_TILE),
        in_specs=[pl.BlockSpec((None, pl.Element(W_ROW_TILE), full), lambda l, t, offs: (l, offs[t] * SUBLANES, 0)),
                  pl.BlockSpec((None, pl.Element(4 * GDN_HEADS), full), lambda l, t, offs: (l, ga, 0)),
                  pl.BlockSpec((None, pl.Element(MLA_ROPE), full), lambda l, t, offs: (l, krope, 0))],
        out_specs=pl.BlockSpec((None, W_ROW_TILE, D_MODEL), lambda l, t, offs: (l, t, 0)))
    return pl.pallas_call(
        _wprep_kernel,
        grid_spec=grid_spec,
        out_shape=jax.ShapeDtypeStruct((L, PROJ_W, D_MODEL), bf16),
        compiler_params=_cparams(2),
        name="wprep",
    )(jnp.asarray(_w_row_offsets()), w_t, w_t, w_t)


def _pad_last(x, n):
    return jnp.pad(x, [(0, 0)] * (x.ndim - 1) + [(0, n - x.shape[-1])])


def _rope_tables(seq):
    t = np.arange(seq)
    row = (t // GRID_W).astype(np.float32)
    colp = (t % GRID_W).astype(np.float32)
    nf = MLA_ROPE // 4
    inv = (ROPE_THETA ** (-np.arange(nf, dtype=np.float32) / nf)).astype(np.float32)
    cos_t = np.ones((seq, LANES), np.float32)
    s_lo = np.zeros((seq, LANES), np.float32)
    s_hi = np.zeros((seq, LANES), np.float32)
    for i, pos in enumerate((row, colp)):
        ang = (pos[:, None] * inv[None, :]).astype(np.float32)
        cs, sn = np.cos(ang), np.sin(ang)
        lo = ROPE_LANE0 + 2 * nf * i
        cos_t[:, lo:lo + nf] = cs
        cos_t[:, lo + nf:lo + 2 * nf] = cs
        s_lo[:, lo:lo + nf] = -sn
        s_hi[:, lo + nf:lo + 2 * nf] = sn
    return tuple(jnp.asarray(p) for p in (cos_t, s_lo, s_hi))


def kernel(x_prompt, x_sample, cache_ckv, cache_krope, state_gdn, c, c_ctx, norm_g, w_mod, b_mod, w_in, q_a_norm, w_uq, kv_a_norm, w_ukv, q_norm, k_norm, conv_w, a_log, dt_bias, gdn_onorm, cm_ln_g, cm_ln_b, w_s, b_s, w_branch, w_o):
    L = DEPTH
    batch, seq, _ = x_prompt.shape
    dbatch, dseq, _ = x_sample.shape
    past = cache_ckv.shape[2]

    w_in_p = _prep_w_in(w_in)
    wuq_p = _pad_last(w_uq.reshape(L, MLA_Q_RANK, MLA_HEADS, MLA_QK), HEAD_SLOT)
    wuq_p = wuq_p.reshape(L, MLA_Q_RANK, MLA_HEADS * HEAD_SLOT).astype(bf16)
    wukv = w_ukv.reshape(L, MLA_KV_RANK, MLA_HEADS, MLA_NOPE + MLA_V)
    wk_p = _pad_last(wukv[..., :MLA_NOPE], HEAD_SLOT).reshape(L, MLA_KV_RANK, MLA_HEADS * HEAD_SLOT).astype(bf16)
    wv_p = wukv[..., MLA_NOPE:].reshape(L, MLA_KV_RANK, MLA_WIDTH).astype(bf16)
    qn_p = _pad_last(q_norm, HEAD_SLOT).reshape(L, 1, HEAD_SLOT)
    kn_p = _pad_last(k_norm, HEAD_SLOT).reshape(L, 1, HEAD_SLOT)
    attn_w = (q_a_norm.reshape(L, 1, MLA_Q_RANK), wuq_p, kv_a_norm.reshape(L, 1, MLA_KV_RANK),
              wk_p, wv_p, qn_p, kn_p)
    cache_kr_p = jnp.pad(cache_krope, [(0, 0)] * 3 + [(ROPE_LANE0, LANES - ROPE_LANE0 - MLA_ROPE)])
    rope_tabs = _rope_tables(dseq)
    alog_p = _pad_last(a_log.reshape(L, 1, 2 * GDN_HEADS), LANES)
    dtb_p = _pad_last(dt_bias.reshape(L, 1, 2 * GDN_HEADS), LANES)
    onorm = gdn_onorm.reshape(L, 1, GDN_DV)
    conv_w = jnp.swapaxes(conv_w.reshape(L, CONV_W, -1, LANES), 1, 2)
    lng =cm_ln_g.reshape(L, 1, CM_WIDTH)
    lnb = cm_ln_b.reshape(L, 1, CM_WIDTH)
    ws_b = w_s.astype(bf16)
    bs_full = jnp.repeat(jnp.swapaxes(b_s, 1, 2), CM_WIDTH // CM_GROUPS, axis=2)
    wbr_b = w_branch.astype(bf16)
    wo_b = w_o.astype(bf16)
    norm_g3 = norm_g.reshape(L, 1, D_MODEL)

    c8 = jnp.concatenate([c, c_ctx[None, :], jnp.zeros((8 - dbatch - 1, D_MODEL), f32)], axis=0)
    mod = _modulation(c8, w_mod, b_mod)

    yp = x_prompt.reshape(batch * seq, D_MODEL)
    ys = x_sample.reshape(dbatch * dseq, D_MODEL)
    ckvs, kropes, states = [], [], []
    for l in range(L):
        mod_l = mod[l].reshape(8, 1, 3 * D_MODEL)
        proj = _inproj(yp, mod_l, norm_g3, w_in_p, l, False, seq)
        o_a, ckvn, kr = _attn_ctx(proj, attn_w, l, batch, seq)
        o_b, s_new = _gdn(proj, conv_w, alog_p, dtb_p, onorm, None, l, batch, seq)
        yp = _merge(yp, mod_l, proj, o_a, o_b, lng, lnb, ws_b, bs_full, wbr_b, wo_b, l, False, seq)
        ckvs.append(ckvn.reshape(batch, seq, MLA_KV_RANK))
        kropes.append(kr.reshape(batch, seq, MLA_ROPE))
        states.append(s_new)
        proj = _inproj(ys, mod_l, norm_g3, w_in_p, l, True, dseq)
        o_a = _attn_lat(proj, cache_ckv, cache_kr_p, rope_tabs, attn_w, l, dbatch, dseq, past)
        o_b = _gdn(proj, conv_w, alog_p, dtb_p, onorm, state_gdn, l, dbatch, dseq)
        ys = _merge(ys, mod_l, proj, o_a, o_b, lng, lnb, ws_b, bs_full, wbr_b, wo_b, l, True, dseq)
    return (yp.reshape(batch, seq, D_MODEL), ys.reshape(dbatch, dseq, D_MODEL),
            jnp.stack(ckvs, axis=1), jnp.stack(kropes, axis=1), jnp.stack(states, axis=1))
```

```python
import functools
import math

import numpy as np

import jax
import jax.numpy as jnp
from jax import lax
from jax.experimental import pallas as pl
from jax.experimental.pallas import tpu as pltpu

D_MODEL = 1024
DEPTH = 2
GRID_W = 64
EPS = 1e-6
MLA_HEADS = 8
MLA_NOPE = 64
MLA_ROPE = 32
MLA_QK = MLA_NOPE + MLA_ROPE
MLA_V = 64
MLA_Q_RANK = 384
MLA_KV_RANK = 256
MLA_WIDTH = MLA_HEADS * MLA_V
ROPE_THETA = 10000.0
GDN_HEADS = 4
GDN_DK = 128
GDN_DV = 128
GDN_KW = GDN_HEADS * GDN_DK
GDN_VW = GDN_HEADS * GDN_DV
GDN_CHUNK = 64
CONV_W = 5
GDN_SOLVE_CHUNKS = 4
CM_GROUPS = 4
CM_CHUNK = 128
CM_WIDTH = 512
N_BRANCH = 3
BRANCH_W = 512
SPLIT_SIZES = (MLA_Q_RANK, MLA_KV_RANK, MLA_ROPE, MLA_WIDTH,
               2 * GDN_KW + GDN_VW, 2 * GDN_HEADS, 2 * GDN_HEADS, GDN_VW,
               CM_WIDTH, CM_WIDTH, CM_WIDTH, N_BRANCH * D_MODEL)

LANES = 128
SUBLANES = 8
HEAD_SLOT = LANES
ROPE_LANE0 = MLA_NOPE

OFF_GL = 0
OFF_ZA = OFF_GL + N_BRANCH * D_MODEL
OFF_ZB = OFF_ZA + 512
OFF_CU = OFF_ZB + 512
OFF_CV = OFF_CU + 512
OFF_ZC = OFF_CV + 512
P16_W = OFF_ZC + 512
OFF_QKV = 0
OFF_CQS = OFF_QKV + 2 * GDN_KW + GDN_VW
OFF_SMALL = OFF_CQS + MLA_Q_RANK
OFF_CKV = OFF_CQS + 512
P32_W = OFF_CKV + MLA_KV_RANK
PROJ_W = P16_W + P32_W

VMEM_LIMIT = 56 * 1024 * 1024

f32 = jnp.float32
bf16 = jnp.bfloat16


def _cparams(n_axes):
    return pltpu.CompilerParams(dimension_semantics=("arbitrary",) * n_axes,
                                vmem_limit_bytes=VMEM_LIMIT)


def _dot(a, b):
    return jnp.dot(a, b, preferred_element_type=f32)


def _dot_nt(a, b):
    return lax.dot_general(a, b, (((1,), (1,)), ((), ())), preferred_element_type=f32)


def _dot_tn(a, b):
    return lax.dot_general(a, b, (((0,), (0,)), ((), ())), preferred_element_type=f32)


def _rms_rows(x, g, n=None):
    n = x.shape[-1] if n is None else n
    ms = jnp.sum(x * x, axis=-1, keepdims=True) * (1.0 / n)
    return x * lax.rsqrt(ms + EPS) * g


def _mod_kernel(c_ref, w_ref, b_ref, o_ref):
    a = c_ref[...]
    a = (a * jax.nn.sigmoid(a)).astype(bf16)
    o_ref[0] = _dot(a, w_ref[0].astype(bf16)) + b_ref[0]


def _modulation(c8, w_mod, b_mod):
    tn = 512
    return pl.pallas_call(
        _mod_kernel,
        grid=(DEPTH, 3 * D_MODEL // tn),
        in_specs=[pl.BlockSpec((8, D_MODEL), lambda l, n: (0, 0)),
                  pl.BlockSpec((1, D_MODEL, tn), lambda l, n: (l, 0, n)),
                  pl.BlockSpec((1, 1, tn), lambda l, n: (l, 0, n))],
        out_specs=pl.BlockSpec((1, 8, tn), lambda l, n: (l, 0, n)),
        out_shape=jax.ShapeDtypeStruct((DEPTH, 8, 3 * D_MODEL), f32),
        compiler_params=_cparams(2),
        name="modulation",
    )(c8, w_mod, b_mod.reshape(DEPTH, 1, 3 * D_MODEL))


def _inproj_kernel(x_ref, mod_ref, g_ref, w_ref, o16_ref, o32_ref):
    x = x_ref[...]
    m = mod_ref[0]
    shift = m[:, :D_MODEL]
    scale = m[:, D_MODEL:2 * D_MODEL]
    h = (_rms_rows(x, g_ref[0]) * (1.0 + scale) + shift).astype(bf16)
    for a in range(0, P16_W, 512):
        o16_ref[:, a:a + 512] = _dot_nt(h, w_ref[a:a + 512, :]).astype(bf16)
    for a in range(0, P32_W, 512):
        b = min(a + 512, P32_W)
        o32_ref[:, a:b] = _dot_nt(h, w_ref[P16_W + a:P16_W + b, :])


def _mod_row(latent, tm, seq):
    if latent:
        return lambda t: ((t * tm) // seq, 0, 0)
    return lambda t: (4, 0, 0)


def _inproj(x, mod_l, norm_g, w_in_p, l, latent, seq):
    ntok = x.shape[0]
    tm = 512
    return pl.pallas_call(
        _inproj_kernel,
        grid=(ntok // tm,),
        in_specs=[pl.BlockSpec((tm, D_MODEL), lambda t: (t, 0)),
                  pl.BlockSpec((1, 1, 3 * D_MODEL), _mod_row(latent, tm, seq)),
                  pl.BlockSpec((1, 1, D_MODEL), lambda t: (l, 0, 0)),
                  pl.BlockSpec((None, PROJ_W, D_MODEL), lambda t: (l, 0, 0),
                               pipeline_mode=pl.Buffered(1))],
        out_specs=[pl.BlockSpec((tm, P16_W), lambda t: (t, 0)),
                   pl.BlockSpec((tm, P32_W), lambda t: (t, 0))],
        out_shape=[jax.ShapeDtypeStruct((ntok, P16_W), bf16),
                   jax.ShapeDtypeStruct((ntok, P32_W), f32)],
        compiler_params=_cparams(1),
        name="inproj",
    )(x, mod_l, norm_g, w_in_p)


def _rope(x, cos_t, sin_lo, sin_hi):
    return x * cos_t + pltpu.roll(x, LANES - 8, 1) * sin_lo + pltpu.roll(x, 8, 1) * sin_hi


def _build_kv(ckvn_b, kr, wk_ref, wv_ref, knorm, rope, k_s, v_s, r0):
    n = ckvn_b.shape[0]
    kfull = _dot(ckvn_b, wk_ref[...])
    v_s[r0:r0 + n, :] = _dot(ckvn_b, wv_ref[...]).astype(bf16)
    krg = kr * knorm
    if rope is not None:
        krg = _rope(krg, *rope)
    kr_ss = jnp.sum(kr * kr, axis=-1, keepdims=True)
    for h in range(MLA_HEADS):
        sl = slice(h * HEAD_SLOT, (h + 1) * HEAD_SLOT)
        kn = kfull[:, sl]
        ms = (jnp.sum(kn * kn, axis=-1, keepdims=True) + kr_ss) * (1.0 / MLA_QK)
        k_s[r0:r0 + n, sl] = ((kn * knorm + krg) * lax.rsqrt(ms + EPS)).astype(bf16)


def _rope_lane_mask(shape):
    lane = lax.broadcasted_iota(jnp.int32, shape, 1)
    return (lane >= ROPE_LANE0) & (lane < ROPE_LANE0 + MLA_ROPE)


def _attend_block(qa_b, wuq_ref, qnorm, rope, k_s, v_s, o_ref):
    tq = qa_b.shape[0]
    qfull = _dot(qa_b, wuq_ref[...])
    qgain = qnorm * (math.log2(math.e) / math.sqrt(MLA_QK))
    lane = lax.broadcasted_iota(jnp.int32, (tq, LANES), 1)

    def scores(h):
        sl = slice(h * HEAD_SLOT, (h + 1) * HEAD_SLOT)
        qh = _rms_rows(qfull[:, sl], qgain, n=MLA_QK)
        if rope is not None:
            qh = _rope(qh, *rope)
        return _dot_nt(qh.astype(bf16), k_s[:, sl])

    s_next = scores(0)
    outs = []
    for h in range(MLA_HEADS):
        s = s_next
        if h + 1 < MLA_HEADS:
            s_next = scores(h + 1)
        p = jnp.exp2(s - jnp.max(s, axis=-1, keepdims=True))
        den = jnp.sum(p, axis=-1, keepdims=True)
        hp = h // 2
        outs.append(_dot(p.astype(bf16), v_s[:, hp * LANES:(hp + 1) * LANES]) / den)
        if h % 2 == 1:
            o_ref[:, hp * LANES:(hp + 1) * LANES] = jnp.where(lane < MLA_V, outs[h - 1], outs[h])


def _attn_ctx_kernel(cqs_ref, ckv_ref, qan_ref, wuq_ref, kvn_ref, wk_ref, wv_ref, qn_ref, kn_ref,
                     o_ref, ckvn_ref, kr_ref, k_s, v_s):
    small = cqs_ref[:, MLA_Q_RANK:]
    kr = jnp.where(_rope_lane_mask(small.shape), small, 0.0)
    kr_ref[...] = small[:, ROPE_LANE0:ROPE_LANE0 + MLA_ROPE]
    ckvn = _rms_rows(ckv_ref[...], kvn_ref[0])
    ckvn_ref[...] = ckvn
    _build_kv(ckvn.astype(bf16), kr, wk_ref, wv_ref, kn_ref[0], None, k_s, v_s, 0)
    qa = _rms_rows(cqs_ref[:, :MLA_Q_RANK], qan_ref[0]).astype(bf16)
    _attend_block(qa, wuq_ref, qn_ref[0], None, k_s, v_s, o_ref)


def _attn_lat_kernel(cqs_ref, ckv_ref, cckv_ref, ckr_ref, cos_ref, slo_ref, shi_ref,
                     qan_ref, wuq_ref, kvn_ref, wk_ref, wv_ref, qn_ref, kn_ref,
                     o_ref, k_s, v_s, *, seq, past, tq):
    qi = pl.program_id(1)
    rb = 256

    @pl.when(qi == 0)
    def _():
        _build_kv(cckv_ref[...].astype(bf16), ckr_ref[...], wk_ref, wv_ref, kn_ref[0], None, k_s, v_s, 0)
        for r in range(seq // rb):
            rs = slice(r * rb, (r + 1) * rb)
            small = cqs_ref[rs, MLA_Q_RANK:]
            kr = jnp.where(_rope_lane_mask(small.shape), small, 0.0)
            ckvn = _rms_rows(ckv_ref[rs, :], kvn_ref[0])
            rope = (cos_ref[rs, :], slo_ref[rs, :], shi_ref[rs, :])
            _build_kv(ckvn.astype(bf16), kr, wk_ref, wv_ref, kn_ref[0], rope, k_s, v_s, past + r * rb)

    rows = pl.ds(pl.multiple_of(qi * tq, tq), tq)
    qa = _rms_rows(cqs_ref[rows, :MLA_Q_RANK], qan_ref[0]).astype(bf16)
    rope = (cos_ref[rows, :], slo_ref[rows, :], shi_ref[rows, :])
    _attend_block(qa, wuq_ref, qn_ref[0], rope, k_s, v_s, o_ref)


def _attn_weight_specs(l, nidx):
    z = (0,) * (nidx - 1)

    def const(*idx):
        return lambda *g: idx

    return [pl.BlockSpec((1, 1, MLA_Q_RANK), const(l, 0, 0)),
            pl.BlockSpec((None, MLA_Q_RANK, MLA_HEADS * HEAD_SLOT), const(l, 0, 0)),
            pl.BlockSpec((1, 1, MLA_KV_RANK), const(l, 0, 0)),
            pl.BlockSpec((None, MLA_KV_RANK, MLA_HEADS * HEAD_SLOT), const(l, 0, 0)),
            pl.BlockSpec((None, MLA_KV_RANK, MLA_WIDTH), const(l, 0, 0)),
            pl.BlockSpec((1, 1, HEAD_SLOT), const(l, 0, 0)),
            pl.BlockSpec((1, 1, HEAD_SLOT), const(l, 0, 0))]


def _attn_ctx(proj, wts, l, batch, seq):
    ntok = batch * seq
    return pl.pallas_call(
        _attn_ctx_kernel,
        grid=(batch,),
        in_specs=[pl.BlockSpec((seq, 512), lambda b: (b, OFF_CQS // 512)),
                  pl.BlockSpec((seq, MLA_KV_RANK), lambda b: (b, OFF_CKV // MLA_KV_RANK))]
        + _attn_weight_specs(l, 1),
        out_specs=[pl.BlockSpec((seq, MLA_WIDTH), lambda b: (b, 0)),
                   pl.BlockSpec((seq, MLA_KV_RANK), lambda b: (b, 0)),
                   pl.BlockSpec((seq, MLA_ROPE), lambda b: (b, 0))],
        out_shape=[jax.ShapeDtypeStruct((ntok, MLA_WIDTH), f32),
                   jax.ShapeDtypeStruct((ntok, MLA_KV_RANK), f32),
                   jax.ShapeDtypeStruct((ntok, MLA_ROPE), f32)],
        scratch_shapes=[pltpu.VMEM((seq, MLA_HEADS * HEAD_SLOT), bf16),
                        pltpu.VMEM((seq, MLA_WIDTH), bf16)],
        compiler_params=_cparams(1),
        name="attn_ctx",
    )(proj, proj, *wts)


def _attn_lat(proj, cache_ckv, cache_kr_p, rope_tabs, wts, l, batch, seq, past):
    ntok = batch * seq
    tq = 256
    nq = seq // tq
    kern = functools.partial(_attn_lat_kernel, seq=seq, past=past, tq=tq)
    tab = pl.BlockSpec((seq, LANES), lambda b, q: (0, 0))
    return pl.pallas_call(
        kern,
        grid=(batch, nq),
        in_specs=[pl.BlockSpec((seq, 512), lambda b, q: (b, OFF_CQS // 512)),
                  pl.BlockSpec((seq, MLA_KV_RANK), lambda b, q: (b, OFF_CKV // MLA_KV_RANK)),
                  pl.BlockSpec((None, None, past, MLA_KV_RANK), lambda b, q: (b, l, 0, 0)),
                  pl.BlockSpec((None, None, past, LANES), lambda b, q: (b, l, 0, 0)),
                  tab, tab, tab]
        + _attn_weight_specs(l, 2),
        out_specs=pl.BlockSpec((tq, MLA_WIDTH), lambda b, q: (b * nq + q, 0)),
        out_shape=jax.ShapeDtypeStruct((ntok, MLA_WIDTH), f32),
        scratch_shapes=[pltpu.VMEM((past + seq, MLA_HEADS * HEAD_SLOT), bf16),
                        pltpu.VMEM((past + seq, MLA_WIDTH), bf16)],
        compiler_params=_cparams(2),
        name="attn_lat",
    )(proj, proj, cache_ckv, cache_kr_p, *rope_tabs, *wts)


def _split3(x):
    hi = x.astype(bf16)
    r1 = x - hi.astype(f32)
    mid = r1.astype(bf16)
    lo = (r1 - mid.astype(f32)).astype(bf16)
    return hi, mid, lo


def _tri_cumsum(tri_b, x):
    hi, mid, lo = _split3(x)
    return _dot(tri_b, hi) + _dot(tri_b, mid) + _dot(tri_b, lo)


def _lane_bcast(x, c):
    return jnp.broadcast_to(x[:, c:c + 1], (x.shape[0], LANES))


def _gdn_kernel(*refs, seq, has_state):
    if has_state:
        (qkv_ref, small_ref, cw_ref, alog_ref, dtb_ref, onorm_ref, s0_ref,
         o_ref, xpad, qkv_s, g_s, b_s, st_s, wq_s, ak_s, u_s, el_s, rhs_s) = refs
        sout_ref = None
    else:
        (qkv_ref, small_ref, cw_ref, alog_ref, dtb_ref, onorm_ref,
         o_ref, sout_ref, xpad, qkv_s, g_s, b_s, st_s, wq_s, ak_s, u_s, el_s, rhs_s) = refs
    C = GDN_CHUNK
    nchunk = seq // C
    H = GDN_HEADS
    width = 2 * GDN_KW + GDN_VW
    halo = 8

    for j in range(width // LANES):
        xpad[j, 0:halo, :] = jnp.zeros((halo, LANES), f32)
        xpad[j, halo + seq:, :] = jnp.zeros((halo, LANES), f32)
        xpad[j, halo:halo + seq, :] = qkv_ref[:, j * LANES:(j + 1) * LANES]
    o_ref[...] = jnp.zeros((seq, GDN_VW), f32)
    if has_state:
        for d in range(2):
            for h in range(H):
                st_s[d * H + h] = s0_ref[d, h]
    else:
        st_s[...] = jnp.zeros((2 * H, GDN_DK, GDN_DV), f32)

    neg_a = -jnp.exp(alog_ref[0])
    dtb = dtb_ref[0]

    def conv_tile(j, l2norm):
        w = cw_ref[j]
        post = jnp.where(j < H, GDN_DK ** -0.5, 1.0)
        for c in range(nchunk):
            base = halo - CONV_W // 2 + c * C
            y = xpad[j, base:base + C, :] * w[0:1]
            for tap in range(1, CONV_W):
                y = y + xpad[j, base + tap:base + tap + C, :] * w[tap:tap + 1]
            y = y * jax.nn.sigmoid(y)
            if l2norm:
                y = y * (lax.rsqrt(jnp.sum(y * y, axis=-1, keepdims=True) + EPS) * post)
            qkv_s[j, c * C:(c + 1) * C, :] = y

    def conv_qk(j, carry):
        conv_tile(j, True)
        return carry

    def conv_v(j, carry):
        conv_tile(j, False)
        return carry

    tile_unroll = 2 if nchunk <= 4 else 1
    lax.fori_loop(0, 2 * H, conv_qk, 0, unroll=tile_unroll)
    lax.fori_loop(2 * H, 3 * H, conv_v, 0, unroll=tile_unroll)

    def prep(c, carry):
        r0 = pl.multiple_of(c * C, C)
        sm = small_ref[pl.ds(r0, C), :]
        z = sm + dtb
        g_s[pl.ds(r0, C), :] = neg_a * (jnp.maximum(z, 0.0) + jnp.log1p(jnp.exp(-jnp.abs(z))))
        b_s[pl.ds(r0, C), :] = pltpu.roll(jax.nn.sigmoid(sm), LANES - 2 * H, 1)
        return carry

    lax.fori_loop(0, nchunk, prep, 0, unroll=4)

    ri = lax.broadcasted_iota(jnp.int32, (C, LANES), 0)
    cl = lax.broadcasted_iota(jnp.int32, (C, LANES), 1)
    fwd = cl < C
    cj = cl & (C - 1)
    eye2 = (ri == cj).astype(f32)
    incl2 = (fwd & (ri >= cj)) | (~fwd & (ri <= cj))
    strict2 = (fwd & (ri > cj)) | (~fwd & (ri < cj))
    xor = ri ^ cj
    level2 = sum((xor >= (1 << b)).astype(jnp.int32) for b in range(C.bit_length() - 1))
    lvl_top = jnp.where(fwd, level2, 0)
    lvl_bot = jnp.where(fwd, 0, level2)
    r2 = lax.broadcasted_iota(jnp.int32, (2 * C, C), 0)
    c2 = lax.broadcasted_iota(jnp.int32, (2 * C, C), 1)
    tri2 = (((r2 < C) & (r2 >= c2)) | ((r2 >= C) & (r2 - C <= c2))).astype(bf16)
    zrhs = jnp.zeros((C, 2 * LANES), bf16)
    zvn = jnp.zeros((C, LANES), bf16)
    fwd_row = fwd[0:1, :]
    cpi = GDN_SOLVE_CHUNKS

    def block_diag(x):
        return jnp.concatenate([jnp.where(fwd, x, 0.0), jnp.where(fwd, 0.0, x)], axis=0).astype(bf16)

    def solve_phase(i, carry):
        chains = []
        for cc in range(cpi):
            c = i * cpi + cc
            rows = pl.ds(pl.multiple_of(c * C, C), C)
            g2 = _tri_cumsum(tri2, g_s[rows, :])
            g2t = g2.T
            bt = b_s[rows, :]
            for h in range(H):
                chains.append((cc, c, h, rows, g2, g2t, bt))

        a2s, t2s = [], []
        for (cc, c, h, rows, g2, g2t, bt) in chains:
            q = qkv_s[h, rows, :]
            k = qkv_s[H + h, rows, :]
            v = qkv_s[2 * H + h, rows, :]
            kq = _dot_nt(jnp.concatenate([k, q], axis=0).astype(bf16),
                         jnp.concatenate([k, k], axis=0).astype(bf16))
            gcc_f = _lane_bcast(g2[:C], h)
            gcc_b = _lane_bcast(g2[C:], H + h)
            btc_f = _lane_bcast(bt, h)
            btc_b = _lane_bcast(bt, H + h)
            grow = jnp.where(fwd_row, g2t[h:h + 1, :], g2t[H + h:H + h + 1, :])
            diff = jnp.where(fwd, gcc_f, gcc_b) - grow
            dec = jnp.where(incl2, jnp.exp(jnp.where(incl2, diff, 0.0)), 0.0)
            a2 = jnp.where(strict2, jnp.where(fwd, btc_f, btc_b) * kq[:C] * dec, 0.0)
            a2s.append(a2)
            t2s.append(eye2 - jnp.where(level2 == 1, a2, 0.0))
            glast_f = gcc_f[C - 1:C, :]
            glast_b = gcc_b[0:1, :]
            e1_f = jnp.exp(gcc_f)
            e1_b = jnp.exp(gcc_b)
            ket = jnp.concatenate([k * jnp.exp(glast_f - gcc_f), k * jnp.exp(glast_b - gcc_b)], axis=0).T
            ak_s[pl.ds(pl.multiple_of((c * H + h) * 3 * C, 3 * C), 3 * C), :] = jnp.concatenate(
                [kq[C:] * dec, ket], axis=0).astype(bf16)
            for d, (btc, e1, glast) in enumerate(((btc_f, e1_f, glast_f), (btc_b, e1_b, glast_b))):
                ch = d * H + h
                j = (cc * H + h) * 2 + d
                rhs_s[j * C:(j + 1) * C, :] = jnp.concatenate([v * btc, k * (btc * e1)], axis=1).astype(bf16)
                wq_s[pl.ds(pl.multiple_of((c * 2 * H + ch) * 2 * C + C, C), C), :] = (q * e1).astype(bf16)
                el_s[pl.ds(pl.multiple_of((c * 2 * H + ch) * 8, 8), 8), :] = jnp.broadcast_to(
                    jnp.exp(glast), (8, LANES))

        for lv in range(2, C.bit_length()):
            rs = []
            for a2, t2 in zip(a2s, t2s):
                abd = jnp.concatenate([jnp.where(lvl_top == lv, a2, 0.0),
                                       jnp.where(lvl_bot == lv, a2, 0.0)], axis=0).astype(bf16)
                rs.append(_dot(t2.astype(bf16), abd))
            t2s = [t2 - _dot(r.astype(bf16), block_diag(t2)) for r, t2 in zip(rs, t2s)]

        for (cc, c, h, rows, g2, g2t, bt), t2 in zip(chains, t2s):
            t2b = t2.astype(bf16)
            for d in range(2):
                ch = d * H + h
                j = (cc * H + h) * 2 + d
                rhs = rhs_s[j * C:(j + 1) * C, :]
                rhs = jnp.concatenate([rhs, zrhs] if d == 0 else [zrhs, rhs], axis=0)
                uw = _dot(t2b, rhs)
                u_s[pl.ds(pl.multiple_of((c * 2 * H + ch) * C, C), C), :] = uw[:, :LANES]
                wq_s[pl.ds(pl.multiple_of((c * 2 * H + ch) * 2 * C, C), C), :] = uw[:, LANES:].astype(bf16)
        return carry

    lax.fori_loop(0, nchunk // cpi, solve_phase, 0)

    def scan_phase(i, carry):
        cs = [i if ch < H else nchunk - 1 - i for ch in range(2 * H)]
        s_old = [st_s[ch] for ch in range(2 * H)]
        r1 = [_dot(wq_s[pl.ds(pl.multiple_of((cs[ch] * 2 * H + ch) * 2 * C, 2 * C), 2 * C), :],
                   s_old[ch].astype(bf16)) for ch in range(2 * H)]
        r2s = []
        for ch in range(2 * H):
            u = u_s[pl.ds(pl.multiple_of((cs[ch] * 2 * H + ch) * C, C), C), :]
            vnb = (u - r1[ch][:C]).astype(bf16)
            rhs = jnp.concatenate([vnb, zvn] if ch < H else [zvn, vnb], axis=0)
            ak = ak_s[pl.ds(pl.multiple_of((cs[ch] * H + ch % H) * 3 * C, 3 * C), 3 * C), :]
            r2s.append(_dot(ak, rhs))
        for ch in range(2 * H):
            el = el_s[pl.ds(pl.multiple_of((cs[ch] * 2 * H + ch) * 8, 8), 8), :][0:1, :]
            st_s[ch] = s_old[ch] * el + r2s[ch][C:]
            rows = pl.ds(pl.multiple_of(cs[ch] * C, C), C)
            ls = slice((ch % H) * LANES, (ch % H + 1) * LANES)
            o_ref[rows, ls] += r1[ch][C:] + r2s[ch][:C]
        return carry

    lax.fori_loop(0, nchunk, scan_phase, 0)

    onorm = onorm_ref[0]

    def fin(c, carry):
        rows = pl.ds(pl.multiple_of(c * C, C), C)
        for h in range(H):
            ls = slice(h * LANES, (h + 1) * LANES)
            o_ref[rows, ls] = _rms_rows(o_ref[rows, ls], onorm)
        return carry

    lax.fori_loop(0, nchunk, fin, 0, unroll=4)
    if sout_ref is not None:
        for d in range(2):
            for h in range(H):
                sout_ref[d, h] = st_s[d * H + h]


def _gdn(proj, conv_w, alog_p, dtb_p, onorm, state, l, batch, seq):
    ntok = batch * seq
    width = 2 * GDN_KW + GDN_VW
    has_state = state is not None
    nchunk = seq // GDN_CHUNK
    kern = functools.partial(_gdn_kernel, seq=seq, has_state=has_state)
    in_specs = [pl.BlockSpec((seq, width), lambda b: (b, OFF_QKV // width)),
                pl.BlockSpec((seq, LANES), lambda b: (b, OFF_SMALL // LANES)),
                pl.BlockSpec((None, width // LANES, CONV_W, LANES), lambda b: (l, 0, 0, 0)),
                pl.BlockSpec((1, 1, LANES), lambda b: (l, 0, 0)),
                pl.BlockSpec((1, 1, LANES), lambda b: (l, 0, 0)),
                pl.BlockSpec((1, 1, GDN_DV), lambda b: (l, 0, 0))]
    args = [proj, proj, conv_w, alog_p, dtb_p, onorm]
    o_spec = pl.BlockSpec((seq, GDN_VW), lambda b: (b, 0))
    o_shape = jax.ShapeDtypeStruct((ntok, GDN_VW), f32)
    st_block = (None, None, 2, GDN_HEADS, GDN_DK, GDN_DV)
    if has_state:
        in_specs.append(pl.BlockSpec(st_block, lambda b: (b, l, 0, 0, 0, 0)))
        args.append(state)
        out_specs, out_shape = o_spec, o_shape
    else:
        out_specs = [o_spec, pl.BlockSpec(st_block[1:], lambda b: (b, 0, 0, 0, 0))]
        out_shape = [o_shape, jax.ShapeDtypeStruct((batch, 2, GDN_HEADS, GDN_DK, GDN_DV), f32)]
    return pl.pallas_call(
        kern,
        grid=(batch,),
        in_specs=in_specs,
        out_specs=out_specs,
        out_shape=out_shape,
        scratch_shapes=[pltpu.VMEM((width // LANES, seq + 16, LANES), f32),
                        pltpu.VMEM((width // LANES, seq, LANES), f32),
                        pltpu.VMEM((seq, LANES), f32),
                        pltpu.VMEM((seq, LANES), f32),
                        pltpu.VMEM((2 * GDN_HEADS, GDN_DK, GDN_DV), f32),
                        pltpu.VMEM((nchunk * 2 * GDN_HEADS * 2 * GDN_CHUNK, LANES), bf16),
                        pltpu.VMEM((nchunk * GDN_HEADS * 3 * GDN_CHUNK, LANES), bf16),
                        pltpu.VMEM((nchunk * 2 * GDN_HEADS * GDN_CHUNK, LANES), f32),
                        pltpu.VMEM((nchunk * 2 * GDN_HEADS * 8, LANES), f32),
                        pltpu.VMEM((GDN_SOLVE_CHUNKS * 2 * GDN_HEADS * GDN_CHUNK, 2 * LANES), bf16)],
        compiler_params=_cparams(1),
        name="gdn_lat" if has_state else "gdn_ctx",
    )(*args)


def _merge_kernel(x_ref, mod_ref, oa_ref, ob_ref, za_ref, zb_ref, cu_ref, cv_ref, zc_ref, gl_ref,
                  lng_ref, lnb_ref, ws_ref, bs_ref, wbr_ref, wo_ref, out_ref, sv_s):
    tm = x_ref.shape[0]
    u = jax.nn.gelu(cu_ref[...].astype(f32))
    vf = jax.nn.gelu(cv_ref[...].astype(f32))
    mu = jnp.mean(vf, axis=-1, keepdims=True)
    vc = vf - mu
    var = jnp.mean(vc * vc, axis=-1, keepdims=True)
    vn = (vc * lax.rsqrt(var + EPS) * lng_ref[0] + lnb_ref[0]).astype(bf16)
    for ck in range(tm // CM_CHUNK):
        rs = slice(ck * CM_CHUNK, (ck + 1) * CM_CHUNK)
        for g in range(CM_GROUPS):
            ls = slice(g * LANES, (g + 1) * LANES)
            sv_s[rs, ls] = _dot(ws_ref[g], vn[rs, ls]) + bs_ref[:, ls]
    o_c = u * sv_s[...]

    def silu(z_ref):
        z = z_ref[...].astype(f32)
        return z * jax.nn.sigmoid(z)

    brs = (oa_ref[...] * silu(za_ref), ob_ref[...] * silu(zb_ref), o_c * silu(zc_ref))
    ysum = None
    for n in range(N_BRANCH):
        yb = _dot(brs[n].astype(bf16), wbr_ref[n])
        t = jax.nn.sigmoid(gl_ref[:, n * D_MODEL:(n + 1) * D_MODEL].astype(f32)) * yb
        ysum = t if ysum is None else ysum + t
    y = _dot(ysum.astype(bf16), wo_ref[...])
    gate = mod_ref[0][:, 2 * D_MODEL:]
    out_ref[...] = x_ref[...] + gate * y


def _merge(x, mod_l, proj, o_a, o_b, lng, lnb, ws_b, bs_full, wbr_b, wo_b, l, latent, seq):
    ntok = x.shape[0]
    tm = 256

    def col(off):
        return pl.BlockSpec((tm, 512), lambda t: (t, off // 512))

    def const(*idx):
        return lambda t: idx

    return pl.pallas_call(
        _merge_kernel,
        grid=(ntok // tm,),
        in_specs=[pl.BlockSpec((tm, D_MODEL), lambda t: (t, 0)),
                  pl.BlockSpec((1, 1, 3 * D_MODEL), _mod_row(latent, tm, seq)),
                  pl.BlockSpec((tm, 512), lambda t: (t, 0)),
                  pl.BlockSpec((tm, 512), lambda t: (t, 0)),
                  col(OFF_ZA), col(OFF_ZB), col(OFF_CU), col(OFF_CV), col(OFF_ZC),
                  pl.BlockSpec((tm, N_BRANCH * D_MODEL), lambda t: (t, 0)),
                  pl.BlockSpec((1, 1, CM_WIDTH), const(l, 0, 0)),
                  pl.BlockSpec((1, 1, CM_WIDTH), const(l, 0, 0)),
                  pl.BlockSpec((None, CM_GROUPS, CM_CHUNK, CM_CHUNK), const(l, 0, 0, 0)),
                  pl.BlockSpec((None, CM_CHUNK, CM_WIDTH), const(l, 0, 0)),
                  pl.BlockSpec((None, N_BRANCH, BRANCH_W, D_MODEL), const(l, 0, 0, 0)),
                  pl.BlockSpec((None, D_MODEL, D_MODEL), const(l, 0, 0))],
        out_specs=pl.BlockSpec((tm, D_MODEL), lambda t: (t, 0)),
        out_shape=jax.ShapeDtypeStruct((ntok, D_MODEL), f32),
        scratch_shapes=[pltpu.VMEM((tm, CM_WIDTH), f32)],
        compiler_params=_cparams(1),
        name="merge",
    )(x, mod_l, o_a, o_b, proj, proj, proj, proj, proj, proj, lng, lnb, ws_b, bs_full, wbr_b, wo_b)


def _w_in_moves():
    offs = [0]
    for s in SPLIT_SIZES:
        offs.append(offs[-1] + s)
    cq, ckv, krope, z_a, qkv, ga, gb, z_b, cu, cv, z_c, gl = offs[:-1]
    moves = [(gl, N_BRANCH * D_MODEL, OFF_GL), (z_a, 512, OFF_ZA), (z_b, 512, OFF_ZB), (cu, 512, OFF_CU),
             (cv, 512, OFF_CV), (z_c, 512, OFF_ZC), (qkv, 2 * GDN_KW + GDN_VW, P16_W + OFF_QKV),
             (cq, MLA_Q_RANK, P16_W + OFF_CQS), (ckv, MLA_KV_RANK, P16_W + OFF_CKV)]
    return moves, ga, krope


W_ROW_TILE = 256
W_SMALL_TILE = (P16_W + OFF_SMALL) // W_ROW_TILE


def _w_row_offsets():
    moves, _, _ = _w_in_moves()
    offs = []
    for r in range(PROJ_W // W_ROW_TILE):
        d = r * W_ROW_TILE
        src = [a + d - dst for (a, w, dst) in moves if dst <= d < dst + w]
        offs.append(src[0])
    return np.asarray(offs, np.int32) // SUBLANES


def _wprep_kernel(offs_ref, a_ref, g_ref, k_ref, o_ref):
    t = pl.program_id(1)

    @pl.when(t != W_SMALL_TILE)
    def _():
        o_ref[...] = a_ref[...].astype(bf16)

    @pl.when(t == W_SMALL_TILE)
    def _():
        half = W_ROW_TILE - LANES
        ngate = 4 * GDN_HEADS
        o_ref[0:half, :] = a_ref[0:half, :].astype(bf16)
        o_ref[half:half + ngate, :] = g_ref[...].astype(bf16)
        o_ref[half + ngate:half + ROPE_LANE0, :] = jnp.zeros((ROPE_LANE0 - ngate, D_MODEL), bf16)
        o_ref[half + ROPE_LANE0:half + ROPE_LANE0 + MLA_ROPE, :] = k_ref[...].astype(bf16)
        o_ref[half + ROPE_LANE0 + MLA_ROPE:, :] = jnp.zeros((LANES - ROPE_LANE0 - MLA_ROPE, D_MODEL), bf16)


def _prep_w_in(w_in):
    L = w_in.shape[0]
    w_t = jnp.swapaxes(w_in, 1, 2)
    _, ga, krope = _w_in_moves()
    full = pl.Element(D_MODEL)
    grid_spec = pltpu.PrefetchScalarGridSpec(
        num_scalar_prefetch=1,
        grid=(L, PROJ_W // W_ROW_TILE),
        in_specs=[pl.BlockSpec((None, pl.Element(W_ROW_TILE), full), lambda l, t, offs: (l, offs[t] * SUBLANES, 0)),
                  pl.BlockSpec((None, pl.Element(4 * GDN_HEADS), full), lambda l, t, offs: (l, ga, 0)),
                  pl.BlockSpec((None, pl.Element(MLA_ROPE), full), lambda l, t, offs: (l, krope, 0))],
        out_specs=pl.BlockSpec((None, W_ROW_TILE, D_MODEL), lambda l, t, offs: (l, t, 0)))
    return pl.pallas_call(
        _wprep_kernel,
        grid_spec=grid_spec,
        out_shape=jax.ShapeDtypeStruct((L, PROJ_W, D_MODEL), bf16),
        compiler_params=_cparams(2),
        name="wprep",
    )(jnp.asarray(_w_row_offsets()), w_t, w_t, w_t)


def _pad_last(x, n):
    return jnp.pad(x, [(0, 0)] * (x.ndim - 1) + [(0, n - x.shape[-1])])


def _rope_tables(seq):
    t = np.arange(seq)
    row = (t // GRID_W).astype(np.float32)
    colp = (t % GRID_W).astype(np.float32)
    nf = MLA_ROPE // 4
    inv = (ROPE_THETA ** (-np.arange(nf, dtype=np.float32) / nf)).astype(np.float32)
    cos_t = np.ones((seq, LANES), np.float32)
    s_lo = np.zeros((seq, LANES), np.float32)
    s_hi = np.zeros((seq, LANES), np.float32)
    for i, pos in enumerate((row, colp)):
        ang = (pos[:, None] * inv[None, :]).astype(np.float32)
        cs, sn = np.cos(ang), np.sin(ang)
        lo = ROPE_LANE0 + 2 * nf * i
        cos_t[:, lo:lo + nf] = cs
        cos_t[:, lo + nf:lo + 2 * nf] = cs
        s_lo[:, lo:lo + nf] = -sn
        s_hi[:, lo + nf:lo + 2 * nf] = sn
    return tuple(jnp.asarray(p) for p in (cos_t, s_lo, s_hi))


def kernel(x_prompt, x_sample, cache_ckv, cache_krope, state_gdn, c, c_ctx, norm_g, w_mod, b_mod, w_in, q_a_norm, w_uq, kv_a_norm, w_ukv, q_norm, k_norm, conv_w, a_log, dt_bias, gdn_onorm, cm_ln_g, cm_ln_b, w_s, b_s, w_branch, w_o):
    L = DEPTH
    batch, seq, _ = x_prompt.shape
    dbatch, dseq, _ = x_sample.shape
    past = cache_ckv.shape[2]

    w_in_p = _prep_w_in(w_in)
    wuq_p = _pad_last(w_uq.reshape(L, MLA_Q_RANK, MLA_HEADS, MLA_QK), HEAD_SLOT)
    wuq_p = wuq_p.reshape(L, MLA_Q_RANK, MLA_HEADS * HEAD_SLOT).astype(bf16)
    wukv = w_ukv.reshape(L, MLA_KV_RANK, MLA_HEADS, MLA_NOPE + MLA_V)
    wk_p = _pad_last(wukv[..., :MLA_NOPE], HEAD_SLOT).reshape(L, MLA_KV_RANK, MLA_HEADS * HEAD_SLOT).astype(bf16)
    wv_p = wukv[..., MLA_NOPE:].reshape(L, MLA_KV_RANK, MLA_WIDTH).astype(bf16)
    qn_p = _pad_last(q_norm, HEAD_SLOT).reshape(L, 1, HEAD_SLOT)
    kn_p = _pad_last(k_norm, HEAD_SLOT).reshape(L, 1, HEAD_SLOT)
    attn_w = (q_a_norm.reshape(L, 1, MLA_Q_RANK), wuq_p, kv_a_norm.reshape(L, 1, MLA_KV_RANK),
              wk_p, wv_p, qn_p, kn_p)
    cache_kr_p = jnp.pad(cache_krope, [(0, 0)] * 3 + [(ROPE_LANE0, LANES - ROPE_LANE0 - MLA_ROPE)])
    rope_tabs = _rope_tables(dseq)
    alog_p = _pad_last(a_log.reshape(L, 1, 2 * GDN_HEADS), LANES)
    dtb_p = _pad_last(dt_bias.reshape(L, 1, 2 * GDN_HEADS), LANES)
    onorm = gdn_onorm.reshape(L, 1, GDN_DV)
    conv_w = jnp.swapaxes(conv_w.reshape(L, CONV_W, -1, LANES), 1, 2)
    lng =cm_ln_g.reshape(L, 1, CM_WIDTH)
    lnb = cm_ln_b.reshape(L, 1, CM_WIDTH)
    ws_b = w_s.astype(bf16)
    bs_full = jnp.repeat(jnp.swapaxes(b_s, 1, 2), CM_WIDTH // CM_GROUPS, axis=2)
    wbr_b = w_branch.astype(bf16)
    wo_b = w_o.astype(bf16)
    norm_g3 = norm_g.reshape(L, 1, D_MODEL)

    c8 = jnp.concatenate([c, c_ctx[None, :], jnp.zeros((8 - dbatch - 1, D_MODEL), f32)], axis=0)
    mod = _modulation(c8, w_mod, b_mod)

    yp = x_prompt.reshape(batch * seq, D_MODEL)
    ys = x_sample.reshape(dbatch * dseq, D_MODEL)
    ckvs, kropes, states = [], [], []
    for l in range(L):
        mod_l = mod[l].reshape(8, 1, 3 * D_MODEL)
        p16, p32 = _inproj(yp, mod_l, norm_g3, w_in_p, l, False, seq)
        o_a, ckvn, kr = _attn_ctx(p32, attn_w, l, batch, seq)
        o_b, s_new = _gdn(p32, conv_w, alog_p, dtb_p, onorm, None, l, batch, seq)
        yp = _merge(yp, mod_l, p16, o_a, o_b, lng, lnb, ws_b, bs_full, wbr_b, wo_b, l, False, seq)
        ckvs.append(ckvn.reshape(batch, seq, MLA_KV_RANK))
        kropes.append(kr.reshape(batch, seq, MLA_ROPE))
        states.append(s_new)
        p16, p32 = _inproj(ys, mod_l, norm_g3, w_in_p, l, True, dseq)
        o_a = _attn_lat(p32, cache_ckv, cache_kr_p, rope_tabs, attn_w, l, dbatch, dseq, past)
        o_b = _gdn(p32, conv_w, alog_p, dtb_p, onorm, state_gdn, l, dbatch, dseq)
        ys = _merge(ys, mod_l, p16, o_a, o_b, lng, lnb, ws_b, bs_full, wbr_b, wo_b, l, True, dseq)
    return (yp.reshape(batch, seq, D_MODEL), ys.reshape(dbatch, dseq, D_MODEL),
            jnp.stack(ckvs, axis=1), jnp.stack(kropes, axis=1), jnp.stack(states, axis=1))
```

```python
import functools
import math

import numpy as np

import jax
import jax.numpy as jnp
from jax import lax
from jax.experimental import pallas as pl
from jax.experimental.pallas import tpu as pltpu

D_MODEL = 1024
DEPTH = 2
GRID_W = 64
EPS = 1e-6
MLA_HEADS = 8
MLA_NOPE = 64
MLA_ROPE = 32
MLA_QK = MLA_NOPE + MLA_ROPE
MLA_V = 64
MLA_Q_RANK = 384
MLA_KV_RANK = 256
MLA_WIDTH = MLA_HEADS * MLA_V
ROPE_THETA = 10000.0
GDN_HEADS = 4
GDN_DK = 128
GDN_DV = 128
GDN_KW = GDN_HEADS * GDN_DK
GDN_VW = GDN_HEADS * GDN_DV
GDN_CHUNK = 64
CONV_W = 5
GDN_SOLVE_CHUNKS = 4
CM_GROUPS = 4
CM_CHUNK = 128
CM_WIDTH = 512
N_BRANCH = 3
BRANCH_W = 512
SPLIT_SIZES = (MLA_Q_RANK, MLA_KV_RANK, MLA_ROPE, MLA_WIDTH,
               2 * GDN_KW + GDN_VW, 2 * GDN_HEADS, 2 * GDN_HEADS, GDN_VW,
               CM_WIDTH, CM_WIDTH, CM_WIDTH, N_BRANCH * D_MODEL)

LANES = 128
SUBLANES = 8
HEAD_SLOT = LANES
ROPE_LANE0 = MLA_NOPE

OFF_GL = 0
OFF_ZA = OFF_GL + N_BRANCH * D_MODEL
OFF_ZB = OFF_ZA + 512
OFF_CU = OFF_ZB + 512
OFF_CV = OFF_CU + 512
OFF_ZC = OFF_CV + 512
P16_W = OFF_ZC + 512
OFF_QKV = 0
OFF_CQS = OFF_QKV + 2 * GDN_KW + GDN_VW
OFF_SMALL = OFF_CQS + MLA_Q_RANK
OFF_CKV = OFF_CQS + 512
P32_W = OFF_CKV + MLA_KV_RANK
PROJ_W = P16_W + P32_W

VMEM_LIMIT = 56 * 1024 * 1024

f32 = jnp.float32
bf16 = jnp.bfloat16


def _cparams(n_axes):
    return pltpu.CompilerParams(dimension_semantics=("arbitrary",) * n_axes,
                                vmem_limit_bytes=VMEM_LIMIT)


def _dot(a, b):
    return jnp.dot(a, b, preferred_element_type=f32)


def _dot_nt(a, b):
    return lax.dot_general(a, b, (((1,), (1,)), ((), ())), preferred_element_type=f32)


def _dot_tn(a, b):
    return lax.dot_general(a, b, (((0,), (0,)), ((), ())), preferred_element_type=f32)


def _rms_rows(x, g, n=None):
    n = x.shape[-1] if n is None else n
    ms = jnp.sum(x * x, axis=-1, keepdims=True) * (1.0 / n)
    return x * lax.rsqrt(ms + EPS) * g


def _mod_kernel(c_ref, w_ref, b_ref, o_ref):
    a = c_ref[...]
    a = (a * jax.nn.sigmoid(a)).astype(bf16)
    o_ref[0] = _dot(a, w_ref[0].astype(bf16)) + b_ref[0]


def _modulation(c8, w_mod, b_mod):
    tn = 512
    return pl.pallas_call(
        _mod_kernel,
        grid=(DEPTH, 3 * D_MODEL // tn),
        in_specs=[pl.BlockSpec((8, D_MODEL), lambda l, n: (0, 0)),
                  pl.BlockSpec((1, D_MODEL, tn), lambda l, n: (l, 0, n)),
                  pl.BlockSpec((1, 1, tn), lambda l, n: (l, 0, n))],
        out_specs=pl.BlockSpec((1, 8, tn), lambda l, n: (l, 0, n)),
        out_shape=jax.ShapeDtypeStruct((DEPTH, 8, 3 * D_MODEL), f32),
        compiler_params=_cparams(2),
        name="modulation",
    )(c8, w_mod, b_mod.reshape(DEPTH, 1, 3 * D_MODEL))


W_CHUNK = 512


def _w_in_moves():
    offs = [0]
    for s in SPLIT_SIZES:
        offs.append(offs[-1] + s)
    cq, ckv, krope, z_a, qkv, ga, gb, z_b, cu, cv, z_c, gl = offs[:-1]
    moves = [(gl, N_BRANCH * D_MODEL, OFF_GL), (z_a, 512, OFF_ZA), (z_b, 512, OFF_ZB), (cu, 512, OFF_CU),
             (cv, 512, OFF_CV), (z_c, 512, OFF_ZC), (qkv, 2 * GDN_KW + GDN_VW, P16_W + OFF_QKV),
             (cq, MLA_Q_RANK, P16_W + OFF_CQS), (ckv, MLA_KV_RANK, P16_W + OFF_CKV)]
    return moves, ga, krope


def _w_chunks():
    moves, ga, krope = _w_in_moves()
    chunks = []
    for (a, w, d) in moves:
        for o in range(0, w, W_CHUNK):
            chunks.append((a + o, min(W_CHUNK, w - o), d + o))
    small0 = P16_W + OFF_SMALL
    chunks.append((ga, 4 * GDN_HEADS, small0))
    chunks.append((krope, MLA_ROPE, small0 + ROPE_LANE0))
    return chunks


def _inproj_kernel(xc_ref, xl_ref, mod_ref, g_ref, wt_hbm, o16_ref, o32_ref, w_s, stage, sem, *, layer, nctx):
    t = pl.program_id(0)

    @pl.when(t == 0)
    def _():
        chunks = _w_chunks()

        def copy(j):
            src, n, _ = chunks[j]
            return pltpu.make_async_copy(wt_hbm.at[layer, pl.ds(src, n), :], stage.at[j % 2, pl.ds(0, n), :],
                                         sem.at[j % 2])

        copy(0).start()
        for j, (_, n, dst) in enumerate(chunks):
            if j + 1 < len(chunks):
                copy(j + 1).start()
            copy(j).wait()
            w_s[dst:dst + n, :] = stage[j % 2, 0:n, :].astype(bf16)
        small0 = P16_W + OFF_SMALL
        for lo, hi in ((4 * GDN_HEADS, ROPE_LANE0), (ROPE_LANE0 + MLA_ROPE, LANES)):
            w_s[small0 + lo:small0 + hi, :] = jnp.zeros((hi - lo, D_MODEL), bf16)

    x = jnp.where(t < nctx, xc_ref[...], xl_ref[...])
    m = mod_ref[0]
    shift = m[:, :D_MODEL]
    scale = m[:, D_MODEL:2 * D_MODEL]
    h = (_rms_rows(x, g_ref[0]) * (1.0 + scale) + shift).astype(bf16)
    for a in range(0, P16_W, 512):
        o16_ref[:, a:a + 512] = _dot_nt(h, w_s[a:a + 512, :]).astype(bf16)
    for a in range(0, P32_W, 512):
        b = min(a + 512, P32_W)
        o32_ref[:, a:b] = _dot_nt(h, w_s[P16_W + a:P16_W + b, :])


def _inproj(xc, xl, mod_l, norm_g, w_t, l, dseq):
    tm = 256
    nctx = xc.shape[0] // tm
    nlat = xl.shape[0] // tm
    ntok = xc.shape[0] + xl.shape[0]
    kern = functools.partial(_inproj_kernel, layer=l, nctx=nctx)
    return pl.pallas_call(
        kern,
        grid=(nctx + nlat,),
        in_specs=[pl.BlockSpec((tm, D_MODEL), lambda t: (jnp.minimum(t, nctx - 1), 0)),
                  pl.BlockSpec((tm, D_MODEL), lambda t: (jnp.maximum(t - nctx, 0), 0)),
                  pl.BlockSpec((1, 1, 3 * D_MODEL),
                               lambda t: (jnp.where(t < nctx, 4, ((t - nctx) * tm) // dseq), 0, 0)),
                  pl.BlockSpec((1, 1, D_MODEL), lambda t: (l, 0, 0)),
                  pl.BlockSpec(memory_space=pl.ANY)],
        out_specs=[pl.BlockSpec((tm, P16_W), lambda t: (t, 0)),
                   pl.BlockSpec((tm, P32_W), lambda t: (t, 0))],
        out_shape=[jax.ShapeDtypeStruct((ntok, P16_W), bf16),
                   jax.ShapeDtypeStruct((ntok, P32_W), f32)],
        scratch_shapes=[pltpu.VMEM((PROJ_W, D_MODEL), bf16),
                        pltpu.VMEM((2, W_CHUNK, D_MODEL), f32),
                        pltpu.SemaphoreType.DMA((2,))],
        compiler_params=_cparams(1),
        name="inproj",
    )(xc, xl, mod_l, norm_g, w_t)


def _mod_row(latent, tm, seq):
    if latent:
        return lambda t: ((t * tm) // seq, 0, 0)
    return lambda t: (4, 0, 0)


def _rope(x, cos_t, sin_lo, sin_hi):
    return x * cos_t + pltpu.roll(x, LANES - 8, 1) * sin_lo + pltpu.roll(x, 8, 1) * sin_hi


def _build_kv(ckvn_b, kr, wk_ref, wv_ref, knorm, rope, k_s, v_s, r0):
    n = ckvn_b.shape[0]
    kfull = _dot(ckvn_b, wk_ref[...])
    v_s[r0:r0 + n, :] = _dot(ckvn_b, wv_ref[...]).astype(bf16)
    krg = kr * knorm
    if rope is not None:
        krg = _rope(krg, *rope)
    kr_ss = jnp.sum(kr * kr, axis=-1, keepdims=True)
    for h in range(MLA_HEADS):
        sl = slice(h * HEAD_SLOT, (h + 1) * HEAD_SLOT)
        kn = kfull[:, sl]
        ms = (jnp.sum(kn * kn, axis=-1, keepdims=True) + kr_ss) * (1.0 / MLA_QK)
        k_s[r0:r0 + n, sl] = ((kn * knorm + krg) * lax.rsqrt(ms + EPS)).astype(bf16)


def _rope_lane_mask(shape):
    lane = lax.broadcasted_iota(jnp.int32, shape, 1)
    return (lane >= ROPE_LANE0) & (lane < ROPE_LANE0 + MLA_ROPE)


def _attend_block(qa_b, wuq_ref, qnorm, rope, k_s, v_s, o_ref):
    tq = qa_b.shape[0]
    qfull = _dot(qa_b, wuq_ref[...])
    qgain = qnorm * (math.log2(math.e) / math.sqrt(MLA_QK))
    lane = lax.broadcasted_iota(jnp.int32, (tq, LANES), 1)

    def scores(h):
        sl = slice(h * HEAD_SLOT, (h + 1) * HEAD_SLOT)
        qh = _rms_rows(qfull[:, sl], qgain, n=MLA_QK)
        if rope is not None:
            qh = _rope(qh, *rope)
        return _dot_nt(qh.astype(bf16), k_s[:, sl])

    s_next = scores(0)
    outs = []
    for h in range(MLA_HEADS):
        s = s_next
        if h + 1 < MLA_HEADS:
            s_next = scores(h + 1)
        p = jnp.exp2(s - jnp.max(s, axis=-1, keepdims=True))
        den = jnp.sum(p, axis=-1, keepdims=True)
        hp = h // 2
        outs.append(_dot(p.astype(bf16), v_s[:, hp * LANES:(hp + 1) * LANES]) / den)
        if h % 2 == 1:
            o_ref[:, hp * LANES:(hp + 1) * LANES] = jnp.where(lane < MLA_V, outs[h - 1], outs[h])


def _attn_ctx_kernel(cqs_ref, ckv_ref, qan_ref, wuq_ref, kvn_ref, wk_ref, wv_ref, qn_ref, kn_ref,
                     o_ref, ckvn_ref, kr_ref, k_s, v_s):
    small = cqs_ref[:, MLA_Q_RANK:]
    kr = jnp.where(_rope_lane_mask(small.shape), small, 0.0)
    kr_ref[...] = small[:, ROPE_LANE0:ROPE_LANE0 + MLA_ROPE]
    ckvn = _rms_rows(ckv_ref[...], kvn_ref[0])
    ckvn_ref[...] = ckvn
    _build_kv(ckvn.astype(bf16), kr, wk_ref, wv_ref, kn_ref[0], None, k_s, v_s, 0)
    qa = _rms_rows(cqs_ref[:, :MLA_Q_RANK], qan_ref[0]).astype(bf16)
    _attend_block(qa, wuq_ref, qn_ref[0], None, k_s, v_s, o_ref)


def _attn_lat_kernel(cqs_ref, ckv_ref, cckv_ref, ckr_ref, cos_ref, slo_ref, shi_ref,
                     qan_ref, wuq_ref, kvn_ref, wk_ref, wv_ref, qn_ref, kn_ref,
                     o_ref, k_s, v_s, *, seq, past, tq):
    qi = pl.program_id(1)
    rb = 256

    @pl.when(qi == 0)
    def _():
        _build_kv(cckv_ref[...].astype(bf16), ckr_ref[...], wk_ref, wv_ref, kn_ref[0], None, k_s, v_s, 0)
        for r in range(seq // rb):
            rs = slice(r * rb, (r + 1) * rb)
            small = cqs_ref[rs, MLA_Q_RANK:]
            kr = jnp.where(_rope_lane_mask(small.shape), small, 0.0)
            ckvn = _rms_rows(ckv_ref[rs, :], kvn_ref[0])
            rope = (cos_ref[rs, :], slo_ref[rs, :], shi_ref[rs, :])
            _build_kv(ckvn.astype(bf16), kr, wk_ref, wv_ref, kn_ref[0], rope, k_s, v_s, past + r * rb)

    rows = pl.ds(pl.multiple_of(qi * tq, tq), tq)
    qa = _rms_rows(cqs_ref[rows, :MLA_Q_RANK], qan_ref[0]).astype(bf16)
    rope = (cos_ref[rows, :], slo_ref[rows, :], shi_ref[rows, :])
    _attend_block(qa, wuq_ref, qn_ref[0], rope, k_s, v_s, o_ref)


def _attn_weight_specs(l, nidx):
    z = (0,) * (nidx - 1)

    def const(*idx):
        return lambda *g: idx

    return [pl.BlockSpec((1, 1, MLA_Q_RANK), const(l, 0, 0)),
            pl.BlockSpec((None, MLA_Q_RANK, MLA_HEADS * HEAD_SLOT), const(l, 0, 0)),
            pl.BlockSpec((1, 1, MLA_KV_RANK), const(l, 0, 0)),
            pl.BlockSpec((None, MLA_KV_RANK, MLA_HEADS * HEAD_SLOT), const(l, 0, 0)),
            pl.BlockSpec((None, MLA_KV_RANK, MLA_WIDTH), const(l, 0, 0)),
            pl.BlockSpec((1, 1, HEAD_SLOT), const(l, 0, 0)),
            pl.BlockSpec((1, 1, HEAD_SLOT), const(l, 0, 0))]


def _attn_ctx(proj, wts, l, batch, seq):
    ntok = batch * seq
    return pl.pallas_call(
        _attn_ctx_kernel,
        grid=(batch,),
        in_specs=[pl.BlockSpec((seq, 512), lambda b: (b, OFF_CQS // 512)),
                  pl.BlockSpec((seq, MLA_KV_RANK), lambda b: (b, OFF_CKV // MLA_KV_RANK))]
        + _attn_weight_specs(l, 1),
        out_specs=[pl.BlockSpec((seq, MLA_WIDTH), lambda b: (b, 0)),
                   pl.BlockSpec((seq, MLA_KV_RANK), lambda b: (b, 0)),
                   pl.BlockSpec((seq, MLA_ROPE), lambda b: (b, 0))],
        out_shape=[jax.ShapeDtypeStruct((ntok, MLA_WIDTH), f32),
                   jax.ShapeDtypeStruct((ntok, MLA_KV_RANK), f32),
                   jax.ShapeDtypeStruct((ntok, MLA_ROPE), f32)],
        scratch_shapes=[pltpu.VMEM((seq, MLA_HEADS * HEAD_SLOT), bf16),
                        pltpu.VMEM((seq, MLA_WIDTH), bf16)],
        compiler_params=_cparams(1),
        name="attn_ctx",
    )(proj, proj, *wts)


def _attn_lat(proj, row0, cache_ckv, cache_kr_p, rope_tabs, wts, l, batch, seq, past):
    ntok = batch * seq
    tq = 256
    nq = seq // tq
    kern = functools.partial(_attn_lat_kernel, seq=seq, past=past, tq=tq)
    tab = pl.BlockSpec((seq, LANES), lambda b, q: (0, 0))
    return pl.pallas_call(
        kern,
        grid=(batch, nq),
        in_specs=[pl.BlockSpec((seq, 512), lambda b, q: (row0 // seq + b, OFF_CQS // 512)),
                  pl.BlockSpec((seq, MLA_KV_RANK), lambda b, q: (row0 // seq + b, OFF_CKV // MLA_KV_RANK)),
                  pl.BlockSpec((None, None, past, MLA_KV_RANK), lambda b, q: (b, l, 0, 0)),
                  pl.BlockSpec((None, None, past, LANES), lambda b, q: (b, l, 0, 0)),
                  tab, tab, tab]
        + _attn_weight_specs(l, 2),
        out_specs=pl.BlockSpec((tq, MLA_WIDTH), lambda b, q: (b * nq + q, 0)),
        out_shape=jax.ShapeDtypeStruct((ntok, MLA_WIDTH), f32),
        scratch_shapes=[pltpu.VMEM((past + seq, MLA_HEADS * HEAD_SLOT), bf16),
                        pltpu.VMEM((past + seq, MLA_WIDTH), bf16)],
        compiler_params=_cparams(2),
        name="attn_lat",
    )(proj, proj, cache_ckv, cache_kr_p, *rope_tabs, *wts)


def _split3(x):
    hi = x.astype(bf16)
    r1 = x - hi.astype(f32)
    mid = r1.astype(bf16)
    lo = (r1 - mid.astype(f32)).astype(bf16)
    return hi, mid, lo


def _tri_cumsum(tri_b, x):
    hi, mid, lo = _split3(x)
    return _dot(tri_b, hi) + _dot(tri_b, mid) + _dot(tri_b, lo)


def _lane_bcast(x, c):
    return jnp.broadcast_to(x[:, c:c + 1], (x.shape[0], LANES))


def _gdn_kernel(*refs, seq, has_state):
    if has_state:
        (qkv_ref, small_ref, cw_ref, alog_ref, dtb_ref, onorm_ref, s0_ref,
         o_ref, xpad, qkv_s, g_s, b_s, st_s, wq_s, ak_s, u_s, el_s, rhs_s) = refs
        sout_ref = None
    else:
        (qkv_ref, small_ref, cw_ref, alog_ref, dtb_ref, onorm_ref,
         o_ref, sout_ref, xpad, qkv_s, g_s, b_s, st_s, wq_s, ak_s, u_s, el_s, rhs_s) = refs
    C = GDN_CHUNK
    nchunk = seq // C
    H = GDN_HEADS
    width = 2 * GDN_KW + GDN_VW
    halo = 8

    for j in range(width // LANES):
        xpad[j, 0:halo, :] = jnp.zeros((halo, LANES), f32)
        xpad[j, halo + seq:, :] = jnp.zeros((halo, LANES), f32)
        xpad[j, halo:halo + seq, :] = qkv_ref[:, j * LANES:(j + 1) * LANES]
    if has_state:
        for d in range(2):
            for h in range(H):
                st_s[d * H + h] = s0_ref[d, h]
    else:
        st_s[...] = jnp.zeros((2 * H, GDN_DK, GDN_DV), f32)

    neg_a = -jnp.exp(alog_ref[0])
    dtb = dtb_ref[0]

    def conv_tile(j, l2norm):
        w = cw_ref[j]
        post = jnp.where(j < H, GDN_DK ** -0.5, 1.0)
        for c in range(nchunk):
            base = halo - CONV_W // 2 + c * C
            y = xpad[j, base:base + C, :] * w[0:1]
            for tap in range(1, CONV_W):
                y = y + xpad[j, base + tap:base + tap + C, :] * w[tap:tap + 1]
            y = y * jax.nn.sigmoid(y)
            if l2norm:
                y = y * (lax.rsqrt(jnp.sum(y * y, axis=-1, keepdims=True) + EPS) * post)
            qkv_s[j, c * C:(c + 1) * C, :] = y

    def conv_qk(j, carry):
        conv_tile(j, True)
        return carry

    def conv_v(j, carry):
        conv_tile(j, False)
        return carry

    tile_unroll = 2 if nchunk <= 4 else 1
    lax.fori_loop(0, 2 * H, conv_qk, 0, unroll=tile_unroll)
    lax.fori_loop(2 * H, 3 * H, conv_v, 0, unroll=tile_unroll)

    def prep(c, carry):
        r0 = pl.multiple_of(c * C, C)
        sm = small_ref[pl.ds(r0, C), :]
        z = sm + dtb
        g_s[pl.ds(r0, C), :] = neg_a * (jnp.maximum(z, 0.0) + jnp.log1p(jnp.exp(-jnp.abs(z))))
        b_s[pl.ds(r0, C), :] = pltpu.roll(jax.nn.sigmoid(sm), LANES - 2 * H, 1)
        return carry

    lax.fori_loop(0, nchunk, prep, 0, unroll=4)

    ri = lax.broadcasted_iota(jnp.int32, (C, LANES), 0)
    cl = lax.broadcasted_iota(jnp.int32, (C, LANES), 1)
    fwd = cl < C
    cj = cl & (C - 1)
    eye2 = (ri == cj).astype(f32)
    incl2 = (fwd & (ri >= cj)) | (~fwd & (ri <= cj))
    strict2 = (fwd & (ri > cj)) | (~fwd & (ri < cj))
    xor = ri ^ cj
    level2 = sum((xor >= (1 << b)).astype(jnp.int32) for b in range(C.bit_length() - 1))
    lvl_top = jnp.where(fwd, level2, 0)
    lvl_bot = jnp.where(fwd, 0, level2)
    r2 = lax.broadcasted_iota(jnp.int32, (2 * C, C), 0)
    c2 = lax.broadcasted_iota(jnp.int32, (2 * C, C), 1)
    tri2 = (((r2 < C) & (r2 >= c2)) | ((r2 >= C) & (r2 - C <= c2))).astype(bf16)
    zrhs = jnp.zeros((C, 2 * LANES), bf16)
    zvn = jnp.zeros((C, LANES), bf16)
    fwd_row = fwd[0:1, :]
    cpi = GDN_SOLVE_CHUNKS

    def block_diag(x):
        return jnp.concatenate([jnp.where(fwd, x, 0.0), jnp.where(fwd, 0.0, x)], axis=0).astype(bf16)

    def solve_phase(i, carry):
        chains = []
        for cc in range(cpi):
            c = i * cpi + cc
            rows = pl.ds(pl.multiple_of(c * C, C), C)
            g2 = _tri_cumsum(tri2, g_s[rows, :])
            g2t = g2.T
            bt = b_s[rows, :]
            for h in range(H):
                chains.append((cc, c, h, rows, g2, g2t, bt))

        a2s, t2s = [], []
        for (cc, c, h, rows, g2, g2t, bt) in chains:
            q = qkv_s[h, rows, :]
            k = qkv_s[H + h, rows, :]
            v = qkv_s[2 * H + h, rows, :]
            kq = _dot_nt(jnp.concatenate([k, q], axis=0).astype(bf16),
                         jnp.concatenate([k, k], axis=0).astype(bf16))
            gcc_f = _lane_bcast(g2[:C], h)
            gcc_b = _lane_bcast(g2[C:], H + h)
            btc_f = _lane_bcast(bt, h)
            btc_b = _lane_bcast(bt, H + h)
            grow = jnp.where(fwd_row, g2t[h:h + 1, :], g2t[H + h:H + h + 1, :])
            diff = jnp.where(fwd, gcc_f, gcc_b) - grow
            dec = jnp.where(incl2, jnp.exp(jnp.where(incl2, diff, 0.0)), 0.0)
            a2 = jnp.where(strict2, jnp.where(fwd, btc_f, btc_b) * kq[:C] * dec, 0.0)
            a2s.append(a2)
            t2s.append(eye2 - jnp.where(level2 == 1, a2, 0.0))
            glast_f = gcc_f[C - 1:C, :]
            glast_b = gcc_b[0:1, :]
            e1_f = jnp.exp(gcc_f)
            e1_b = jnp.exp(gcc_b)
            ket = jnp.concatenate([k * jnp.exp(glast_f - gcc_f), k * jnp.exp(glast_b - gcc_b)], axis=0).T
            ak_s[pl.ds(pl.multiple_of((c * H + h) * 3 * C, 3 * C), 3 * C), :] = jnp.concatenate(
                [kq[C:] * dec, ket], axis=0).astype(bf16)
            for d, (btc, e1, glast) in enumerate(((btc_f, e1_f, glast_f), (btc_b, e1_b, glast_b))):
                ch = d * H + h
                j = (cc * H + h) * 2 + d
                rhs_s[j * C:(j + 1) * C, :] = jnp.concatenate([v * btc, k * (btc * e1)], axis=1).astype(bf16)
                wq_s[pl.ds(pl.multiple_of((c * 2 * H + ch) * 2 * C + C, C), C), :] = (q * e1).astype(bf16)
                el_s[pl.ds(pl.multiple_of((c * 2 * H + ch) * 8, 8), 8), :] = jnp.broadcast_to(
                    jnp.exp(glast), (8, LANES))

        for lv in range(2, C.bit_length()):
            rs = []
            for a2, t2 in zip(a2s, t2s):
                abd = jnp.concatenate([jnp.where(lvl_top == lv, a2, 0.0),
                                       jnp.where(lvl_bot == lv, a2, 0.0)], axis=0).astype(bf16)
                rs.append(_dot(t2.astype(bf16), abd))
            t2s = [t2 - _dot(r.astype(bf16), block_diag(t2)) for r, t2 in zip(rs, t2s)]

        for (cc, c, h, rows, g2, g2t, bt), t2 in zip(chains, t2s):
            t2b = t2.astype(bf16)
            for d in range(2):
                ch = d * H + h
                j = (cc * H + h) * 2 + d
                rhs = rhs_s[j * C:(j + 1) * C, :]
                rhs = jnp.concatenate([rhs, zrhs] if d == 0 else [zrhs, rhs], axis=0)
                uw = _dot(t2b, rhs)
                u_s[pl.ds(pl.multiple_of((c * 2 * H + ch) * C, C), C), :] = uw[:, :LANES]
                wq_s[pl.ds(pl.multiple_of((c * 2 * H + ch) * 2 * C, C), C), :] = uw[:, LANES:].astype(bf16)
        return carry

    lax.fori_loop(0, nchunk // cpi, solve_phase, 0)

    def scan_phase(i, carry):
        cs = [i if ch < H else nchunk - 1 - i for ch in range(2 * H)]
        s_old = [st_s[ch] for ch in range(2 * H)]
        r1 = [_dot(wq_s[pl.ds(pl.multiple_of((cs[ch] * 2 * H + ch) * 2 * C, 2 * C), 2 * C), :],
                   s_old[ch].astype(bf16)) for ch in range(2 * H)]
        r2s = []
        for ch in range(2 * H):
            u = u_s[pl.ds(pl.multiple_of((cs[ch] * 2 * H + ch) * C, C), C), :]
            vnb = (u - r1[ch][:C]).astype(bf16)
            rhs = jnp.concatenate([vnb, zvn] if ch < H else [zvn, vnb], axis=0)
            ak = ak_s[pl.ds(pl.multiple_of((cs[ch] * H + ch % H) * 3 * C, 3 * C), 3 * C), :]
            r2s.append(_dot(ak, rhs))
        for ch in range(2 * H):
            el = el_s[pl.ds(pl.multiple_of((cs[ch] * 2 * H + ch) * 8, 8), 8), :][0:1, :]
            st_s[ch] = s_old[ch] * el + r2s[ch][C:]
            xpad[ch, pl.ds(pl.multiple_of(cs[ch] * C, C), C), :] = r1[ch][C:] + r2s[ch][:C]
        return carry

    lax.fori_loop(0, nchunk, scan_phase, 0, unroll=4)

    onorm = onorm_ref[0]

    def fin(c, carry):
        rows = pl.ds(pl.multiple_of(c * C, C), C)
        for h in range(H):
            ls = slice(h * LANES, (h + 1) * LANES)
            o_ref[rows, ls] = _rms_rows(xpad[h, rows, :] + xpad[H + h, rows, :], onorm)
        return carry

    lax.fori_loop(0, nchunk, fin, 0, unroll=4)
    if sout_ref is not None:
        for d in range(2):
            for h in range(H):
                sout_ref[d, h] = st_s[d * H + h]


def _gdn(proj, row0, conv_w, alog_p, dtb_p, onorm, state, l, batch, seq):
    ntok = batch * seq
    width = 2 * GDN_KW + GDN_VW
    has_state = state is not None
    nchunk = seq // GDN_CHUNK
    kern = functools.partial(_gdn_kernel, seq=seq, has_state=has_state)
    in_specs = [pl.BlockSpec((seq, width), lambda b: (row0 // seq + b, OFF_QKV // width)),
                pl.BlockSpec((seq, LANES), lambda b: (row0 // seq + b, OFF_SMALL // LANES)),
                pl.BlockSpec((None, width // LANES, CONV_W, LANES), lambda b: (l, 0, 0, 0)),
                pl.BlockSpec((1, 1, LANES), lambda b: (l, 0, 0)),
                pl.BlockSpec((1, 1, LANES), lambda b: (l, 0, 0)),
                pl.BlockSpec((1, 1, GDN_DV), lambda b: (l, 0, 0))]
    args = [proj, proj, conv_w, alog_p, dtb_p, onorm]
    o_spec = pl.BlockSpec((seq, GDN_VW), lambda b: (b, 0))
    o_shape = jax.ShapeDtypeStruct((ntok, GDN_VW), f32)
    st_block = (None, None, 2, GDN_HEADS, GDN_DK, GDN_DV)
    if has_state:
        in_specs.append(pl.BlockSpec(st_block, lambda b: (b, l, 0, 0, 0, 0)))
        args.append(state)
        out_specs, out_shape = o_spec, o_shape
    else:
        out_specs = [o_spec, pl.BlockSpec(st_block[1:], lambda b: (b, 0, 0, 0, 0))]
        out_shape = [o_shape, jax.ShapeDtypeStruct((batch, 2, GDN_HEADS, GDN_DK, GDN_DV), f32)]
    return pl.pallas_call(
        kern,
        grid=(batch,),
        in_specs=in_specs,
        out_specs=out_specs,
        out_shape=out_shape,
        scratch_shapes=[pltpu.VMEM((width // LANES, seq + 16, LANES), f32),
                        pltpu.VMEM((width // LANES, seq, LANES), f32),
                        pltpu.VMEM((seq, LANES), f32),
                        pltpu.VMEM((seq, LANES), f32),
                        pltpu.VMEM((2 * GDN_HEADS, GDN_DK, GDN_DV), f32),
                        pltpu.VMEM((nchunk * 2 * GDN_HEADS * 2 * GDN_CHUNK, LANES), bf16),
                        pltpu.VMEM((nchunk * GDN_HEADS * 3 * GDN_CHUNK, LANES), bf16),
                        pltpu.VMEM((nchunk * 2 * GDN_HEADS * GDN_CHUNK, LANES), f32),
                        pltpu.VMEM((nchunk * 2 * GDN_HEADS * 8, LANES), f32),
                        pltpu.VMEM((GDN_SOLVE_CHUNKS * 2 * GDN_HEADS * GDN_CHUNK, 2 * LANES), bf16)],
        compiler_params=_cparams(1),
        name="gdn_lat" if has_state else "gdn_ctx",
    )(*args)


def _merge_kernel(x_ref, mod_ref, oa_ref, ob_ref, za_ref, zb_ref, cu_ref, cv_ref, zc_ref, gl_ref,
                  lng_ref, lnb_ref, ws_ref, bs_ref, wbr_ref, wo_ref, out_ref, sv_s):
    tm = x_ref.shape[0]
    u = jax.nn.gelu(cu_ref[...].astype(f32))
    vf = jax.nn.gelu(cv_ref[...].astype(f32))
    mu = jnp.mean(vf, axis=-1, keepdims=True)
    vc = vf - mu
    var = jnp.mean(vc * vc, axis=-1, keepdims=True)
    vn = (vc * lax.rsqrt(var + EPS) * lng_ref[0] + lnb_ref[0]).astype(bf16)
    for ck in range(tm // CM_CHUNK):
        rs = slice(ck * CM_CHUNK, (ck + 1) * CM_CHUNK)
        for g in range(CM_GROUPS):
            ls = slice(g * LANES, (g + 1) * LANES)
            sv_s[rs, ls] = _dot(ws_ref[g], vn[rs, ls]) + bs_ref[:, ls]
    o_c = u * sv_s[...]

    def silu(z_ref):
        z = z_ref[...].astype(f32)
        return z * jax.nn.sigmoid(z)

    brs = (oa_ref[...] * silu(za_ref), ob_ref[...] * silu(zb_ref), o_c * silu(zc_ref))
    ysum = None
    for n in range(N_BRANCH):
        yb = _dot(brs[n].astype(bf16), wbr_ref[n])
        t = jax.nn.sigmoid(gl_ref[:, n * D_MODEL:(n + 1) * D_MODEL].astype(f32)) * yb
        ysum = t if ysum is None else ysum + t
    y = _dot(ysum.astype(bf16), wo_ref[...])
    gate = mod_ref[0][:, 2 * D_MODEL:]
    out_ref[...] = x_ref[...] + gate * y


def _merge(x, mod_l, proj, row0, o_a, o_b, lng, lnb, ws_b, bs_full, wbr_b, wo_b, l, latent, seq):
    ntok = x.shape[0]
    tm = 256

    def col(off):
        return pl.BlockSpec((tm, 512), lambda t: (row0 // tm + t, off // 512))

    def const(*idx):
        return lambda t: idx

    return pl.pallas_call(
        _merge_kernel,
        grid=(ntok // tm,),
        in_specs=[pl.BlockSpec((tm, D_MODEL), lambda t: (t, 0)),
                  pl.BlockSpec((1, 1, 3 * D_MODEL), _mod_row(latent, tm, seq)),
                  pl.BlockSpec((tm, 512), lambda t: (t, 0)),
                  pl.BlockSpec((tm, 512), lambda t: (t, 0)),
                  col(OFF_ZA), col(OFF_ZB), col(OFF_CU), col(OFF_CV), col(OFF_ZC),
                  pl.BlockSpec((tm, N_BRANCH * D_MODEL), lambda t: (row0 // tm + t, 0)),
                  pl.BlockSpec((1, 1, CM_WIDTH), const(l, 0, 0)),
                  pl.BlockSpec((1, 1, CM_WIDTH), const(l, 0, 0)),
                  pl.BlockSpec((None, CM_GROUPS, CM_CHUNK, CM_CHUNK), const(l, 0, 0, 0)),
                  pl.BlockSpec((None, CM_CHUNK, CM_WIDTH), const(l, 0, 0)),
                  pl.BlockSpec((None, N_BRANCH, BRANCH_W, D_MODEL), const(l, 0, 0, 0)),
                  pl.BlockSpec((None, D_MODEL, D_MODEL), const(l, 0, 0))],
        out_specs=pl.BlockSpec((tm, D_MODEL), lambda t: (t, 0)),
        out_shape=jax.ShapeDtypeStruct((ntok, D_MODEL), f32),
        scratch_shapes=[pltpu.VMEM((tm, CM_WIDTH), f32)],
        compiler_params=_cparams(1),
        name="merge",
    )(x, mod_l, o_a, o_b, proj, proj, proj, proj, proj, proj, lng, lnb, ws_b, bs_full, wbr_b, wo_b)


def _pad_last(x, n):
    return jnp.pad(x, [(0, 0)] * (x.ndim - 1) + [(0, n - x.shape[-1])])


def _rope_tables(seq):
    t = np.arange(seq)
    row = (t // GRID_W).astype(np.float32)
    colp = (t % GRID_W).astype(np.float32)
    nf = MLA_ROPE // 4
    inv = (ROPE_THETA ** (-np.arange(nf, dtype=np.float32) / nf)).astype(np.float32)
    cos_t = np.ones((seq, LANES), np.float32)
    s_lo = np.zeros((seq, LANES), np.float32)
    s_hi = np.zeros((seq, LANES), np.float32)
    for i, pos in enumerate((row, colp)):
        ang = (pos[:, None] * inv[None, :]).astype(np.float32)
        cs, sn = np.cos(ang), np.sin(ang)
        lo = ROPE_LANE0 + 2 * nf * i
        cos_t[:, lo:lo + nf] = cs
        cos_t[:, lo + nf:lo + 2 * nf] = cs
        s_lo[:, lo:lo + nf] = -sn
        s_hi[:, lo + nf:lo + 2 * nf] = sn
    return tuple(jnp.asarray(p) for p in (cos_t, s_lo, s_hi))


def kernel(x_prompt, x_sample, cache_ckv, cache_krope, state_gdn, c, c_ctx, norm_g, w_mod, b_mod, w_in, q_a_norm, w_uq, kv_a_norm, w_ukv, q_norm, k_norm, conv_w, a_log, dt_bias, gdn_onorm, cm_ln_g, cm_ln_b, w_s, b_s, w_branch, w_o):
    L = DEPTH
    batch, seq, _ = x_prompt.shape
    dbatch, dseq, _ = x_sample.shape
    past = cache_ckv.shape[2]

    w_t = jnp.swapaxes(w_in, 1, 2)
    wuq_p = _pad_last(w_uq.reshape(L, MLA_Q_RANK, MLA_HEADS, MLA_QK), HEAD_SLOT)
    wuq_p = wuq_p.reshape(L, MLA_Q_RANK, MLA_HEADS * HEAD_SLOT).astype(bf16)
    wukv = w_ukv.reshape(L, MLA_KV_RANK, MLA_HEADS, MLA_NOPE + MLA_V)
    wk_p = _pad_last(wukv[..., :MLA_NOPE], HEAD_SLOT).reshape(L, MLA_KV_RANK, MLA_HEADS * HEAD_SLOT).astype(bf16)
    wv_p = wukv[..., MLA_NOPE:].reshape(L, MLA_KV_RANK, MLA_WIDTH).astype(bf16)
    qn_p = _pad_last(q_norm, HEAD_SLOT).reshape(L, 1, HEAD_SLOT)
    kn_p = _pad_last(k_norm, HEAD_SLOT).reshape(L, 1, HEAD_SLOT)
    attn_w = (q_a_norm.reshape(L, 1, MLA_Q_RANK), wuq_p, kv_a_norm.reshape(L, 1, MLA_KV_RANK),
              wk_p, wv_p, qn_p, kn_p)
    cache_kr_p = jnp.pad(cache_krope, [(0, 0)] * 3 + [(ROPE_LANE0, LANES - ROPE_LANE0 - MLA_ROPE)])
    rope_tabs = _rope_tables(dseq)
    alog_p = _pad_last(a_log.reshape(L, 1, 2 * GDN_HEADS), LANES)
    dtb_p = _pad_last(dt_bias.reshape(L, 1, 2 * GDN_HEADS), LANES)
    onorm = gdn_onorm.reshape(L, 1, GDN_DV)
    conv_w = jnp.swapaxes(conv_w.reshape(L, CONV_W, -1, LANES), 1, 2)
    lng =cm_ln_g.reshape(L, 1, CM_WIDTH)
    lnb = cm_ln_b.reshape(L, 1, CM_WIDTH)
    ws_b = w_s.astype(bf16)
    bs_full = jnp.repeat(jnp.swapaxes(b_s, 1, 2), CM_WIDTH // CM_GROUPS, axis=2)
    wbr_b = w_branch.astype(bf16)
    wo_b = w_o.astype(bf16)
    norm_g3 = norm_g.reshape(L, 1, D_MODEL)

    c8 = jnp.concatenate([c, c_ctx[None, :], jnp.zeros((8 - dbatch - 1, D_MODEL), f32)], axis=0)
    mod = _modulation(c8, w_mod, b_mod)

    yp = x_prompt.reshape(batch * seq, D_MODEL)
    ys = x_sample.reshape(dbatch * dseq, D_MODEL)
    ckvs, kropes, states = [], [], []
    for l in range(L):
        mod_l = mod[l].reshape(8, 1, 3 * D_MODEL)
        p16, p32 = _inproj(yp, ys, mod_l, norm_g3, w_t, l, dseq)
        lat0 = batch * seq
        o_a, ckvn, kr = _attn_ctx(p32, attn_w, l, batch, seq)
        o_b, s_new = _gdn(p32, 0, conv_w, alog_p, dtb_p, onorm, None, l, batch, seq)
        yp = _merge(yp, mod_l, p16, 0, o_a, o_b, lng, lnb, ws_b, bs_full, wbr_b, wo_b, l, False, seq)
        ckvs.append(ckvn.reshape(batch, seq, MLA_KV_RANK))
        kropes.append(kr.reshape(batch, seq, MLA_ROPE))
        states.append(s_new)
        o_a = _attn_lat(p32, lat0, cache_ckv, cache_kr_p, rope_tabs, attn_w, l, dbatch, dseq, past)
        o_b = _gdn(p32, lat0, conv_w, alog_p, dtb_p, onorm, state_gdn, l, dbatch, dseq)
        ys = _merge(ys, mod_l, p16, lat0, o_a, o_b, lng, lnb, ws_b, bs_full, wbr_b, wo_b, l, True, dseq)
    return (yp.reshape(batch, seq, D_MODEL), ys.reshape(dbatch, dseq, D_MODEL),
            jnp.stack(ckvs, axis=1), jnp.stack(kropes, axis=1), jnp.stack(states, axis=1))
```

```python
import functools
import math

import numpy as np

import jax
import jax.numpy as jnp
from jax import lax
from jax.experimental import pallas as pl
from jax.experimental.pallas import tpu as pltpu

D_MODEL = 1024
DEPTH = 2
GRID_W = 64
EPS = 1e-6
MLA_HEADS = 8
MLA_NOPE = 64
MLA_ROPE = 32
MLA_QK = MLA_NOPE + MLA_ROPE
MLA_V = 64
MLA_Q_RANK = 384
MLA_KV_RANK = 256
MLA_WIDTH = MLA_HEADS * MLA_V
ROPE_THETA = 10000.0
GDN_HEADS = 4
GDN_DK = 128
GDN_DV = 128
GDN_KW = GDN_HEADS * GDN_DK
GDN_VW = GDN_HEADS * GDN_DV
GDN_CHUNK = 64
CONV_W = 5
GDN_SOLVE_CHUNKS = 4
CM_GROUPS = 4
CM_CHUNK = 128
CM_WIDTH = 512
N_BRANCH = 3
BRANCH_W = 512
SPLIT_SIZES = (MLA_Q_RANK, MLA_KV_RANK, MLA_ROPE, MLA_WIDTH,
               2 * GDN_KW + GDN_VW, 2 * GDN_HEADS, 2 * GDN_HEADS, GDN_VW,
               CM_WIDTH, CM_WIDTH, CM_WIDTH, N_BRANCH * D_MODEL)

LANES = 128
SUBLANES = 8
HEAD_SLOT = LANES
ROPE_LANE0 = MLA_NOPE

OFF_GL = 0
OFF_ZA = OFF_GL + N_BRANCH * D_MODEL
OFF_ZB = OFF_ZA + 512
OFF_CU = OFF_ZB + 512
OFF_CV = OFF_CU + 512
OFF_ZC = OFF_CV + 512
P16_W = OFF_ZC + 512
OFF_QKV = 0
OFF_CQS = OFF_QKV + 2 * GDN_KW + GDN_VW
OFF_SMALL = OFF_CQS + MLA_Q_RANK
OFF_CKV = OFF_CQS + 512
P32_W = OFF_CKV + MLA_KV_RANK
PROJ_W = P16_W + P32_W

VMEM_LIMIT = 56 * 1024 * 1024

f32 = jnp.float32
bf16 = jnp.bfloat16


def _cparams(n_axes):
    return pltpu.CompilerParams(dimension_semantics=("arbitrary",) * n_axes,
                                vmem_limit_bytes=VMEM_LIMIT)


def _dot(a, b):
    return jnp.dot(a, b, preferred_element_type=f32)


def _dot_nt(a, b):
    return lax.dot_general(a, b, (((1,), (1,)), ((), ())), preferred_element_type=f32)


def _dot_tn(a, b):
    return lax.dot_general(a, b, (((0,), (0,)), ((), ())), preferred_element_type=f32)


def _rms_rows(x, g, n=None):
    n = x.shape[-1] if n is None else n
    ms = jnp.sum(x * x, axis=-1, keepdims=True) * (1.0 / n)
    return x * lax.rsqrt(ms + EPS) * g


def _mod_kernel(c_ref, w_ref, b_ref, o_ref):
    a = c_ref[...]
    a = (a * jax.nn.sigmoid(a)).astype(bf16)
    o_ref[0] = _dot(a, w_ref[0].astype(bf16)) + b_ref[0]


def _modulation(c8, w_mod, b_mod):
    tn = 1536
    return pl.pallas_call(
        _mod_kernel,
        grid=(DEPTH, 3 * D_MODEL // tn),
        in_specs=[pl.BlockSpec((8, D_MODEL), lambda l, n: (0, 0)),
                  pl.BlockSpec((1, D_MODEL, tn), lambda l, n: (l, 0, n)),
                  pl.BlockSpec((1, 1, tn), lambda l, n: (l, 0, n))],
        out_specs=pl.BlockSpec((1, 8, tn), lambda l, n: (l, 0, n)),
        out_shape=jax.ShapeDtypeStruct((DEPTH, 8, 3 * D_MODEL), f32),
        compiler_params=_cparams(2),
        name="modulation",
    )(c8, w_mod, b_mod.reshape(DEPTH, 1, 3 * D_MODEL))


W_CHUNK = 512


def _w_in_moves():
    offs = [0]
    for s in SPLIT_SIZES:
        offs.append(offs[-1] + s)
    cq, ckv, krope, z_a, qkv, ga, gb, z_b, cu, cv, z_c, gl = offs[:-1]
    moves = [(gl, N_BRANCH * D_MODEL, OFF_GL), (z_a, 512, OFF_ZA), (z_b, 512, OFF_ZB), (cu, 512, OFF_CU),
             (cv, 512, OFF_CV), (z_c, 512, OFF_ZC), (qkv, 2 * GDN_KW + GDN_VW, P16_W + OFF_QKV),
             (cq, MLA_Q_RANK, P16_W + OFF_CQS), (ckv, MLA_KV_RANK, P16_W + OFF_CKV)]
    return moves, ga, krope


def _w_chunks():
    moves, ga, krope = _w_in_moves()
    chunks = []
    for (a, w, d) in moves:
        for o in range(0, w, W_CHUNK):
            chunks.append((a + o, min(W_CHUNK, w - o), d + o))
    small0 = P16_W + OFF_SMALL
    chunks.append((ga, 4 * GDN_HEADS, small0))
    chunks.append((krope, MLA_ROPE, small0 + ROPE_LANE0))
    return chunks


def _inproj_kernel(xc_ref, xl_ref, mod_ref, g_ref, wt_hbm, o16_ref, o32_ref, w_s, stage, sem, *, layer, nctx):
    t = pl.program_id(0)

    @pl.when(t == 0)
    def _():
        chunks = _w_chunks()

        def copy(j):
            src, n, _ = chunks[j]
            return pltpu.make_async_copy(wt_hbm.at[layer, pl.ds(src, n), :], stage.at[j % 2, pl.ds(0, n), :],
                                         sem.at[j % 2])

        copy(0).start()
        for j, (_, n, dst) in enumerate(chunks):
            if j + 1 < len(chunks):
                copy(j + 1).start()
            copy(j).wait()
            w_s[dst:dst + n, :] = stage[j % 2, 0:n, :].astype(bf16)
        small0 = P16_W + OFF_SMALL
        for lo, hi in ((4 * GDN_HEADS, ROPE_LANE0), (ROPE_LANE0 + MLA_ROPE, LANES)):
            w_s[small0 + lo:small0 + hi, :] = jnp.zeros((hi - lo, D_MODEL), bf16)

    x = jnp.where(t < nctx, xc_ref[...], xl_ref[...])
    m = mod_ref[0]
    shift = m[:, :D_MODEL]
    scale = m[:, D_MODEL:2 * D_MODEL]
    h = (_rms_rows(x, g_ref[0]) * (1.0 + scale) + shift).astype(bf16)
    for a in range(0, P16_W, 512):
        o16_ref[:, a:a + 512] = _dot_nt(h, w_s[a:a + 512, :]).astype(bf16)
    for a in range(0, P32_W, 512):
        b = min(a + 512, P32_W)
        o32_ref[:, a:b] = _dot_nt(h, w_s[P16_W + a:P16_W + b, :])


def _inproj(xc, xl, mod_l, norm_g, w_t, l, dseq):
    tm = 512
    nctx = xc.shape[0] // tm
    nlat = xl.shape[0] // tm
    ntok = xc.shape[0] + xl.shape[0]
    kern = functools.partial(_inproj_kernel, layer=l, nctx=nctx)
    return pl.pallas_call(
        kern,
        grid=(nctx + nlat,),
        in_specs=[pl.BlockSpec((tm, D_MODEL), lambda t: (jnp.minimum(t, nctx - 1), 0)),
                  pl.BlockSpec((tm, D_MODEL), lambda t: (jnp.maximum(t - nctx, 0), 0)),
                  pl.BlockSpec((1, 1, 3 * D_MODEL),
                               lambda t: (jnp.where(t < nctx, 4, ((t - nctx) * tm) // dseq), 0, 0)),
                  pl.BlockSpec((1, 1, D_MODEL), lambda t: (l, 0, 0)),
                  pl.BlockSpec(memory_space=pl.ANY)],
        out_specs=[pl.BlockSpec((tm, P16_W), lambda t: (t, 0)),
                   pl.BlockSpec((tm, P32_W), lambda t: (t, 0))],
        out_shape=[jax.ShapeDtypeStruct((ntok, P16_W), bf16),
                   jax.ShapeDtypeStruct((ntok, P32_W), f32)],
        scratch_shapes=[pltpu.VMEM((PROJ_W, D_MODEL), bf16),
                        pltpu.VMEM((2, W_CHUNK, D_MODEL), f32),
                        pltpu.SemaphoreType.DMA((2,))],
        compiler_params=_cparams(1),
        name="inproj",
    )(xc, xl, mod_l, norm_g, w_t)


def _mod_row(latent, tm, seq):
    if latent:
        return lambda t: ((t * tm) // seq, 0, 0)
    return lambda t: (4, 0, 0)


def _rope(x, cos_t, sin_lo, sin_hi):
    return x * cos_t + pltpu.roll(x, LANES - 8, 1) * sin_lo + pltpu.roll(x, 8, 1) * sin_hi


def _build_kv(ckvn_b, kr, wk_ref, wv_ref, knorm, rope, k_s, v_s, r0):
    n = ckvn_b.shape[0]
    kfull = _dot(ckvn_b, wk_ref[...])
    v_s[r0:r0 + n, :] = _dot(ckvn_b, wv_ref[...]).astype(bf16)
    krg = kr * knorm
    if rope is not None:
        krg = _rope(krg, *rope)
    kr_ss = jnp.sum(kr * kr, axis=-1, keepdims=True)
    for h in range(MLA_HEADS):
        sl = slice(h * HEAD_SLOT, (h + 1) * HEAD_SLOT)
        kn = kfull[:, sl]
        ms = (jnp.sum(kn * kn, axis=-1, keepdims=True) + kr_ss) * (1.0 / MLA_QK)
        k_s[r0:r0 + n, sl] = ((kn * knorm + krg) * lax.rsqrt(ms + EPS)).astype(bf16)


def _rope_lane_mask(shape):
    lane = lax.broadcasted_iota(jnp.int32, shape, 1)
    return (lane >= ROPE_LANE0) & (lane < ROPE_LANE0 + MLA_ROPE)


def _attend_block(qa_b, wuq_ref, qnorm, rope, k_s, v_s, o_ref):
    tq = qa_b.shape[0]
    qfull = _dot(qa_b, wuq_ref[...])
    qgain = qnorm * (math.log2(math.e) / math.sqrt(MLA_QK))
    lane = lax.broadcasted_iota(jnp.int32, (tq, LANES), 1)

    def scores(h):
        sl = slice(h * HEAD_SLOT, (h + 1) * HEAD_SLOT)
        qh = _rms_rows(qfull[:, sl], qgain, n=MLA_QK)
        if rope is not None:
            qh = _rope(qh, *rope)
        return _dot_nt(qh.astype(bf16), k_s[:, sl])

    s_next = scores(0)
    outs = []
    for h in range(MLA_HEADS):
        s = s_next
        if h + 1 < MLA_HEADS:
            s_next = scores(h + 1)
        p = jnp.exp2(s - jnp.max(s, axis=-1, keepdims=True))
        den = jnp.sum(p, axis=-1, keepdims=True)
        hp = h // 2
        outs.append(_dot(p.astype(bf16), v_s[:, hp * LANES:(hp + 1) * LANES]) / den)
        if h % 2 == 1:
            o_ref[:, hp * LANES:(hp + 1) * LANES] = jnp.where(lane < MLA_V, outs[h - 1], outs[h])


def _attn_ctx_kernel(*refs, layer):
    if layer:
        cqs_ref, ckv_ref, pckvn_ref, pkr_ref = refs[:4]
        refs = refs[4:]
    else:
        cqs_ref, ckv_ref = refs[:2]
        refs = refs[2:]
    qan_ref, wuq_ref, kvn_ref, wk_ref, wv_ref, qn_ref, kn_ref, o_ref, ckvn_ref, kr_ref, k_s, v_s = refs
    for i in range(layer):
        ckvn_ref[i] = pckvn_ref[i]
        kr_ref[i] = pkr_ref[i]
    small = cqs_ref[:, MLA_Q_RANK:]
    kr = jnp.where(_rope_lane_mask(small.shape), small, 0.0)
    kr_ref[layer] = small[:, ROPE_LANE0:ROPE_LANE0 + MLA_ROPE]
    ckvn = _rms_rows(ckv_ref[...], kvn_ref[0])
    ckvn_ref[layer] = ckvn
    _build_kv(ckvn.astype(bf16), kr, wk_ref, wv_ref, kn_ref[0], None, k_s, v_s, 0)
    qa = _rms_rows(cqs_ref[:, :MLA_Q_RANK], qan_ref[0]).astype(bf16)
    _attend_block(qa, wuq_ref, qn_ref[0], None, k_s, v_s, o_ref)


def _attn_lat_kernel(cqs_ref, ckv_ref, cckv_ref, ckr_ref, cos_ref, slo_ref, shi_ref,
                     qan_ref, wuq_ref, kvn_ref, wk_ref, wv_ref, qn_ref, kn_ref,
                     o_ref, k_s, v_s, *, seq, past, tq):
    qi = pl.program_id(1)
    rb = 256

    @pl.when(qi == 0)
    def _():
        _build_kv(cckv_ref[...].astype(bf16), ckr_ref[...], wk_ref, wv_ref, kn_ref[0], None, k_s, v_s, 0)
        for r in range(seq // rb):
            rs = slice(r * rb, (r + 1) * rb)
            small = cqs_ref[rs, MLA_Q_RANK:]
            kr = jnp.where(_rope_lane_mask(small.shape), small, 0.0)
            ckvn = _rms_rows(ckv_ref[rs, :], kvn_ref[0])
            rope = (cos_ref[rs, :], slo_ref[rs, :], shi_ref[rs, :])
            _build_kv(ckvn.astype(bf16), kr, wk_ref, wv_ref, kn_ref[0], rope, k_s, v_s, past + r * rb)

    rows = pl.ds(pl.multiple_of(qi * tq, tq), tq)
    qa = _rms_rows(cqs_ref[rows, :MLA_Q_RANK], qan_ref[0]).astype(bf16)
    rope = (cos_ref[rows, :], slo_ref[rows, :], shi_ref[rows, :])
    _attend_block(qa, wuq_ref, qn_ref[0], rope, k_s, v_s, o_ref)


def _attn_weight_specs(l, nidx):
    z = (0,) * (nidx - 1)

    def const(*idx):
        return lambda *g: idx

    return [pl.BlockSpec((1, 1, MLA_Q_RANK), const(l, 0, 0)),
            pl.BlockSpec((None, MLA_Q_RANK, MLA_HEADS * HEAD_SLOT), const(l, 0, 0)),
            pl.BlockSpec((1, 1, MLA_KV_RANK), const(l, 0, 0)),
            pl.BlockSpec((None, MLA_KV_RANK, MLA_HEADS * HEAD_SLOT), const(l, 0, 0)),
            pl.BlockSpec((None, MLA_KV_RANK, MLA_WIDTH), const(l, 0, 0)),
            pl.BlockSpec((1, 1, HEAD_SLOT), const(l, 0, 0)),
            pl.BlockSpec((1, 1, HEAD_SLOT), const(l, 0, 0))]


def _attn_ctx(proj, prev, wts, l, batch, seq):
    ntok = batch * seq

    def stacked(n, width):
        return pl.BlockSpec((None, n, seq, width), lambda b: (b, 0, 0, 0))

    prev_specs = [stacked(l, MLA_KV_RANK), stacked(l, MLA_ROPE)] if l else []
    return pl.pallas_call(
        functools.partial(_attn_ctx_kernel, layer=l),
        grid=(batch,),
        in_specs=[pl.BlockSpec((seq, 512), lambda b: (b, OFF_CQS // 512)),
                  pl.BlockSpec((seq, MLA_KV_RANK), lambda b: (b, OFF_CKV // MLA_KV_RANK))]
        + prev_specs + _attn_weight_specs(l, 1),
        out_specs=[pl.BlockSpec((seq, MLA_WIDTH), lambda b: (b, 0)),
                   stacked(l + 1, MLA_KV_RANK), stacked(l + 1, MLA_ROPE)],
        out_shape=[jax.ShapeDtypeStruct((ntok, MLA_WIDTH), f32),
                   jax.ShapeDtypeStruct((batch, l + 1, seq, MLA_KV_RANK), f32),
                   jax.ShapeDtypeStruct((batch, l + 1, seq, MLA_ROPE), f32)],
        scratch_shapes=[pltpu.VMEM((seq, MLA_HEADS * HEAD_SLOT), bf16),
                        pltpu.VMEM((seq, MLA_WIDTH), bf16)],
        compiler_params=_cparams(1),
        name="attn_ctx",
    )(proj, proj, *prev, *wts)


def _attn_lat(proj, row0, cache_ckv, cache_kr_p, rope_tabs, wts, l, batch, seq, past):
    ntok = batch * seq
    tq = 256
    nq = seq // tq
    kern = functools.partial(_attn_lat_kernel, seq=seq, past=past, tq=tq)
    tab = pl.BlockSpec((seq, LANES), lambda b, q: (0, 0))
    return pl.pallas_call(
        kern,
        grid=(batch, nq),
        in_specs=[pl.BlockSpec((seq, 512), lambda b, q: (row0 // seq + b, OFF_CQS // 512)),
                  pl.BlockSpec((seq, MLA_KV_RANK), lambda b, q: (row0 // seq + b, OFF_CKV // MLA_KV_RANK)),
                  pl.BlockSpec((None, None, past, MLA_KV_RANK), lambda b, q: (b, l, 0, 0)),
                  pl.BlockSpec((None, None, past, LANES), lambda b, q: (b, l, 0, 0)),
                  tab, tab, tab]
        + _attn_weight_specs(l, 2),
        out_specs=pl.BlockSpec((tq, MLA_WIDTH), lambda b, q: (b * nq + q, 0)),
        out_shape=jax.ShapeDtypeStruct((ntok, MLA_WIDTH), f32),
        scratch_shapes=[pltpu.VMEM((past + seq, MLA_HEADS * HEAD_SLOT), bf16),
                        pltpu.VMEM((past + seq, MLA_WIDTH), bf16)],
        compiler_params=_cparams(2),
        name="attn_lat",
    )(proj, proj, cache_ckv, cache_kr_p, *rope_tabs, *wts)


def _split3(x):
    hi = x.astype(bf16)
    r1 = x - hi.astype(f32)
    mid = r1.astype(bf16)
    lo = (r1 - mid.astype(f32)).astype(bf16)
    return hi, mid, lo


def _tri_cumsum(tri_b, x):
    hi, mid, lo = _split3(x)
    return _dot(tri_b, hi) + _dot(tri_b, mid) + _dot(tri_b, lo)


def _lane_bcast(x, c):
    return jnp.broadcast_to(x[:, c:c + 1], (x.shape[0], LANES))


def _gdn_kernel(*refs, seq, has_state, nprev):
    qkv_ref, small_ref, cw_ref, alog_ref, dtb_ref, onorm_ref = refs[:6]
    refs = refs[6:]
    s0_ref = sprev_ref = sout_ref = None
    if has_state:
        s0_ref, o_ref = refs[:2]
        refs = refs[2:]
    else:
        if nprev:
            sprev_ref = refs[0]
            refs = refs[1:]
        o_ref, sout_ref = refs[:2]
        refs = refs[2:]
    xpad, qkv_s, g_s, b_s, st_s, wq_s, ak_s, u_s, el_s, rhs_s = refs
    C = GDN_CHUNK
    nchunk = seq // C
    H = GDN_HEADS
    width = 2 * GDN_KW + GDN_VW
    halo = 8

    for j in range(width // LANES):
        xpad[j, 0:halo, :] = jnp.zeros((halo, LANES), f32)
        xpad[j, halo + seq:, :] = jnp.zeros((halo, LANES), f32)
        xpad[j, halo:halo + seq, :] = qkv_ref[:, j * LANES:(j + 1) * LANES]
    if has_state:
        for d in range(2):
            for h in range(H):
                st_s[d * H + h] = s0_ref[d, h]
    else:
        st_s[...] = jnp.zeros((2 * H, GDN_DK, GDN_DV), f32)

    neg_a = -jnp.exp(alog_ref[0])
    dtb = dtb_ref[0]

    def conv_tile(j, l2norm):
        w = cw_ref[j]
        post = jnp.where(j < H, GDN_DK ** -0.5, 1.0)
        for c in range(nchunk):
            base = halo - CONV_W // 2 + c * C
            y = xpad[j, base:base + C, :] * w[0:1]
            for tap in range(1, CONV_W):
                y = y + xpad[j, base + tap:base + tap + C, :] * w[tap:tap + 1]
            y = y * jax.nn.sigmoid(y)
            if l2norm:
                y = y * (lax.rsqrt(jnp.sum(y * y, axis=-1, keepdims=True) + EPS) * post)
            qkv_s[j, c * C:(c + 1) * C, :] = y

    def conv_qk(j, carry):
        conv_tile(j, True)
        return carry

    def conv_v(j, carry):
        conv_tile(j, False)
        return carry

    tile_unroll = 2 if nchunk <= 4 else 1
    lax.fori_loop(0, 2 * H, conv_qk, 0, unroll=tile_unroll)
    lax.fori_loop(2 * H, 3 * H, conv_v, 0, unroll=tile_unroll)

    def prep(c, carry):
        r0 = pl.multiple_of(c * C, C)
        sm = small_ref[pl.ds(r0, C), :]
        z = sm + dtb
        g_s[pl.ds(r0, C), :] = neg_a * (jnp.maximum(z, 0.0) + jnp.log1p(jnp.exp(-jnp.abs(z))))
        b_s[pl.ds(r0, C), :] = pltpu.roll(jax.nn.sigmoid(sm), LANES - 2 * H, 1)
        return carry

    lax.fori_loop(0, nchunk, prep, 0, unroll=4)

    ri = lax.broadcasted_iota(jnp.int32, (C, LANES), 0)
    cl = lax.broadcasted_iota(jnp.int32, (C, LANES), 1)
    fwd = cl < C
    cj = cl & (C - 1)
    eye2 = (ri == cj).astype(f32)
    incl2 = (fwd & (ri >= cj)) | (~fwd & (ri <= cj))
    strict2 = (fwd & (ri > cj)) | (~fwd & (ri < cj))
    xor = ri ^ cj
    level2 = sum((xor >= (1 << b)).astype(jnp.int32) for b in range(C.bit_length() - 1))
    lvl_top = jnp.where(fwd, level2, 0)
    lvl_bot = jnp.where(fwd, 0, level2)
    r2 = lax.broadcasted_iota(jnp.int32, (2 * C, C), 0)
    c2 = lax.broadcasted_iota(jnp.int32, (2 * C, C), 1)
    tri2 = (((r2 < C) & (r2 >= c2)) | ((r2 >= C) & (r2 - C <= c2))).astype(bf16)
    zrhs = jnp.zeros((C, 2 * LANES), bf16)
    zvn = jnp.zeros((C, LANES), bf16)
    fwd_row = fwd[0:1, :]
    cpi = min(GDN_SOLVE_CHUNKS, nchunk)

    def block_diag(x):
        return jnp.concatenate([jnp.where(fwd, x, 0.0), jnp.where(fwd, 0.0, x)], axis=0).astype(bf16)

    def solve_phase(i, carry):
        chains = []
        for cc in range(cpi):
            c = i * cpi + cc
            rows = pl.ds(pl.multiple_of(c * C, C), C)
            g2 = _tri_cumsum(tri2, g_s[rows, :])
            g2t = g2.T
            bt = b_s[rows, :]
            for h in range(H):
                chains.append((cc, c, h, rows, g2, g2t, bt))

        a2s, t2s = [], []
        for (cc, c, h, rows, g2, g2t, bt) in chains:
            q = qkv_s[h, rows, :]
            k = qkv_s[H + h, rows, :]
            v = qkv_s[2 * H + h, rows, :]
            kq = _dot_nt(jnp.concatenate([k, q], axis=0).astype(bf16),
                         jnp.concatenate([k, k], axis=0).astype(bf16))
            gcc_f = _lane_bcast(g2[:C], h)
            gcc_b = _lane_bcast(g2[C:], H + h)
            btc_f = _lane_bcast(bt, h)
            btc_b = _lane_bcast(bt, H + h)
            grow = jnp.where(fwd_row, g2t[h:h + 1, :], g2t[H + h:H + h + 1, :])
            diff = jnp.where(fwd, gcc_f, gcc_b) - grow
            dec = jnp.where(incl2, jnp.exp(jnp.where(incl2, diff, 0.0)), 0.0)
            a2 = jnp.where(strict2, jnp.where(fwd, btc_f, btc_b) * kq[:C] * dec, 0.0)
            a2s.append(a2)
            t2s.append(eye2 - jnp.where(level2 == 1, a2, 0.0))
            glast_f = gcc_f[C - 1:C, :]
            glast_b = gcc_b[0:1, :]
            e1_f = jnp.exp(gcc_f)
            e1_b = jnp.exp(gcc_b)
            ket = jnp.concatenate([k * jnp.exp(glast_f - gcc_f), k * jnp.exp(glast_b - gcc_b)], axis=0).T
            ak_s[pl.ds(pl.multiple_of((c * H + h) * 3 * C, 3 * C), 3 * C), :] = jnp.concatenate(
                [kq[C:] * dec, ket], axis=0).astype(bf16)
            for d, (btc, e1, glast) in enumerate(((btc_f, e1_f, glast_f), (btc_b, e1_b, glast_b))):
                ch = d * H + h
                j = (cc * H + h) * 2 + d
                rhs_s[j * C:(j + 1) * C, :] = jnp.concatenate([v * btc, k * (btc * e1)], axis=1).astype(bf16)
                wq_s[pl.ds(pl.multiple_of((c * 2 * H + ch) * 2 * C + C, C), C), :] = (q * e1).astype(bf16)
                el_s[pl.ds(pl.multiple_of((c * 2 * H + ch) * 8, 8), 8), :] = jnp.broadcast_to(
                    jnp.exp(glast), (8, LANES))

        for lv in range(2, C.bit_length()):
            rs = []
            for a2, t2 in zip(a2s, t2s):
                abd = jnp.concatenate([jnp.where(lvl_top == lv, a2, 0.0),
                                       jnp.where(lvl_bot == lv, a2, 0.0)], axis=0).astype(bf16)
                rs.append(_dot(t2.astype(bf16), abd))
            t2s = [t2 - _dot(r.astype(bf16), block_diag(t2)) for r, t2 in zip(rs, t2s)]

        for (cc, c, h, rows, g2, g2t, bt), t2 in zip(chains, t2s):
            t2b = t2.astype(bf16)
            for d in range(2):
                ch = d * H + h
                j = (cc * H + h) * 2 + d
                rhs = rhs_s[j * C:(j + 1) * C, :]
                rhs = jnp.concatenate([rhs, zrhs] if d == 0 else [zrhs, rhs], axis=0)
                uw = _dot(t2b, rhs)
                u_s[pl.ds(pl.multiple_of((c * 2 * H + ch) * C, C), C), :] = uw[:, :LANES]
                wq_s[pl.ds(pl.multiple_of((c * 2 * H + ch) * 2 * C, C), C), :] = uw[:, LANES:].astype(bf16)
        return carry

    lax.fori_loop(0, nchunk // cpi, solve_phase, 0)

    def scan_phase(i, carry):
        cs = [i if ch < H else nchunk - 1 - i for ch in range(2 * H)]
        s_old = [st_s[ch] for ch in range(2 * H)]
        r1 = [_dot(wq_s[pl.ds(pl.multiple_of((cs[ch] * 2 * H + ch) * 2 * C, 2 * C), 2 * C), :],
                   s_old[ch].astype(bf16)) for ch in range(2 * H)]
        r2s = []
        for ch in range(2 * H):
            u = u_s[pl.ds(pl.multiple_of((cs[ch] * 2 * H + ch) * C, C), C), :]
            vnb = (u - r1[ch][:C]).astype(bf16)
            rhs = jnp.concatenate([vnb, zvn] if ch < H else [zvn, vnb], axis=0)
            ak = ak_s[pl.ds(pl.multiple_of((cs[ch] * H + ch % H) * 3 * C, 3 * C), 3 * C), :]
            r2s.append(_dot(ak, rhs))
        for ch in range(2 * H):
            el = el_s[pl.ds(pl.multiple_of((cs[ch] * 2 * H + ch) * 8, 8), 8), :][0:1, :]
            st_s[ch] = s_old[ch] * el + r2s[ch][C:]
            xpad[ch, pl.ds(pl.multiple_of(cs[ch] * C, C), C), :] = r1[ch][C:] + r2s[ch][:C]
        return carry

    lax.fori_loop(0, nchunk, scan_phase, 0, unroll=4)

    onorm = onorm_ref[0]

    def fin(c, carry):
        rows = pl.ds(pl.multiple_of(c * C, C), C)
        for h in range(H):
            ls = slice(h * LANES, (h + 1) * LANES)
            o_ref[rows, ls] = _rms_rows(xpad[h, rows, :] + xpad[H + h, rows, :], onorm)
        return carry

    lax.fori_loop(0, nchunk, fin, 0, unroll=4)
    if sout_ref is not None:
        for i in range(nprev):
            sout_ref[i] = sprev_ref[i]
        for d in range(2):
            for h in range(H):
                sout_ref[nprev, d, h] = st_s[d * H + h]


def _gdn(proj, row0, conv_w, alog_p, dtb_p, onorm, state, prev_states, l, batch, seq):
    ntok = batch * seq
    width = 2 * GDN_KW + GDN_VW
    has_state = state is not None
    nprev = 0 if has_state else l
    nchunk = seq // GDN_CHUNK
    kern = functools.partial(_gdn_kernel, seq=seq, has_state=has_state, nprev=nprev)
    in_specs = [pl.BlockSpec((seq, width), lambda b: (row0 // seq + b, OFF_QKV // width)),
                pl.BlockSpec((seq, LANES), lambda b: (row0 // seq + b, OFF_SMALL // LANES)),
                pl.BlockSpec((None, width // LANES, CONV_W, LANES), lambda b: (l, 0, 0, 0)),
                pl.BlockSpec((1, 1, LANES), lambda b: (l, 0, 0)),
                pl.BlockSpec((1, 1, LANES), lambda b: (l, 0, 0)),
                pl.BlockSpec((1, 1, GDN_DV), lambda b: (l, 0, 0))]
    args = [proj, proj, conv_w, alog_p, dtb_p, onorm]
    o_spec = pl.BlockSpec((seq, GDN_VW), lambda b: (b, 0))
    o_shape = jax.ShapeDtypeStruct((ntok, GDN_VW), f32)
    st_block = (None, None, 2, GDN_HEADS, GDN_DK, GDN_DV)
    if has_state:
        in_specs.append(pl.BlockSpec(st_block, lambda b: (b, l, 0, 0, 0, 0)))
        args.append(state)
        out_specs, out_shape = o_spec, o_shape
    else:
        def stacked(n):
            return pl.BlockSpec((None, n) + st_block[2:], lambda b: (b, 0, 0, 0, 0, 0))

        if nprev:
            in_specs.append(stacked(nprev))
            args.append(prev_states)
        out_specs = [o_spec, stacked(nprev + 1)]
        out_shape = [o_shape, jax.ShapeDtypeStruct((batch, nprev + 1, 2, GDN_HEADS, GDN_DK, GDN_DV), f32)]
    return pl.pallas_call(
        kern,
        grid=(batch,),
        in_specs=in_specs,
        out_specs=out_specs,
        out_shape=out_shape,
        scratch_shapes=[pltpu.VMEM((width // LANES, seq + 16, LANES), f32),
                        pltpu.VMEM((width // LANES, seq, LANES), f32),
                        pltpu.VMEM((seq, LANES), f32),
                        pltpu.VMEM((seq, LANES), f32),
                        pltpu.VMEM((2 * GDN_HEADS, GDN_DK, GDN_DV), f32),
                        pltpu.VMEM((nchunk * 2 * GDN_HEADS * 2 * GDN_CHUNK, LANES), bf16),
                        pltpu.VMEM((nchunk * GDN_HEADS * 3 * GDN_CHUNK, LANES), bf16),
                        pltpu.VMEM((nchunk * 2 * GDN_HEADS * GDN_CHUNK, LANES), f32),
                        pltpu.VMEM((nchunk * 2 * GDN_HEADS * 8, LANES), f32),
                        pltpu.VMEM((min(GDN_SOLVE_CHUNKS, nchunk) * 2 * GDN_HEADS * GDN_CHUNK, 2 * LANES), bf16)],
        compiler_params=_cparams(1),
        name="gdn_lat" if has_state else "gdn_ctx",
    )(*args)


def _merge_kernel(x_ref, mod_ref, oa_ref, ob_ref, za_ref, zb_ref, cu_ref, cv_ref, zc_ref, gl_ref,
                  lng_ref, lnb_ref, ws_ref, bs_ref, wbr_ref, wo_ref, out_ref, sv_s):
    tm = x_ref.shape[0]
    def gelu(x_ref):
        x = x_ref[...].astype(f32)
        c1 = math.sqrt(2.0 / math.pi)
        half = 0.5 * x
        return half + half * jnp.tanh(x * (c1 + (c1 * 0.044715) * (x * x)))

    u = gelu(cu_ref)
    vf = gelu(cv_ref)
    mu = jnp.mean(vf, axis=-1, keepdims=True)
    vc = vf - mu
    var = jnp.mean(vc * vc, axis=-1, keepdims=True)
    vn = (vc * lax.rsqrt(var + EPS) * lng_ref[0] + lnb_ref[0]).astype(bf16)
    for ck in range(tm // CM_CHUNK):
        rs = slice(ck * CM_CHUNK, (ck + 1) * CM_CHUNK)
        for g in range(CM_GROUPS):
            ls = slice(g * LANES, (g + 1) * LANES)
            sv_s[rs, ls] = _dot(ws_ref[g], vn[rs, ls]) + bs_ref[:, ls]
    o_c = u * sv_s[...]

    def silu(z_ref):
        z = z_ref[...].astype(f32)
        return z * jax.nn.sigmoid(z)

    brs = (oa_ref[...] * silu(za_ref), ob_ref[...] * silu(zb_ref), o_c * silu(zc_ref))
    ysum = None
    for n in range(N_BRANCH):
        yb = _dot(brs[n].astype(bf16), wbr_ref[n])
        t = jax.nn.sigmoid(gl_ref[:, n * D_MODEL:(n + 1) * D_MODEL].astype(f32)) * yb
        ysum = t if ysum is None else ysum + t
    y = _dot(ysum.astype(bf16), wo_ref[...])
    gate = mod_ref[0][:, 2 * D_MODEL:]
    out_ref[...] = x_ref[...] + gate * y


def _merge(x, mod_l, proj, row0, o_a, o_b, lng, lnb, ws_b, bs_full, wbr_b, wo_b, l, latent, seq):
    ntok = x.shape[0]
    tm = 256

    def col(off):
        return pl.BlockSpec((tm, 512), lambda t: (row0 // tm + t, off // 512))

    def const(*idx):
        return lambda t: idx

    return pl.pallas_call(
        _merge_kernel,
        grid=(ntok // tm,),
        in_specs=[pl.BlockSpec((tm, D_MODEL), lambda t: (t, 0)),
                  pl.BlockSpec((1, 1, 3 * D_MODEL), _mod_row(latent, tm, seq)),
                  pl.BlockSpec((tm, 512), lambda t: (t, 0)),
                  pl.BlockSpec((tm, 512), lambda t: (t, 0)),
                  col(OFF_ZA), col(OFF_ZB), col(OFF_CU), col(OFF_CV), col(OFF_ZC),
                  pl.BlockSpec((tm, N_BRANCH * D_MODEL), lambda t: (row0 // tm + t, 0)),
                  pl.BlockSpec((1, 1, CM_WIDTH), const(l, 0, 0)),
                  pl.BlockSpec((1, 1, CM_WIDTH), const(l, 0, 0)),
                  pl.BlockSpec((None, CM_GROUPS, CM_CHUNK, CM_CHUNK), const(l, 0, 0, 0)),
                  pl.BlockSpec((None, CM_CHUNK, CM_WIDTH), const(l, 0, 0)),
                  pl.BlockSpec((None, N_BRANCH, BRANCH_W, D_MODEL), const(l, 0, 0, 0)),
                  pl.BlockSpec((None, D_MODEL, D_MODEL), const(l, 0, 0))],
        out_specs=pl.BlockSpec((tm, D_MODEL), lambda t: (t, 0)),
        out_shape=jax.ShapeDtypeStruct((ntok, D_MODEL), f32),
        scratch_shapes=[pltpu.VMEM((tm, CM_WIDTH), f32)],
        compiler_params=_cparams(1),
        name="merge",
    )(x, mod_l, o_a, o_b, proj, proj, proj, proj, proj, proj, lng, lnb, ws_b, bs_full, wbr_b, wo_b)


def _pad_last(x, n):
    return jnp.pad(x, [(0, 0)] * (x.ndim - 1) + [(0, n - x.shape[-1])])


def _rope_tables(seq):
    t = np.arange(seq)
    row = (t // GRID_W).astype(np.float32)
    colp = (t % GRID_W).astype(np.float32)
    nf = MLA_ROPE // 4
    inv = (ROPE_THETA ** (-np.arange(nf, dtype=np.float32) / nf)).astype(np.float32)
    cos_t = np.ones((seq, LANES), np.float32)
    s_lo = np.zeros((seq, LANES), np.float32)
    s_hi = np.zeros((seq, LANES), np.float32)
    for i, pos in enumerate((row, colp)):
        ang = (pos[:, None] * inv[None, :]).astype(np.float32)
        cs, sn = np.cos(ang), np.sin(ang)
        lo = ROPE_LANE0 + 2 * nf * i
        cos_t[:, lo:lo + nf] = cs
        cos_t[:, lo + nf:lo + 2 * nf] = cs
        s_lo[:, lo:lo + nf] = -sn
        s_hi[:, lo + nf:lo + 2 * nf] = sn
    return tuple(jnp.asarray(p) for p in (cos_t, s_lo, s_hi))


def kernel(x_prompt, x_sample, cache_ckv, cache_krope, state_gdn, c, c_ctx, norm_g, w_mod, b_mod, w_in, q_a_norm, w_uq, kv_a_norm, w_ukv, q_norm, k_norm, conv_w, a_log, dt_bias, gdn_onorm, cm_ln_g, cm_ln_b, w_s, b_s, w_branch, w_o):
    L = DEPTH
    batch, seq, _ = x_prompt.shape
    dbatch, dseq, _ = x_sample.shape
    past = cache_ckv.shape[2]

    w_t = jnp.swapaxes(w_in, 1, 2)
    wuq_p = _pad_last(w_uq.reshape(L, MLA_Q_RANK, MLA_HEADS, MLA_QK), HEAD_SLOT)
    wuq_p = wuq_p.reshape(L, MLA_Q_RANK, MLA_HEADS * HEAD_SLOT).astype(bf16)
    wukv = w_ukv.reshape(L, MLA_KV_RANK, MLA_HEADS, MLA_NOPE + MLA_V)
    wk_p = _pad_last(wukv[..., :MLA_NOPE], HEAD_SLOT).reshape(L, MLA_KV_RANK, MLA_HEADS * HEAD_SLOT).astype(bf16)
    wv_p = wukv[..., MLA_NOPE:].reshape(L, MLA_KV_RANK, MLA_WIDTH).astype(bf16)
    qn_p = _pad_last(q_norm, HEAD_SLOT).reshape(L, 1, HEAD_SLOT)
    kn_p = _pad_last(k_norm, HEAD_SLOT).reshape(L, 1, HEAD_SLOT)
    attn_w = (q_a_norm.reshape(L, 1, MLA_Q_RANK), wuq_p, kv_a_norm.reshape(L, 1, MLA_KV_RANK),
              wk_p, wv_p, qn_p, kn_p)
    cache_kr_p = jnp.pad(cache_krope, [(0, 0)] * 3 + [(ROPE_LANE0, LANES - ROPE_LANE0 - MLA_ROPE)])
    rope_tabs = _rope_tables(dseq)
    alog_p = _pad_last(a_log.reshape(L, 1, 2 * GDN_HEADS), LANES)
    dtb_p = _pad_last(dt_bias.reshape(L, 1, 2 * GDN_HEADS), LANES)
    onorm = gdn_onorm.reshape(L, 1, GDN_DV)
    conv_w = jnp.swapaxes(conv_w.reshape(L, CONV_W, -1, LANES), 1, 2)
    lng =cm_ln_g.reshape(L, 1, CM_WIDTH)
    lnb = cm_ln_b.reshape(L, 1, CM_WIDTH)
    ws_b = w_s.astype(bf16)
    bs_full = jnp.repeat(jnp.swapaxes(b_s, 1, 2), CM_WIDTH // CM_GROUPS, axis=2)
    wbr_b = w_branch.astype(bf16)
    wo_b = w_o.astype(bf16)
    norm_g3 = norm_g.reshape(L, 1, D_MODEL)

    c8 = jnp.concatenate([c, c_ctx[None, :], jnp.zeros((8 - dbatch - 1, D_MODEL), f32)], axis=0)
    mod = _modulation(c8, w_mod, b_mod)

    yp = x_prompt.reshape(batch * seq, D_MODEL)
    ys = x_sample.reshape(dbatch * dseq, D_MODEL)
    caches, states = (), None
    for l in range(L):
        mod_l = mod[l].reshape(8, 1, 3 * D_MODEL)
        p16, p32 = _inproj(yp, ys, mod_l, norm_g3, w_t, l, dseq)
        lat0 = batch * seq
        o_a, new_ckv, new_kr = _attn_ctx(p32, caches, attn_w, l, batch, seq)
        caches = (new_ckv, new_kr)
        o_b, states = _gdn(p32, 0, conv_w, alog_p, dtb_p, onorm, None, states, l, batch, seq)
        yp = _merge(yp, mod_l, p16, 0, o_a, o_b, lng, lnb, ws_b, bs_full, wbr_b, wo_b, l, False, seq)
        o_a = _attn_lat(p32, lat0, cache_ckv, cache_kr_p, rope_tabs, attn_w, l, dbatch, dseq, past)
        o_b = _gdn(p32, lat0, conv_w, alog_p, dtb_p, onorm, state_gdn, None, l, dbatch, dseq)
        ys = _merge(ys, mod_l, p16, lat0, o_a, o_b, lng, lnb, ws_b, bs_full, wbr_b, wo_b, l, True, dseq)
    return (yp.reshape(batch, seq, D_MODEL), ys.reshape(dbatch, dseq, D_MODEL), caches[0], caches[1], states)
```

```python
import functools
import math

import numpy as np

import jax
import jax.numpy as jnp
from jax import lax
from jax.experimental import pallas as pl
from jax.experimental.pallas import tpu as pltpu

D_MODEL = 1024
DEPTH = 2
GRID_W = 64
EPS = 1e-6
MLA_HEADS = 8
MLA_NOPE = 64
MLA_ROPE = 32
MLA_QK = MLA_NOPE + MLA_ROPE
MLA_V = 64
MLA_Q_RANK = 384
MLA_KV_RANK = 256
MLA_WIDTH = MLA_HEADS * MLA_V
ROPE_THETA = 10000.0
GDN_HEADS = 4
GDN_DK = 128
GDN_DV = 128
GDN_KW = GDN_HEADS * GDN_DK
GDN_VW = GDN_HEADS * GDN_DV
GDN_CHUNK = 64
CONV_W = 5
GDN_SOLVE_CHUNKS = 4
CM_GROUPS = 4
CM_CHUNK = 128
CM_WIDTH = 512
N_BRANCH = 3
BRANCH_W = 512
SPLIT_SIZES = (MLA_Q_RANK, MLA_KV_RANK, MLA_ROPE, MLA_WIDTH,
               2 * GDN_KW + GDN_VW, 2 * GDN_HEADS, 2 * GDN_HEADS, GDN_VW,
               CM_WIDTH, CM_WIDTH, CM_WIDTH, N_BRANCH * D_MODEL)

LANES = 128
SUBLANES = 8
HEAD_SLOT = LANES
ROPE_LANE0 = MLA_NOPE

OFF_GL = 0
OFF_ZA = OFF_GL + N_BRANCH * D_MODEL
OFF_ZB = OFF_ZA + 512
OFF_CU = OFF_ZB + 512
OFF_CV = OFF_CU + 512
OFF_ZC = OFF_CV + 512
P16_W = OFF_ZC + 512
OFF_QKV = 0
OFF_CQS = OFF_QKV + 2 * GDN_KW + GDN_VW
OFF_SMALL = OFF_CQS + MLA_Q_RANK
OFF_CKV = OFF_CQS + 512
P32_W = OFF_CKV + MLA_KV_RANK
PROJ_W = P16_W + P32_W

VMEM_LIMIT = 56 * 1024 * 1024

f32 = jnp.float32
bf16 = jnp.bfloat16


def _cparams(n_axes):
    return pltpu.CompilerParams(dimension_semantics=("arbitrary",) * n_axes,
                                vmem_limit_bytes=VMEM_LIMIT)


def _dot(a, b):
    return jnp.dot(a, b, preferred_element_type=f32)


def _dot_nt(a, b):
    return lax.dot_general(a, b, (((1,), (1,)), ((), ())), preferred_element_type=f32)


def _dot_tn(a, b):
    return lax.dot_general(a, b, (((0,), (0,)), ((), ())), preferred_element_type=f32)


def _rms_rows(x, g, n=None):
    n = x.shape[-1] if n is None else n
    ms = jnp.sum(x * x, axis=-1, keepdims=True) * (1.0 / n)
    return x * lax.rsqrt(ms + EPS) * g


def _mod_kernel(c_ref, w_ref, b_ref, o_ref):
    a = c_ref[...]
    a = (a * jax.nn.sigmoid(a)).astype(bf16)
    o_ref[0] = _dot(a, w_ref[0].astype(bf16)) + b_ref[0]


def _modulation(c8, w_mod, b_mod):
    tn = 1536
    return pl.pallas_call(
        _mod_kernel,
        grid=(DEPTH, 3 * D_MODEL // tn),
        in_specs=[pl.BlockSpec((8, D_MODEL), lambda l, n: (0, 0)),
                  pl.BlockSpec((1, D_MODEL, tn), lambda l, n: (l, 0, n)),
                  pl.BlockSpec((1, 1, tn), lambda l, n: (l, 0, n))],
        out_specs=pl.BlockSpec((1, 8, tn), lambda l, n: (l, 0, n)),
        out_shape=jax.ShapeDtypeStruct((DEPTH, 8, 3 * D_MODEL), f32),
        compiler_params=_cparams(2),
        name="modulation",
    )(c8, w_mod, b_mod.reshape(DEPTH, 1, 3 * D_MODEL))


W_CHUNK = 512


def _w_in_moves():
    offs = [0]
    for s in SPLIT_SIZES:
        offs.append(offs[-1] + s)
    cq, ckv, krope, z_a, qkv, ga, gb, z_b, cu, cv, z_c, gl = offs[:-1]
    moves = [(gl, N_BRANCH * D_MODEL, OFF_GL), (z_a, 512, OFF_ZA), (z_b, 512, OFF_ZB), (cu, 512, OFF_CU),
             (cv, 512, OFF_CV), (z_c, 512, OFF_ZC), (qkv, 2 * GDN_KW + GDN_VW, P16_W + OFF_QKV),
             (cq, MLA_Q_RANK, P16_W + OFF_CQS), (ckv, MLA_KV_RANK, P16_W + OFF_CKV)]
    return moves, ga, krope


def _w_chunks():
    moves, ga, krope = _w_in_moves()
    chunks = []
    for (a, w, d) in moves:
        for o in range(0, w, W_CHUNK):
            chunks.append((a + o, min(W_CHUNK, w - o), d + o))
    small0 = P16_W + OFF_SMALL
    chunks.append((ga, 4 * GDN_HEADS, small0))
    chunks.append((krope, MLA_ROPE, small0 + ROPE_LANE0))
    return chunks


def _inproj_kernel(xc_ref, xl_ref, mod_ref, g_ref, wt_hbm, o16_ref, o32_ref, w_s, stage, sem, *, layer, nctx):
    t = pl.program_id(0)

    @pl.when(t == 0)
    def _():
        chunks = _w_chunks()

        def copy(j):
            src, n, _ = chunks[j]
            return pltpu.make_async_copy(wt_hbm.at[layer, pl.ds(src, n), :], stage.at[j % 2, pl.ds(0, n), :],
                                         sem.at[j % 2])

        copy(0).start()
        for j, (_, n, dst) in enumerate(chunks):
            if j + 1 < len(chunks):
                copy(j + 1).start()
            copy(j).wait()
            w_s[dst:dst + n, :] = stage[j % 2, 0:n, :].astype(bf16)
        small0 = P16_W + OFF_SMALL
        for lo, hi in ((4 * GDN_HEADS, ROPE_LANE0), (ROPE_LANE0 + MLA_ROPE, LANES)):
            w_s[small0 + lo:small0 + hi, :] = jnp.zeros((hi - lo, D_MODEL), bf16)

    x = jnp.where(t < nctx, xc_ref[...], xl_ref[...])
    m = mod_ref[0]
    shift = m[:, :D_MODEL]
    scale = m[:, D_MODEL:2 * D_MODEL]
    h = (_rms_rows(x, g_ref[0]) * (1.0 + scale) + shift).astype(bf16)
    for a in range(0, P16_W, 512):
        o16_ref[:, a:a + 512] = _dot_nt(h, w_s[a:a + 512, :]).astype(bf16)
    for a in range(0, P32_W, 512):
        b = min(a + 512, P32_W)
        o32_ref[:, a:b] = _dot_nt(h, w_s[P16_W + a:P16_W + b, :])


def _inproj(xc, xl, mod_l, norm_g, w_t, l, dseq):
    tm = 512
    nctx = xc.shape[0] // tm
    nlat = xl.shape[0] // tm
    ntok = xc.shape[0] + xl.shape[0]
    kern = functools.partial(_inproj_kernel, layer=l, nctx=nctx)
    return pl.pallas_call(
        kern,
        grid=(nctx + nlat,),
        in_specs=[pl.BlockSpec((tm, D_MODEL), lambda t: (jnp.minimum(t, nctx - 1), 0)),
                  pl.BlockSpec((tm, D_MODEL), lambda t: (jnp.maximum(t - nctx, 0), 0)),
                  pl.BlockSpec((1, 1, 3 * D_MODEL),
                               lambda t: (jnp.where(t < nctx, 4, ((t - nctx) * tm) // dseq), 0, 0)),
                  pl.BlockSpec((1, 1, D_MODEL), lambda t: (l, 0, 0)),
                  pl.BlockSpec(memory_space=pl.ANY)],
        out_specs=[pl.BlockSpec((tm, P16_W), lambda t: (t, 0)),
                   pl.BlockSpec((tm, P32_W), lambda t: (t, 0))],
        out_shape=[jax.ShapeDtypeStruct((ntok, P16_W), bf16),
                   jax.ShapeDtypeStruct((ntok, P32_W), f32)],
        scratch_shapes=[pltpu.VMEM((PROJ_W, D_MODEL), bf16),
                        pltpu.VMEM((2, W_CHUNK, D_MODEL), f32),
                        pltpu.SemaphoreType.DMA((2,))],
        compiler_params=_cparams(1),
        name="inproj",
    )(xc, xl, mod_l, norm_g, w_t)


def _mod_row(latent, tm, seq):
    if latent:
        return lambda t: ((t * tm) // seq, 0, 0)
    return lambda t: (4, 0, 0)


def _rope(x, cos_t, sin_lo, sin_hi):
    return x * cos_t + pltpu.roll(x, LANES - 8, 1) * sin_lo + pltpu.roll(x, 8, 1) * sin_hi


def _build_kv(ckvn_b, kr, wk_ref, wv_ref, knorm, rope, k_s, v_s, r0):
    n = ckvn_b.shape[0]
    kfull = _dot(ckvn_b, wk_ref[...])
    v_s[r0:r0 + n, :] = _dot(ckvn_b, wv_ref[...]).astype(bf16)
    krg = kr * knorm
    if rope is not None:
        krg = _rope(krg, *rope)
    kr_ss = jnp.sum(kr * kr, axis=-1, keepdims=True)
    for h in range(MLA_HEADS):
        sl = slice(h * HEAD_SLOT, (h + 1) * HEAD_SLOT)
        kn = kfull[:, sl]
        ms = (jnp.sum(kn * kn, axis=-1, keepdims=True) + kr_ss) * (1.0 / MLA_QK)
        k_s[r0:r0 + n, sl] = ((kn * knorm + krg) * lax.rsqrt(ms + EPS)).astype(bf16)


def _rope_lane_mask(shape):
    lane = lax.broadcasted_iota(jnp.int32, shape, 1)
    return (lane >= ROPE_LANE0) & (lane < ROPE_LANE0 + MLA_ROPE)


def _attend_block(qa_b, wuq_ref, qnorm, rope, k_s, v_s, o_ref):
    tq = qa_b.shape[0]
    qfull = _dot(qa_b, wuq_ref[...])
    qgain = qnorm * (math.log2(math.e) / math.sqrt(MLA_QK))
    lane = lax.broadcasted_iota(jnp.int32, (tq, LANES), 1)

    def scores(h):
        sl = slice(h * HEAD_SLOT, (h + 1) * HEAD_SLOT)
        qh = _rms_rows(qfull[:, sl], qgain, n=MLA_QK)
        if rope is not None:
            qh = _rope(qh, *rope)
        return _dot_nt(qh.astype(bf16), k_s[:, sl])

    s_next = scores(0)
    outs = []
    for h in range(MLA_HEADS):
        s = s_next
        if h + 1 < MLA_HEADS:
            s_next = scores(h + 1)
        p = jnp.exp2(s - jnp.max(s, axis=-1, keepdims=True))
        den = jnp.sum(p, axis=-1, keepdims=True)
        hp = h // 2
        outs.append(_dot(p.astype(bf16), v_s[:, hp * LANES:(hp + 1) * LANES]) / den)
        if h % 2 == 1:
            o_ref[:, hp * LANES:(hp + 1) * LANES] = jnp.where(lane < MLA_V, outs[h - 1], outs[h])


def _attn_ctx_kernel(*refs, layer):
    if layer:
        cqs_ref, ckv_ref, pckvn_ref, pkr_ref = refs[:4]
        refs = refs[4:]
    else:
        cqs_ref, ckv_ref = refs[:2]
        refs = refs[2:]
    qan_ref, wuq_ref, kvn_ref, wk_ref, wv_ref, qn_ref, kn_ref, o_ref, ckvn_ref, kr_ref, k_s, v_s = refs
    for i in range(layer):
        ckvn_ref[i] = pckvn_ref[i]
        kr_ref[i] = pkr_ref[i]
    small = cqs_ref[:, MLA_Q_RANK:]
    kr = jnp.where(_rope_lane_mask(small.shape), small, 0.0)
    kr_ref[layer] = small[:, ROPE_LANE0:ROPE_LANE0 + MLA_ROPE]
    ckvn = _rms_rows(ckv_ref[...], kvn_ref[0])
    ckvn_ref[layer] = ckvn
    _build_kv(ckvn.astype(bf16), kr, wk_ref, wv_ref, kn_ref[0], None, k_s, v_s, 0)
    qa = _rms_rows(cqs_ref[:, :MLA_Q_RANK], qan_ref[0]).astype(bf16)
    _attend_block(qa, wuq_ref, qn_ref[0], None, k_s, v_s, o_ref)


def _attn_lat_kernel(cqs_ref, ckv_ref, cckv_ref, ckr_ref, cos_ref, slo_ref, shi_ref,
                     qan_ref, wuq_ref, kvn_ref, wk_ref, wv_ref, qn_ref, kn_ref,
                     o_ref, k_s, v_s, *, seq, past, tq):
    qi = pl.program_id(1)
    rb = 256

    @pl.when(qi == 0)
    def _():
        _build_kv(cckv_ref[...].astype(bf16), ckr_ref[...], wk_ref, wv_ref, kn_ref[0], None, k_s, v_s, 0)
        for r in range(seq // rb):
            rs = slice(r * rb, (r + 1) * rb)
            small = cqs_ref[rs, MLA_Q_RANK:]
            kr = jnp.where(_rope_lane_mask(small.shape), small, 0.0)
            ckvn = _rms_rows(ckv_ref[rs, :], kvn_ref[0])
            rope = (cos_ref[rs, :], slo_ref[rs, :], shi_ref[rs, :])
            _build_kv(ckvn.astype(bf16), kr, wk_ref, wv_ref, kn_ref[0], rope, k_s, v_s, past + r * rb)

    rows = pl.ds(pl.multiple_of(qi * tq, tq), tq)
    qa = _rms_rows(cqs_ref[rows, :MLA_Q_RANK], qan_ref[0]).astype(bf16)
    rope = (cos_ref[rows, :], slo_ref[rows, :], shi_ref[rows, :])
    _attend_block(qa, wuq_ref, qn_ref[0], rope, k_s, v_s, o_ref)


def _attn_weight_specs(l, nidx):
    z = (0,) * (nidx - 1)

    def const(*idx):
        return lambda *g: idx

    return [pl.BlockSpec((1, 1, MLA_Q_RANK), const(l, 0, 0)),
            pl.BlockSpec((None, MLA_Q_RANK, MLA_HEADS * HEAD_SLOT), const(l, 0, 0)),
            pl.BlockSpec((1, 1, MLA_KV_RANK), const(l, 0, 0)),
            pl.BlockSpec((None, MLA_KV_RANK, MLA_HEADS * HEAD_SLOT), const(l, 0, 0)),
            pl.BlockSpec((None, MLA_KV_RANK, MLA_WIDTH), const(l, 0, 0)),
            pl.BlockSpec((1, 1, HEAD_SLOT), const(l, 0, 0)),
            pl.BlockSpec((1, 1, HEAD_SLOT), const(l, 0, 0))]


def _attn_ctx(proj, prev, wts, l, batch, seq):
    ntok = batch * seq

    def stacked(n, width):
        return pl.BlockSpec((None, n, seq, width), lambda b: (b, 0, 0, 0))

    prev_specs = [stacked(l, MLA_KV_RANK), stacked(l, MLA_ROPE)] if l else []
    return pl.pallas_call(
        functools.partial(_attn_ctx_kernel, layer=l),
        grid=(batch,),
        in_specs=[pl.BlockSpec((seq, 512), lambda b: (b, OFF_CQS // 512)),
                  pl.BlockSpec((seq, MLA_KV_RANK), lambda b: (b, OFF_CKV // MLA_KV_RANK))]
        + prev_specs + _attn_weight_specs(l, 1),
        out_specs=[pl.BlockSpec((seq, MLA_WIDTH), lambda b: (b, 0)),
                   stacked(l + 1, MLA_KV_RANK), stacked(l + 1, MLA_ROPE)],
        out_shape=[jax.ShapeDtypeStruct((ntok, MLA_WIDTH), f32),
                   jax.ShapeDtypeStruct((batch, l + 1, seq, MLA_KV_RANK), f32),
                   jax.ShapeDtypeStruct((batch, l + 1, seq, MLA_ROPE), f32)],
        scratch_shapes=[pltpu.VMEM((seq, MLA_HEADS * HEAD_SLOT), bf16),
                        pltpu.VMEM((seq, MLA_WIDTH), bf16)],
        compiler_params=_cparams(1),
        name="attn_ctx",
    )(proj, proj, *prev, *wts)


def _attn_lat(proj, row0, cache_ckv, cache_kr_p, rope_tabs, wts, l, batch, seq, past):
    ntok = batch * seq
    tq = 256
    nq = seq // tq
    kern = functools.partial(_attn_lat_kernel, seq=seq, past=past, tq=tq)
    tab = pl.BlockSpec((seq, LANES), lambda b, q: (0, 0))
    return pl.pallas_call(
        kern,
        grid=(batch, nq),
        in_specs=[pl.BlockSpec((seq, 512), lambda b, q: (row0 // seq + b, OFF_CQS // 512)),
                  pl.BlockSpec((seq, MLA_KV_RANK), lambda b, q: (row0 // seq + b, OFF_CKV // MLA_KV_RANK)),
                  pl.BlockSpec((None, None, past, MLA_KV_RANK), lambda b, q: (b, l, 0, 0)),
                  pl.BlockSpec((None, None, past, LANES), lambda b, q: (b, l, 0, 0)),
                  tab, tab, tab]
        + _attn_weight_specs(l, 2),
        out_specs=pl.BlockSpec((tq, MLA_WIDTH), lambda b, q: (b * nq + q, 0)),
        out_shape=jax.ShapeDtypeStruct((ntok, MLA_WIDTH), f32),
        scratch_shapes=[pltpu.VMEM((past + seq, MLA_HEADS * HEAD_SLOT), bf16),
                        pltpu.VMEM((past + seq, MLA_WIDTH), bf16)],
        compiler_params=_cparams(2),
        name="attn_lat",
    )(proj, proj, cache_ckv, cache_kr_p, *rope_tabs, *wts)


def _split3(x):
    hi = x.astype(bf16)
    r1 = x - hi.astype(f32)
    mid = r1.astype(bf16)
    lo = (r1 - mid.astype(f32)).astype(bf16)
    return hi, mid, lo


def _tri_cumsum(tri_b, x):
    hi, mid, lo = _split3(x)
    return _dot(tri_b, hi) + _dot(tri_b, mid) + _dot(tri_b, lo)


def _lane_bcast(x, c):
    return jnp.broadcast_to(x[:, c:c + 1], (x.shape[0], LANES))


def _gdn_kernel(*refs, seq, has_state, nprev):
    qkv_ref, small_ref, cw_ref, alog_ref, dtb_ref, onorm_ref = refs[:6]
    refs = refs[6:]
    s0_ref = sprev_ref = sout_ref = None
    if has_state:
        s0_ref, o_ref = refs[:2]
        refs = refs[2:]
    else:
        if nprev:
            sprev_ref = refs[0]
            refs = refs[1:]
        o_ref, sout_ref = refs[:2]
        refs = refs[2:]
    xpad, qkv_s, g_s, b_s, st_s, wq_s, ak_s, u_s, el_s, rhs_s = refs
    C = GDN_CHUNK
    nchunk = seq // C
    H = GDN_HEADS
    width = 2 * GDN_KW + GDN_VW
    halo = 8

    for j in range(width // LANES):
        xpad[j, 0:halo, :] = jnp.zeros((halo, LANES), f32)
        xpad[j, halo + seq:, :] = jnp.zeros((halo, LANES), f32)
        xpad[j, halo:halo + seq, :] = qkv_ref[:, j * LANES:(j + 1) * LANES]
    if has_state:
        for d in range(2):
            for h in range(H):
                st_s[d * H + h] = s0_ref[d, h]
    else:
        st_s[...] = jnp.zeros((2 * H, GDN_DK, GDN_DV), f32)

    neg_a = -jnp.exp(alog_ref[0])
    dtb = dtb_ref[0]

    def conv_tile(j, l2norm):
        w = cw_ref[j]
        post = jnp.where(j < H, GDN_DK ** -0.5, 1.0)
        for c in range(nchunk):
            base = halo - CONV_W // 2 + c * C
            y = xpad[j, base:base + C, :] * w[0:1]
            for tap in range(1, CONV_W):
                y = y + xpad[j, base + tap:base + tap + C, :] * w[tap:tap + 1]
            y = y * jax.nn.sigmoid(y)
            if l2norm:
                y = y * (lax.rsqrt(jnp.sum(y * y, axis=-1, keepdims=True) + EPS) * post)
            qkv_s[j, c * C:(c + 1) * C, :] = y

    def conv_qk(j, carry):
        conv_tile(j, True)
        return carry

    def conv_v(j, carry):
        conv_tile(j, False)
        return carry

    tile_unroll = 4 if nchunk <= 4 else 1
    lax.fori_loop(0, 2 * H, conv_qk, 0, unroll=tile_unroll)
    lax.fori_loop(2 * H, 3 * H, conv_v, 0, unroll=tile_unroll)

    def prep(c, carry):
        r0 = pl.multiple_of(c * C, C)
        sm = small_ref[pl.ds(r0, C), :]
        z = sm + dtb
        g_s[pl.ds(r0, C), :] = neg_a * (jnp.maximum(z, 0.0) + jnp.log1p(jnp.exp(-jnp.abs(z))))
        b_s[pl.ds(r0, C), :] = pltpu.roll(jax.nn.sigmoid(sm), LANES - 2 * H, 1)
        return carry

    lax.fori_loop(0, nchunk, prep, 0, unroll=4)

    ri = lax.broadcasted_iota(jnp.int32, (C, LANES), 0)
    cl = lax.broadcasted_iota(jnp.int32, (C, LANES), 1)
    fwd = cl < C
    cj = cl & (C - 1)
    eye2 = (ri == cj).astype(f32)
    incl2 = (fwd & (ri >= cj)) | (~fwd & (ri <= cj))
    strict2 = (fwd & (ri > cj)) | (~fwd & (ri < cj))
    xor = ri ^ cj
    level2 = sum((xor >= (1 << b)).astype(jnp.int32) for b in range(C.bit_length() - 1))
    lvl_top = jnp.where(fwd, level2, 0)
    lvl_bot = jnp.where(fwd, 0, level2)
    r2 = lax.broadcasted_iota(jnp.int32, (2 * C, C), 0)
    c2 = lax.broadcasted_iota(jnp.int32, (2 * C, C), 1)
    tri2 = (((r2 < C) & (r2 >= c2)) | ((r2 >= C) & (r2 - C <= c2))).astype(bf16)
    zrhs = jnp.zeros((C, 2 * LANES), bf16)
    zvn = jnp.zeros((C, LANES), bf16)
    fwd_row = fwd[0:1, :]
    cpi = min(GDN_SOLVE_CHUNKS, nchunk)

    def block_diag(x):
        return jnp.concatenate([jnp.where(fwd, x, 0.0), jnp.where(fwd, 0.0, x)], axis=0).astype(bf16)

    def solve_phase(i, carry):
        chains = []
        for cc in range(cpi):
            c = i * cpi + cc
            rows = pl.ds(pl.multiple_of(c * C, C), C)
            g2 = _tri_cumsum(tri2, g_s[rows, :])
            g2t = g2.T
            bt = b_s[rows, :]
            for h in range(H):
                chains.append((cc, c, h, rows, g2, g2t, bt))

        a2s, t2s = [], []
        for (cc, c, h, rows, g2, g2t, bt) in chains:
            q = qkv_s[h, rows, :]
            k = qkv_s[H + h, rows, :]
            v = qkv_s[2 * H + h, rows, :]
            kq = _dot_nt(jnp.concatenate([k, q], axis=0).astype(bf16),
                         jnp.concatenate([k, k], axis=0).astype(bf16))
            gcc_f = _lane_bcast(g2[:C], h)
            gcc_b = _lane_bcast(g2[C:], H + h)
            btc_f = _lane_bcast(bt, h)
            btc_b = _lane_bcast(bt, H + h)
            grow = jnp.where(fwd_row, g2t[h:h + 1, :], g2t[H + h:H + h + 1, :])
            diff = jnp.where(fwd, gcc_f, gcc_b) - grow
            dec = jnp.where(incl2, jnp.exp(jnp.where(incl2, diff, 0.0)), 0.0)
            a2 = jnp.where(strict2, jnp.where(fwd, btc_f, btc_b) * kq[:C] * dec, 0.0)
            a2s.append(a2)
            t2s.append(eye2 - jnp.where(level2 == 1, a2, 0.0))
            glast_f = gcc_f[C - 1:C, :]
            glast_b = gcc_b[0:1, :]
            e1_f = jnp.exp(gcc_f)
            e1_b = jnp.exp(gcc_b)
            ket = jnp.concatenate([k * jnp.exp(glast_f - gcc_f), k * jnp.exp(glast_b - gcc_b)], axis=0).T
            ak_s[pl.ds(pl.multiple_of((c * H + h) * 3 * C, 3 * C), 3 * C), :] = jnp.concatenate(
                [kq[C:] * dec, ket], axis=0).astype(bf16)
            for d, (btc, e1, glast) in enumerate(((btc_f, e1_f, glast_f), (btc_b, e1_b, glast_b))):
                ch = d * H + h
                j = (cc * H + h) * 2 + d
                rhs_s[j * C:(j + 1) * C, :] = jnp.concatenate([v * btc, k * (btc * e1)], axis=1).astype(bf16)
                wq_s[pl.ds(pl.multiple_of((c * 2 * H + ch) * 2 * C + C, C), C), :] = (q * e1).astype(bf16)
                el_s[pl.ds(pl.multiple_of((c * 2 * H + ch) * 8, 8), 8), :] = jnp.broadcast_to(
                    jnp.exp(glast), (8, LANES))

        for lv in range(2, C.bit_length()):
            rs = []
            for a2, t2 in zip(a2s, t2s):
                abd = jnp.concatenate([jnp.where(lvl_top == lv, a2, 0.0),
                                       jnp.where(lvl_bot == lv, a2, 0.0)], axis=0).astype(bf16)
                rs.append(_dot(t2.astype(bf16), abd))
            t2s = [t2 - _dot(r.astype(bf16), block_diag(t2)) for r, t2 in zip(rs, t2s)]

        for (cc, c, h, rows, g2, g2t, bt), t2 in zip(chains, t2s):
            t2b = t2.astype(bf16)
            for d in range(2):
                ch = d * H + h
                j = (cc * H + h) * 2 + d
                rhs = rhs_s[j * C:(j + 1) * C, :]
                rhs = jnp.concatenate([rhs, zrhs] if d == 0 else [zrhs, rhs], axis=0)
                uw = _dot(t2b, rhs)
                u_s[pl.ds(pl.multiple_of((c * 2 * H + ch) * C, C), C), :] = uw[:, :LANES]
                wq_s[pl.ds(pl.multiple_of((c * 2 * H + ch) * 2 * C, C), C), :] = uw[:, LANES:].astype(bf16)
        return carry

    lax.fori_loop(0, nchunk // cpi, solve_phase, 0)

    def scan_phase(i, carry):
        cs = [i if ch < H else nchunk - 1 - i for ch in range(2 * H)]
        s_old = [st_s[ch] for ch in range(2 * H)]
        r1 = [_dot(wq_s[pl.ds(pl.multiple_of((cs[ch] * 2 * H + ch) * 2 * C, 2 * C), 2 * C), :],
                   s_old[ch].astype(bf16)) for ch in range(2 * H)]
        r2s = []
        for ch in range(2 * H):
            u = u_s[pl.ds(pl.multiple_of((cs[ch] * 2 * H + ch) * C, C), C), :]
            vnb = (u - r1[ch][:C]).astype(bf16)
            rhs = jnp.concatenate([vnb, zvn] if ch < H else [zvn, vnb], axis=0)
            ak = ak_s[pl.ds(pl.multiple_of((cs[ch] * H + ch % H) * 3 * C, 3 * C), 3 * C), :]
            r2s.append(_dot(ak, rhs))
        for ch in range(2 * H):
            el = el_s[pl.ds(pl.multiple_of((cs[ch] * 2 * H + ch) * 8, 8), 8), :][0:1, :]
            st_s[ch] = s_old[ch] * el + r2s[ch][C:]
            xpad[ch, pl.ds(pl.multiple_of(cs[ch] * C, C), C), :] = r1[ch][C:] + r2s[ch][:C]
        return carry

    lax.fori_loop(0, nchunk, scan_phase, 0, unroll=4)

    onorm = onorm_ref[0]

    def fin(c, carry):
        rows = pl.ds(pl.multiple_of(c * C, C), C)
        for h in range(H):
            ls = slice(h * LANES, (h + 1) * LANES)
            o_ref[rows, ls] = _rms_rows(xpad[h, rows, :] + xpad[H + h, rows, :], onorm)
        return carry

    lax.fori_loop(0, nchunk, fin, 0, unroll=4)
    if sout_ref is not None:
        for i in range(nprev):
            sout_ref[i] = sprev_ref[i]
        for d in range(2):
            for h in range(H):
                sout_ref[nprev, d, h] = st_s[d * H + h]


def _gdn(proj, row0, conv_w, alog_p, dtb_p, onorm, state, prev_states, l, batch, seq):
    ntok = batch * seq
    width = 2 * GDN_KW + GDN_VW
    has_state = state is not None
    nprev = 0 if has_state else l
    nchunk = seq // GDN_CHUNK
    kern = functools.partial(_gdn_kernel, seq=seq, has_state=has_state, nprev=nprev)
    in_specs = [pl.BlockSpec((seq, width), lambda b: (row0 // seq + b, OFF_QKV // width)),
                pl.BlockSpec((seq, LANES), lambda b: (row0 // seq + b, OFF_SMALL // LANES)),
                pl.BlockSpec((None, width // LANES, CONV_W, LANES), lambda b: (l, 0, 0, 0)),
                pl.BlockSpec((1, 1, LANES), lambda b: (l, 0, 0)),
                pl.BlockSpec((1, 1, LANES), lambda b: (l, 0, 0)),
                pl.BlockSpec((1, 1, GDN_DV), lambda b: (l, 0, 0))]
    args = [proj, proj, conv_w, alog_p, dtb_p, onorm]
    o_spec = pl.BlockSpec((seq, GDN_VW), lambda b: (b, 0))
    o_shape = jax.ShapeDtypeStruct((ntok, GDN_VW), f32)
    st_block = (None, None, 2, GDN_HEADS, GDN_DK, GDN_DV)
    if has_state:
        in_specs.append(pl.BlockSpec(st_block, lambda b: (b, l, 0, 0, 0, 0)))
        args.append(state)
        out_specs, out_shape = o_spec, o_shape
    else:
        def stacked(n):
            return pl.BlockSpec((None, n) + st_block[2:], lambda b: (b, 0, 0, 0, 0, 0))

        if nprev:
            in_specs.append(stacked(nprev))
            args.append(prev_states)
        out_specs = [o_spec, stacked(nprev + 1)]
        out_shape = [o_shape, jax.ShapeDtypeStruct((batch, nprev + 1, 2, GDN_HEADS, GDN_DK, GDN_DV), f32)]
    return pl.pallas_call(
        kern,
        grid=(batch,),
        in_specs=in_specs,
        out_specs=out_specs,
        out_shape=out_shape,
        scratch_shapes=[pltpu.VMEM((width // LANES, seq + 16, LANES), f32),
                        pltpu.VMEM((width // LANES, seq, LANES), f32),
                        pltpu.VMEM((seq, LANES), f32),
                        pltpu.VMEM((seq, LANES), f32),
                        pltpu.VMEM((2 * GDN_HEADS, GDN_DK, GDN_DV), f32),
                        pltpu.VMEM((nchunk * 2 * GDN_HEADS * 2 * GDN_CHUNK, LANES), bf16),
                        pltpu.VMEM((nchunk * GDN_HEADS * 3 * GDN_CHUNK, LANES), bf16),
                        pltpu.VMEM((nchunk * 2 * GDN_HEADS * GDN_CHUNK, LANES), f32),
                        pltpu.VMEM((nchunk * 2 * GDN_HEADS * 8, LANES), f32),
                        pltpu.VMEM((min(GDN_SOLVE_CHUNKS, nchunk) * 2 * GDN_HEADS * GDN_CHUNK, 2 * LANES), bf16)],
        compiler_params=_cparams(1),
        name="gdn_lat" if has_state else "gdn_ctx",
    )(*args)


def _merge_kernel(x_ref, mod_ref, oa_ref, ob_ref, za_ref, zb_ref, cu_ref, cv_ref, zc_ref, gl_ref,
                  lng_ref, lnb_ref, ws_ref, bs_ref, wbr_ref, wo_ref, out_ref, sv_s):
    tm = x_ref.shape[0]
    def gelu(x_ref):
        x = x_ref[...].astype(f32)
        c1 = math.sqrt(2.0 / math.pi)
        half = 0.5 * x
        return half + half * jnp.tanh(x * (c1 + (c1 * 0.044715) * (x * x)))

    u = gelu(cu_ref)
    vf = gelu(cv_ref)
    mu = jnp.mean(vf, axis=-1, keepdims=True)
    vc = vf - mu
    var = jnp.mean(vc * vc, axis=-1, keepdims=True)
    vn = (vc * lax.rsqrt(var + EPS) * lng_ref[0] + lnb_ref[0]).astype(bf16)
    for ck in range(tm // CM_CHUNK):
        rs = slice(ck * CM_CHUNK, (ck + 1) * CM_CHUNK)
        for g in range(CM_GROUPS):
            ls = slice(g * LANES, (g + 1) * LANES)
            sv_s[rs, ls] = _dot(ws_ref[g], vn[rs, ls]) + bs_ref[:, ls]
    o_c = u * sv_s[...]

    def silu(z_ref):
        z = z_ref[...].astype(f32)
        return z * jax.nn.sigmoid(z)

    brs = (oa_ref[...] * silu(za_ref), ob_ref[...] * silu(zb_ref), o_c * silu(zc_ref))
    ysum = None
    for n in range(N_BRANCH):
        yb = _dot(brs[n].astype(bf16), wbr_ref[n])
        t = jax.nn.sigmoid(gl_ref[:, n * D_MODEL:(n + 1) * D_MODEL].astype(f32)) * yb
        ysum = t if ysum is None else ysum + t
    y = _dot(ysum.astype(bf16), wo_ref[...])
    gate = mod_ref[0][:, 2 * D_MODEL:]
    out_ref[...] = x_ref[...] + gate * y


def _merge(x, mod_l, proj, row0, o_a, o_b, lng, lnb, ws_b, bs_full, wbr_b, wo_b, l, latent, seq):
    ntok = x.shape[0]
    tm = 512

    def col(off):
        return pl.BlockSpec((tm, 512), lambda t: (row0 // tm + t, off // 512))

    def const(*idx):
        return lambda t: idx

    return pl.pallas_call(
        _merge_kernel,
        grid=(ntok // tm,),
        in_specs=[pl.BlockSpec((tm, D_MODEL), lambda t: (t, 0)),
                  pl.BlockSpec((1, 1, 3 * D_MODEL), _mod_row(latent, tm, seq)),
                  pl.BlockSpec((tm, 512), lambda t: (t, 0)),
                  pl.BlockSpec((tm, 512), lambda t: (t, 0)),
                  col(OFF_ZA), col(OFF_ZB), col(OFF_CU), col(OFF_CV), col(OFF_ZC),
                  pl.BlockSpec((tm, N_BRANCH * D_MODEL), lambda t: (row0 // tm + t, 0)),
                  pl.BlockSpec((1, 1, CM_WIDTH), const(l, 0, 0)),
                  pl.BlockSpec((1, 1, CM_WIDTH), const(l, 0, 0)),
                  pl.BlockSpec((None, CM_GROUPS, CM_CHUNK, CM_CHUNK), const(l, 0, 0, 0)),
                  pl.BlockSpec((None, CM_CHUNK, CM_WIDTH), const(l, 0, 0)),
                  pl.BlockSpec((None, N_BRANCH, BRANCH_W, D_MODEL), const(l, 0, 0, 0)),
                  pl.BlockSpec((None, D_MODEL, D_MODEL), const(l, 0, 0))],
        out_specs=pl.BlockSpec((tm, D_MODEL), lambda t: (t, 0)),
        out_shape=jax.ShapeDtypeStruct((ntok, D_MODEL), f32),
        scratch_shapes=[pltpu.VMEM((tm, CM_WIDTH), f32)],
        compiler_params=_cparams(1),
        name="merge",
    )(x, mod_l, o_a, o_b, proj, proj, proj, proj, proj, proj, lng, lnb, ws_b, bs_full, wbr_b, wo_b)


def _pad_last(x, n):
    return jnp.pad(x, [(0, 0)] * (x.ndim - 1) + [(0, n - x.shape[-1])])


def _rope_tables(seq):
    t = np.arange(seq)
    row = (t // GRID_W).astype(np.float32)
    colp = (t % GRID_W).astype(np.float32)
    nf = MLA_ROPE // 4
    inv = (ROPE_THETA ** (-np.arange(nf, dtype=np.float32) / nf)).astype(np.float32)
    cos_t = np.ones((seq, LANES), np.float32)
    s_lo = np.zeros((seq, LANES), np.float32)
    s_hi = np.zeros((seq, LANES), np.float32)
    for i, pos in enumerate((row, colp)):
        ang = (pos[:, None] * inv[None, :]).astype(np.float32)
        cs, sn = np.cos(ang), np.sin(ang)
        lo = ROPE_LANE0 + 2 * nf * i
        cos_t[:, lo:lo + nf] = cs
        cos_t[:, lo + nf:lo + 2 * nf] = cs
        s_lo[:, lo:lo + nf] = -sn
        s_hi[:, lo + nf:lo + 2 * nf] = sn
    return tuple(jnp.asarray(p) for p in (cos_t, s_lo, s_hi))


def kernel(x_prompt, x_sample, cache_ckv, cache_krope, state_gdn, c, c_ctx, norm_g, w_mod, b_mod, w_in, q_a_norm, w_uq, kv_a_norm, w_ukv, q_norm, k_norm, conv_w, a_log, dt_bias, gdn_onorm, cm_ln_g, cm_ln_b, w_s, b_s, w_branch, w_o):
    L = DEPTH
    batch, seq, _ = x_prompt.shape
    dbatch, dseq, _ = x_sample.shape
    past = cache_ckv.shape[2]

    w_t = jnp.swapaxes(w_in, 1, 2)
    wuq_p = _pad_last(w_uq.reshape(L, MLA_Q_RANK, MLA_HEADS, MLA_QK), HEAD_SLOT)
    wuq_p = wuq_p.reshape(L, MLA_Q_RANK, MLA_HEADS * HEAD_SLOT).astype(bf16)
    wukv = w_ukv.reshape(L, MLA_KV_RANK, MLA_HEADS, MLA_NOPE + MLA_V)
    wk_p = _pad_last(wukv[..., :MLA_NOPE], HEAD_SLOT).reshape(L, MLA_KV_RANK, MLA_HEADS * HEAD_SLOT).astype(bf16)
    wv_p = wukv[..., MLA_NOPE:].reshape(L, MLA_KV_RANK, MLA_WIDTH).astype(bf16)
    qn_p = _pad_last(q_norm, HEAD_SLOT).reshape(L, 1, HEAD_SLOT)
    kn_p = _pad_last(k_norm, HEAD_SLOT).reshape(L, 1, HEAD_SLOT)
    attn_w = (q_a_norm.reshape(L, 1, MLA_Q_RANK), wuq_p, kv_a_norm.reshape(L, 1, MLA_KV_RANK),
              wk_p, wv_p, qn_p, kn_p)
    cache_kr_p = jnp.pad(cache_krope, [(0, 0)] * 3 + [(ROPE_LANE0, LANES - ROPE_LANE0 - MLA_ROPE)])
    rope_tabs = _rope_tables(dseq)
    alog_p = _pad_last(a_log.reshape(L, 1, 2 * GDN_HEADS), LANES)
    dtb_p = _pad_last(dt_bias.reshape(L, 1, 2 * GDN_HEADS), LANES)
    onorm = gdn_onorm.reshape(L, 1, GDN_DV)
    conv_w = jnp.swapaxes(conv_w.reshape(L, CONV_W, -1, LANES), 1, 2)
    lng =cm_ln_g.reshape(L, 1, CM_WIDTH)
    lnb = cm_ln_b.reshape(L, 1, CM_WIDTH)
    ws_b = w_s.astype(bf16)
    bs_full = jnp.repeat(jnp.swapaxes(b_s, 1, 2), CM_WIDTH // CM_GROUPS, axis=2)
    wbr_b = w_branch.astype(bf16)
    wo_b = w_o.astype(bf16)
    norm_g3 = norm_g.reshape(L, 1, D_MODEL)

    c8 = jnp.concatenate([c, c_ctx[None, :], jnp.zeros((8 - dbatch - 1, D_MODEL), f32)], axis=0)
    mod = _modulation(c8, w_mod, b_mod)

    yp = x_prompt.reshape(batch * seq, D_MODEL)
    ys = x_sample.reshape(dbatch * dseq, D_MODEL)
    caches, states = (), None
    for l in range(L):
        mod_l = mod[l].reshape(8, 1, 3 * D_MODEL)
        p16, p32 = _inproj(yp, ys, mod_l, norm_g3, w_t, l, dseq)
        lat0 = batch * seq
        o_a, new_ckv, new_kr = _attn_ctx(p32, caches, attn_w, l, batch, seq)
        caches = (new_ckv, new_kr)
        o_b, states = _gdn(p32, 0, conv_w, alog_p, dtb_p, onorm, None, states, l, batch, seq)
        yp = _merge(yp, mod_l, p16, 0, o_a, o_b, lng, lnb, ws_b, bs_full, wbr_b, wo_b, l, False, seq)
        o_a = _attn_lat(p32, lat0, cache_ckv, cache_kr_p, rope_tabs, attn_w, l, dbatch, dseq, past)
        o_b = _gdn(p32, lat0, conv_w, alog_p, dtb_p, onorm, state_gdn, None, l, dbatch, dseq)
        ys = _merge(ys, mod_l, p16, lat0, o_a, o_b, lng, lnb, ws_b, bs_full, wbr_b, wo_b, l, True, dseq)
    return (yp.reshape(batch, seq, D_MODEL), ys.reshape(dbatch, dseq, D_MODEL), caches[0], caches[1], states)
```

```python
import functools
import math

import numpy as np

import jax
import jax.numpy as jnp
from jax import lax
from jax.experimental import pallas as pl
from jax.experimental.pallas import tpu as pltpu

D_MODEL = 1024
DEPTH = 2
GRID_W = 64
EPS = 1e-6
MLA_HEADS = 8
MLA_NOPE = 64
MLA_ROPE = 32
MLA_QK = MLA_NOPE + MLA_ROPE
MLA_V = 64
MLA_Q_RANK = 384
MLA_KV_RANK = 256
MLA_WIDTH = MLA_HEADS * MLA_V
ROPE_THETA = 10000.0
GDN_HEADS = 4
GDN_DK = 128
GDN_DV = 128
GDN_KW = GDN_HEADS * GDN_DK
GDN_VW = GDN_HEADS * GDN_DV
GDN_CHUNK = 64
CONV_W = 5
GDN_SOLVE_CHUNKS = 4
CM_GROUPS = 4
CM_CHUNK = 128
CM_WIDTH = 512
N_BRANCH = 3
BRANCH_W = 512
SPLIT_SIZES = (MLA_Q_RANK, MLA_KV_RANK, MLA_ROPE, MLA_WIDTH,
               2 * GDN_KW + GDN_VW, 2 * GDN_HEADS, 2 * GDN_HEADS, GDN_VW,
               CM_WIDTH, CM_WIDTH, CM_WIDTH, N_BRANCH * D_MODEL)

LANES = 128
SUBLANES = 8
HEAD_SLOT = LANES
ROPE_LANE0 = MLA_NOPE

OFF_GL = 0
OFF_ZA = OFF_GL + N_BRANCH * D_MODEL
OFF_ZB = OFF_ZA + 512
OFF_CU = OFF_ZB + 512
OFF_CV = OFF_CU + 512
OFF_ZC = OFF_CV + 512
P16_W = OFF_ZC + 512
OFF_QKV = 0
OFF_CQS = OFF_QKV + 2 * GDN_KW + GDN_VW
OFF_SMALL = OFF_CQS + MLA_Q_RANK
OFF_CKV = OFF_CQS + 512
P32_W = OFF_CKV + MLA_KV_RANK
PROJ_W = P16_W + P32_W

VMEM_LIMIT = 56 * 1024 * 1024

f32 = jnp.float32
bf16 = jnp.bfloat16


def _cparams(n_axes):
    return pltpu.CompilerParams(dimension_semantics=("arbitrary",) * n_axes,
                                vmem_limit_bytes=VMEM_LIMIT)


def _dot(a, b):
    return jnp.dot(a, b, preferred_element_type=f32)


def _dot_nt(a, b):
    return lax.dot_general(a, b, (((1,), (1,)), ((), ())), preferred_element_type=f32)


def _dot_tn(a, b):
    return lax.dot_general(a, b, (((0,), (0,)), ((), ())), preferred_element_type=f32)


def _rms_rows(x, g, n=None):
    n = x.shape[-1] if n is None else n
    ms = jnp.sum(x * x, axis=-1, keepdims=True) * (1.0 / n)
    return x * lax.rsqrt(ms + EPS) * g


def _mod_kernel(c_ref, w_ref, b_ref, o_ref):
    a = c_ref[...]
    a = (a * jax.nn.sigmoid(a)).astype(bf16)
    o_ref[0] = _dot(a, w_ref[0].astype(bf16)) + b_ref[0]


def _modulation(c8, w_mod, b_mod):
    tn = 1536
    return pl.pallas_call(
        _mod_kernel,
        grid=(DEPTH, 3 * D_MODEL // tn),
        in_specs=[pl.BlockSpec((8, D_MODEL), lambda l, n: (0, 0)),
                  pl.BlockSpec((1, D_MODEL, tn), lambda l, n: (l, 0, n)),
                  pl.BlockSpec((1, 1, tn), lambda l, n: (l, 0, n))],
        out_specs=pl.BlockSpec((1, 8, tn), lambda l, n: (l, 0, n)),
        out_shape=jax.ShapeDtypeStruct((DEPTH, 8, 3 * D_MODEL), f32),
        compiler_params=_cparams(2),
        name="modulation",
    )(c8, w_mod, b_mod.reshape(DEPTH, 1, 3 * D_MODEL))


W_CHUNK = 512


def _w_in_moves():
    offs = [0]
    for s in SPLIT_SIZES:
        offs.append(offs[-1] + s)
    cq, ckv, krope, z_a, qkv, ga, gb, z_b, cu, cv, z_c, gl = offs[:-1]
    moves = [(gl, N_BRANCH * D_MODEL, OFF_GL), (z_a, 512, OFF_ZA), (z_b, 512, OFF_ZB), (cu, 512, OFF_CU),
             (cv, 512, OFF_CV), (z_c, 512, OFF_ZC), (qkv, 2 * GDN_KW + GDN_VW, P16_W + OFF_QKV),
             (cq, MLA_Q_RANK, P16_W + OFF_CQS), (ckv, MLA_KV_RANK, P16_W + OFF_CKV)]
    return moves, ga, krope


def _w_chunks():
    moves, ga, krope = _w_in_moves()
    chunks = []
    for (a, w, d) in moves:
        for o in range(0, w, W_CHUNK):
            chunks.append((a + o, min(W_CHUNK, w - o), d + o))
    small0 = P16_W + OFF_SMALL
    chunks.append((ga, 4 * GDN_HEADS, small0))
    chunks.append((krope, MLA_ROPE, small0 + ROPE_LANE0))
    return chunks


def _inproj_kernel(xc_ref, xl_ref, mod_ref, g_ref, wt_hbm, o16_ref, o32_ref, w_s, stage, sem, *, layer, nctx):
    t = pl.program_id(0)
    x = jnp.where(t < nctx, xc_ref[...], xl_ref[...])
    m = mod_ref[0]
    shift = m[:, :D_MODEL]
    scale = m[:, D_MODEL:2 * D_MODEL]
    h = (_rms_rows(x, g_ref[0]) * (1.0 + scale) + shift).astype(bf16)

    def project(a, b):
        y = _dot_nt(h, w_s[a:b, :])
        if b <= P16_W:
            o16_ref[:, a:b] = y.astype(bf16)
        else:
            o32_ref[:, a - P16_W:b - P16_W] = y

    ranges = [(a, a + 512) for a in range(0, P16_W, 512)]
    ranges += [(P16_W + a, P16_W + min(a + 512, P32_W)) for a in range(0, P32_W, 512)]

    @pl.when(t == 0)
    def _():
        chunks = _w_chunks()
        small0 = P16_W + OFF_SMALL
        zero_fill = ((small0 + 4 * GDN_HEADS, small0 + ROPE_LANE0), (small0 + ROPE_LANE0 + MLA_ROPE, small0 + LANES))
        for lo, hi in zero_fill:
            w_s[lo:hi, :] = jnp.zeros((hi - lo, D_MODEL), bf16)
        filled = list(zero_fill)
        pending = list(ranges)

        def copy(j):
            src, n, _ = chunks[j]
            return pltpu.make_async_copy(wt_hbm.at[layer, pl.ds(src, n), :], stage.at[j % 2, pl.ds(0, n), :],
                                         sem.at[j % 2])

        copy(0).start()
        for j, (_, n, dst) in enumerate(chunks):
            if j + 1 < len(chunks):
                copy(j + 1).start()
            copy(j).wait()
            w_s[dst:dst + n, :] = stage[j % 2, 0:n, :].astype(bf16)
            filled.append((dst, dst + n))
            for (a, b) in list(pending):
                if sum(min(b, hi) - max(a, lo) for lo, hi in filled if min(b, hi) > max(a, lo)) == b - a:
                    project(a, b)
                    pending.remove((a, b))
        assert not pending

    @pl.when(t > 0)
    def _():
        for a, b in ranges:
            project(a, b)


def _inproj(xc, xl, mod_l, norm_g, w_t, l, dseq):
    tm = 512
    nctx = xc.shape[0] // tm
    nlat = xl.shape[0] // tm
    ntok = xc.shape[0] + xl.shape[0]
    kern = functools.partial(_inproj_kernel, layer=l, nctx=nctx)
    return pl.pallas_call(
        kern,
        grid=(nctx + nlat,),
        in_specs=[pl.BlockSpec((tm, D_MODEL), lambda t: (jnp.minimum(t, nctx - 1), 0)),
                  pl.BlockSpec((tm, D_MODEL), lambda t: (jnp.maximum(t - nctx, 0), 0)),
                  pl.BlockSpec((1, 1, 3 * D_MODEL),
                               lambda t: (jnp.where(t < nctx, 4, ((t - nctx) * tm) // dseq), 0, 0)),
                  pl.BlockSpec((1, 1, D_MODEL), lambda t: (l, 0, 0)),
                  pl.BlockSpec(memory_space=pl.ANY)],
        out_specs=[pl.BlockSpec((tm, P16_W), lambda t: (t, 0)),
                   pl.BlockSpec((tm, P32_W), lambda t: (t, 0))],
        out_shape=[jax.ShapeDtypeStruct((ntok, P16_W), bf16),
                   jax.ShapeDtypeStruct((ntok, P32_W), f32)],
        scratch_shapes=[pltpu.VMEM((PROJ_W, D_MODEL), bf16),
                        pltpu.VMEM((2, W_CHUNK, D_MODEL), f32),
                        pltpu.SemaphoreType.DMA((2,))],
        compiler_params=_cparams(1),
        name="inproj",
    )(xc, xl, mod_l, norm_g, w_t)


def _mod_row(latent, tm, seq):
    if latent:
        return lambda t: ((t * tm) // seq, 0, 0)
    return lambda t: (4, 0, 0)


def _rope(x, cos_t, sin_lo, sin_hi):
    return x * cos_t + pltpu.roll(x, LANES - 8, 1) * sin_lo + pltpu.roll(x, 8, 1) * sin_hi


def _build_kv(ckvn_b, kr, wk_ref, wv_ref, knorm, rope, k_s, v_s, r0):
    n = ckvn_b.shape[0]
    kfull = _dot(ckvn_b, wk_ref[...])
    v_s[r0:r0 + n, :] = _dot(ckvn_b, wv_ref[...]).astype(bf16)
    krg = kr * knorm
    if rope is not None:
        krg = _rope(krg, *rope)
    kr_ss = jnp.sum(kr * kr, axis=-1, keepdims=True)
    for h in range(MLA_HEADS):
        sl = slice(h * HEAD_SLOT, (h + 1) * HEAD_SLOT)
        kn = kfull[:, sl]
        ms = (jnp.sum(kn * kn, axis=-1, keepdims=True) + kr_ss) * (1.0 / MLA_QK)
        k_s[r0:r0 + n, sl] = ((kn * knorm + krg) * lax.rsqrt(ms + EPS)).astype(bf16)


def _rope_lane_mask(shape):
    lane = lax.broadcasted_iota(jnp.int32, shape, 1)
    return (lane >= ROPE_LANE0) & (lane < ROPE_LANE0 + MLA_ROPE)


def _attend_block(qa_b, wuq_ref, qnorm, rope, k_s, v_s, o_ref, rows=slice(None)):
    tq = qa_b.shape[0]
    qfull = _dot(qa_b, wuq_ref[...])
    qgain = qnorm * (math.log2(math.e) / math.sqrt(MLA_QK))
    lane = lax.broadcasted_iota(jnp.int32, (tq, LANES), 1)

    def scores(h):
        sl = slice(h * HEAD_SLOT, (h + 1) * HEAD_SLOT)
        qh = _rms_rows(qfull[:, sl], qgain, n=MLA_QK)
        if rope is not None:
            qh = _rope(qh, *rope)
        return _dot_nt(qh.astype(bf16), k_s[rows, sl])

    s_next = scores(0)
    outs = []
    for h in range(MLA_HEADS):
        s = s_next
        if h + 1 < MLA_HEADS:
            s_next = scores(h + 1)
        p = jnp.exp2(s - jnp.max(s, axis=-1, keepdims=True))
        den = jnp.sum(p, axis=-1, keepdims=True)
        hp = h // 2
        outs.append(_dot(p.astype(bf16), v_s[rows, hp * LANES:(hp + 1) * LANES]) / den)
        if h % 2 == 1:
            o_ref[rows, hp * LANES:(hp + 1) * LANES] = jnp.where(lane < MLA_V, outs[h - 1], outs[h])


def _attn_ctx_kernel(*refs, layer):
    if layer:
        cqs_ref, ckv_ref, pckvn_ref, pkr_ref = refs[:4]
        refs = refs[4:]
    else:
        cqs_ref, ckv_ref = refs[:2]
        refs = refs[2:]
    qan_ref, wuq_ref, kvn_ref, wk_ref, wv_ref, qn_ref, kn_ref, o_ref, ckvn_ref, kr_ref, k_s, v_s = refs
    nb, _, seq, _ = ckvn_ref.shape
    for b in range(nb):
        rows = slice(b * seq, (b + 1) * seq)
        for i in range(layer):
            ckvn_ref[b, i] = pckvn_ref[b, i]
            kr_ref[b, i] = pkr_ref[b, i]
        small = cqs_ref[rows, MLA_Q_RANK:]
        kr = jnp.where(_rope_lane_mask(small.shape), small, 0.0)
        kr_ref[b, layer] = small[:, ROPE_LANE0:ROPE_LANE0 + MLA_ROPE]
        ckvn = _rms_rows(ckv_ref[rows, :], kvn_ref[0])
        ckvn_ref[b, layer] = ckvn
        _build_kv(ckvn.astype(bf16), kr, wk_ref, wv_ref, kn_ref[0], None, k_s, v_s, b * seq)
        qa = _rms_rows(cqs_ref[rows, :MLA_Q_RANK], qan_ref[0]).astype(bf16)
        _attend_block(qa, wuq_ref, qn_ref[0], None, k_s, v_s, o_ref, rows)


def _attn_lat_kernel(cqs_ref, ckv_ref, cckv_ref, ckr_ref, cos_ref, slo_ref, shi_ref,
                     qan_ref, wuq_ref, kvn_ref, wk_ref, wv_ref, qn_ref, kn_ref,
                     o_ref, k_s, v_s, *, seq, past, tq):
    qi = pl.program_id(1)
    rb = 256

    @pl.when(qi == 0)
    def _():
        _build_kv(cckv_ref[...].astype(bf16), ckr_ref[...], wk_ref, wv_ref, kn_ref[0], None, k_s, v_s, 0)
        for r in range(seq // rb):
            rs = slice(r * rb, (r + 1) * rb)
            small = cqs_ref[rs, MLA_Q_RANK:]
            kr = jnp.where(_rope_lane_mask(small.shape), small, 0.0)
            ckvn = _rms_rows(ckv_ref[rs, :], kvn_ref[0])
            rope = (cos_ref[rs, :], slo_ref[rs, :], shi_ref[rs, :])
            _build_kv(ckvn.astype(bf16), kr, wk_ref, wv_ref, kn_ref[0], rope, k_s, v_s, past + r * rb)

    rows = pl.ds(pl.multiple_of(qi * tq, tq), tq)
    qa = _rms_rows(cqs_ref[rows, :MLA_Q_RANK], qan_ref[0]).astype(bf16)
    rope = (cos_ref[rows, :], slo_ref[rows, :], shi_ref[rows, :])
    _attend_block(qa, wuq_ref, qn_ref[0], rope, k_s, v_s, o_ref)


def _attn_weight_specs(l, nidx):
    z = (0,) * (nidx - 1)

    def const(*idx):
        return lambda *g: idx

    return [pl.BlockSpec((1, 1, MLA_Q_RANK), const(l, 0, 0)),
            pl.BlockSpec((None, MLA_Q_RANK, MLA_HEADS * HEAD_SLOT), const(l, 0, 0)),
            pl.BlockSpec((1, 1, MLA_KV_RANK), const(l, 0, 0)),
            pl.BlockSpec((None, MLA_KV_RANK, MLA_HEADS * HEAD_SLOT), const(l, 0, 0)),
            pl.BlockSpec((None, MLA_KV_RANK, MLA_WIDTH), const(l, 0, 0)),
            pl.BlockSpec((1, 1, HEAD_SLOT), const(l, 0, 0)),
            pl.BlockSpec((1, 1, HEAD_SLOT), const(l, 0, 0))]


def _attn_ctx(proj, prev, wts, l, batch, seq):
    ntok = batch * seq
    nb = 2
    rows = nb * seq

    def stacked(n, width):
        return pl.BlockSpec((nb, n, seq, width), lambda b: (b, 0, 0, 0))

    prev_specs = [stacked(l, MLA_KV_RANK), stacked(l, MLA_ROPE)] if l else []
    return pl.pallas_call(
        functools.partial(_attn_ctx_kernel, layer=l),
        grid=(batch // nb,),
        in_specs=[pl.BlockSpec((rows, 512), lambda b: (b, OFF_CQS // 512)),
                  pl.BlockSpec((rows, MLA_KV_RANK), lambda b: (b, OFF_CKV // MLA_KV_RANK))]
        + prev_specs + _attn_weight_specs(l, 1),
        out_specs=[pl.BlockSpec((rows, MLA_WIDTH), lambda b: (b, 0)),
                   stacked(l + 1, MLA_KV_RANK), stacked(l + 1, MLA_ROPE)],
        out_shape=[jax.ShapeDtypeStruct((ntok, MLA_WIDTH), f32),
                   jax.ShapeDtypeStruct((batch, l + 1, seq, MLA_KV_RANK), f32),
                   jax.ShapeDtypeStruct((batch, l + 1, seq, MLA_ROPE), f32)],
        scratch_shapes=[pltpu.VMEM((rows, MLA_HEADS * HEAD_SLOT), bf16),
                        pltpu.VMEM((rows, MLA_WIDTH), bf16)],
        compiler_params=_cparams(1),
        name="attn_ctx",
    )(proj, proj, *prev, *wts)


def _attn_lat(proj, row0, cache_ckv, cache_kr_p, rope_tabs, wts, l, batch, seq, past):
    ntok = batch * seq
    tq = 256
    nq = seq // tq
    kern = functools.partial(_attn_lat_kernel, seq=seq, past=past, tq=tq)
    tab = pl.BlockSpec((seq, LANES), lambda b, q: (0, 0))
    return pl.pallas_call(
        kern,
        grid=(batch, nq),
        in_specs=[pl.BlockSpec((seq, 512), lambda b, q: (row0 // seq + b, OFF_CQS // 512)),
                  pl.BlockSpec((seq, MLA_KV_RANK), lambda b, q: (row0 // seq + b, OFF_CKV // MLA_KV_RANK)),
                  pl.BlockSpec((None, None, past, MLA_KV_RANK), lambda b, q: (b, l, 0, 0)),
                  pl.BlockSpec((None, None, past, LANES), lambda b, q: (b, l, 0, 0)),
                  tab, tab, tab]
        + _attn_weight_specs(l, 2),
        out_specs=pl.BlockSpec((tq, MLA_WIDTH), lambda b, q: (b * nq + q, 0)),
        out_shape=jax.ShapeDtypeStruct((ntok, MLA_WIDTH), f32),
        scratch_shapes=[pltpu.VMEM((past + seq, MLA_HEADS * HEAD_SLOT), bf16),
                        pltpu.VMEM((past + seq, MLA_WIDTH), bf16)],
        compiler_params=_cparams(2),
        name="attn_lat",
    )(proj, proj, cache_ckv, cache_kr_p, *rope_tabs, *wts)


def _split3(x):
    hi = x.astype(bf16)
    r1 = x - hi.astype(f32)
    mid = r1.astype(bf16)
    lo = (r1 - mid.astype(f32)).astype(bf16)
    return hi, mid, lo


def _tri_cumsum(tri_b, x):
    hi, mid, lo = _split3(x)
    return _dot(tri_b, hi) + _dot(tri_b, mid) + _dot(tri_b, lo)


def _lane_bcast(x, c):
    return jnp.broadcast_to(x[:, c:c + 1], (x.shape[0], LANES))


def _gdn_kernel(*refs, seq, has_state, nprev):
    qkv_ref, small_ref, cw_ref, alog_ref, dtb_ref, onorm_ref = refs[:6]
    refs = refs[6:]
    s0_ref = sprev_ref = sout_ref = None
    if has_state:
        s0_ref, o_ref = refs[:2]
        refs = refs[2:]
    else:
        if nprev:
            sprev_ref = refs[0]
            refs = refs[1:]
        o_ref, sout_ref = refs[:2]
        refs = refs[2:]
    xpad, qkv_s, g_s, b_s, st_s, wq_s, ak_s, u_s, el_s, rhs_s = refs
    C = GDN_CHUNK
    nchunk = seq // C
    H = GDN_HEADS
    width = 2 * GDN_KW + GDN_VW
    halo = 8

    for j in range(width // LANES):
        xpad[j, 0:halo, :] = jnp.zeros((halo, LANES), f32)
        xpad[j, halo + seq:, :] = jnp.zeros((halo, LANES), f32)
        xpad[j, halo:halo + seq, :] = qkv_ref[:, j * LANES:(j + 1) * LANES]
    if has_state:
        for d in range(2):
            for h in range(H):
                st_s[d * H + h] = s0_ref[d, h]
    else:
        st_s[...] = jnp.zeros((2 * H, GDN_DK, GDN_DV), f32)

    neg_a = -jnp.exp(alog_ref[0])
    dtb = dtb_ref[0]

    def conv_tile(j, l2norm):
        w = cw_ref[j]
        post = jnp.where(j < H, GDN_DK ** -0.5, 1.0)
        for c in range(nchunk):
            base = halo - CONV_W // 2 + c * C
            y = xpad[j, base:base + C, :] * w[0:1]
            for tap in range(1, CONV_W):
                y = y + xpad[j, base + tap:base + tap + C, :] * w[tap:tap + 1]
            y = y * jax.nn.sigmoid(y)
            if l2norm:
                y = y * (lax.rsqrt(jnp.sum(y * y, axis=-1, keepdims=True) + EPS) * post)
            qkv_s[j, c * C:(c + 1) * C, :] = y

    def conv_qk(j, carry):
        conv_tile(j, True)
        return carry

    def conv_v(j, carry):
        conv_tile(j, False)
        return carry

    tile_unroll = 4 if nchunk <= 4 else 1
    lax.fori_loop(0, 2 * H, conv_qk, 0, unroll=tile_unroll)
    lax.fori_loop(2 * H, 3 * H, conv_v, 0, unroll=tile_unroll)

    def prep(c, carry):
        r0 = pl.multiple_of(c * C, C)
        sm = small_ref[pl.ds(r0, C), :]
        z = sm + dtb
        g_s[pl.ds(r0, C), :] = neg_a * (jnp.maximum(z, 0.0) + jnp.log1p(jnp.exp(-jnp.abs(z))))
        b_s[pl.ds(r0, C), :] = pltpu.roll(jax.nn.sigmoid(sm), LANES - 2 * H, 1)
        return carry

    lax.fori_loop(0, nchunk, prep, 0, unroll=4)

    ri = lax.broadcasted_iota(jnp.int32, (C, LANES), 0)
    cl = lax.broadcasted_iota(jnp.int32, (C, LANES), 1)
    fwd = cl < C
    cj = cl & (C - 1)
    eye2 = (ri == cj).astype(f32)
    incl2 = (fwd & (ri >= cj)) | (~fwd & (ri <= cj))
    strict2 = (fwd & (ri > cj)) | (~fwd & (ri < cj))
    xor = ri ^ cj
    level2 = sum((xor >= (1 << b)).astype(jnp.int32) for b in range(C.bit_length() - 1))
    lvl_top = jnp.where(fwd, level2, 0)
    lvl_bot = jnp.where(fwd, 0, level2)
    r2 = lax.broadcasted_iota(jnp.int32, (2 * C, C), 0)
    c2 = lax.broadcasted_iota(jnp.int32, (2 * C, C), 1)
    tri2 = (((r2 < C) & (r2 >= c2)) | ((r2 >= C) & (r2 - C <= c2))).astype(bf16)
    zrhs = jnp.zeros((C, 2 * LANES), bf16)
    zvn = jnp.zeros((C, LANES), bf16)
    fwd_row = fwd[0:1, :]
    cpi = min(GDN_SOLVE_CHUNKS, nchunk)

    def block_diag(x):
        return jnp.concatenate([jnp.where(fwd, x, 0.0), jnp.where(fwd, 0.0, x)], axis=0).astype(bf16)

    def solve_phase(i, carry):
        chains = []
        for cc in range(cpi):
            c = i * cpi + cc
            rows = pl.ds(pl.multiple_of(c * C, C), C)
            g2 = _tri_cumsum(tri2, g_s[rows, :])
            g2t = g2.T
            bt = b_s[rows, :]
            for h in range(H):
                chains.append((cc, c, h, rows, g2, g2t, bt))

        a2s, t2s = [], []
        for (cc, c, h, rows, g2, g2t, bt) in chains:
            q = qkv_s[h, rows, :]
            k = qkv_s[H + h, rows, :]
            v = qkv_s[2 * H + h, rows, :]
            kq = _dot_nt(jnp.concatenate([k, q], axis=0).astype(bf16),
                         jnp.concatenate([k, k], axis=0).astype(bf16))
            gcc_f = _lane_bcast(g2[:C], h)
            gcc_b = _lane_bcast(g2[C:], H + h)
            btc_f = _lane_bcast(bt, h)
            btc_b = _lane_bcast(bt, H + h)
            grow = jnp.where(fwd_row, g2t[h:h + 1, :], g2t[H + h:H + h + 1, :])
            diff = jnp.where(fwd, gcc_f, gcc_b) - grow
            dec = jnp.where(incl2, jnp.exp(jnp.where(incl2, diff, 0.0)), 0.0)
            a2 = jnp.where(strict2, jnp.where(fwd, btc_f, btc_b) * kq[:C] * dec, 0.0)
            a2s.append(a2)
            t2s.append(eye2 - jnp.where(level2 == 1, a2, 0.0))
            glast_f = gcc_f[C - 1:C, :]
            glast_b = gcc_b[0:1, :]
            e1_f = jnp.exp(gcc_f)
            e1_b = jnp.exp(gcc_b)
            ket = jnp.concatenate([k * jnp.exp(glast_f - gcc_f), k * jnp.exp(glast_b - gcc_b)], axis=0).T
            ak_s[pl.ds(pl.multiple_of((c * H + h) * 3 * C, 3 * C), 3 * C), :] = jnp.concatenate(
                [kq[C:] * dec, ket], axis=0).astype(bf16)
            for d, (btc, e1, glast) in enumerate(((btc_f, e1_f, glast_f), (btc_b, e1_b, glast_b))):
                ch = d * H + h
                j = (cc * H + h) * 2 + d
                rhs_s[j * C:(j + 1) * C, :] = jnp.concatenate([v * btc, k * (btc * e1)], axis=1).astype(bf16)
                wq_s[pl.ds(pl.multiple_of((c * 2 * H + ch) * 2 * C + C, C), C), :] = (q * e1).astype(bf16)
                el_s[pl.ds(pl.multiple_of((c * 2 * H + ch) * 8, 8), 8), :] = jnp.broadcast_to(
                    jnp.exp(glast), (8, LANES))

        for lv in range(2, C.bit_length()):
            rs = []
            for a2, t2 in zip(a2s, t2s):
                abd = jnp.concatenate([jnp.where(lvl_top == lv, a2, 0.0),
                                       jnp.where(lvl_bot == lv, a2, 0.0)], axis=0).astype(bf16)
                rs.append(_dot(t2.astype(bf16), abd))
            t2s = [t2 - _dot(r.astype(bf16), block_diag(t2)) for r, t2 in zip(rs, t2s)]

        for (cc, c, h, rows, g2, g2t, bt), t2 in zip(chains, t2s):
            t2b = t2.astype(bf16)
            for d in range(2):
                ch = d * H + h
                j = (cc * H + h) * 2 + d
                rhs = rhs_s[j * C:(j + 1) * C, :]
                rhs = jnp.concatenate([rhs, zrhs] if d == 0 else [zrhs, rhs], axis=0)
                uw = _dot(t2b, rhs)
                u_s[pl.ds(pl.multiple_of((c * 2 * H + ch) * C, C), C), :] = uw[:, :LANES]
                wq_s[pl.ds(pl.multiple_of((c * 2 * H + ch) * 2 * C, C), C), :] = uw[:, LANES:].astype(bf16)
        return carry

    lax.fori_loop(0, nchunk // cpi, solve_phase, 0)

    def scan_phase(i, carry):
        cs = [i if ch < H else nchunk - 1 - i for ch in range(2 * H)]
        s_old = [st_s[ch] for ch in range(2 * H)]
        r1 = [_dot(wq_s[pl.ds(pl.multiple_of((cs[ch] * 2 * H + ch) * 2 * C, 2 * C), 2 * C), :],
                   s_old[ch].astype(bf16)) for ch in range(2 * H)]
        r2s = []
        for ch in range(2 * H):
            u = u_s[pl.ds(pl.multiple_of((cs[ch] * 2 * H + ch) * C, C), C), :]
            vnb = (u - r1[ch][:C]).astype(bf16)
            rhs = jnp.concatenate([vnb, zvn] if ch < H else [zvn, vnb], axis=0)
            ak = ak_s[pl.ds(pl.multiple_of((cs[ch] * H + ch % H) * 3 * C, 3 * C), 3 * C), :]
            r2s.append(_dot(ak, rhs))
        for ch in range(2 * H):
            el = el_s[pl.ds(pl.multiple_of((cs[ch] * 2 * H + ch) * 8, 8), 8), :][0:1, :]
            st_s[ch] = s_old[ch] * el + r2s[ch][C:]
            xpad[ch, pl.ds(pl.multiple_of(cs[ch] * C, C), C), :] = r1[ch][C:] + r2s[ch][:C]
        return carry

    lax.fori_loop(0, nchunk, scan_phase, 0, unroll=4)

    onorm = onorm_ref[0]

    def fin(c, carry):
        rows = pl.ds(pl.multiple_of(c * C, C), C)
        for h in range(H):
            ls = slice(h * LANES, (h + 1) * LANES)
            o_ref[rows, ls] = _rms_rows(xpad[h, rows, :] + xpad[H + h, rows, :], onorm)
        return carry

    lax.fori_loop(0, nchunk, fin, 0, unroll=4)
    if sout_ref is not None:
        for i in range(nprev):
            sout_ref[i] = sprev_ref[i]
        for d in range(2):
            for h in range(H):
                sout_ref[nprev, d, h] = st_s[d * H + h]


def _gdn(proj, row0, conv_w, alog_p, dtb_p, onorm, state, prev_states, l, batch, seq):
    ntok = batch * seq
    width = 2 * GDN_KW + GDN_VW
    has_state = state is not None
    nprev = 0 if has_state else l
    nchunk = seq // GDN_CHUNK
    kern = functools.partial(_gdn_kernel, seq=seq, has_state=has_state, nprev=nprev)
    in_specs = [pl.BlockSpec((seq, width), lambda b: (row0 // seq + b, OFF_QKV // width)),
                pl.BlockSpec((seq, LANES), lambda b: (row0 // seq + b, OFF_SMALL // LANES)),
                pl.BlockSpec((None, width // LANES, CONV_W, LANES), lambda b: (l, 0, 0, 0)),
                pl.BlockSpec((1, 1, LANES), lambda b: (l, 0, 0)),
                pl.BlockSpec((1, 1, LANES), lambda b: (l, 0, 0)),
                pl.BlockSpec((1, 1, GDN_DV), lambda b: (l, 0, 0))]
    args = [proj, proj, conv_w, alog_p, dtb_p, onorm]
    o_spec = pl.BlockSpec((seq, GDN_VW), lambda b: (b, 0))
    o_shape = jax.ShapeDtypeStruct((ntok, GDN_VW), f32)
    st_block = (None, None, 2, GDN_HEADS, GDN_DK, GDN_DV)
    if has_state:
        in_specs.append(pl.BlockSpec(st_block, lambda b: (b, l, 0, 0, 0, 0)))
        args.append(state)
        out_specs, out_shape = o_spec, o_shape
    else:
        def stacked(n):
            return pl.BlockSpec((None, n) + st_block[2:], lambda b: (b, 0, 0, 0, 0, 0))

        if nprev:
            in_specs.append(stacked(nprev))
            args.append(prev_states)
        out_specs = [o_spec, stacked(nprev + 1)]
        out_shape = [o_shape, jax.ShapeDtypeStruct((batch, nprev + 1, 2, GDN_HEADS, GDN_DK, GDN_DV), f32)]
    return pl.pallas_call(
        kern,
        grid=(batch,),
        in_specs=in_specs,
        out_specs=out_specs,
        out_shape=out_shape,
        scratch_shapes=[pltpu.VMEM((width // LANES, seq + 16, LANES), f32),
                        pltpu.VMEM((width // LANES, seq, LANES), f32),
                        pltpu.VMEM((seq, LANES), f32),
                        pltpu.VMEM((seq, LANES), f32),
                        pltpu.VMEM((2 * GDN_HEADS, GDN_DK, GDN_DV), f32),
                        pltpu.VMEM((nchunk * 2 * GDN_HEADS * 2 * GDN_CHUNK, LANES), bf16),
                        pltpu.VMEM((nchunk * GDN_HEADS * 3 * GDN_CHUNK, LANES), bf16),
                        pltpu.VMEM((nchunk * 2 * GDN_HEADS * GDN_CHUNK, LANES), f32),
                        pltpu.VMEM((nchunk * 2 * GDN_HEADS * 8, LANES), f32),
                        pltpu.VMEM((min(GDN_SOLVE_CHUNKS, nchunk) * 2 * GDN_HEADS * GDN_CHUNK, 2 * LANES), bf16)],
        compiler_params=_cparams(1),
        name="gdn_lat" if has_state else "gdn_ctx",
    )(*args)


def _merge_kernel(x_ref, mod_ref, oa_ref, ob_ref, za_ref, zb_ref, cu_ref, cv_ref, zc_ref, gl_ref,
                  lng_ref, lnb_ref, ws_ref, bs_ref, wbr_ref, wo_ref, out_ref, sv_s):
    tm = x_ref.shape[0]
    def gelu(x_ref):
        x = x_ref[...].astype(f32)
        c1 = math.sqrt(2.0 / math.pi)
        half = 0.5 * x
        return half + half * jnp.tanh(x * (c1 + (c1 * 0.044715) * (x * x)))

    u = gelu(cu_ref)
    vf = gelu(cv_ref)
    mu = jnp.mean(vf, axis=-1, keepdims=True)
    vc = vf - mu
    var = jnp.mean(vc * vc, axis=-1, keepdims=True)
    vn = (vc * lax.rsqrt(var + EPS) * lng_ref[0] + lnb_ref[0]).astype(bf16)
    for ck in range(tm // CM_CHUNK):
        rs = slice(ck * CM_CHUNK, (ck + 1) * CM_CHUNK)
        for g in range(CM_GROUPS):
            ls = slice(g * LANES, (g + 1) * LANES)
            sv_s[rs, ls] = _dot(ws_ref[g], vn[rs, ls]) + bs_ref[:, ls]
    o_c = u * sv_s[...]

    def silu(z_ref):
        z = z_ref[...].astype(f32)
        return z * jax.nn.sigmoid(z)

    brs = (oa_ref[...] * silu(za_ref), ob_ref[...] * silu(zb_ref), o_c * silu(zc_ref))
    ysum = None
    for n in range(N_BRANCH):
        yb = _dot(brs[n].astype(bf16), wbr_ref[n])
        t = jax.nn.sigmoid(gl_ref[:, n * D_MODEL:(n + 1) * D_MODEL].astype(f32)) * yb
        ysum = t if ysum is None else ysum + t
    y = _dot(ysum.astype(bf16), wo_ref[...])
    gate = mod_ref[0][:, 2 * D_MODEL:]
    out_ref[...] = x_ref[...] + gate * y


def _merge(x, mod_l, proj, row0, o_a, o_b, lng, lnb, ws_b, bs_full, wbr_b, wo_b, l, latent, seq):
    ntok = x.shape[0]
    tm = 512

    def col(off):
        return pl.BlockSpec((tm, 512), lambda t: (row0 // tm + t, off // 512))

    def const(*idx):
        return lambda t: idx

    return pl.pallas_call(
        _merge_kernel,
        grid=(ntok // tm,),
        in_specs=[pl.BlockSpec((tm, D_MODEL), lambda t: (t, 0)),
                  pl.BlockSpec((1, 1, 3 * D_MODEL), _mod_row(latent, tm, seq)),
                  pl.BlockSpec((tm, 512), lambda t: (t, 0)),
                  pl.BlockSpec((tm, 512), lambda t: (t, 0)),
                  col(OFF_ZA), col(OFF_ZB), col(OFF_CU), col(OFF_CV), col(OFF_ZC),
                  pl.BlockSpec((tm, N_BRANCH * D_MODEL), lambda t: (row0 // tm + t, 0)),
                  pl.BlockSpec((1, 1, CM_WIDTH), const(l, 0, 0)),
                  pl.BlockSpec((1, 1, CM_WIDTH), const(l, 0, 0)),
                  pl.BlockSpec((None, CM_GROUPS, CM_CHUNK, CM_CHUNK), const(l, 0, 0, 0)),
                  pl.BlockSpec((None, CM_CHUNK, CM_WIDTH), const(l, 0, 0)),
                  pl.BlockSpec((None, N_BRANCH, BRANCH_W, D_MODEL), const(l, 0, 0, 0)),
                  pl.BlockSpec((None, D_MODEL, D_MODEL), const(l, 0, 0))],
        out_specs=pl.BlockSpec((tm, D_MODEL), lambda t: (t, 0)),
        out_shape=jax.ShapeDtypeStruct((ntok, D_MODEL), f32),
        scratch_shapes=[pltpu.VMEM((tm, CM_WIDTH), f32)],
        compiler_params=_cparams(1),
        name="merge",
    )(x, mod_l, o_a, o_b, proj, proj, proj, proj, proj, proj, lng, lnb, ws_b, bs_full, wbr_b, wo_b)


def _pad_last(x, n):
    return jnp.pad(x, [(0, 0)] * (x.ndim - 1) + [(0, n - x.shape[-1])])


def _rope_tables(seq):
    t = np.arange(seq)
    row = (t // GRID_W).astype(np.float32)
    colp = (t % GRID_W).astype(np.float32)
    nf = MLA_ROPE // 4
    inv = (ROPE_THETA ** (-np.arange(nf, dtype=np.float32) / nf)).astype(np.float32)
    cos_t = np.ones((seq, LANES), np.float32)
    s_lo = np.zeros((seq, LANES), np.float32)
    s_hi = np.zeros((seq, LANES), np.float32)
    for i, pos in enumerate((row, colp)):
        ang = (pos[:, None] * inv[None, :]).astype(np.float32)
        cs, sn = np.cos(ang), np.sin(ang)
        lo = ROPE_LANE0 + 2 * nf * i
        cos_t[:, lo:lo + nf] = cs
        cos_t[:, lo + nf:lo + 2 * nf] = cs
        s_lo[:, lo:lo + nf] = -sn
        s_hi[:, lo + nf:lo + 2 * nf] = sn
    return tuple(jnp.asarray(p) for p in (cos_t, s_lo, s_hi))


def kernel(x_prompt, x_sample, cache_ckv, cache_krope, state_gdn, c, c_ctx, norm_g, w_mod, b_mod, w_in, q_a_norm, w_uq, kv_a_norm, w_ukv, q_norm, k_norm, conv_w, a_log, dt_bias, gdn_onorm, cm_ln_g, cm_ln_b, w_s, b_s, w_branch, w_o):
    L = DEPTH
    batch, seq, _ = x_prompt.shape
    dbatch, dseq, _ = x_sample.shape
    past = cache_ckv.shape[2]

    w_t = jnp.swapaxes(w_in, 1, 2)
    wuq_p = _pad_last(w_uq.reshape(L, MLA_Q_RANK, MLA_HEADS, MLA_QK), HEAD_SLOT)
    wuq_p = wuq_p.reshape(L, MLA_Q_RANK, MLA_HEADS * HEAD_SLOT).astype(bf16)
    wukv = w_ukv.reshape(L, MLA_KV_RANK, MLA_HEADS, MLA_NOPE + MLA_V)
    wk_p = _pad_last(wukv[..., :MLA_NOPE], HEAD_SLOT).reshape(L, MLA_KV_RANK, MLA_HEADS * HEAD_SLOT).astype(bf16)
    wv_p = wukv[..., MLA_NOPE:].reshape(L, MLA_KV_RANK, MLA_WIDTH).astype(bf16)
    qn_p = _pad_last(q_norm, HEAD_SLOT).reshape(L, 1, HEAD_SLOT)
    kn_p = _pad_last(k_norm, HEAD_SLOT).reshape(L, 1, HEAD_SLOT)
    attn_w = (q_a_norm.reshape(L, 1, MLA_Q_RANK), wuq_p, kv_a_norm.reshape(L, 1, MLA_KV_RANK),
              wk_p, wv_p, qn_p, kn_p)
    cache_kr_p = jnp.pad(cache_krope, [(0, 0)] * 3 + [(ROPE_LANE0, LANES - ROPE_LANE0 - MLA_ROPE)])
    rope_tabs = _rope_tables(dseq)
    alog_p = _pad_last(a_log.reshape(L, 1, 2 * GDN_HEADS), LANES)
    dtb_p = _pad_last(dt_bias.reshape(L, 1, 2 * GDN_HEADS), LANES)
    onorm = gdn_onorm.reshape(L, 1, GDN_DV)
    conv_w = jnp.swapaxes(conv_w.reshape(L, CONV_W, -1, LANES), 1, 2)
    lng =cm_ln_g.reshape(L, 1, CM_WIDTH)
    lnb = cm_ln_b.reshape(L, 1, CM_WIDTH)
    ws_b = w_s.astype(bf16)
    bs_full = jnp.repeat(jnp.swapaxes(b_s, 1, 2), CM_WIDTH // CM_GROUPS, axis=2)
    wbr_b = w_branch.astype(bf16)
    wo_b = w_o.astype(bf16)
    norm_g3 = norm_g.reshape(L, 1, D_MODEL)

    c8 = jnp.concatenate([c, c_ctx[None, :], jnp.zeros((8 - dbatch - 1, D_MODEL), f32)], axis=0)
    mod = _modulation(c8, w_mod, b_mod)

    yp = x_prompt.reshape(batch * seq, D_MODEL)
    ys = x_sample.reshape(dbatch * dseq, D_MODEL)
    caches, states = (), None
    for l in range(L):
        mod_l = mod[l].reshape(8, 1, 3 * D_MODEL)
        p16, p32 = _inproj(yp, ys, mod_l, norm_g3, w_t, l, dseq)
        lat0 = batch * seq
        o_a, new_ckv, new_kr = _attn_ctx(p32, caches, attn_w, l, batch, seq)
        caches = (new_ckv, new_kr)
        o_b, states = _gdn(p32, 0, conv_w, alog_p, dtb_p, onorm, None, states, l, batch, seq)
        yp = _merge(yp, mod_l, p16, 0, o_a, o_b, lng, lnb, ws_b, bs_full, wbr_b, wo_b, l, False, seq)
        o_a = _attn_lat(p32, lat0, cache_ckv, cache_kr_p, rope_tabs, attn_w, l, dbatch, dseq, past)
        o_b = _gdn(p32, lat0, conv_w, alog_p, dtb_p, onorm, state_gdn, None, l, dbatch, dseq)
        ys = _merge(ys, mod_l, p16, lat0, o_a, o_b, lng, lnb, ws_b, bs_full, wbr_b, wo_b, l, True, dseq)
    return (yp.reshape(batch, seq, D_MODEL), ys.reshape(dbatch, dseq, D_MODEL), caches[0], caches[1], states)
```

```python
import functools
import math

import numpy as np

import jax
import jax.numpy as jnp
from jax import lax
from jax.experimental import pallas as pl
from jax.experimental.pallas import tpu as pltpu

D_MODEL = 1024
DEPTH = 2
GRID_W = 64
EPS = 1e-6
MLA_HEADS = 8
MLA_NOPE = 64
MLA_ROPE = 32
MLA_QK = MLA_NOPE + MLA_ROPE
MLA_V = 64
MLA_Q_RANK = 384
MLA_KV_RANK = 256
MLA_WIDTH = MLA_HEADS * MLA_V
ROPE_THETA = 10000.0
GDN_HEADS = 4
GDN_DK = 128
GDN_DV = 128
GDN_KW = GDN_HEADS * GDN_DK
GDN_VW = GDN_HEADS * GDN_DV
GDN_CHUNK = 64
CONV_W = 5
GDN_SOLVE_CHUNKS = 4
CM_GROUPS = 4
CM_CHUNK = 128
CM_WIDTH = 512
N_BRANCH = 3
BRANCH_W = 512
SPLIT_SIZES = (MLA_Q_RANK, MLA_KV_RANK, MLA_ROPE, MLA_WIDTH,
               2 * GDN_KW + GDN_VW, 2 * GDN_HEADS, 2 * GDN_HEADS, GDN_VW,
               CM_WIDTH, CM_WIDTH, CM_WIDTH, N_BRANCH * D_MODEL)

LANES = 128
SUBLANES = 8
HEAD_SLOT = LANES
ROPE_LANE0 = MLA_NOPE

OFF_GL = 0
OFF_ZA = OFF_GL + N_BRANCH * D_MODEL
OFF_ZB = OFF_ZA + 512
OFF_CU = OFF_ZB + 512
OFF_CV = OFF_CU + 512
OFF_ZC = OFF_CV + 512
P16_W = OFF_ZC + 512
OFF_QKV = 0
OFF_SMALL = OFF_QKV + 2 * GDN_KW + GDN_VW
OFF_CQ = OFF_SMALL + LANES
OFF_CKV = OFF_CQ + MLA_Q_RANK
P32_W = OFF_CKV + MLA_KV_RANK
PROJ_W = P16_W + P32_W
GDN_IN_W = OFF_CQ
MLA_IN_W = P32_W - OFF_SMALL
MLA_SMALL = slice(0, LANES)
MLA_CQ = slice(OFF_CQ - OFF_SMALL, OFF_CKV - OFF_SMALL)
MLA_CKV = slice(OFF_CKV - OFF_SMALL, MLA_IN_W)

VMEM_LIMIT = 56 * 1024 * 1024

PK_ROWS = 4
PK = {"norm_g": (0, 0, D_MODEL),
      "q_a_norm": (1, 0, MLA_Q_RANK), "kv_a_norm": (1, 384, MLA_KV_RANK), "q_norm": (1, 640, HEAD_SLOT),
      "k_norm": (1, 768, HEAD_SLOT), "gdn_onorm": (1, 896, GDN_DV),
      "cm_ln_g": (2, 0, CM_WIDTH), "cm_ln_b": (2, 512, CM_WIDTH),
      "a_log": (3, 0, LANES), "dt_bias": (3, LANES, LANES)}


def _pk(pk_ref, name):
    row, lane0, width = PK[name]
    return pk_ref[row:row + 1, lane0:lane0 + width]


def _pk_spec(l):
    return pl.BlockSpec((None, PK_ROWS, D_MODEL), lambda *g: (l, 0, 0))

f32 = jnp.float32
bf16 = jnp.bfloat16


def _cparams(n_axes):
    return pltpu.CompilerParams(dimension_semantics=("arbitrary",) * n_axes,
                                vmem_limit_bytes=VMEM_LIMIT)


def _dot(a, b):
    return jnp.dot(a, b, preferred_element_type=f32)


def _dot_nt(a, b):
    return lax.dot_general(a, b, (((1,), (1,)), ((), ())), preferred_element_type=f32)


def _dot_tn(a, b):
    return lax.dot_general(a, b, (((0,), (0,)), ((), ())), preferred_element_type=f32)


def _rms_rows(x, g, n=None):
    n = x.shape[-1] if n is None else n
    ms = jnp.sum(x * x, axis=-1, keepdims=True) * (1.0 / n)
    return x * lax.rsqrt(ms + EPS) * g


def _mod_kernel(c_ref, w_ref, b_ref, o_ref):
    a = c_ref[...]
    a = (a * jax.nn.sigmoid(a)).astype(bf16)
    o_ref[0] = _dot(a, w_ref[0].astype(bf16)) + b_ref[0]


def _modulation(c8, w_mod, b_mod):
    tn = 1536
    return pl.pallas_call(
        _mod_kernel,
        grid=(DEPTH, 3 * D_MODEL // tn),
        in_specs=[pl.BlockSpec((8, D_MODEL), lambda l, n: (0, 0)),
                  pl.BlockSpec((1, D_MODEL, tn), lambda l, n: (l, 0, n)),
                  pl.BlockSpec((1, 1, tn), lambda l, n: (l, 0, n))],
        out_specs=pl.BlockSpec((1, 8, tn), lambda l, n: (l, 0, n)),
        out_shape=jax.ShapeDtypeStruct((DEPTH, 8, 3 * D_MODEL), f32),
        compiler_params=_cparams(2),
        name="modulation",
    )(c8, w_mod, b_mod.reshape(DEPTH, 1, 3 * D_MODEL))


W_CHUNK = 512


def _w_in_moves():
    offs = [0]
    for s in SPLIT_SIZES:
        offs.append(offs[-1] + s)
    cq, ckv, krope, z_a, qkv, ga, gb, z_b, cu, cv, z_c, gl = offs[:-1]
    moves = [(gl, N_BRANCH * D_MODEL, OFF_GL), (z_a, 512, OFF_ZA), (z_b, 512, OFF_ZB), (cu, 512, OFF_CU),
             (cv, 512, OFF_CV), (z_c, 512, OFF_ZC), (qkv, 2 * GDN_KW + GDN_VW, P16_W + OFF_QKV),
             (cq, MLA_Q_RANK, P16_W + OFF_CQ), (ckv, MLA_KV_RANK, P16_W + OFF_CKV)]
    return moves, ga, krope


def _w_chunks():
    moves, ga, krope = _w_in_moves()
    chunks = []
    for (a, w, d) in moves:
        for o in range(0, w, W_CHUNK):
            chunks.append((a + o, min(W_CHUNK, w - o), d + o))
    small0 = P16_W + OFF_SMALL
    chunks.append((ga, 4 * GDN_HEADS, small0))
    chunks.append((krope, MLA_ROPE, small0 + ROPE_LANE0))
    return chunks


def _inproj_kernel(xc_ref, xl_ref, mod_ref, pk_ref, wt_hbm, o16_ref, o32_ref, w_s, stage, sem, *, layer, nctx):
    t = pl.program_id(0)

    @pl.when(t == 0)
    def _():
        chunks = _w_chunks()

        def copy(j):
            src, n, _ = chunks[j]
            return pltpu.make_async_copy(wt_hbm.at[layer, pl.ds(src, n), :], stage.at[j % 2, pl.ds(0, n), :],
                                         sem.at[j % 2])

        copy(0).start()
        for j, (_, n, dst) in enumerate(chunks):
            if j + 1 < len(chunks):
                copy(j + 1).start()
            copy(j).wait()
            w_s[dst:dst + n, :] = stage[j % 2, 0:n, :].astype(bf16)
        small0 = P16_W + OFF_SMALL
        for lo, hi in ((4 * GDN_HEADS, ROPE_LANE0), (ROPE_LANE0 + MLA_ROPE, LANES)):
            w_s[small0 + lo:small0 + hi, :] = jnp.zeros((hi - lo, D_MODEL), bf16)

    x = jnp.where(t < nctx, xc_ref[...], xl_ref[...])
    m = mod_ref[0]
    shift = m[:, :D_MODEL]
    scale = m[:, D_MODEL:2 * D_MODEL]
    h = (_rms_rows(x, _pk(pk_ref, "norm_g")) * (1.0 + scale) + shift).astype(bf16)
    for a in range(0, P16_W, 512):
        o16_ref[:, a:a + 512] = _dot_nt(h, w_s[a:a + 512, :]).astype(bf16)
    for a in range(0, P32_W, 512):
        b = min(a + 512, P32_W)
        o32_ref[:, a:b] = _dot_nt(h, w_s[P16_W + a:P16_W + b, :])


def _inproj(xc, xl, mod_l, norm_g, w_t, l, dseq):
    tm = 512
    nctx = xc.shape[0] // tm
    nlat = xl.shape[0] // tm
    ntok = xc.shape[0] + xl.shape[0]
    kern = functools.partial(_inproj_kernel, layer=l, nctx=nctx)
    return pl.pallas_call(
        kern,
        grid=(nctx + nlat,),
        in_specs=[pl.BlockSpec((tm, D_MODEL), lambda t: (jnp.minimum(t, nctx - 1), 0)),
                  pl.BlockSpec((tm, D_MODEL), lambda t: (jnp.maximum(t - nctx, 0), 0)),
                  pl.BlockSpec((1, 1, 3 * D_MODEL),
                               lambda t: (jnp.where(t < nctx, 4, ((t - nctx) * tm) // dseq), 0, 0)),
                  _pk_spec(l),
                  pl.BlockSpec(memory_space=pl.ANY)],
        out_specs=[pl.BlockSpec((tm, P16_W), lambda t: (t, 0)),
                   pl.BlockSpec((tm, P32_W), lambda t: (t, 0))],
        out_shape=[jax.ShapeDtypeStruct((ntok, P16_W), bf16),
                   jax.ShapeDtypeStruct((ntok, P32_W), f32)],
        scratch_shapes=[pltpu.VMEM((PROJ_W, D_MODEL), bf16),
                        pltpu.VMEM((2, W_CHUNK, D_MODEL), f32),
                        pltpu.SemaphoreType.DMA((2,))],
        compiler_params=_cparams(1),
        name="inproj",
    )(xc, xl, mod_l, norm_g, w_t)


def _mod_row(latent, tm, seq):
    if latent:
        return lambda t: ((t * tm) // seq, 0, 0)
    return lambda t: (4, 0, 0)


def _rope(x, cos_t, sin_lo, sin_hi):
    return x * cos_t + pltpu.roll(x, LANES - 8, 1) * sin_lo + pltpu.roll(x, 8, 1) * sin_hi


def _build_kv(ckvn_b, kr, wk_ref, wv_ref, knorm, rope, k_s, v_s, r0):
    n = ckvn_b.shape[0]
    kfull = _dot(ckvn_b, wk_ref[...])
    v_s[r0:r0 + n, :] = _dot(ckvn_b, wv_ref[...]).astype(bf16)
    krg = kr * knorm
    if rope is not None:
        krg = _rope(krg, *rope)
    kr_ss = jnp.sum(kr * kr, axis=-1, keepdims=True)
    for h in range(MLA_HEADS):
        sl = slice(h * HEAD_SLOT, (h + 1) * HEAD_SLOT)
        kn = kfull[:, sl]
        ms = (jnp.sum(kn * kn, axis=-1, keepdims=True) + kr_ss) * (1.0 / MLA_QK)
        k_s[r0:r0 + n, sl] = ((kn * knorm + krg) * lax.rsqrt(ms + EPS)).astype(bf16)


def _rope_lane_mask(shape):
    lane = lax.broadcasted_iota(jnp.int32, shape, 1)
    return (lane >= ROPE_LANE0) & (lane < ROPE_LANE0 + MLA_ROPE)


def _attend_block(qa_b, wuq_ref, qnorm, rope, k_s, v_s, o_ref):
    tq = qa_b.shape[0]
    qfull = _dot(qa_b, wuq_ref[...])
    qgain = qnorm * (math.log2(math.e) / math.sqrt(MLA_QK))
    lane = lax.broadcasted_iota(jnp.int32, (tq, LANES), 1)

    def scores(h):
        sl = slice(h * HEAD_SLOT, (h + 1) * HEAD_SLOT)
        qh = _rms_rows(qfull[:, sl], qgain, n=MLA_QK)
        if rope is not None:
            qh = _rope(qh, *rope)
        return _dot_nt(qh.astype(bf16), k_s[:, sl])

    s_next = scores(0)
    outs = []
    for h in range(MLA_HEADS):
        s = s_next
        if h + 1 < MLA_HEADS:
            s_next = scores(h + 1)
        p = jnp.exp2(s - jnp.max(s, axis=-1, keepdims=True))
        den = jnp.sum(p, axis=-1, keepdims=True)
        hp = h // 2
        outs.append(_dot(p.astype(bf16), v_s[:, hp * LANES:(hp + 1) * LANES]) / den)
        if h % 2 == 1:
            o_ref[:, hp * LANES:(hp + 1) * LANES] = jnp.where(lane < MLA_V, outs[h - 1], outs[h])


def _attn_ctx_kernel(*refs, layer):
    if layer:
        mla_ref, pckvn_ref, pkr_ref = refs[:3]
        refs = refs[3:]
    else:
        mla_ref = refs[0]
        refs = refs[1:]
    pk_ref, wuq_ref, wk_ref, wv_ref, o_ref, ckvn_ref, kr_ref, k_s, v_s = refs
    for i in range(layer):
        ckvn_ref[i] = pckvn_ref[i]
        kr_ref[i] = pkr_ref[i]
    small = mla_ref[:, MLA_SMALL]
    kr = jnp.where(_rope_lane_mask(small.shape), small, 0.0)
    kr_ref[layer] = small[:, ROPE_LANE0:ROPE_LANE0 + MLA_ROPE]
    ckvn = _rms_rows(mla_ref[:, MLA_CKV], _pk(pk_ref, "kv_a_norm"))
    ckvn_ref[layer] = ckvn
    _build_kv(ckvn.astype(bf16), kr, wk_ref, wv_ref, _pk(pk_ref, "k_norm"), None, k_s, v_s, 0)
    qa = _rms_rows(mla_ref[:, MLA_CQ], _pk(pk_ref, "q_a_norm")).astype(bf16)
    _attend_block(qa, wuq_ref, _pk(pk_ref, "q_norm"), None, k_s, v_s, o_ref)


def _attn_lat_kernel(mla_ref, cckv_ref, ckr_ref, cos_ref, slo_ref, shi_ref,
                     pk_ref, wuq_ref, wk_ref, wv_ref,
                     o_ref, k_s, v_s, *, seq, past, tq):
    qi = pl.program_id(1)
    rb = 256

    @pl.when(qi == 0)
    def _():
        _build_kv(cckv_ref[...].astype(bf16), ckr_ref[...], wk_ref, wv_ref, _pk(pk_ref, "k_norm"), None,
                  k_s, v_s, 0)
        for r in range(seq // rb):
            rs = slice(r * rb, (r + 1) * rb)
            small = mla_ref[rs, MLA_SMALL]
            kr = jnp.where(_rope_lane_mask(small.shape), small, 0.0)
            ckvn = _rms_rows(mla_ref[rs, MLA_CKV], _pk(pk_ref, "kv_a_norm"))
            rope = (cos_ref[rs, :], slo_ref[rs, :], shi_ref[rs, :])
            _build_kv(ckvn.astype(bf16), kr, wk_ref, wv_ref, _pk(pk_ref, "k_norm"), rope, k_s, v_s, past + r * rb)

    rows = pl.ds(pl.multiple_of(qi * tq, tq), tq)
    qa = _rms_rows(mla_ref[rows, MLA_CQ], _pk(pk_ref, "q_a_norm")).astype(bf16)
    rope = (cos_ref[rows, :], slo_ref[rows, :], shi_ref[rows, :])
    _attend_block(qa, wuq_ref, _pk(pk_ref, "q_norm"), rope, k_s, v_s, o_ref)


def _attn_weight_specs(l, nidx):
    z = (0,) * (nidx - 1)

    def const(*idx):
        return lambda *g: idx

    return [_pk_spec(l),
            pl.BlockSpec((None, MLA_Q_RANK, MLA_HEADS * HEAD_SLOT), const(l, 0, 0)),
            pl.BlockSpec((None, MLA_KV_RANK, MLA_HEADS * HEAD_SLOT), const(l, 0, 0)),
            pl.BlockSpec((None, MLA_KV_RANK, MLA_WIDTH), const(l, 0, 0))]


def _attn_ctx(proj, prev, wts, l, batch, seq):
    ntok = batch * seq

    def stacked(n, width):
        return pl.BlockSpec((None, n, seq, width), lambda b: (b, 0, 0, 0))

    prev_specs = [stacked(l, MLA_KV_RANK), stacked(l, MLA_ROPE)] if l else []
    return pl.pallas_call(
        functools.partial(_attn_ctx_kernel, layer=l),
        grid=(batch,),
        in_specs=[pl.BlockSpec((seq, MLA_IN_W), lambda b: (b, OFF_SMALL // MLA_IN_W))]
        + prev_specs + _attn_weight_specs(l, 1),
        out_specs=[pl.BlockSpec((seq, MLA_WIDTH), lambda b: (b, 0)),
                   stacked(l + 1, MLA_KV_RANK), stacked(l + 1, MLA_ROPE)],
        out_shape=[jax.ShapeDtypeStruct((ntok, MLA_WIDTH), f32),
                   jax.ShapeDtypeStruct((batch, l + 1, seq, MLA_KV_RANK), f32),
                   jax.ShapeDtypeStruct((batch, l + 1, seq, MLA_ROPE), f32)],
        scratch_shapes=[pltpu.VMEM((seq, MLA_HEADS * HEAD_SLOT), bf16),
                        pltpu.VMEM((seq, MLA_WIDTH), bf16)],
        compiler_params=_cparams(1),
        name="attn_ctx",
    )(proj, *prev, *wts)


def _attn_lat(proj, row0, cache_ckv, cache_kr_p, rope_tabs, wts, l, batch, seq, past):
    ntok = batch * seq
    tq = 256
    nq = seq // tq
    kern = functools.partial(_attn_lat_kernel, seq=seq, past=past, tq=tq)
    tab = pl.BlockSpec((seq, LANES), lambda b, q: (0, 0))
    return pl.pallas_call(
        kern,
        grid=(batch, nq),
        in_specs=[pl.BlockSpec((seq, MLA_IN_W), lambda b, q: (row0 // seq + b, OFF_SMALL // MLA_IN_W)),
                  pl.BlockSpec((None, None, past, MLA_KV_RANK), lambda b, q: (b, l, 0, 0)),
                  pl.BlockSpec((None, None, past, LANES), lambda b, q: (b, l, 0, 0)),
                  tab, tab, tab]
        + _attn_weight_specs(l, 2),
        out_specs=pl.BlockSpec((tq, MLA_WIDTH), lambda b, q: (b * nq + q, 0)),
        out_shape=jax.ShapeDtypeStruct((ntok, MLA_WIDTH), f32),
        scratch_shapes=[pltpu.VMEM((past + seq, MLA_HEADS * HEAD_SLOT), bf16),
                        pltpu.VMEM((past + seq, MLA_WIDTH), bf16)],
        compiler_params=_cparams(2),
        name="attn_lat",
    )(proj, cache_ckv, cache_kr_p, *rope_tabs, *wts)


def _split3(x):
    hi = x.astype(bf16)
    r1 = x - hi.astype(f32)
    mid = r1.astype(bf16)
    lo = (r1 - mid.astype(f32)).astype(bf16)
    return hi, mid, lo


def _tri_cumsum(tri_b, x):
    hi, mid, lo = _split3(x)
    return _dot(tri_b, hi) + _dot(tri_b, mid) + _dot(tri_b, lo)


def _lane_bcast(x, c):
    return jnp.broadcast_to(x[:, c:c + 1], (x.shape[0], LANES))


def _gdn_kernel(*refs, seq, has_state, nprev):
    gin_ref, cw_ref, pk_ref = refs[:3]
    refs = refs[3:]
    s0_ref = sprev_ref = sout_ref = None
    if has_state:
        s0_ref, o_ref = refs[:2]
        refs = refs[2:]
    else:
        if nprev:
            sprev_ref = refs[0]
            refs = refs[1:]
        o_ref, sout_ref = refs[:2]
        refs = refs[2:]
    xpad, qkv_s, g_s, b_s, st_s, wq_s, ak_s, u_s, el_s, rhs_s = refs
    C = GDN_CHUNK
    nchunk = seq // C
    H = GDN_HEADS
    width = 2 * GDN_KW + GDN_VW
    halo = 8

    for j in range(width // LANES):
        xpad[j, 0:halo, :] = jnp.zeros((halo, LANES), f32)
        xpad[j, halo + seq:, :] = jnp.zeros((halo, LANES), f32)
        xpad[j, halo:halo + seq, :] = gin_ref[:, j * LANES:(j + 1) * LANES]
    if has_state:
        for d in range(2):
            for h in range(H):
                st_s[d * H + h] = s0_ref[d, h]
    else:
        st_s[...] = jnp.zeros((2 * H, GDN_DK, GDN_DV), f32)

    neg_a = -jnp.exp(_pk(pk_ref, "a_log"))
    dtb = _pk(pk_ref, "dt_bias")

    def conv_tile(j, l2norm):
        w = cw_ref[j]
        post = jnp.where(j < H, GDN_DK ** -0.5, 1.0)
        for c in range(nchunk):
            base = halo - CONV_W // 2 + c * C
            y = xpad[j, base:base + C, :] * w[0:1]
            for tap in range(1, CONV_W):
                y = y + xpad[j, base + tap:base + tap + C, :] * w[tap:tap + 1]
            y = y * jax.nn.sigmoid(y)
            if l2norm:
                y = y * (lax.rsqrt(jnp.sum(y * y, axis=-1, keepdims=True) + EPS) * post)
            qkv_s[j, c * C:(c + 1) * C, :] = y

    def conv_qk(j, carry):
        conv_tile(j, True)
        return carry

    def conv_v(j, carry):
        conv_tile(j, False)
        return carry

    tile_unroll = 4 if nchunk <= 4 else 1
    lax.fori_loop(0, 2 * H, conv_qk, 0, unroll=tile_unroll)
    lax.fori_loop(2 * H, 3 * H, conv_v, 0, unroll=tile_unroll)

    def prep(c, carry):
        r0 = pl.multiple_of(c * C, C)
        sm = gin_ref[pl.ds(r0, C), OFF_SMALL:OFF_SMALL + LANES]
        z = sm + dtb
        g_s[pl.ds(r0, C), :] = neg_a * (jnp.maximum(z, 0.0) + jnp.log1p(jnp.exp(-jnp.abs(z))))
        b_s[pl.ds(r0, C), :] = pltpu.roll(jax.nn.sigmoid(sm), LANES - 2 * H, 1)
        return carry

    lax.fori_loop(0, nchunk, prep, 0, unroll=4)

    ri = lax.broadcasted_iota(jnp.int32, (C, LANES), 0)
    cl = lax.broadcasted_iota(jnp.int32, (C, LANES), 1)
    fwd = cl < C
    cj = cl & (C - 1)
    eye2 = (ri == cj).astype(f32)
    incl2 = (fwd & (ri >= cj)) | (~fwd & (ri <= cj))
    strict2 = (fwd & (ri > cj)) | (~fwd & (ri < cj))
    xor = ri ^ cj
    level2 = sum((xor >= (1 << b)).astype(jnp.int32) for b in range(C.bit_length() - 1))
    lvl_top = jnp.where(fwd, level2, 0)
    lvl_bot = jnp.where(fwd, 0, level2)
    r2 = lax.broadcasted_iota(jnp.int32, (2 * C, C), 0)
    c2 = lax.broadcasted_iota(jnp.int32, (2 * C, C), 1)
    tri2 = (((r2 < C) & (r2 >= c2)) | ((r2 >= C) & (r2 - C <= c2))).astype(bf16)
    zrhs = jnp.zeros((C, 2 * LANES), bf16)
    zvn = jnp.zeros((C, LANES), bf16)
    fwd_row = fwd[0:1, :]
    cpi = min(GDN_SOLVE_CHUNKS, nchunk)

    def block_diag(x):
        return jnp.concatenate([jnp.where(fwd, x, 0.0), jnp.where(fwd, 0.0, x)], axis=0).astype(bf16)

    def solve_phase(i, carry):
        chains = []
        for cc in range(cpi):
            c = i * cpi + cc
            rows = pl.ds(pl.multiple_of(c * C, C), C)
            g2 = _tri_cumsum(tri2, g_s[rows, :])
            g2t = g2.T
            bt = b_s[rows, :]
            for h in range(H):
                chains.append((cc, c, h, rows, g2, g2t, bt))

        a2s, t2s = [], []
        for (cc, c, h, rows, g2, g2t, bt) in chains:
            q = qkv_s[h, rows, :]
            k = qkv_s[H + h, rows, :]
            v = qkv_s[2 * H + h, rows, :]
            kq = _dot_nt(jnp.concatenate([k, q], axis=0).astype(bf16),
                         jnp.concatenate([k, k], axis=0).astype(bf16))
            gcc_f = _lane_bcast(g2[:C], h)
            gcc_b = _lane_bcast(g2[C:], H + h)
            btc_f = _lane_bcast(bt, h)
            btc_b = _lane_bcast(bt, H + h)
            grow = jnp.where(fwd_row, g2t[h:h + 1, :], g2t[H + h:H + h + 1, :])
            diff = jnp.where(fwd, gcc_f, gcc_b) - grow
            dec = jnp.where(incl2, jnp.exp(jnp.where(incl2, diff, 0.0)), 0.0)
            a2 = jnp.where(strict2, jnp.where(fwd, btc_f, btc_b) * kq[:C] * dec, 0.0)
            a2s.append(a2)
            t2s.append(eye2 - jnp.where(level2 == 1, a2, 0.0))
            glast_f = gcc_f[C - 1:C, :]
            glast_b = gcc_b[0:1, :]
            e1_f = jnp.exp(gcc_f)
            e1_b = jnp.exp(gcc_b)
            ket = jnp.concatenate([k * jnp.exp(glast_f - gcc_f), k * jnp.exp(glast_b - gcc_b)], axis=0).T
            ak_s[pl.ds(pl.multiple_of((c * H + h) * 3 * C, 3 * C), 3 * C), :] = jnp.concatenate(
                [kq[C:] * dec, ket], axis=0).astype(bf16)
            for d, (btc, e1, glast) in enumerate(((btc_f, e1_f, glast_f), (btc_b, e1_b, glast_b))):
                ch = d * H + h
                j = (cc * H + h) * 2 + d
                rhs_s[j * C:(j + 1) * C, :] = jnp.concatenate([v * btc, k * (btc * e1)], axis=1).astype(bf16)
                wq_s[pl.ds(pl.multiple_of((c * 2 * H + ch) * 2 * C + C, C), C), :] = (q * e1).astype(bf16)
                el_s[pl.ds(pl.multiple_of((c * 2 * H + ch) * 8, 8), 8), :] = jnp.broadcast_to(
                    jnp.exp(glast), (8, LANES))

        for lv in range(2, C.bit_length()):
            rs = []
            for a2, t2 in zip(a2s, t2s):
                abd = jnp.concatenate([jnp.where(lvl_top == lv, a2, 0.0),
                                       jnp.where(lvl_bot == lv, a2, 0.0)], axis=0).astype(bf16)
                rs.append(_dot(t2.astype(bf16), abd))
            t2s = [t2 - _dot(r.astype(bf16), block_diag(t2)) for r, t2 in zip(rs, t2s)]

        for (cc, c, h, rows, g2, g2t, bt), t2 in zip(chains, t2s):
            t2b = t2.astype(bf16)
            for d in range(2):
                ch = d * H + h
                j = (cc * H + h) * 2 + d
                rhs = rhs_s[j * C:(j + 1) * C, :]
                rhs = jnp.concatenate([rhs, zrhs] if d == 0 else [zrhs, rhs], axis=0)
                uw = _dot(t2b, rhs)
                u_s[pl.ds(pl.multiple_of((c * 2 * H + ch) * C, C), C), :] = uw[:, :LANES]
                wq_s[pl.ds(pl.multiple_of((c * 2 * H + ch) * 2 * C, C), C), :] = uw[:, LANES:].astype(bf16)
        return carry

    lax.fori_loop(0, nchunk // cpi, solve_phase, 0)

    def scan_phase(i, carry):
        cs = [i if ch < H else nchunk - 1 - i for ch in range(2 * H)]
        s_old = [st_s[ch] for ch in range(2 * H)]
        r1 = [_dot(wq_s[pl.ds(pl.multiple_of((cs[ch] * 2 * H + ch) * 2 * C, 2 * C), 2 * C), :],
                   s_old[ch].astype(bf16)) for ch in range(2 * H)]
        r2s = []
        for ch in range(2 * H):
            u = u_s[pl.ds(pl.multiple_of((cs[ch] * 2 * H + ch) * C, C), C), :]
            vnb = (u - r1[ch][:C]).astype(bf16)
            rhs = jnp.concatenate([vnb, zvn] if ch < H else [zvn, vnb], axis=0)
            ak = ak_s[pl.ds(pl.multiple_of((cs[ch] * H + ch % H) * 3 * C, 3 * C), 3 * C), :]
            r2s.append(_dot(ak, rhs))
        for ch in range(2 * H):
            el = el_s[pl.ds(pl.multiple_of((cs[ch] * 2 * H + ch) * 8, 8), 8), :][0:1, :]
            st_s[ch] = s_old[ch] * el + r2s[ch][C:]
            xpad[ch, pl.ds(pl.multiple_of(cs[ch] * C, C), C), :] = r1[ch][C:] + r2s[ch][:C]
        return carry

    lax.fori_loop(0, nchunk, scan_phase, 0, unroll=4)

    onorm = _pk(pk_ref, "gdn_onorm")

    def fin(c, carry):
        rows = pl.ds(pl.multiple_of(c * C, C), C)
        for h in range(H):
            ls = slice(h * LANES, (h + 1) * LANES)
            o_ref[rows, ls] = _rms_rows(xpad[h, rows, :] + xpad[H + h, rows, :], onorm)
        return carry

    lax.fori_loop(0, nchunk, fin, 0, unroll=4)
    if sout_ref is not None:
        for i in range(nprev):
            sout_ref[i] = sprev_ref[i]
        for d in range(2):
            for h in range(H):
                sout_ref[nprev, d, h] = st_s[d * H + h]


def _gdn(proj, row0, conv_w, pack, state, prev_states, l, batch, seq):
    ntok = batch * seq
    width = 2 * GDN_KW + GDN_VW
    has_state = state is not None
    nprev = 0 if has_state else l
    nchunk = seq // GDN_CHUNK
    kern = functools.partial(_gdn_kernel, seq=seq, has_state=has_state, nprev=nprev)
    in_specs = [pl.BlockSpec((seq, GDN_IN_W), lambda b: (row0 // seq + b, OFF_QKV // GDN_IN_W)),
                pl.BlockSpec((None, width // LANES, CONV_W, LANES), lambda b: (l, 0, 0, 0)),
                _pk_spec(l)]
    args = [proj, conv_w, pack]
    o_spec = pl.BlockSpec((seq, GDN_VW), lambda b: (b, 0))
    o_shape = jax.ShapeDtypeStruct((ntok, GDN_VW), f32)
    st_block = (None, None, 2, GDN_HEADS, GDN_DK, GDN_DV)
    if has_state:
        in_specs.append(pl.BlockSpec(st_block, lambda b: (b, l, 0, 0, 0, 0)))
        args.append(state)
        out_specs, out_shape = o_spec, o_shape
    else:
        def stacked(n):
            return pl.BlockSpec((None, n) + st_block[2:], lambda b: (b, 0, 0, 0, 0, 0))

        if nprev:
            in_specs.append(stacked(nprev))
            args.append(prev_states)
        out_specs = [o_spec, stacked(nprev + 1)]
        out_shape = [o_shape, jax.ShapeDtypeStruct((batch, nprev + 1, 2, GDN_HEADS, GDN_DK, GDN_DV), f32)]
    return pl.pallas_call(
        kern,
        grid=(batch,),
        in_specs=in_specs,
        out_specs=out_specs,
        out_shape=out_shape,
        scratch_shapes=[pltpu.VMEM((width // LANES, seq + 16, LANES), f32),
                        pltpu.VMEM((width // LANES, seq, LANES), f32),
                        pltpu.VMEM((seq, LANES), f32),
                        pltpu.VMEM((seq, LANES), f32),
                        pltpu.VMEM((2 * GDN_HEADS, GDN_DK, GDN_DV), f32),
                        pltpu.VMEM((nchunk * 2 * GDN_HEADS * 2 * GDN_CHUNK, LANES), bf16),
                        pltpu.VMEM((nchunk * GDN_HEADS * 3 * GDN_CHUNK, LANES), bf16),
                        pltpu.VMEM((nchunk * 2 * GDN_HEADS * GDN_CHUNK, LANES), f32),
                        pltpu.VMEM((nchunk * 2 * GDN_HEADS * 8, LANES), f32),
                        pltpu.VMEM((min(GDN_SOLVE_CHUNKS, nchunk) * 2 * GDN_HEADS * GDN_CHUNK, 2 * LANES), bf16)],
        compiler_params=_cparams(1),
        name="gdn_lat" if has_state else "gdn_ctx",
    )(*args)


def _merge_kernel(x_ref, mod_ref, oa_ref, ob_ref, p_ref, pk_ref, ws_ref, bs_ref, wbr_ref, wo_ref,
                  out_ref, sv_s):
    tm = x_ref.shape[0]

    def group(off, width=512):
        return p_ref[:, off:off + width].astype(f32)

    def gelu(x):
        c1 = math.sqrt(2.0 / math.pi)
        half = 0.5 * x
        return half + half * jnp.tanh(x * (c1 + (c1 * 0.044715) * (x * x)))

    u = gelu(group(OFF_CU))
    vf = gelu(group(OFF_CV))
    mu = jnp.mean(vf, axis=-1, keepdims=True)
    vc = vf - mu
    var = jnp.mean(vc * vc, axis=-1, keepdims=True)
    vn = (vc * lax.rsqrt(var + EPS) * _pk(pk_ref, "cm_ln_g") + _pk(pk_ref, "cm_ln_b")).astype(bf16)
    for ck in range(tm // CM_CHUNK):
        rs = slice(ck * CM_CHUNK, (ck + 1) * CM_CHUNK)
        for g in range(CM_GROUPS):
            ls = slice(g * LANES, (g + 1) * LANES)
            sv_s[rs, ls] = _dot(ws_ref[g], vn[rs, ls]) + bs_ref[:, ls]
    o_c = u * sv_s[...]

    def silu(z):
        return z * jax.nn.sigmoid(z)

    brs = (oa_ref[...] * silu(group(OFF_ZA)), ob_ref[...] * silu(group(OFF_ZB)), o_c * silu(group(OFF_ZC)))
    ysum = None
    for n in range(N_BRANCH):
        yb = _dot(brs[n].astype(bf16), wbr_ref[n])
        t = jax.nn.sigmoid(group(OFF_GL + n * D_MODEL, D_MODEL)) * yb
        ysum = t if ysum is None else ysum + t
    y = _dot(ysum.astype(bf16), wo_ref[...])
    gate = mod_ref[0][:, 2 * D_MODEL:]
    out_ref[...] = x_ref[...] + gate * y


def _merge(x, mod_l, proj, row0, o_a, o_b, pack, ws_b, bs_full, wbr_b, wo_b, l, latent, seq):
    ntok = x.shape[0]
    tm = 512

    def const(*idx):
        return lambda t: idx

    return pl.pallas_call(
        _merge_kernel,
        grid=(ntok // tm,),
        in_specs=[pl.BlockSpec((tm, D_MODEL), lambda t: (t, 0)),
                  pl.BlockSpec((1, 1, 3 * D_MODEL), _mod_row(latent, tm, seq)),
                  pl.BlockSpec((tm, 512), lambda t: (t, 0)),
                  pl.BlockSpec((tm, 512), lambda t: (t, 0)),
                  pl.BlockSpec((tm, P16_W), lambda t: (row0 // tm + t, 0)),
                  _pk_spec(l),
                  pl.BlockSpec((None, CM_GROUPS, CM_CHUNK, CM_CHUNK), const(l, 0, 0, 0)),
                  pl.BlockSpec((None, CM_CHUNK, CM_WIDTH), const(l, 0, 0)),
                  pl.BlockSpec((None, N_BRANCH, BRANCH_W, D_MODEL), const(l, 0, 0, 0)),
                  pl.BlockSpec((None, D_MODEL, D_MODEL), const(l, 0, 0))],
        out_specs=pl.BlockSpec((tm, D_MODEL), lambda t: (t, 0)),
        out_shape=jax.ShapeDtypeStruct((ntok, D_MODEL), f32),
        scratch_shapes=[pltpu.VMEM((tm, CM_WIDTH), f32)],
        compiler_params=_cparams(1),
        name="merge",
    )(x, mod_l, o_a, o_b, proj, pack, ws_b, bs_full, wbr_b, wo_b)


def _pad_last(x, n):
    return jnp.pad(x, [(0, 0)] * (x.ndim - 1) + [(0, n - x.shape[-1])])


def _rope_tables(seq):
    t = np.arange(seq)
    row = (t // GRID_W).astype(np.float32)
    colp = (t % GRID_W).astype(np.float32)
    nf = MLA_ROPE // 4
    inv = (ROPE_THETA ** (-np.arange(nf, dtype=np.float32) / nf)).astype(np.float32)
    cos_t = np.ones((seq, LANES), np.float32)
    s_lo = np.zeros((seq, LANES), np.float32)
    s_hi = np.zeros((seq, LANES), np.float32)
    for i, pos in enumerate((row, colp)):
        ang = (pos[:, None] * inv[None, :]).astype(np.float32)
        cs, sn = np.cos(ang), np.sin(ang)
        lo = ROPE_LANE0 + 2 * nf * i
        cos_t[:, lo:lo + nf] = cs
        cos_t[:, lo + nf:lo + 2 * nf] = cs
        s_lo[:, lo:lo + nf] = -sn
        s_hi[:, lo + nf:lo + 2 * nf] = sn
    return tuple(jnp.asarray(p) for p in (cos_t, s_lo, s_hi))


def kernel(x_prompt, x_sample, cache_ckv, cache_krope, state_gdn, c, c_ctx, norm_g, w_mod, b_mod, w_in, q_a_norm, w_uq, kv_a_norm, w_ukv, q_norm, k_norm, conv_w, a_log, dt_bias, gdn_onorm, cm_ln_g, cm_ln_b, w_s, b_s, w_branch, w_o):
    L = DEPTH
    batch, seq, _ = x_prompt.shape
    dbatch, dseq, _ = x_sample.shape
    past = cache_ckv.shape[2]

    w_t = jnp.swapaxes(w_in, 1, 2)
    wuq_p = _pad_last(w_uq.reshape(L, MLA_Q_RANK, MLA_HEADS, MLA_QK), HEAD_SLOT)
    wuq_p = wuq_p.reshape(L, MLA_Q_RANK, MLA_HEADS * HEAD_SLOT).astype(bf16)
    wukv = w_ukv.reshape(L, MLA_KV_RANK, MLA_HEADS, MLA_NOPE + MLA_V)
    wk_p = _pad_last(wukv[..., :MLA_NOPE], HEAD_SLOT).reshape(L, MLA_KV_RANK, MLA_HEADS * HEAD_SLOT).astype(bf16)
    wv_p = wukv[..., MLA_NOPE:].reshape(L, MLA_KV_RANK, MLA_WIDTH).astype(bf16)
    cache_kr_p = jnp.pad(cache_krope, [(0, 0)] * 3 + [(ROPE_LANE0, LANES - ROPE_LANE0 - MLA_ROPE)])
    rope_tabs = _rope_tables(dseq)
    conv_w = jnp.swapaxes(conv_w.reshape(L, CONV_W, -1, LANES), 1, 2)
    ws_b = w_s.astype(bf16)
    bs_full = jnp.repeat(jnp.swapaxes(b_s, 1, 2), CM_WIDTH // CM_GROUPS, axis=2)
    wbr_b = w_branch.astype(bf16)
    wo_b = w_o.astype(bf16)
    gates8 = [_pad_last(v.reshape(L, 2 * GDN_HEADS), LANES) for v in (a_log, dt_bias)]
    pack = jnp.stack([norm_g,
                      jnp.concatenate([q_a_norm, kv_a_norm, _pad_last(q_norm, HEAD_SLOT),
                                       _pad_last(k_norm, HEAD_SLOT), gdn_onorm], axis=-1),
                      jnp.concatenate([cm_ln_g, cm_ln_b], axis=-1),
                      _pad_last(jnp.concatenate(gates8, axis=-1), D_MODEL)], axis=1)
    attn_w = (pack, wuq_p, wk_p, wv_p)

    c8 = jnp.concatenate([c, c_ctx[None, :], jnp.zeros((8 - dbatch - 1, D_MODEL), f32)], axis=0)
    mod = _modulation(c8, w_mod, b_mod)

    yp = x_prompt.reshape(batch * seq, D_MODEL)
    ys = x_sample.reshape(dbatch * dseq, D_MODEL)
    caches, states = (), None
    for l in range(L):
        mod_l = mod[l].reshape(8, 1, 3 * D_MODEL)
        p16, p32 = _inproj(yp, ys, mod_l, pack, w_t, l, dseq)
        lat0 = batch * seq
        o_a, new_ckv, new_kr = _attn_ctx(p32, caches, attn_w, l, batch, seq)
        caches = (new_ckv, new_kr)
        o_b, states = _gdn(p32, 0, conv_w, pack, None, states, l, batch, seq)
        yp = _merge(yp, mod_l, p16, 0, o_a, o_b, pack, ws_b, bs_full, wbr_b, wo_b, l, False, seq)
        o_a = _attn_lat(p32, lat0, cache_ckv, cache_kr_p, rope_tabs, attn_w, l, dbatch, dseq, past)
        o_b = _gdn(p32, lat0, conv_w, pack, state_gdn, None, l, dbatch, dseq)
        ys = _merge(ys, mod_l, p16, lat0, o_a, o_b, pack, ws_b, bs_full, wbr_b, wo_b, l, True, dseq)
    return (yp.reshape(batch, seq, D_MODEL), ys.reshape(dbatch, dseq, D_MODEL), caches[0], caches[1], states)
```

```python
import functools
import math

import numpy as np

import jax
import jax.numpy as jnp
from jax import lax
from jax.experimental import pallas as pl
from jax.experimental.pallas import tpu as pltpu

D_MODEL = 1024
DEPTH = 2
GRID_W = 64
EPS = 1e-6
MLA_HEADS = 8
MLA_NOPE = 64
MLA_ROPE = 32
MLA_QK = MLA_NOPE + MLA_ROPE
MLA_V = 64
MLA_Q_RANK = 384
MLA_KV_RANK = 256
MLA_WIDTH = MLA_HEADS * MLA_V
ROPE_THETA = 10000.0
GDN_HEADS = 4
GDN_DK = 128
GDN_DV = 128
GDN_KW = GDN_HEADS * GDN_DK
GDN_VW = GDN_HEADS * GDN_DV
GDN_CHUNK = 64
CONV_W = 5
GDN_SOLVE_CHUNKS = 4
CM_GROUPS = 4
CM_CHUNK = 128
CM_WIDTH = 512
N_BRANCH = 3
BRANCH_W = 512
SPLIT_SIZES = (MLA_Q_RANK, MLA_KV_RANK, MLA_ROPE, MLA_WIDTH,
               2 * GDN_KW + GDN_VW, 2 * GDN_HEADS, 2 * GDN_HEADS, GDN_VW,
               CM_WIDTH, CM_WIDTH, CM_WIDTH, N_BRANCH * D_MODEL)

LANES = 128
SUBLANES = 8
HEAD_SLOT = LANES
ROPE_LANE0 = MLA_NOPE

OFF_GL = 0
OFF_ZA = OFF_GL + N_BRANCH * D_MODEL
OFF_ZB = OFF_ZA + 512
OFF_CU = OFF_ZB + 512
OFF_CV = OFF_CU + 512
OFF_ZC = OFF_CV + 512
P16_W = OFF_ZC + 512
OFF_QKV = 0
OFF_SMALL = OFF_QKV + 2 * GDN_KW + GDN_VW
OFF_CQ = OFF_SMALL + LANES
OFF_CKV = OFF_CQ + MLA_Q_RANK
P32_W = OFF_CKV + MLA_KV_RANK
PROJ_W = P16_W + P32_W
GDN_IN_W = OFF_CQ
MLA_IN_W = P32_W - OFF_SMALL
MLA_SMALL = slice(0, LANES)
MLA_CQ = slice(OFF_CQ - OFF_SMALL, OFF_CKV - OFF_SMALL)
MLA_CKV = slice(OFF_CKV - OFF_SMALL, MLA_IN_W)

VMEM_LIMIT = 56 * 1024 * 1024

PK_ROWS = 4
PK = {"norm_g": (0, 0, D_MODEL),
      "q_a_norm": (1, 0, MLA_Q_RANK), "kv_a_norm": (1, 384, MLA_KV_RANK), "q_norm": (1, 640, HEAD_SLOT),
      "k_norm": (1, 768, HEAD_SLOT), "gdn_onorm": (1, 896, GDN_DV),
      "cm_ln_g": (2, 0, CM_WIDTH), "cm_ln_b": (2, 512, CM_WIDTH),
      "a_log": (3, 0, LANES), "dt_bias": (3, LANES, LANES)}


def _pk(pk_ref, name):
    row, lane0, width = PK[name]
    return pk_ref[row:row + 1, lane0:lane0 + width]


def _pk_spec(l):
    return pl.BlockSpec((None, PK_ROWS, D_MODEL), lambda *g: (l, 0, 0))

f32 = jnp.float32
bf16 = jnp.bfloat16


def _cparams(n_axes):
    return pltpu.CompilerParams(dimension_semantics=("arbitrary",) * n_axes,
                                vmem_limit_bytes=VMEM_LIMIT)


def _dot(a, b):
    return jnp.dot(a, b, preferred_element_type=f32)


def _dot_nt(a, b):
    return lax.dot_general(a, b, (((1,), (1,)), ((), ())), preferred_element_type=f32)


def _dot_tn(a, b):
    return lax.dot_general(a, b, (((0,), (0,)), ((), ())), preferred_element_type=f32)


def _rms_rows(x, g, n=None):
    n = x.shape[-1] if n is None else n
    ms = jnp.sum(x * x, axis=-1, keepdims=True) * (1.0 / n)
    return x * lax.rsqrt(ms + EPS) * g


def _mod_kernel(c_ref, w_ref, b_ref, o_ref):
    a = c_ref[...]
    a = (a * jax.nn.sigmoid(a)).astype(bf16)
    y = _dot(a, w_ref[0].astype(bf16)) + b_ref[0]
    for r in range(y.shape[0]):
        o_ref[0, r] = y[r:r + 1]


def _modulation(c8, w_mod, b_mod):
    tn = 1536
    return pl.pallas_call(
        _mod_kernel,
        grid=(DEPTH, 3 * D_MODEL // tn),
        in_specs=[pl.BlockSpec((8, D_MODEL), lambda l, n: (0, 0)),
                  pl.BlockSpec((1, D_MODEL, tn), lambda l, n: (l, 0, n)),
                  pl.BlockSpec((1, 1, tn), lambda l, n: (l, 0, n))],
        out_specs=pl.BlockSpec((1, 8, 1, tn), lambda l, n: (l, 0, 0, n)),
        out_shape=jax.ShapeDtypeStruct((DEPTH, 8, 1, 3 * D_MODEL), f32),
        compiler_params=_cparams(2),
        name="modulation",
    )(c8, w_mod, b_mod.reshape(DEPTH, 1, 3 * D_MODEL))


W_CHUNK = 512


def _w_in_moves():
    offs = [0]
    for s in SPLIT_SIZES:
        offs.append(offs[-1] + s)
    cq, ckv, krope, z_a, qkv, ga, gb, z_b, cu, cv, z_c, gl = offs[:-1]
    moves = [(gl, N_BRANCH * D_MODEL, OFF_GL), (z_a, 512, OFF_ZA), (z_b, 512, OFF_ZB), (cu, 512, OFF_CU),
             (cv, 512, OFF_CV), (z_c, 512, OFF_ZC), (qkv, 2 * GDN_KW + GDN_VW, P16_W + OFF_QKV),
             (cq, MLA_Q_RANK, P16_W + OFF_CQ), (ckv, MLA_KV_RANK, P16_W + OFF_CKV)]
    return moves, ga, krope


def _w_chunks():
    moves, ga, krope = _w_in_moves()
    chunks = []
    for (a, w, d) in moves:
        for o in range(0, w, W_CHUNK):
            chunks.append((a + o, min(W_CHUNK, w - o), d + o))
    small0 = P16_W + OFF_SMALL
    chunks.append((ga, 4 * GDN_HEADS, small0))
    chunks.append((krope, MLA_ROPE, small0 + ROPE_LANE0))
    return chunks


def _inproj_kernel(xc_ref, xl_ref, mod_ref, pk_ref, wt_hbm, o16_ref, o32_ref, w_s, stage, sem, *, layer, nctx):
    t = pl.program_id(0)

    @pl.when(t == 0)
    def _():
        chunks = _w_chunks()

        def copy(j):
            src, n, _ = chunks[j]
            return pltpu.make_async_copy(wt_hbm.at[layer, pl.ds(src, n), :], stage.at[j % 2, pl.ds(0, n), :],
                                         sem.at[j % 2])

        copy(0).start()
        for j, (_, n, dst) in enumerate(chunks):
            if j + 1 < len(chunks):
                copy(j + 1).start()
            copy(j).wait()
            w_s[dst:dst + n, :] = stage[j % 2, 0:n, :].astype(bf16)
        small0 = P16_W + OFF_SMALL
        for lo, hi in ((4 * GDN_HEADS, ROPE_LANE0), (ROPE_LANE0 + MLA_ROPE, LANES)):
            w_s[small0 + lo:small0 + hi, :] = jnp.zeros((hi - lo, D_MODEL), bf16)

    x = jnp.where(t < nctx, xc_ref[...], xl_ref[...])
    m = mod_ref[0]
    shift = m[:, :D_MODEL]
    scale = m[:, D_MODEL:2 * D_MODEL]
    h = (_rms_rows(x, _pk(pk_ref, "norm_g")) * (1.0 + scale) + shift).astype(bf16)
    for a in range(0, P16_W, 512):
        o16_ref[:, a:a + 512] = _dot_nt(h, w_s[a:a + 512, :]).astype(bf16)
    for a in range(0, P32_W, 512):
        b = min(a + 512, P32_W)
        o32_ref[:, a:b] = _dot_nt(h, w_s[P16_W + a:P16_W + b, :])


def _inproj(xc, xl, mod_l, norm_g, w_t, l, dseq):
    tm = 512
    nctx = xc.shape[0] // tm
    nlat = xl.shape[0] // tm
    ntok = xc.shape[0] + xl.shape[0]
    kern = functools.partial(_inproj_kernel, layer=l, nctx=nctx)
    return pl.pallas_call(
        kern,
        grid=(nctx + nlat,),
        in_specs=[pl.BlockSpec((tm, D_MODEL), lambda t: (jnp.minimum(t, nctx - 1), 0)),
                  pl.BlockSpec((tm, D_MODEL), lambda t: (jnp.maximum(t - nctx, 0), 0)),
                  pl.BlockSpec((None, 1, 1, 3 * D_MODEL),
                               lambda t: (l, jnp.where(t < nctx, 4, ((t - nctx) * tm) // dseq), 0, 0)),
                  _pk_spec(l),
                  pl.BlockSpec(memory_space=pl.ANY)],
        out_specs=[pl.BlockSpec((tm, P16_W), lambda t: (t, 0)),
                   pl.BlockSpec((tm, P32_W), lambda t: (t, 0))],
        out_shape=[jax.ShapeDtypeStruct((ntok, P16_W), bf16),
                   jax.ShapeDtypeStruct((ntok, P32_W), f32)],
        scratch_shapes=[pltpu.VMEM((PROJ_W, D_MODEL), bf16),
                        pltpu.VMEM((2, W_CHUNK, D_MODEL), f32),
                        pltpu.SemaphoreType.DMA((2,))],
        compiler_params=_cparams(1),
        name="inproj",
    )(xc, xl, mod_l, norm_g, w_t)


def _mod_row(l, latent, tm, seq):
    if latent:
        return lambda t: (l, (t * tm) // seq, 0, 0)
    return lambda t: (l, 4, 0, 0)


def _rope(x, cos_t, sin_lo, sin_hi):
    return x * cos_t + pltpu.roll(x, LANES - 8, 1) * sin_lo + pltpu.roll(x, 8, 1) * sin_hi


def _build_kv(ckvn_b, kr, wk_ref, wv_ref, knorm, rope, k_s, v_s, r0):
    n = ckvn_b.shape[0]
    kfull = _dot(ckvn_b, wk_ref[...])
    v_s[r0:r0 + n, :] = _dot(ckvn_b, wv_ref[...]).astype(bf16)
    krg = kr * knorm
    if rope is not None:
        krg = _rope(krg, *rope)
    kr_ss = jnp.sum(kr * kr, axis=-1, keepdims=True)
    for h in range(MLA_HEADS):
        sl = slice(h * HEAD_SLOT, (h + 1) * HEAD_SLOT)
        kn = kfull[:, sl]
        ms = (jnp.sum(kn * kn, axis=-1, keepdims=True) + kr_ss) * (1.0 / MLA_QK)
        k_s[r0:r0 + n, sl] = ((kn * knorm + krg) * lax.rsqrt(ms + EPS)).astype(bf16)


def _rope_lane_mask(shape):
    lane = lax.broadcasted_iota(jnp.int32, shape, 1)
    return (lane >= ROPE_LANE0) & (lane < ROPE_LANE0 + MLA_ROPE)


def _attend_block(qa_b, wuq_ref, qnorm, rope, k_s, v_s, o_ref):
    tq = qa_b.shape[0]
    qfull = _dot(qa_b, wuq_ref[...])
    qgain = qnorm * (math.log2(math.e) / math.sqrt(MLA_QK))
    lane = lax.broadcasted_iota(jnp.int32, (tq, LANES), 1)

    def scores(h):
        sl = slice(h * HEAD_SLOT, (h + 1) * HEAD_SLOT)
        qh = _rms_rows(qfull[:, sl], qgain, n=MLA_QK)
        if rope is not None:
            qh = _rope(qh, *rope)
        return _dot_nt(qh.astype(bf16), k_s[:, sl])

    s_next = scores(0)
    outs = []
    for h in range(MLA_HEADS):
        s = s_next
        if h + 1 < MLA_HEADS:
            s_next = scores(h + 1)
        p = jnp.exp2(s - jnp.max(s, axis=-1, keepdims=True))
        den = jnp.sum(p, axis=-1, keepdims=True)
        hp = h // 2
        outs.append(_dot(p.astype(bf16), v_s[:, hp * LANES:(hp + 1) * LANES]) / den)
        if h % 2 == 1:
            o_ref[:, hp * LANES:(hp + 1) * LANES] = jnp.where(lane < MLA_V, outs[h - 1], outs[h])


def _attn_ctx_kernel(*refs, layer):
    if layer:
        mla_ref, pckvn_ref, pkr_ref = refs[:3]
        refs = refs[3:]
    else:
        mla_ref = refs[0]
        refs = refs[1:]
    pk_ref, wuq_ref, wk_ref, wv_ref, o_ref, ckvn_ref, kr_ref, k_s, v_s = refs
    for i in range(layer):
        ckvn_ref[i] = pckvn_ref[i]
        kr_ref[i] = pkr_ref[i]
    small = mla_ref[:, MLA_SMALL]
    kr = jnp.where(_rope_lane_mask(small.shape), small, 0.0)
    kr_ref[layer] = small[:, ROPE_LANE0:ROPE_LANE0 + MLA_ROPE]
    ckvn = _rms_rows(mla_ref[:, MLA_CKV], _pk(pk_ref, "kv_a_norm"))
    ckvn_ref[layer] = ckvn
    _build_kv(ckvn.astype(bf16), kr, wk_ref, wv_ref, _pk(pk_ref, "k_norm"), None, k_s, v_s, 0)
    qa = _rms_rows(mla_ref[:, MLA_CQ], _pk(pk_ref, "q_a_norm")).astype(bf16)
    _attend_block(qa, wuq_ref, _pk(pk_ref, "q_norm"), None, k_s, v_s, o_ref)


def _attn_lat_kernel(mla_ref, cckv_ref, ckr_ref, cos_ref, slo_ref, shi_ref,
                     pk_ref, wuq_ref, wk_ref, wv_ref,
                     o_ref, k_s, v_s, *, seq, past, tq):
    qi = pl.program_id(1)
    rb = 256

    @pl.when(qi == 0)
    def _():
        _build_kv(cckv_ref[...].astype(bf16), ckr_ref[...], wk_ref, wv_ref, _pk(pk_ref, "k_norm"), None,
                  k_s, v_s, 0)
        for r in range(seq // rb):
            rs = slice(r * rb, (r + 1) * rb)
            small = mla_ref[rs, MLA_SMALL]
            kr = jnp.where(_rope_lane_mask(small.shape), small, 0.0)
            ckvn = _rms_rows(mla_ref[rs, MLA_CKV], _pk(pk_ref, "kv_a_norm"))
            rope = (cos_ref[rs, :], slo_ref[rs, :], shi_ref[rs, :])
            _build_kv(ckvn.astype(bf16), kr, wk_ref, wv_ref, _pk(pk_ref, "k_norm"), rope, k_s, v_s, past + r * rb)

    rows = pl.ds(pl.multiple_of(qi * tq, tq), tq)
    qa = _rms_rows(mla_ref[rows, MLA_CQ], _pk(pk_ref, "q_a_norm")).astype(bf16)
    rope = (cos_ref[rows, :], slo_ref[rows, :], shi_ref[rows, :])
    _attend_block(qa, wuq_ref, _pk(pk_ref, "q_norm"), rope, k_s, v_s, o_ref)


def _attn_weight_specs(l, nidx):
    z = (0,) * (nidx - 1)

    def const(*idx):
        return lambda *g: idx

    return [_pk_spec(l),
            pl.BlockSpec((None, MLA_Q_RANK, MLA_HEADS * HEAD_SLOT), const(l, 0, 0)),
            pl.BlockSpec((None, MLA_KV_RANK, MLA_HEADS * HEAD_SLOT), const(l, 0, 0)),
            pl.BlockSpec((None, MLA_KV_RANK, MLA_WIDTH), const(l, 0, 0))]


def _attn_ctx(proj, prev, wts, l, batch, seq):
    ntok = batch * seq

    def stacked(n, width):
        return pl.BlockSpec((None, n, seq, width), lambda b: (b, 0, 0, 0))

    prev_specs = [stacked(l, MLA_KV_RANK), stacked(l, MLA_ROPE)] if l else []
    return pl.pallas_call(
        functools.partial(_attn_ctx_kernel, layer=l),
        grid=(batch,),
        in_specs=[pl.BlockSpec((seq, MLA_IN_W), lambda b: (b, OFF_SMALL // MLA_IN_W))]
        + prev_specs + _attn_weight_specs(l, 1),
        out_specs=[pl.BlockSpec((seq, MLA_WIDTH), lambda b: (b, 0)),
                   stacked(l + 1, MLA_KV_RANK), stacked(l + 1, MLA_ROPE)],
        out_shape=[jax.ShapeDtypeStruct((ntok, MLA_WIDTH), f32),
                   jax.ShapeDtypeStruct((batch, l + 1, seq, MLA_KV_RANK), f32),
                   jax.ShapeDtypeStruct((batch, l + 1, seq, MLA_ROPE), f32)],
        scratch_shapes=[pltpu.VMEM((seq, MLA_HEADS * HEAD_SLOT), bf16),
                        pltpu.VMEM((seq, MLA_WIDTH), bf16)],
        compiler_params=_cparams(1),
        name="attn_ctx",
    )(proj, *prev, *wts)


def _attn_lat(proj, row0, cache_ckv, cache_kr_p, rope_tabs, wts, l, batch, seq, past):
    ntok = batch * seq
    tq = 256
    nq = seq // tq
    kern = functools.partial(_attn_lat_kernel, seq=seq, past=past, tq=tq)
    tab = pl.BlockSpec((seq, LANES), lambda b, q: (0, 0))
    return pl.pallas_call(
        kern,
        grid=(batch, nq),
        in_specs=[pl.BlockSpec((seq, MLA_IN_W), lambda b, q: (row0 // seq + b, OFF_SMALL // MLA_IN_W)),
                  pl.BlockSpec((None, None, past, MLA_KV_RANK), lambda b, q: (b, l, 0, 0)),
                  pl.BlockSpec((None, None, past, LANES), lambda b, q: (b, l, 0, 0)),
                  tab, tab, tab]
        + _attn_weight_specs(l, 2),
        out_specs=pl.BlockSpec((tq, MLA_WIDTH), lambda b, q: (b * nq + q, 0)),
        out_shape=jax.ShapeDtypeStruct((ntok, MLA_WIDTH), f32),
        scratch_shapes=[pltpu.VMEM((past + seq, MLA_HEADS * HEAD_SLOT), bf16),
                        pltpu.VMEM((past + seq, MLA_WIDTH), bf16)],
        compiler_params=_cparams(2),
        name="attn_lat",
    )(proj, cache_ckv, cache_kr_p, *rope_tabs, *wts)


def _split3(x):
    hi = x.astype(bf16)
    r1 = x - hi.astype(f32)
    mid = r1.astype(bf16)
    lo = (r1 - mid.astype(f32)).astype(bf16)
    return hi, mid, lo


def _tri_cumsum(tri_b, x):
    hi, mid, lo = _split3(x)
    return _dot(tri_b, hi) + _dot(tri_b, mid) + _dot(tri_b, lo)


def _lane_bcast(x, c):
    return jnp.broadcast_to(x[:, c:c + 1], (x.shape[0], LANES))


def _gdn_kernel(*refs, seq, has_state, nprev):
    gin_ref, cw_ref, pk_ref = refs[:3]
    refs = refs[3:]
    s0_ref = sprev_ref = sout_ref = None
    if has_state:
        s0_ref, o_ref = refs[:2]
        refs = refs[2:]
    else:
        if nprev:
            sprev_ref = refs[0]
            refs = refs[1:]
        o_ref, sout_ref = refs[:2]
        refs = refs[2:]
    xpad, qkv_s, g_s, b_s, st_s, wq_s, ak_s, u_s, el_s, rhs_s = refs
    C = GDN_CHUNK
    nchunk = seq // C
    H = GDN_HEADS
    width = 2 * GDN_KW + GDN_VW
    halo = 8

    for j in range(width // LANES):
        xpad[j, 0:halo, :] = jnp.zeros((halo, LANES), f32)
        xpad[j, halo + seq:, :] = jnp.zeros((halo, LANES), f32)
        xpad[j, halo:halo + seq, :] = gin_ref[:, j * LANES:(j + 1) * LANES]
    if has_state:
        for d in range(2):
            for h in range(H):
                st_s[d * H + h] = s0_ref[d, h]
    else:
        st_s[...] = jnp.zeros((2 * H, GDN_DK, GDN_DV), f32)

    neg_a = -jnp.exp(_pk(pk_ref, "a_log"))
    dtb = _pk(pk_ref, "dt_bias")

    def conv_tile(j, l2norm):
        w = cw_ref[j]
        post = jnp.where(j < H, GDN_DK ** -0.5, 1.0)
        for c in range(nchunk):
            base = halo - CONV_W // 2 + c * C
            y = xpad[j, base:base + C, :] * w[0:1]
            for tap in range(1, CONV_W):
                y = y + xpad[j, base + tap:base + tap + C, :] * w[tap:tap + 1]
            y = y * jax.nn.sigmoid(y)
            if l2norm:
                y = y * (lax.rsqrt(jnp.sum(y * y, axis=-1, keepdims=True) + EPS) * post)
            qkv_s[j, c * C:(c + 1) * C, :] = y

    def conv_qk(j, carry):
        conv_tile(j, True)
        return carry

    def conv_v(j, carry):
        conv_tile(j, False)
        return carry

    tile_unroll = 4 if nchunk <= 4 else 1
    lax.fori_loop(0, 2 * H, conv_qk, 0, unroll=tile_unroll)
    lax.fori_loop(2 * H, 3 * H, conv_v, 0, unroll=tile_unroll)

    def prep(c, carry):
        r0 = pl.multiple_of(c * C, C)
        sm = gin_ref[pl.ds(r0, C), OFF_SMALL:OFF_SMALL + LANES]
        z = sm + dtb
        g_s[pl.ds(r0, C), :] = neg_a * (jnp.maximum(z, 0.0) + jnp.log1p(jnp.exp(-jnp.abs(z))))
        b_s[pl.ds(r0, C), :] = pltpu.roll(jax.nn.sigmoid(sm), LANES - 2 * H, 1)
        return carry

    lax.fori_loop(0, nchunk, prep, 0, unroll=4)

    ri = lax.broadcasted_iota(jnp.int32, (C, LANES), 0)
    cl = lax.broadcasted_iota(jnp.int32, (C, LANES), 1)
    fwd = cl < C
    cj = cl & (C - 1)
    eye2 = (ri == cj).astype(f32)
    incl2 = (fwd & (ri >= cj)) | (~fwd & (ri <= cj))
    strict2 = (fwd & (ri > cj)) | (~fwd & (ri < cj))
    xor = ri ^ cj
    level2 = sum((xor >= (1 << b)).astype(jnp.int32) for b in range(C.bit_length() - 1))
    lvl_top = jnp.where(fwd, level2, 0)
    lvl_bot = jnp.where(fwd, 0, level2)
    r2 = lax.broadcasted_iota(jnp.int32, (2 * C, C), 0)
    c2 = lax.broadcasted_iota(jnp.int32, (2 * C, C), 1)
    tri2 = (((r2 < C) & (r2 >= c2)) | ((r2 >= C) & (r2 - C <= c2))).astype(bf16)
    zrhs = jnp.zeros((C, 2 * LANES), bf16)
    zvn = jnp.zeros((C, LANES), bf16)
    fwd_row = fwd[0:1, :]
    cpi = min(GDN_SOLVE_CHUNKS, nchunk)

    def block_diag(x):
        return jnp.concatenate([jnp.where(fwd, x, 0.0), jnp.where(fwd, 0.0, x)], axis=0).astype(bf16)

    def solve_phase(i, carry):
        chains = []
        for cc in range(cpi):
            c = i * cpi + cc
            rows = pl.ds(pl.multiple_of(c * C, C), C)
            g2 = _tri_cumsum(tri2, g_s[rows, :])
            g2t = g2.T
            bt = b_s[rows, :]
            for h in range(H):
                chains.append((cc, c, h, rows, g2, g2t, bt))

        a2s, t2s = [], []
        for (cc, c, h, rows, g2, g2t, bt) in chains:
            q = qkv_s[h, rows, :]
            k = qkv_s[H + h, rows, :]
            v = qkv_s[2 * H + h, rows, :]
            kq = _dot_nt(jnp.concatenate([k, q], axis=0).astype(bf16),
                         jnp.concatenate([k, k], axis=0).astype(bf16))
            gcc_f = _lane_bcast(g2[:C], h)
            gcc_b = _lane_bcast(g2[C:], H + h)
            btc_f = _lane_bcast(bt, h)
            btc_b = _lane_bcast(bt, H + h)
            grow = jnp.where(fwd_row, g2t[h:h + 1, :], g2t[H + h:H + h + 1, :])
            diff = jnp.where(fwd, gcc_f, gcc_b) - grow
            dec = jnp.where(incl2, jnp.exp(jnp.where(incl2, diff, 0.0)), 0.0)
            a2 = jnp.where(strict2, jnp.where(fwd, btc_f, btc_b) * kq[:C] * dec, 0.0)
            a2s.append(a2)
            t2s.append(eye2 - jnp.where(level2 == 1, a2, 0.0))
            glast_f = gcc_f[C - 1:C, :]
            glast_b = gcc_b[0:1, :]
            e1_f = jnp.exp(gcc_f)
            e1_b = jnp.exp(gcc_b)
            ket = jnp.concatenate([k * jnp.exp(glast_f - gcc_f), k * jnp.exp(glast_b - gcc_b)], axis=0).T
            ak_s[pl.ds(pl.multiple_of((c * H + h) * 3 * C, 3 * C), 3 * C), :] = jnp.concatenate(
                [kq[C:] * dec, ket], axis=0).astype(bf16)
            for d, (btc, e1, glast) in enumerate(((btc_f, e1_f, glast_f), (btc_b, e1_b, glast_b))):
                ch = d * H + h
                j = (cc * H + h) * 2 + d
                rhs_s[j * C:(j + 1) * C, :] = jnp.concatenate([v * btc, k * (btc * e1)], axis=1).astype(bf16)
                wq_s[pl.ds(pl.multiple_of((c * 2 * H + ch) * 2 * C + C, C), C), :] = (q * e1).astype(bf16)
                el_s[pl.ds(pl.multiple_of((c * 2 * H + ch) * 8, 8), 8), :] = jnp.broadcast_to(
                    jnp.exp(glast), (8, LANES))

        for lv in range(2, C.bit_length()):
            rs = []
            for a2, t2 in zip(a2s, t2s):
                abd = jnp.concatenate([jnp.where(lvl_top == lv, a2, 0.0),
                                       jnp.where(lvl_bot == lv, a2, 0.0)], axis=0).astype(bf16)
                rs.append(_dot(t2.astype(bf16), abd))
            t2s = [t2 - _dot(r.astype(bf16), block_diag(t2)) for r, t2 in zip(rs, t2s)]

        for (cc, c, h, rows, g2, g2t, bt), t2 in zip(chains, t2s):
            t2b = t2.astype(bf16)
            for d in range(2):
                ch = d * H + h
                j = (cc * H + h) * 2 + d
                rhs = rhs_s[j * C:(j + 1) * C, :]
                rhs = jnp.concatenate([rhs, zrhs] if d == 0 else [zrhs, rhs], axis=0)
                uw = _dot(t2b, rhs)
                u_s[pl.ds(pl.multiple_of((c * 2 * H + ch) * C, C), C), :] = uw[:, :LANES]
                wq_s[pl.ds(pl.multiple_of((c * 2 * H + ch) * 2 * C, C), C), :] = uw[:, LANES:].astype(bf16)
        return carry

    lax.fori_loop(0, nchunk // cpi, solve_phase, 0)

    def scan_phase(i, carry):
        cs = [i if ch < H else nchunk - 1 - i for ch in range(2 * H)]
        s_old = [st_s[ch] for ch in range(2 * H)]
        r1 = [_dot(wq_s[pl.ds(pl.multiple_of((cs[ch] * 2 * H + ch) * 2 * C, 2 * C), 2 * C), :],
                   s_old[ch].astype(bf16)) for ch in range(2 * H)]
        r2s = []
        for ch in range(2 * H):
            u = u_s[pl.ds(pl.multiple_of((cs[ch] * 2 * H + ch) * C, C), C), :]
            vnb = (u - r1[ch][:C]).astype(bf16)
            rhs = jnp.concatenate([vnb, zvn] if ch < H else [zvn, vnb], axis=0)
            ak = ak_s[pl.ds(pl.multiple_of((cs[ch] * H + ch % H) * 3 * C, 3 * C), 3 * C), :]
            r2s.append(_dot(ak, rhs))
        for ch in range(2 * H):
            el = el_s[pl.ds(pl.multiple_of((cs[ch] * 2 * H + ch) * 8, 8), 8), :][0:1, :]
            st_s[ch] = s_old[ch] * el + r2s[ch][C:]
            xpad[ch, pl.ds(pl.multiple_of(cs[ch] * C, C), C), :] = r1[ch][C:] + r2s[ch][:C]
        return carry

    lax.fori_loop(0, nchunk, scan_phase, 0, unroll=4)

    onorm = _pk(pk_ref, "gdn_onorm")

    def fin(c, carry):
        rows = pl.ds(pl.multiple_of(c * C, C), C)
        for h in range(H):
            ls = slice(h * LANES, (h + 1) * LANES)
            o_ref[rows, ls] = _rms_rows(xpad[h, rows, :] + xpad[H + h, rows, :], onorm)
        return carry

    lax.fori_loop(0, nchunk, fin, 0, unroll=4)
    if sout_ref is not None:
        for i in range(nprev):
            sout_ref[i] = sprev_ref[i]
        for d in range(2):
            for h in range(H):
                sout_ref[nprev, d, h] = st_s[d * H + h]


def _gdn(proj, row0, conv_w, pack, state, prev_states, l, batch, seq):
    ntok = batch * seq
    width = 2 * GDN_KW + GDN_VW
    has_state = state is not None
    nprev = 0 if has_state else l
    nchunk = seq // GDN_CHUNK
    kern = functools.partial(_gdn_kernel, seq=seq, has_state=has_state, nprev=nprev)
    in_specs = [pl.BlockSpec((seq, GDN_IN_W), lambda b: (row0 // seq + b, OFF_QKV // GDN_IN_W)),
                pl.BlockSpec((None, width // LANES, CONV_W, LANES), lambda b: (l, 0, 0, 0)),
                _pk_spec(l)]
    args = [proj, conv_w, pack]
    o_spec = pl.BlockSpec((seq, GDN_VW), lambda b: (b, 0))
    o_shape = jax.ShapeDtypeStruct((ntok, GDN_VW), f32)
    st_block = (None, None, 2, GDN_HEADS, GDN_DK, GDN_DV)
    if has_state:
        in_specs.append(pl.BlockSpec(st_block, lambda b: (b, l, 0, 0, 0, 0)))
        args.append(state)
        out_specs, out_shape = o_spec, o_shape
    else:
        def stacked(n):
            return pl.BlockSpec((None, n) + st_block[2:], lambda b: (b, 0, 0, 0, 0, 0))

        if nprev:
            in_specs.append(stacked(nprev))
            args.append(prev_states)
        out_specs = [o_spec, stacked(nprev + 1)]
        out_shape = [o_shape, jax.ShapeDtypeStruct((batch, nprev + 1, 2, GDN_HEADS, GDN_DK, GDN_DV), f32)]
    return pl.pallas_call(
        kern,
        grid=(batch,),
        in_specs=in_specs,
        out_specs=out_specs,
        out_shape=out_shape,
        scratch_shapes=[pltpu.VMEM((width // LANES, seq + 16, LANES), f32),
                        pltpu.VMEM((width // LANES, seq, LANES), f32),
                        pltpu.VMEM((seq, LANES), f32),
                        pltpu.VMEM((seq, LANES), f32),
                        pltpu.VMEM((2 * GDN_HEADS, GDN_DK, GDN_DV), f32),
                        pltpu.VMEM((nchunk * 2 * GDN_HEADS * 2 * GDN_CHUNK, LANES), bf16),
                        pltpu.VMEM((nchunk * GDN_HEADS * 3 * GDN_CHUNK, LANES), bf16),
                        pltpu.VMEM((nchunk * 2 * GDN_HEADS * GDN_CHUNK, LANES), f32),
                        pltpu.VMEM((nchunk * 2 * GDN_HEADS * 8, LANES), f32),
                        pltpu.VMEM((min(GDN_SOLVE_CHUNKS, nchunk) * 2 * GDN_HEADS * GDN_CHUNK, 2 * LANES), bf16)],
        compiler_params=_cparams(1),
        name="gdn_lat" if has_state else "gdn_ctx",
    )(*args)


def _merge_kernel(x_ref, mod_ref, oa_ref, ob_ref, p_ref, pk_ref, ws_ref, bs_ref, wbr_ref, wo_ref,
                  out_ref, sv_s):
    tm = x_ref.shape[0]

    def group(off, width=512):
        return p_ref[:, off:off + width].astype(f32)

    def gelu(x):
        c1 = math.sqrt(2.0 / math.pi)
        half = 0.5 * x
        return half + half * jnp.tanh(x * (c1 + (c1 * 0.044715) * (x * x)))

    u = gelu(group(OFF_CU))
    vf = gelu(group(OFF_CV))
    mu = jnp.mean(vf, axis=-1, keepdims=True)
    vc = vf - mu
    var = jnp.mean(vc * vc, axis=-1, keepdims=True)
    vn = (vc * lax.rsqrt(var + EPS) * _pk(pk_ref, "cm_ln_g") + _pk(pk_ref, "cm_ln_b")).astype(bf16)
    for ck in range(tm // CM_CHUNK):
        rs = slice(ck * CM_CHUNK, (ck + 1) * CM_CHUNK)
        for g in range(CM_GROUPS):
            ls = slice(g * LANES, (g + 1) * LANES)
            sv_s[rs, ls] = _dot(ws_ref[g], vn[rs, ls]) + bs_ref[:, ls]
    o_c = u * sv_s[...]

    def silu(z):
        return z * jax.nn.sigmoid(z)

    brs = (oa_ref[...] * silu(group(OFF_ZA)), ob_ref[...] * silu(group(OFF_ZB)), o_c * silu(group(OFF_ZC)))
    ysum = None
    for n in range(N_BRANCH):
        yb = _dot(brs[n].astype(bf16), wbr_ref[n])
        t = jax.nn.sigmoid(group(OFF_GL + n * D_MODEL, D_MODEL)) * yb
        ysum = t if ysum is None else ysum + t
    y = _dot(ysum.astype(bf16), wo_ref[...])
    gate = mod_ref[0][:, 2 * D_MODEL:]
    out_ref[...] = x_ref[...] + gate * y


def _merge(x, mod_l, proj, row0, o_a, o_b, pack, ws_b, bs_full, wbr_b, wo_b, l, latent, seq):
    ntok = x.shape[0]
    tm = 512

    def const(*idx):
        return lambda t: idx

    return pl.pallas_call(
        _merge_kernel,
        grid=(ntok // tm,),
        in_specs=[pl.BlockSpec((tm, D_MODEL), lambda t: (t, 0)),
                  pl.BlockSpec((None, 1, 1, 3 * D_MODEL), _mod_row(l, latent, tm, seq)),
                  pl.BlockSpec((tm, 512), lambda t: (t, 0)),
                  pl.BlockSpec((tm, 512), lambda t: (t, 0)),
                  pl.BlockSpec((tm, P16_W), lambda t: (row0 // tm + t, 0)),
                  _pk_spec(l),
                  pl.BlockSpec((None, CM_GROUPS, CM_CHUNK, CM_CHUNK), const(l, 0, 0, 0)),
                  pl.BlockSpec((None, CM_CHUNK, CM_WIDTH), const(l, 0, 0)),
                  pl.BlockSpec((None, N_BRANCH, BRANCH_W, D_MODEL), const(l, 0, 0, 0)),
                  pl.BlockSpec((None, D_MODEL, D_MODEL), const(l, 0, 0))],
        out_specs=pl.BlockSpec((tm, D_MODEL), lambda t: (t, 0)),
        out_shape=jax.ShapeDtypeStruct((ntok, D_MODEL), f32),
        scratch_shapes=[pltpu.VMEM((tm, CM_WIDTH), f32)],
        compiler_params=_cparams(1),
        name="merge",
    )(x, mod_l, o_a, o_b, proj, pack, ws_b, bs_full, wbr_b, wo_b)


def _pad_last(x, n):
    return jnp.pad(x, [(0, 0)] * (x.ndim - 1) + [(0, n - x.shape[-1])])


def _rope_tables(seq):
    t = np.arange(seq)
    row = (t // GRID_W).astype(np.float32)
    colp = (t % GRID_W).astype(np.float32)
    nf = MLA_ROPE // 4
    inv = (ROPE_THETA ** (-np.arange(nf, dtype=np.float32) / nf)).astype(np.float32)
    cos_t = np.ones((seq, LANES), np.float32)
    s_lo = np.zeros((seq, LANES), np.float32)
    s_hi = np.zeros((seq, LANES), np.float32)
    for i, pos in enumerate((row, colp)):
        ang = (pos[:, None] * inv[None, :]).astype(np.float32)
        cs, sn = np.cos(ang), np.sin(ang)
        lo = ROPE_LANE0 + 2 * nf * i
        cos_t[:, lo:lo + nf] = cs
        cos_t[:, lo + nf:lo + 2 * nf] = cs
        s_lo[:, lo:lo + nf] = -sn
        s_hi[:, lo + nf:lo + 2 * nf] = sn
    return tuple(jnp.asarray(p) for p in (cos_t, s_lo, s_hi))


def kernel(x_prompt, x_sample, cache_ckv, cache_krope, state_gdn, c, c_ctx, norm_g, w_mod, b_mod, w_in, q_a_norm, w_uq, kv_a_norm, w_ukv, q_norm, k_norm, conv_w, a_log, dt_bias, gdn_onorm, cm_ln_g, cm_ln_b, w_s, b_s, w_branch, w_o):
    L = DEPTH
    batch, seq, _ = x_prompt.shape
    dbatch, dseq, _ = x_sample.shape
    past = cache_ckv.shape[2]

    w_t = jnp.swapaxes(w_in, 1, 2)
    wuq_p = _pad_last(w_uq.reshape(L, MLA_Q_RANK, MLA_HEADS, MLA_QK), HEAD_SLOT)
    wuq_p = wuq_p.reshape(L, MLA_Q_RANK, MLA_HEADS * HEAD_SLOT).astype(bf16)
    wukv = w_ukv.reshape(L, MLA_KV_RANK, MLA_HEADS, MLA_NOPE + MLA_V)
    wk_p = _pad_last(wukv[..., :MLA_NOPE], HEAD_SLOT).reshape(L, MLA_KV_RANK, MLA_HEADS * HEAD_SLOT).astype(bf16)
    wv_p = wukv[..., MLA_NOPE:].reshape(L, MLA_KV_RANK, MLA_WIDTH).astype(bf16)
    cache_kr_p = jnp.pad(cache_krope, [(0, 0)] * 3 + [(ROPE_LANE0, LANES - ROPE_LANE0 - MLA_ROPE)])
    rope_tabs = _rope_tables(dseq)
    conv_w = jnp.swapaxes(conv_w.reshape(L, CONV_W, -1, LANES), 1, 2)
    ws_b = w_s.astype(bf16)
    bs_full = jnp.repeat(jnp.swapaxes(b_s, 1, 2), CM_WIDTH // CM_GROUPS, axis=2)
    gates8 = [_pad_last(v.reshape(L, 2 * GDN_HEADS), LANES) for v in (a_log, dt_bias)]
    pack = jnp.stack([norm_g,
                      jnp.concatenate([q_a_norm, kv_a_norm, _pad_last(q_norm, HEAD_SLOT),
                                       _pad_last(k_norm, HEAD_SLOT), gdn_onorm], axis=-1),
                      jnp.concatenate([cm_ln_g, cm_ln_b], axis=-1),
                      _pad_last(jnp.concatenate(gates8, axis=-1), D_MODEL)], axis=1)
    attn_w = (pack, wuq_p, wk_p, wv_p)

    c8 = jnp.concatenate([c, c_ctx[None, :], jnp.zeros((8 - dbatch - 1, D_MODEL), f32)], axis=0)
    wbr_b = w_branch.astype(bf16)
    wo_b = w_o.astype(bf16)
    mod = _modulation(c8, w_mod, b_mod)

    yp = x_prompt.reshape(batch * seq, D_MODEL)
    ys = x_sample.reshape(dbatch * dseq, D_MODEL)
    caches, states = (), None
    for l in range(L):
        mod_l = mod
        p16, p32 = _inproj(yp, ys, mod_l, pack, w_t, l, dseq)
        lat0 = batch * seq
        o_a, new_ckv, new_kr = _attn_ctx(p32, caches, attn_w, l, batch, seq)
        caches = (new_ckv, new_kr)
        o_b, states = _gdn(p32, 0, conv_w, pack, None, states, l, batch, seq)
        yp = _merge(yp, mod_l, p16, 0, o_a, o_b, pack, ws_b, bs_full, wbr_b, wo_b, l, False, seq)
        o_a = _attn_lat(p32, lat0, cache_ckv, cache_kr_p, rope_tabs, attn_w, l, dbatch, dseq, past)
        o_b = _gdn(p32, lat0, conv_w, pack, state_gdn, None, l, dbatch, dseq)
        ys = _merge(ys, mod_l, p16, lat0, o_a, o_b, pack, ws_b, bs_full, wbr_b, wo_b, l, True, dseq)
    return (yp.reshape(batch, seq, D_MODEL), ys.reshape(dbatch, dseq, D_MODEL), caches[0], caches[1], states)
```

```python
import functools
import math

import numpy as np

import jax
import jax.numpy as jnp
from jax import lax
from jax.experimental import pallas as pl
from jax.experimental.pallas import tpu as pltpu

D_MODEL = 1024
DEPTH = 2
GRID_W = 64
EPS = 1e-6
MLA_HEADS = 8
MLA_NOPE = 64
MLA_ROPE = 32
MLA_QK = MLA_NOPE + MLA_ROPE
MLA_V = 64
MLA_Q_RANK = 384
MLA_KV_RANK = 256
MLA_WIDTH = MLA_HEADS * MLA_V
ROPE_THETA = 10000.0
GDN_HEADS = 4
GDN_DK = 128
GDN_DV = 128
GDN_KW = GDN_HEADS * GDN_DK
GDN_VW = GDN_HEADS * GDN_DV
GDN_CHUNK = 64
CONV_W = 5
GDN_SOLVE_CHUNKS = 4
CM_GROUPS = 4
CM_CHUNK = 128
CM_WIDTH = 512
N_BRANCH = 3
BRANCH_W = 512
SPLIT_SIZES = (MLA_Q_RANK, MLA_KV_RANK, MLA_ROPE, MLA_WIDTH,
               2 * GDN_KW + GDN_VW, 2 * GDN_HEADS, 2 * GDN_HEADS, GDN_VW,
               CM_WIDTH, CM_WIDTH, CM_WIDTH, N_BRANCH * D_MODEL)

LANES = 128
SUBLANES = 8
HEAD_SLOT = LANES
ROPE_LANE0 = MLA_NOPE

OFF_GL = 0
OFF_ZA = OFF_GL + N_BRANCH * D_MODEL
OFF_ZB = OFF_ZA + 512
OFF_CU = OFF_ZB + 512
OFF_CV = OFF_CU + 512
OFF_ZC = OFF_CV + 512
P16_W = OFF_ZC + 512
OFF_QKV = 0
OFF_SMALL = OFF_QKV + 2 * GDN_KW + GDN_VW
OFF_CQ = OFF_SMALL + LANES
OFF_CKV = OFF_CQ + MLA_Q_RANK
P32_W = OFF_CKV + MLA_KV_RANK
PROJ_W = P16_W + P32_W
GDN_IN_W = OFF_CQ
MLA_IN_W = P32_W - OFF_SMALL
MLA_SMALL = slice(0, LANES)
MLA_CQ = slice(OFF_CQ - OFF_SMALL, OFF_CKV - OFF_SMALL)
MLA_CKV = slice(OFF_CKV - OFF_SMALL, MLA_IN_W)

VMEM_LIMIT = 56 * 1024 * 1024

PK_ROWS = 4
PK = {}
for _row, _fields in enumerate((
        (("norm_g", D_MODEL),),
        (("q_a_norm", MLA_Q_RANK), ("kv_a_norm", MLA_KV_RANK), ("q_norm", HEAD_SLOT), ("k_norm", HEAD_SLOT),
         ("gdn_onorm", GDN_DV)),
        (("cm_ln_g", CM_WIDTH), ("cm_ln_b", CM_WIDTH)),
        (("a_log", LANES), ("dt_bias", LANES)))):
    _lane = 0
    for _name, _width in _fields:
        PK[_name] = (_row, _lane, _width)
        _lane += _width
    assert _lane <= D_MODEL


def _pk(pk_ref, name):
    row, lane0, width = PK[name]
    return pk_ref[row:row + 1, lane0:lane0 + width]


def _pk_spec(l):
    return pl.BlockSpec((None, PK_ROWS, D_MODEL), lambda *g: (l, 0, 0))

f32 = jnp.float32
bf16 = jnp.bfloat16


def _cparams(n_axes):
    return pltpu.CompilerParams(dimension_semantics=("arbitrary",) * n_axes,
                                vmem_limit_bytes=VMEM_LIMIT)


def _dot(a, b):
    return jnp.dot(a, b, preferred_element_type=f32)


def _dot_nt(a, b):
    return lax.dot_general(a, b, (((1,), (1,)), ((), ())), preferred_element_type=f32)


def _rms_rows(x, g, n=None):
    n = x.shape[-1] if n is None else n
    ms = jnp.sum(x * x, axis=-1, keepdims=True) * (1.0 / n)
    return x * lax.rsqrt(ms + EPS) * g


def _mod_kernel(c_ref, w_ref, b_ref, o_ref):
    a = c_ref[...]
    a = (a * jax.nn.sigmoid(a)).astype(bf16)
    y = _dot(a, w_ref[0].astype(bf16)) + b_ref[0]
    for r in range(y.shape[0]):
        o_ref[0, r] = y[r:r + 1]


def _modulation(c8, w_mod, b_mod):
    tn = 1536
    return pl.pallas_call(
        _mod_kernel,
        grid=(DEPTH, 3 * D_MODEL // tn),
        in_specs=[pl.BlockSpec((8, D_MODEL), lambda l, n: (0, 0)),
                  pl.BlockSpec((1, D_MODEL, tn), lambda l, n: (l, 0, n)),
                  pl.BlockSpec((1, 1, tn), lambda l, n: (l, 0, n))],
        out_specs=pl.BlockSpec((1, 8, 1, tn), lambda l, n: (l, 0, 0, n)),
        out_shape=jax.ShapeDtypeStruct((DEPTH, 8, 1, 3 * D_MODEL), f32),
        compiler_params=_cparams(2),
        name="modulation",
    )(c8, w_mod, b_mod.reshape(DEPTH, 1, 3 * D_MODEL))


W_CHUNK = 512


def _w_in_moves():
    offs = [0]
    for s in SPLIT_SIZES:
        offs.append(offs[-1] + s)
    cq, ckv, krope, z_a, qkv, ga, gb, z_b, cu, cv, z_c, gl = offs[:-1]
    moves = [(gl, N_BRANCH * D_MODEL, OFF_GL), (z_a, 512, OFF_ZA), (z_b, 512, OFF_ZB), (cu, 512, OFF_CU),
             (cv, 512, OFF_CV), (z_c, 512, OFF_ZC), (qkv, 2 * GDN_KW + GDN_VW, P16_W + OFF_QKV),
             (cq, MLA_Q_RANK, P16_W + OFF_CQ), (ckv, MLA_KV_RANK, P16_W + OFF_CKV)]
    return moves, ga, krope


def _w_chunks():
    moves, ga, krope = _w_in_moves()
    chunks = []
    for (a, w, d) in moves:
        for o in range(0, w, W_CHUNK):
            chunks.append((a + o, min(W_CHUNK, w - o), d + o))
    small0 = P16_W + OFF_SMALL
    chunks.append((ga, 4 * GDN_HEADS, small0))
    chunks.append((krope, MLA_ROPE, small0 + ROPE_LANE0))
    return chunks


def _inproj_kernel(xc_ref, xl_ref, mod_ref, pk_ref, wt_hbm, o16_ref, o32_ref, w_s, stage, sem, *, layer, nctx):
    t = pl.program_id(0)

    @pl.when(t == 0)
    def _():
        chunks = _w_chunks()

        def copy(j):
            src, n, _ = chunks[j]
            return pltpu.make_async_copy(wt_hbm.at[layer, pl.ds(src, n), :], stage.at[j % 2, pl.ds(0, n), :],
                                         sem.at[j % 2])

        copy(0).start()
        for j, (_, n, dst) in enumerate(chunks):
            if j + 1 < len(chunks):
                copy(j + 1).start()
            copy(j).wait()
            w_s[dst:dst + n, :] = stage[j % 2, 0:n, :].astype(bf16)
        small0 = P16_W + OFF_SMALL
        for lo, hi in ((4 * GDN_HEADS, ROPE_LANE0), (ROPE_LANE0 + MLA_ROPE, LANES)):
            w_s[small0 + lo:small0 + hi, :] = jnp.zeros((hi - lo, D_MODEL), bf16)

    x = jnp.where(t < nctx, xc_ref[...], xl_ref[...])
    m = mod_ref[0]
    shift = m[:, :D_MODEL]
    scale = m[:, D_MODEL:2 * D_MODEL]
    h = (_rms_rows(x, _pk(pk_ref, "norm_g")) * (1.0 + scale) + shift).astype(bf16)
    for a in range(0, P16_W, 512):
        o16_ref[:, a:a + 512] = _dot_nt(h, w_s[a:a + 512, :]).astype(bf16)
    for a in range(0, P32_W, 512):
        b = min(a + 512, P32_W)
        o32_ref[:, a:b] = _dot_nt(h, w_s[P16_W + a:P16_W + b, :])


def _inproj(xc, xl, mod_l, norm_g, w_t, l, dseq):
    tm = 512
    nctx = xc.shape[0] // tm
    nlat = xl.shape[0] // tm
    ntok = xc.shape[0] + xl.shape[0]
    kern = functools.partial(_inproj_kernel, layer=l, nctx=nctx)
    return pl.pallas_call(
        kern,
        grid=(nctx + nlat,),
        in_specs=[pl.BlockSpec((tm, D_MODEL), lambda t: (jnp.minimum(t, nctx - 1), 0)),
                  pl.BlockSpec((tm, D_MODEL), lambda t: (jnp.maximum(t - nctx, 0), 0)),
                  pl.BlockSpec((None, 1, 1, 3 * D_MODEL),
                               lambda t: (l, jnp.where(t < nctx, 4, ((t - nctx) * tm) // dseq), 0, 0)),
                  _pk_spec(l),
                  pl.BlockSpec(memory_space=pl.ANY)],
        out_specs=[pl.BlockSpec((tm, P16_W), lambda t: (t, 0)),
                   pl.BlockSpec((tm, P32_W), lambda t: (t, 0))],
        out_shape=[jax.ShapeDtypeStruct((ntok, P16_W), bf16),
                   jax.ShapeDtypeStruct((ntok, P32_W), f32)],
        scratch_shapes=[pltpu.VMEM((PROJ_W, D_MODEL), bf16),
                        pltpu.VMEM((2, W_CHUNK, D_MODEL), f32),
                        pltpu.SemaphoreType.DMA((2,))],
        compiler_params=_cparams(1),
        name="inproj",
    )(xc, xl, mod_l, norm_g, w_t)


def _mod_row(l, latent, tm, seq):
    if latent:
        return lambda t: (l, (t * tm) // seq, 0, 0)
    return lambda t: (l, 4, 0, 0)


def _rope(x, cos_t, sin_lo, sin_hi):
    return x * cos_t + pltpu.roll(x, LANES - 8, 1) * sin_lo + pltpu.roll(x, 8, 1) * sin_hi


def _build_kv(ckvn_b, kr, wk_ref, wv_ref, knorm, rope, k_s, v_s, r0):
    n = ckvn_b.shape[0]
    kfull = _dot(ckvn_b, wk_ref[...])
    v_s[r0:r0 + n, :] = _dot(ckvn_b, wv_ref[...]).astype(bf16)
    krg = kr * knorm
    if rope is not None:
        krg = _rope(krg, *rope)
    kr_ss = jnp.sum(kr * kr, axis=-1, keepdims=True)
    for h in range(MLA_HEADS):
        sl = slice(h * HEAD_SLOT, (h + 1) * HEAD_SLOT)
        kn = kfull[:, sl]
        ms = (jnp.sum(kn * kn, axis=-1, keepdims=True) + kr_ss) * (1.0 / MLA_QK)
        k_s[r0:r0 + n, sl] = ((kn * knorm + krg) * lax.rsqrt(ms + EPS)).astype(bf16)


def _rope_lane_mask(shape):
    lane = lax.broadcasted_iota(jnp.int32, shape, 1)
    return (lane >= ROPE_LANE0) & (lane < ROPE_LANE0 + MLA_ROPE)


def _attend_block(qa_b, wuq_ref, qnorm, rope, k_s, v_s, o_ref):
    tq = qa_b.shape[0]
    qfull = _dot(qa_b, wuq_ref[...])
    qgain = qnorm * (math.log2(math.e) / math.sqrt(MLA_QK))
    lane = lax.broadcasted_iota(jnp.int32, (tq, LANES), 1)

    def scores(h):
        sl = slice(h * HEAD_SLOT, (h + 1) * HEAD_SLOT)
        qh = _rms_rows(qfull[:, sl], qgain, n=MLA_QK)
        if rope is not None:
            qh = _rope(qh, *rope)
        return _dot_nt(qh.astype(bf16), k_s[:, sl])

    s_next = scores(0)
    outs = []
    for h in range(MLA_HEADS):
        s = s_next
        if h + 1 < MLA_HEADS:
            s_next = scores(h + 1)
        p = jnp.exp2(s - jnp.max(s, axis=-1, keepdims=True))
        den = jnp.sum(p, axis=-1, keepdims=True)
        hp = h // 2
        outs.append(_dot(p.astype(bf16), v_s[:, hp * LANES:(hp + 1) * LANES]) / den)
        if h % 2 == 1:
            o_ref[:, hp * LANES:(hp + 1) * LANES] = jnp.where(lane < MLA_V, outs[h - 1], outs[h])


def _attn_ctx_kernel(*refs, layer):
    if layer:
        mla_ref, pckvn_ref, pkr_ref = refs[:3]
        refs = refs[3:]
    else:
        mla_ref = refs[0]
        refs = refs[1:]
    pk_ref, wuq_ref, wk_ref, wv_ref, o_ref, ckvn_ref, kr_ref, k_s, v_s = refs
    for i in range(layer):
        ckvn_ref[i] = pckvn_ref[i]
        kr_ref[i] = pkr_ref[i]
    small = mla_ref[:, MLA_SMALL]
    kr = jnp.where(_rope_lane_mask(small.shape), small, 0.0)
    kr_ref[layer] = small[:, ROPE_LANE0:ROPE_LANE0 + MLA_ROPE]
    ckvn = _rms_rows(mla_ref[:, MLA_CKV], _pk(pk_ref, "kv_a_norm"))
    ckvn_ref[layer] = ckvn
    _build_kv(ckvn.astype(bf16), kr, wk_ref, wv_ref, _pk(pk_ref, "k_norm"), None, k_s, v_s, 0)
    qa = _rms_rows(mla_ref[:, MLA_CQ], _pk(pk_ref, "q_a_norm")).astype(bf16)
    _attend_block(qa, wuq_ref, _pk(pk_ref, "q_norm"), None, k_s, v_s, o_ref)


def _attn_lat_kernel(mla_ref, cckv_ref, ckr_ref, cos_ref, slo_ref, shi_ref,
                     pk_ref, wuq_ref, wk_ref, wv_ref,
                     o_ref, k_s, v_s, *, seq, past, tq):
    qi = pl.program_id(1)
    rb = 256

    @pl.when(qi == 0)
    def _():
        _build_kv(cckv_ref[...].astype(bf16), ckr_ref[...], wk_ref, wv_ref, _pk(pk_ref, "k_norm"), None,
                  k_s, v_s, 0)
        for r in range(seq // rb):
            rs = slice(r * rb, (r + 1) * rb)
            small = mla_ref[rs, MLA_SMALL]
            kr = jnp.where(_rope_lane_mask(small.shape), small, 0.0)
            ckvn = _rms_rows(mla_ref[rs, MLA_CKV], _pk(pk_ref, "kv_a_norm"))
            rope = (cos_ref[rs, :], slo_ref[rs, :], shi_ref[rs, :])
            _build_kv(ckvn.astype(bf16), kr, wk_ref, wv_ref, _pk(pk_ref, "k_norm"), rope, k_s, v_s, past + r * rb)

    rows = pl.ds(pl.multiple_of(qi * tq, tq), tq)
    qa = _rms_rows(mla_ref[rows, MLA_CQ], _pk(pk_ref, "q_a_norm")).astype(bf16)
    rope = (cos_ref[rows, :], slo_ref[rows, :], shi_ref[rows, :])
    _attend_block(qa, wuq_ref, _pk(pk_ref, "q_norm"), rope, k_s, v_s, o_ref)


def _attn_weight_specs(l, nidx):
    z = (0,) * (nidx - 1)

    def const(*idx):
        return lambda *g: idx

    return [_pk_spec(l),
            pl.BlockSpec((None, MLA_Q_RANK, MLA_HEADS * HEAD_SLOT), const(l, 0, 0)),
            pl.BlockSpec((None, MLA_KV_RANK, MLA_HEADS * HEAD_SLOT), const(l, 0, 0)),
            pl.BlockSpec((None, MLA_KV_RANK, MLA_WIDTH), const(l, 0, 0))]


def _attn_ctx(proj, prev, wts, l, batch, seq):
    ntok = batch * seq

    def stacked(n, width):
        return pl.BlockSpec((None, n, seq, width), lambda b: (b, 0, 0, 0))

    prev_specs = [stacked(l, MLA_KV_RANK), stacked(l, MLA_ROPE)] if l else []
    return pl.pallas_call(
        functools.partial(_attn_ctx_kernel, layer=l),
        grid=(batch,),
        in_specs=[pl.BlockSpec((seq, MLA_IN_W), lambda b: (b, OFF_SMALL // MLA_IN_W))]
        + prev_specs + _attn_weight_specs(l, 1),
        out_specs=[pl.BlockSpec((seq, MLA_WIDTH), lambda b: (b, 0)),
                   stacked(l + 1, MLA_KV_RANK), stacked(l + 1, MLA_ROPE)],
        out_shape=[jax.ShapeDtypeStruct((ntok, MLA_WIDTH), f32),
                   jax.ShapeDtypeStruct((batch, l + 1, seq, MLA_KV_RANK), f32),
                   jax.ShapeDtypeStruct((batch, l + 1, seq, MLA_ROPE), f32)],
        scratch_shapes=[pltpu.VMEM((seq, MLA_HEADS * HEAD_SLOT), bf16),
                        pltpu.VMEM((seq, MLA_WIDTH), bf16)],
        compiler_params=_cparams(1),
        name="attn_ctx",
    )(proj, *prev, *wts)


def _attn_lat(proj, row0, cache_ckv, cache_kr_p, rope_tabs, wts, l, batch, seq, past):
    ntok = batch * seq
    tq = 256
    nq = seq // tq
    kern = functools.partial(_attn_lat_kernel, seq=seq, past=past, tq=tq)
    tab = pl.BlockSpec((seq, LANES), lambda b, q: (0, 0))
    return pl.pallas_call(
        kern,
        grid=(batch, nq),
        in_specs=[pl.BlockSpec((seq, MLA_IN_W), lambda b, q: (row0 // seq + b, OFF_SMALL // MLA_IN_W)),
                  pl.BlockSpec((None, None, past, MLA_KV_RANK), lambda b, q: (b, l, 0, 0)),
                  pl.BlockSpec((None, None, past, LANES), lambda b, q: (b, l, 0, 0)),
                  tab, tab, tab]
        + _attn_weight_specs(l, 2),
        out_specs=pl.BlockSpec((tq, MLA_WIDTH), lambda b, q: (b * nq + q, 0)),
        out_shape=jax.ShapeDtypeStruct((ntok, MLA_WIDTH), f32),
        scratch_shapes=[pltpu.VMEM((past + seq, MLA_HEADS * HEAD_SLOT), bf16),
                        pltpu.VMEM((past + seq, MLA_WIDTH), bf16)],
        compiler_params=_cparams(2),
        name="attn_lat",
    )(proj, cache_ckv, cache_kr_p, *rope_tabs, *wts)


def _split3(x):
    hi = x.astype(bf16)
    r1 = x - hi.astype(f32)
    mid = r1.astype(bf16)
    lo = (r1 - mid.astype(f32)).astype(bf16)
    return hi, mid, lo


def _tri_cumsum(tri_b, x):
    hi, mid, lo = _split3(x)
    return _dot(tri_b, hi) + _dot(tri_b, mid) + _dot(tri_b, lo)


def _lane_bcast(x, c):
    return jnp.broadcast_to(x[:, c:c + 1], (x.shape[0], LANES))


def _gdn_kernel(*refs, seq, has_state, nprev):
    gin_ref, cw_ref, pk_ref = refs[:3]
    refs = refs[3:]
    s0_ref = sprev_ref = sout_ref = None
    if has_state:
        s0_ref, o_ref = refs[:2]
        refs = refs[2:]
    else:
        if nprev:
            sprev_ref = refs[0]
            refs = refs[1:]
        o_ref, sout_ref = refs[:2]
        refs = refs[2:]
    xpad, qkv_s, g_s, b_s, st_s, wq_s, ak_s, u_s, el_s, rhs_s = refs
    C = GDN_CHUNK
    nchunk = seq // C
    H = GDN_HEADS
    width = 2 * GDN_KW + GDN_VW
    halo = 8

    for j in range(width // LANES):
        xpad[j, 0:halo, :] = jnp.zeros((halo, LANES), f32)
        xpad[j, halo + seq:, :] = jnp.zeros((halo, LANES), f32)
        xpad[j, halo:halo + seq, :] = gin_ref[:, j * LANES:(j + 1) * LANES]
    if has_state:
        for d in range(2):
            for h in range(H):
                st_s[d * H + h] = s0_ref[d, h]
    else:
        st_s[...] = jnp.zeros((2 * H, GDN_DK, GDN_DV), f32)

    neg_a = -jnp.exp(_pk(pk_ref, "a_log"))
    dtb = _pk(pk_ref, "dt_bias")

    def conv_tile(j, l2norm):
        w = cw_ref[j]
        post = jnp.where(j < H, GDN_DK ** -0.5, 1.0)
        for c in range(nchunk):
            base = halo - CONV_W // 2 + c * C
            y = xpad[j, base:base + C, :] * w[0:1]
            for tap in range(1, CONV_W):
                y = y + xpad[j, base + tap:base + tap + C, :] * w[tap:tap + 1]
            y = y * jax.nn.sigmoid(y)
            if l2norm:
                y = y * (lax.rsqrt(jnp.sum(y * y, axis=-1, keepdims=True) + EPS) * post)
            qkv_s[j, c * C:(c + 1) * C, :] = y

    def conv_qk(j, carry):
        conv_tile(j, True)
        return carry

    def conv_v(j, carry):
        conv_tile(j, False)
        return carry

    tile_unroll = 4 if nchunk <= 4 else 1
    lax.fori_loop(0, 2 * H, conv_qk, 0, unroll=tile_unroll)
    lax.fori_loop(2 * H, 3 * H, conv_v, 0, unroll=tile_unroll)

    def prep(c, carry):
        r0 = pl.multiple_of(c * C, C)
        sm = gin_ref[pl.ds(r0, C), OFF_SMALL:OFF_SMALL + LANES]
        z = sm + dtb
        g_s[pl.ds(r0, C), :] = neg_a * (jnp.maximum(z, 0.0) + jnp.log1p(jnp.exp(-jnp.abs(z))))
        b_s[pl.ds(r0, C), :] = pltpu.roll(jax.nn.sigmoid(sm), LANES - 2 * H, 1)
        return carry

    lax.fori_loop(0, nchunk, prep, 0, unroll=4)

    ri = lax.broadcasted_iota(jnp.int32, (C, LANES), 0)
    cl = lax.broadcasted_iota(jnp.int32, (C, LANES), 1)
    fwd = cl < C
    cj = cl & (C - 1)
    eye2 = (ri == cj).astype(f32)
    incl2 = (fwd & (ri >= cj)) | (~fwd & (ri <= cj))
    strict2 = (fwd & (ri > cj)) | (~fwd & (ri < cj))
    xor = ri ^ cj
    level2 = sum((xor >= (1 << b)).astype(jnp.int32) for b in range(C.bit_length() - 1))
    lvl_top = jnp.where(fwd, level2, 0)
    lvl_bot = jnp.where(fwd, 0, level2)
    r2 = lax.broadcasted_iota(jnp.int32, (2 * C, C), 0)
    c2 = lax.broadcasted_iota(jnp.int32, (2 * C, C), 1)
    tri2 = (((r2 < C) & (r2 >= c2)) | ((r2 >= C) & (r2 - C <= c2))).astype(bf16)
    zrhs = jnp.zeros((C, 2 * LANES), bf16)
    zvn = jnp.zeros((C, LANES), bf16)
    fwd_row = fwd[0:1, :]
    cpi = min(GDN_SOLVE_CHUNKS, nchunk)

    def block_diag(x):
        return jnp.concatenate([jnp.where(fwd, x, 0.0), jnp.where(fwd, 0.0, x)], axis=0).astype(bf16)

    def solve_phase(i, carry):
        chains = []
        for cc in range(cpi):
            c = i * cpi + cc
            rows = pl.ds(pl.multiple_of(c * C, C), C)
            g2 = _tri_cumsum(tri2, g_s[rows, :])
            g2t = g2.T
            bt = b_s[rows, :]
            for h in range(H):
                chains.append((cc, c, h, rows, g2, g2t, bt))

        a2s, t2s = [], []
        for (cc, c, h, rows, g2, g2t, bt) in chains:
            q = qkv_s[h, rows, :]
            k = qkv_s[H + h, rows, :]
            v = qkv_s[2 * H + h, rows, :]
            kq = _dot_nt(jnp.concatenate([k, q], axis=0).astype(bf16),
                         jnp.concatenate([k, k], axis=0).astype(bf16))
            gcc_f = _lane_bcast(g2[:C], h)
            gcc_b = _lane_bcast(g2[C:], H + h)
            btc_f = _lane_bcast(bt, h)
            btc_b = _lane_bcast(bt, H + h)
            grow = jnp.where(fwd_row, g2t[h:h + 1, :], g2t[H + h:H + h + 1, :])
            diff = jnp.where(fwd, gcc_f, gcc_b) - grow
            dec = jnp.where(incl2, jnp.exp(jnp.where(incl2, diff, 0.0)), 0.0)
            a2 = jnp.where(strict2, jnp.where(fwd, btc_f, btc_b) * kq[:C] * dec, 0.0)
            a2s.append(a2)
            t2s.append(eye2 - jnp.where(level2 == 1, a2, 0.0))
            glast_f = gcc_f[C - 1:C, :]
            glast_b = gcc_b[0:1, :]
            e1_f = jnp.exp(gcc_f)
            e1_b = jnp.exp(gcc_b)
            ket = jnp.concatenate([k * jnp.exp(glast_f - gcc_f), k * jnp.exp(glast_b - gcc_b)], axis=0).T
            ak_s[pl.ds(pl.multiple_of((c * H + h) * 3 * C, 3 * C), 3 * C), :] = jnp.concatenate(
                [kq[C:] * dec, ket], axis=0).astype(bf16)
            for d, (btc, e1, glast) in enumerate(((btc_f, e1_f, glast_f), (btc_b, e1_b, glast_b))):
                ch = d * H + h
                j = (cc * H + h) * 2 + d
                rhs_s[j * C:(j + 1) * C, :] = jnp.concatenate([v * btc, k * (btc * e1)], axis=1).astype(bf16)
                wq_s[pl.ds(pl.multiple_of((c * 2 * H + ch) * 2 * C + C, C), C), :] = (q * e1).astype(bf16)
                el_s[pl.ds(pl.multiple_of((c * 2 * H + ch) * 8, 8), 8), :] = jnp.broadcast_to(
                    jnp.exp(glast), (8, LANES))

        for lv in range(2, C.bit_length()):
            rs = []
            for a2, t2 in zip(a2s, t2s):
                abd = jnp.concatenate([jnp.where(lvl_top == lv, a2, 0.0),
                                       jnp.where(lvl_bot == lv, a2, 0.0)], axis=0).astype(bf16)
                rs.append(_dot(t2.astype(bf16), abd))
            t2s = [t2 - _dot(r.astype(bf16), block_diag(t2)) for r, t2 in zip(rs, t2s)]

        for (cc, c, h, rows, g2, g2t, bt), t2 in zip(chains, t2s):
            t2b = t2.astype(bf16)
            for d in range(2):
                ch = d * H + h
                j = (cc * H + h) * 2 + d
                rhs = rhs_s[j * C:(j + 1) * C, :]
                rhs = jnp.concatenate([rhs, zrhs] if d == 0 else [zrhs, rhs], axis=0)
                uw = _dot(t2b, rhs)
                u_s[pl.ds(pl.multiple_of((c * 2 * H + ch) * C, C), C), :] = uw[:, :LANES]
                wq_s[pl.ds(pl.multiple_of((c * 2 * H + ch) * 2 * C, C), C), :] = uw[:, LANES:].astype(bf16)
        return carry

    lax.fori_loop(0, nchunk // cpi, solve_phase, 0)

    def scan_phase(i, carry):
        cs = [i if ch < H else nchunk - 1 - i for ch in range(2 * H)]
        s_old = [st_s[ch] for ch in range(2 * H)]
        r1 = [_dot(wq_s[pl.ds(pl.multiple_of((cs[ch] * 2 * H + ch) * 2 * C, 2 * C), 2 * C), :],
                   s_old[ch].astype(bf16)) for ch in range(2 * H)]
        r2s = []
        for ch in range(2 * H):
            u = u_s[pl.ds(pl.multiple_of((cs[ch] * 2 * H + ch) * C, C), C), :]
            vnb = (u - r1[ch][:C]).astype(bf16)
            rhs = jnp.concatenate([vnb, zvn] if ch < H else [zvn, vnb], axis=0)
            ak = ak_s[pl.ds(pl.multiple_of((cs[ch] * H + ch % H) * 3 * C, 3 * C), 3 * C), :]
            r2s.append(_dot(ak, rhs))
        for ch in range(2 * H):
            el = el_s[pl.ds(pl.multiple_of((cs[ch] * 2 * H + ch) * 8, 8), 8), :][0:1, :]
            st_s[ch] = s_old[ch] * el + r2s[ch][C:]
            xpad[ch, pl.ds(pl.multiple_of(cs[ch] * C, C), C), :] = r1[ch][C:] + r2s[ch][:C]
        return carry

    lax.fori_loop(0, nchunk, scan_phase, 0, unroll=4)

    onorm = _pk(pk_ref, "gdn_onorm")

    def fin(c, carry):
        rows = pl.ds(pl.multiple_of(c * C, C), C)
        for h in range(H):
            ls = slice(h * LANES, (h + 1) * LANES)
            o_ref[rows, ls] = _rms_rows(xpad[h, rows, :] + xpad[H + h, rows, :], onorm)
        return carry

    lax.fori_loop(0, nchunk, fin, 0, unroll=4)
    if sout_ref is not None:
        for i in range(nprev):
            sout_ref[i] = sprev_ref[i]
        for d in range(2):
            for h in range(H):
                sout_ref[nprev, d, h] = st_s[d * H + h]


def _gdn(proj, row0, conv_w, pack, state, prev_states, l, batch, seq):
    ntok = batch * seq
    width = 2 * GDN_KW + GDN_VW
    has_state = state is not None
    nprev = 0 if has_state else l
    nchunk = seq // GDN_CHUNK
    kern = functools.partial(_gdn_kernel, seq=seq, has_state=has_state, nprev=nprev)
    in_specs = [pl.BlockSpec((seq, GDN_IN_W), lambda b: (row0 // seq + b, OFF_QKV // GDN_IN_W)),
                pl.BlockSpec((None, width // LANES, CONV_W, LANES), lambda b: (l, 0, 0, 0)),
                _pk_spec(l)]
    args = [proj, conv_w, pack]
    o_spec = pl.BlockSpec((seq, GDN_VW), lambda b: (b, 0))
    o_shape = jax.ShapeDtypeStruct((ntok, GDN_VW), f32)
    st_block = (None, None, 2, GDN_HEADS, GDN_DK, GDN_DV)
    if has_state:
        in_specs.append(pl.BlockSpec(st_block, lambda b: (b, l, 0, 0, 0, 0)))
        args.append(state)
        out_specs, out_shape = o_spec, o_shape
    else:
        def stacked(n):
            return pl.BlockSpec((None, n) + st_block[2:], lambda b: (b, 0, 0, 0, 0, 0))

        if nprev:
            in_specs.append(stacked(nprev))
            args.append(prev_states)
        out_specs = [o_spec, stacked(nprev + 1)]
        out_shape = [o_shape, jax.ShapeDtypeStruct((batch, nprev + 1, 2, GDN_HEADS, GDN_DK, GDN_DV), f32)]
    return pl.pallas_call(
        kern,
        grid=(batch,),
        in_specs=in_specs,
        out_specs=out_specs,
        out_shape=out_shape,
        scratch_shapes=[pltpu.VMEM((width // LANES, seq + 16, LANES), f32),
                        pltpu.VMEM((width // LANES, seq, LANES), f32),
                        pltpu.VMEM((seq, LANES), f32),
                        pltpu.VMEM((seq, LANES), f32),
                        pltpu.VMEM((2 * GDN_HEADS, GDN_DK, GDN_DV), f32),
                        pltpu.VMEM((nchunk * 2 * GDN_HEADS * 2 * GDN_CHUNK, LANES), bf16),
                        pltpu.VMEM((nchunk * GDN_HEADS * 3 * GDN_CHUNK, LANES), bf16),
                        pltpu.VMEM((nchunk * 2 * GDN_HEADS * GDN_CHUNK, LANES), f32),
                        pltpu.VMEM((nchunk * 2 * GDN_HEADS * 8, LANES), f32),
                        pltpu.VMEM((min(GDN_SOLVE_CHUNKS, nchunk) * 2 * GDN_HEADS * GDN_CHUNK, 2 * LANES), bf16)],
        compiler_params=_cparams(1),
        name="gdn_lat" if has_state else "gdn_ctx",
    )(*args)


def _merge_kernel(x_ref, mod_ref, oa_ref, ob_ref, p_ref, pk_ref, ws_ref, bs_ref, wbr_ref, wo_ref,
                  out_ref, sv_s):
    tm = x_ref.shape[0]

    def group(off, width=512):
        return p_ref[:, off:off + width].astype(f32)

    def gelu(x):
        c1 = math.sqrt(2.0 / math.pi)
        half = 0.5 * x
        return half + half * jnp.tanh(x * (c1 + (c1 * 0.044715) * (x * x)))

    u = gelu(group(OFF_CU))
    vf = gelu(group(OFF_CV))
    mu = jnp.mean(vf, axis=-1, keepdims=True)
    vc = vf - mu
    var = jnp.mean(vc * vc, axis=-1, keepdims=True)
    vn = (vc * lax.rsqrt(var + EPS) * _pk(pk_ref, "cm_ln_g") + _pk(pk_ref, "cm_ln_b")).astype(bf16)
    for ck in range(tm // CM_CHUNK):
        rs = slice(ck * CM_CHUNK, (ck + 1) * CM_CHUNK)
        for g in range(CM_GROUPS):
            ls = slice(g * LANES, (g + 1) * LANES)
            sv_s[rs, ls] = _dot(ws_ref[g], vn[rs, ls]) + bs_ref[:, ls]
    o_c = u * sv_s[...]

    def silu(z):
        return z * jax.nn.sigmoid(z)

    brs = (oa_ref[...] * silu(group(OFF_ZA)), ob_ref[...] * silu(group(OFF_ZB)), o_c * silu(group(OFF_ZC)))
    ysum = None
    for n in range(N_BRANCH):
        yb = _dot(brs[n].astype(bf16), wbr_ref[n])
        t = jax.nn.sigmoid(group(OFF_GL + n * D_MODEL, D_MODEL)) * yb
        ysum = t if ysum is None else ysum + t
    y = _dot(ysum.astype(bf16), wo_ref[...])
    gate = mod_ref[0][:, 2 * D_MODEL:]
    out_ref[...] = x_ref[...] + gate * y


def _merge(x, mod_l, proj, row0, o_a, o_b, pack, ws_b, bs_full, wbr_b, wo_b, l, latent, seq):
    ntok = x.shape[0]
    tm = 512

    def const(*idx):
        return lambda t: idx

    return pl.pallas_call(
        _merge_kernel,
        grid=(ntok // tm,),
        in_specs=[pl.BlockSpec((tm, D_MODEL), lambda t: (t, 0)),
                  pl.BlockSpec((None, 1, 1, 3 * D_MODEL), _mod_row(l, latent, tm, seq)),
                  pl.BlockSpec((tm, 512), lambda t: (t, 0)),
                  pl.BlockSpec((tm, 512), lambda t: (t, 0)),
                  pl.BlockSpec((tm, P16_W), lambda t: (row0 // tm + t, 0)),
                  _pk_spec(l),
                  pl.BlockSpec((None, CM_GROUPS, CM_CHUNK, CM_CHUNK), const(l, 0, 0, 0)),
                  pl.BlockSpec((None, CM_CHUNK, CM_WIDTH), const(l, 0, 0)),
                  pl.BlockSpec((None, N_BRANCH, BRANCH_W, D_MODEL), const(l, 0, 0, 0)),
                  pl.BlockSpec((None, D_MODEL, D_MODEL), const(l, 0, 0))],
        out_specs=pl.BlockSpec((tm, D_MODEL), lambda t: (t, 0)),
        out_shape=jax.ShapeDtypeStruct((ntok, D_MODEL), f32),
        scratch_shapes=[pltpu.VMEM((tm, CM_WIDTH), f32)],
        compiler_params=_cparams(1),
        name="merge",
    )(x, mod_l, o_a, o_b, proj, pack, ws_b, bs_full, wbr_b, wo_b)


def _pad_last(x, n):
    return jnp.pad(x, [(0, 0)] * (x.ndim - 1) + [(0, n - x.shape[-1])])


def _rope_tables(seq):
    t = np.arange(seq)
    row = (t // GRID_W).astype(np.float32)
    colp = (t % GRID_W).astype(np.float32)
    nf = MLA_ROPE // 4
    inv = (ROPE_THETA ** (-np.arange(nf, dtype=np.float32) / nf)).astype(np.float32)
    cos_t = np.ones((seq, LANES), np.float32)
    s_lo = np.zeros((seq, LANES), np.float32)
    s_hi = np.zeros((seq, LANES), np.float32)
    for i, pos in enumerate((row, colp)):
        ang = (pos[:, None] * inv[None, :]).astype(np.float32)
        cs, sn = np.cos(ang), np.sin(ang)
        lo = ROPE_LANE0 + 2 * nf * i
        cos_t[:, lo:lo + nf] = cs
        cos_t[:, lo + nf:lo + 2 * nf] = cs
        s_lo[:, lo:lo + nf] = -sn
        s_hi[:, lo + nf:lo + 2 * nf] = sn
    return tuple(jnp.asarray(p) for p in (cos_t, s_lo, s_hi))


def kernel(x_prompt, x_sample, cache_ckv, cache_krope, state_gdn, c, c_ctx, norm_g, w_mod, b_mod, w_in, q_a_norm, w_uq, kv_a_norm, w_ukv, q_norm, k_norm, conv_w, a_log, dt_bias, gdn_onorm, cm_ln_g, cm_ln_b, w_s, b_s, w_branch, w_o):
    L = DEPTH
    batch, seq, _ = x_prompt.shape
    dbatch, dseq, _ = x_sample.shape
    past = cache_ckv.shape[2]

    w_t = jnp.swapaxes(w_in, 1, 2)
    wuq_p = _pad_last(w_uq.reshape(L, MLA_Q_RANK, MLA_HEADS, MLA_QK), HEAD_SLOT)
    wuq_p = wuq_p.reshape(L, MLA_Q_RANK, MLA_HEADS * HEAD_SLOT).astype(bf16)
    wukv = w_ukv.reshape(L, MLA_KV_RANK, MLA_HEADS, MLA_NOPE + MLA_V)
    wk_p = _pad_last(wukv[..., :MLA_NOPE], HEAD_SLOT).reshape(L, MLA_KV_RANK, MLA_HEADS * HEAD_SLOT).astype(bf16)
    wv_p = wukv[..., MLA_NOPE:].reshape(L, MLA_KV_RANK, MLA_WIDTH).astype(bf16)
    cache_kr_p = jnp.pad(cache_krope, [(0, 0)] * 3 + [(ROPE_LANE0, LANES - ROPE_LANE0 - MLA_ROPE)])
    rope_tabs = _rope_tables(dseq)
    conv_w = jnp.swapaxes(conv_w.reshape(L, CONV_W, -1, LANES), 1, 2)
    ws_b = w_s.astype(bf16)
    bs_full = jnp.repeat(jnp.swapaxes(b_s, 1, 2), CM_WIDTH // CM_GROUPS, axis=2)
    gates8 = [_pad_last(v.reshape(L, 2 * GDN_HEADS), LANES) for v in (a_log, dt_bias)]
    pack = jnp.stack([norm_g,
                      jnp.concatenate([q_a_norm, kv_a_norm, _pad_last(q_norm, HEAD_SLOT),
                                       _pad_last(k_norm, HEAD_SLOT), gdn_onorm], axis=-1),
                      jnp.concatenate([cm_ln_g, cm_ln_b], axis=-1),
                      _pad_last(jnp.concatenate(gates8, axis=-1), D_MODEL)], axis=1)
    attn_w = (pack, wuq_p, wk_p, wv_p)

    c8 = jnp.concatenate([c, c_ctx[None, :], jnp.zeros((8 - dbatch - 1, D_MODEL), f32)], axis=0)
    wbr_b = w_branch.astype(bf16)
    wo_b = w_o.astype(bf16)
    mod = _modulation(c8, w_mod, b_mod)

    yp = x_prompt.reshape(batch * seq, D_MODEL)
    ys = x_sample.reshape(dbatch * dseq, D_MODEL)
    caches, states = (), None
    for l in range(L):
        mod_l = mod
        p16, p32 = _inproj(yp, ys, mod_l, pack, w_t, l, dseq)
        lat0 = batch * seq
        o_a, new_ckv, new_kr = _attn_ctx(p32, caches, attn_w, l, batch, seq)
        caches = (new_ckv, new_kr)
        o_b, states = _gdn(p32, 0, conv_w, pack, None, states, l, batch, seq)
        yp = _merge(yp, mod_l, p16, 0, o_a, o_b, pack, ws_b, bs_full, wbr_b, wo_b, l, False, seq)
        o_a = _attn_lat(p32, lat0, cache_ckv, cache_kr_p, rope_tabs, attn_w, l, dbatch, dseq, past)
        o_b = _gdn(p32, lat0, conv_w, pack, state_gdn, None, l, dbatch, dseq)
        ys = _merge(ys, mod_l, p16, lat0, o_a, o_b, pack, ws_b, bs_full, wbr_b, wo_b, l, True, dseq)
    return (yp.reshape(batch, seq, D_MODEL), ys.reshape(dbatch, dseq, D_MODEL), caches[0], caches[1], states)
```

```python
import collections
import functools
import math

import numpy as np

import jax
import jax.numpy as jnp
from jax import lax
from jax.experimental import pallas as pl
from jax.experimental.pallas import tpu as pltpu

D_MODEL = 1024
DEPTH = 2
GRID_W = 64
EPS = 1e-6
MLA_HEADS = 8
MLA_NOPE = 64
MLA_ROPE = 32
MLA_QK = MLA_NOPE + MLA_ROPE
MLA_V = 64
MLA_Q_RANK = 384
MLA_KV_RANK = 256
MLA_WIDTH = MLA_HEADS * MLA_V
ROPE_THETA = 10000.0
GDN_HEADS = 4
GDN_DK = 128
GDN_DV = 128
GDN_KW = GDN_HEADS * GDN_DK
GDN_VW = GDN_HEADS * GDN_DV
GDN_CHUNK = 64
CONV_W = 5
GDN_SOLVE_CHUNKS = 4
CM_GROUPS = 4
CM_CHUNK = 128
CM_WIDTH = 512
N_BRANCH = 3
BRANCH_W = 512
SPLIT_SIZES = (MLA_Q_RANK, MLA_KV_RANK, MLA_ROPE, MLA_WIDTH,
               2 * GDN_KW + GDN_VW, 2 * GDN_HEADS, 2 * GDN_HEADS, GDN_VW,
               CM_WIDTH, CM_WIDTH, CM_WIDTH, N_BRANCH * D_MODEL)

LANES = 128
SUBLANES = 8
HEAD_SLOT = LANES
ROPE_LANE0 = MLA_NOPE

OFF_GL = 0
OFF_ZA = OFF_GL + N_BRANCH * D_MODEL
OFF_ZB = OFF_ZA + 512
OFF_CU = OFF_ZB + 512
OFF_CV = OFF_CU + 512
OFF_ZC = OFF_CV + 512
P16_W = OFF_ZC + 512
OFF_QKV = 0
OFF_SMALL = OFF_QKV + 2 * GDN_KW + GDN_VW
OFF_CQ = OFF_SMALL + LANES
OFF_CKV = OFF_CQ + MLA_Q_RANK
P32_W = OFF_CKV + MLA_KV_RANK
PROJ_W = P16_W + P32_W
GDN_IN_W = OFF_CQ
MLA_IN_W = P32_W - OFF_SMALL
MLA_SMALL = slice(0, LANES)
MLA_CQ = slice(OFF_CQ - OFF_SMALL, OFF_CKV - OFF_SMALL)
MLA_CKV = slice(OFF_CKV - OFF_SMALL, MLA_IN_W)

VMEM_LIMIT = 56 * 1024 * 1024

PK_ROWS = 4
PK = {}
for _row, _fields in enumerate((
        (("norm_g", D_MODEL),),
        (("q_a_norm", MLA_Q_RANK), ("kv_a_norm", MLA_KV_RANK), ("q_norm", HEAD_SLOT), ("k_norm", HEAD_SLOT),
         ("gdn_onorm", GDN_DV)),
        (("cm_ln_g", CM_WIDTH), ("cm_ln_b", CM_WIDTH)),
        (("a_log", LANES), ("dt_bias", LANES)))):
    _lane = 0
    for _name, _width in _fields:
        PK[_name] = (_row, _lane, _width)
        _lane += _width
    assert _lane <= D_MODEL


def _pk(pk_ref, name):
    row, lane0, width = PK[name]
    return pk_ref[row:row + 1, lane0:lane0 + width]


def _pk_spec(l):
    return pl.BlockSpec((None, PK_ROWS, D_MODEL), lambda *g: (l, 0, 0))

f32 = jnp.float32
bf16 = jnp.bfloat16


def _cparams(n_axes):
    return pltpu.CompilerParams(dimension_semantics=("arbitrary",) * n_axes,
                                vmem_limit_bytes=VMEM_LIMIT)


_Part = collections.namedtuple("_Part", "kernel in_specs args out_specs out_shape scratch_shapes")


def _run_parts(parts, grid, name):
    n_in = [len(p.in_specs) for p in parts]
    n_out = [len(p.out_specs) for p in parts]
    n_scr = [len(p.scratch_shapes) for p in parts]

    def body(*refs):
        ins, outs, scr = refs[:sum(n_in)], refs[sum(n_in):sum(n_in) + sum(n_out)], refs[sum(n_in) + sum(n_out):]
        i = o = s = 0
        for p, ni, no, ns in zip(parts, n_in, n_out, n_scr):
            p.kernel(*ins[i:i + ni], *outs[o:o + no], *scr[s:s + ns])
            i, o, s = i + ni, o + no, s + ns

    res = pl.pallas_call(
        body,
        grid=grid,
        in_specs=[sp for p in parts for sp in p.in_specs],
        out_specs=[sp for p in parts for sp in p.out_specs],
        out_shape=[sh for p in parts for sh in p.out_shape],
        scratch_shapes=[sc for p in parts for sc in p.scratch_shapes],
        compiler_params=_cparams(len(grid)),
        name=name,
    )(*[a for p in parts for a in p.args])
    out, o = [], 0
    for no in n_out:
        out.append(res[o:o + no])
        o += no
    return out


def _dot(a, b):
    return jnp.dot(a, b, preferred_element_type=f32)


def _dot_nt(a, b):
    return lax.dot_general(a, b, (((1,), (1,)), ((), ())), preferred_element_type=f32)


def _rms_rows(x, g, n=None):
    n = x.shape[-1] if n is None else n
    ms = jnp.sum(x * x, axis=-1, keepdims=True) * (1.0 / n)
    return x * lax.rsqrt(ms + EPS) * g


def _mod_kernel(c_ref, w_ref, b_ref, o_ref):
    a = c_ref[...]
    a = (a * jax.nn.sigmoid(a)).astype(bf16)
    y = _dot(a, w_ref[0].astype(bf16)) + b_ref[0]
    for r in range(y.shape[0]):
        o_ref[0, r] = y[r:r + 1]


def _modulation(c8, w_mod, b_mod):
    tn = 1536
    return pl.pallas_call(
        _mod_kernel,
        grid=(DEPTH, 3 * D_MODEL // tn),
        in_specs=[pl.BlockSpec((8, D_MODEL), lambda l, n: (0, 0)),
                  pl.BlockSpec((1, D_MODEL, tn), lambda l, n: (l, 0, n)),
                  pl.BlockSpec((1, 1, tn), lambda l, n: (l, 0, n))],
        out_specs=pl.BlockSpec((1, 8, 1, tn), lambda l, n: (l, 0, 0, n)),
        out_shape=jax.ShapeDtypeStruct((DEPTH, 8, 1, 3 * D_MODEL), f32),
        compiler_params=_cparams(2),
        name="modulation",
    )(c8, w_mod, b_mod.reshape(DEPTH, 1, 3 * D_MODEL))


W_CHUNK = 512


def _w_in_moves():
    offs = [0]
    for s in SPLIT_SIZES:
        offs.append(offs[-1] + s)
    cq, ckv, krope, z_a, qkv, ga, gb, z_b, cu, cv, z_c, gl = offs[:-1]
    moves = [(gl, N_BRANCH * D_MODEL, OFF_GL), (z_a, 512, OFF_ZA), (z_b, 512, OFF_ZB), (cu, 512, OFF_CU),
             (cv, 512, OFF_CV), (z_c, 512, OFF_ZC), (qkv, 2 * GDN_KW + GDN_VW, P16_W + OFF_QKV),
             (cq, MLA_Q_RANK, P16_W + OFF_CQ), (ckv, MLA_KV_RANK, P16_W + OFF_CKV)]
    return moves, ga, krope


def _w_chunks():
    moves, ga, krope = _w_in_moves()
    chunks = []
    for (a, w, d) in moves:
        for o in range(0, w, W_CHUNK):
            chunks.append((a + o, min(W_CHUNK, w - o), d + o))
    small0 = P16_W + OFF_SMALL
    chunks.append((ga, 4 * GDN_HEADS, small0))
    chunks.append((krope, MLA_ROPE, small0 + ROPE_LANE0))
    return chunks


def _inproj_kernel(xc_ref, xl_ref, mod_ref, pk_ref, wt_hbm, o16_ref, o32_ref, w_s, stage, sem, *, layer, nctx):
    t = pl.program_id(0)

    @pl.when(t == 0)
    def _():
        chunks = _w_chunks()

        def copy(j):
            src, n, _ = chunks[j]
            return pltpu.make_async_copy(wt_hbm.at[layer, pl.ds(src, n), :], stage.at[j % 2, pl.ds(0, n), :],
                                         sem.at[j % 2])

        copy(0).start()
        for j, (_, n, dst) in enumerate(chunks):
            if j + 1 < len(chunks):
                copy(j + 1).start()
            copy(j).wait()
            w_s[dst:dst + n, :] = stage[j % 2, 0:n, :].astype(bf16)
        small0 = P16_W + OFF_SMALL
        for lo, hi in ((4 * GDN_HEADS, ROPE_LANE0), (ROPE_LANE0 + MLA_ROPE, LANES)):
            w_s[small0 + lo:small0 + hi, :] = jnp.zeros((hi - lo, D_MODEL), bf16)

    x = jnp.where(t < nctx, xc_ref[...], xl_ref[...])
    m = mod_ref[0]
    shift = m[:, :D_MODEL]
    scale = m[:, D_MODEL:2 * D_MODEL]
    h = (_rms_rows(x, _pk(pk_ref, "norm_g")) * (1.0 + scale) + shift).astype(bf16)
    for a in range(0, P16_W, 512):
        o16_ref[:, a:a + 512] = _dot_nt(h, w_s[a:a + 512, :]).astype(bf16)
    for a in range(0, P32_W, 512):
        b = min(a + 512, P32_W)
        o32_ref[:, a:b] = _dot_nt(h, w_s[P16_W + a:P16_W + b, :])


def _inproj(xc, xl, mod_l, norm_g, w_t, l, dseq):
    tm = 512
    nctx = xc.shape[0] // tm
    nlat = xl.shape[0] // tm
    ntok = xc.shape[0] + xl.shape[0]
    kern = functools.partial(_inproj_kernel, layer=l, nctx=nctx)
    return pl.pallas_call(
        kern,
        grid=(nctx + nlat,),
        in_specs=[pl.BlockSpec((tm, D_MODEL), lambda t: (jnp.minimum(t, nctx - 1), 0)),
                  pl.BlockSpec((tm, D_MODEL), lambda t: (jnp.maximum(t - nctx, 0), 0)),
                  pl.BlockSpec((None, 1, 1, 3 * D_MODEL),
                               lambda t: (l, jnp.where(t < nctx, 4, ((t - nctx) * tm) // dseq), 0, 0)),
                  _pk_spec(l),
                  pl.BlockSpec(memory_space=pl.ANY)],
        out_specs=[pl.BlockSpec((tm, P16_W), lambda t: (t, 0)),
                   pl.BlockSpec((tm, P32_W), lambda t: (t, 0))],
        out_shape=[jax.ShapeDtypeStruct((ntok, P16_W), bf16),
                   jax.ShapeDtypeStruct((ntok, P32_W), f32)],
        scratch_shapes=[pltpu.VMEM((PROJ_W, D_MODEL), bf16),
                        pltpu.VMEM((2, W_CHUNK, D_MODEL), f32),
                        pltpu.SemaphoreType.DMA((2,))],
        compiler_params=_cparams(1),
        name="inproj",
    )(xc, xl, mod_l, norm_g, w_t)


def _mod_row(l, latent, tm, seq):
    if latent:
        return lambda t: (l, (t * tm) // seq, 0, 0)
    return lambda t: (l, 4, 0, 0)


def _rope(x, cos_t, sin_lo, sin_hi):
    return x * cos_t + pltpu.roll(x, LANES - 8, 1) * sin_lo + pltpu.roll(x, 8, 1) * sin_hi


def _build_kv(ckvn_b, kr, wk_ref, wv_ref, knorm, rope, k_s, v_s, r0):
    n = ckvn_b.shape[0]
    kfull = _dot(ckvn_b, wk_ref[...])
    v_s[r0:r0 + n, :] = _dot(ckvn_b, wv_ref[...]).astype(bf16)
    krg = kr * knorm
    if rope is not None:
        krg = _rope(krg, *rope)
    kr_ss = jnp.sum(kr * kr, axis=-1, keepdims=True)
    for h in range(MLA_HEADS):
        sl = slice(h * HEAD_SLOT, (h + 1) * HEAD_SLOT)
        kn = kfull[:, sl]
        ms = (jnp.sum(kn * kn, axis=-1, keepdims=True) + kr_ss) * (1.0 / MLA_QK)
        k_s[r0:r0 + n, sl] = ((kn * knorm + krg) * lax.rsqrt(ms + EPS)).astype(bf16)


def _rope_lane_mask(shape):
    lane = lax.broadcasted_iota(jnp.int32, shape, 1)
    return (lane >= ROPE_LANE0) & (lane < ROPE_LANE0 + MLA_ROPE)


def _attend_block(qa_b, wuq_ref, qnorm, rope, k_s, v_s, o_ref):
    tq = qa_b.shape[0]
    qfull = _dot(qa_b, wuq_ref[...])
    qgain = qnorm * (math.log2(math.e) / math.sqrt(MLA_QK))
    lane = lax.broadcasted_iota(jnp.int32, (tq, LANES), 1)

    def scores(h):
        sl = slice(h * HEAD_SLOT, (h + 1) * HEAD_SLOT)
        qh = _rms_rows(qfull[:, sl], qgain, n=MLA_QK)
        if rope is not None:
            qh = _rope(qh, *rope)
        return _dot_nt(qh.astype(bf16), k_s[:, sl])

    s_next = scores(0)
    outs = []
    for h in range(MLA_HEADS):
        s = s_next
        if h + 1 < MLA_HEADS:
            s_next = scores(h + 1)
        p = jnp.exp2(s - jnp.max(s, axis=-1, keepdims=True))
        den = jnp.sum(p, axis=-1, keepdims=True)
        hp = h // 2
        outs.append(_dot(p.astype(bf16), v_s[:, hp * LANES:(hp + 1) * LANES]) / den)
        if h % 2 == 1:
            o_ref[:, hp * LANES:(hp + 1) * LANES] = jnp.where(lane < MLA_V, outs[h - 1], outs[h])


def _attn_ctx_kernel(*refs, layer):
    if layer:
        mla_ref, pckvn_ref, pkr_ref = refs[:3]
        refs = refs[3:]
    else:
        mla_ref = refs[0]
        refs = refs[1:]
    pk_ref, wuq_ref, wk_ref, wv_ref, o_ref, ckvn_ref, kr_ref, k_s, v_s = refs
    for i in range(layer):
        ckvn_ref[i] = pckvn_ref[i]
        kr_ref[i] = pkr_ref[i]
    small = mla_ref[:, MLA_SMALL]
    kr = jnp.where(_rope_lane_mask(small.shape), small, 0.0)
    kr_ref[layer] = small[:, ROPE_LANE0:ROPE_LANE0 + MLA_ROPE]
    ckvn = _rms_rows(mla_ref[:, MLA_CKV], _pk(pk_ref, "kv_a_norm"))
    ckvn_ref[layer] = ckvn
    _build_kv(ckvn.astype(bf16), kr, wk_ref, wv_ref, _pk(pk_ref, "k_norm"), None, k_s, v_s, 0)
    qa = _rms_rows(mla_ref[:, MLA_CQ], _pk(pk_ref, "q_a_norm")).astype(bf16)
    _attend_block(qa, wuq_ref, _pk(pk_ref, "q_norm"), None, k_s, v_s, o_ref)


def _attn_lat_kernel(mla_ref, cckv_ref, ckr_ref, cos_ref, slo_ref, shi_ref,
                     pk_ref, wuq_ref, wk_ref, wv_ref,
                     o_ref, k_s, v_s, *, seq, past, tq):
    qi = pl.program_id(1)
    rb = 256

    @pl.when(qi == 0)
    def _():
        _build_kv(cckv_ref[...].astype(bf16), ckr_ref[...], wk_ref, wv_ref, _pk(pk_ref, "k_norm"), None,
                  k_s, v_s, 0)
        for r in range(seq // rb):
            rs = slice(r * rb, (r + 1) * rb)
            small = mla_ref[rs, MLA_SMALL]
            kr = jnp.where(_rope_lane_mask(small.shape), small, 0.0)
            ckvn = _rms_rows(mla_ref[rs, MLA_CKV], _pk(pk_ref, "kv_a_norm"))
            rope = (cos_ref[rs, :], slo_ref[rs, :], shi_ref[rs, :])
            _build_kv(ckvn.astype(bf16), kr, wk_ref, wv_ref, _pk(pk_ref, "k_norm"), rope, k_s, v_s, past + r * rb)

    rows = pl.ds(pl.multiple_of(qi * tq, tq), tq)
    qa = _rms_rows(mla_ref[rows, MLA_CQ], _pk(pk_ref, "q_a_norm")).astype(bf16)
    rope = (cos_ref[rows, :], slo_ref[rows, :], shi_ref[rows, :])
    _attend_block(qa, wuq_ref, _pk(pk_ref, "q_norm"), rope, k_s, v_s, o_ref)


def _attn_weight_specs(l, nidx):
    z = (0,) * (nidx - 1)

    def const(*idx):
        return lambda *g: idx

    return [_pk_spec(l),
            pl.BlockSpec((None, MLA_Q_RANK, MLA_HEADS * HEAD_SLOT), const(l, 0, 0)),
            pl.BlockSpec((None, MLA_KV_RANK, MLA_HEADS * HEAD_SLOT), const(l, 0, 0)),
            pl.BlockSpec((None, MLA_KV_RANK, MLA_WIDTH), const(l, 0, 0))]


def _attn_ctx(proj, prev, wts, l, batch, seq):
    ntok = batch * seq

    def stacked(n, width):
        return pl.BlockSpec((None, n, seq, width), lambda b: (b, 0, 0, 0))

    prev_specs = [stacked(l, MLA_KV_RANK), stacked(l, MLA_ROPE)] if l else []
    return _Part(
        kernel=functools.partial(_attn_ctx_kernel, layer=l),
        in_specs=[pl.BlockSpec((seq, MLA_IN_W), lambda b: (b, OFF_SMALL // MLA_IN_W))]
        + prev_specs + _attn_weight_specs(l, 1),
        args=[proj, *prev, *wts],
        out_specs=[pl.BlockSpec((seq, MLA_WIDTH), lambda b: (b, 0)),
                   stacked(l + 1, MLA_KV_RANK), stacked(l + 1, MLA_ROPE)],
        out_shape=[jax.ShapeDtypeStruct((ntok, MLA_WIDTH), f32),
                   jax.ShapeDtypeStruct((batch, l + 1, seq, MLA_KV_RANK), f32),
                   jax.ShapeDtypeStruct((batch, l + 1, seq, MLA_ROPE), f32)],
        scratch_shapes=[pltpu.VMEM((seq, MLA_HEADS * HEAD_SLOT), bf16),
                        pltpu.VMEM((seq, MLA_WIDTH), bf16)])


def _attn_lat(proj, row0, cache_ckv, cache_kr_p, rope_tabs, wts, l, batch, seq, past):
    ntok = batch * seq
    tq = 256
    nq = seq // tq
    kern = functools.partial(_attn_lat_kernel, seq=seq, past=past, tq=tq)
    tab = pl.BlockSpec((seq, LANES), lambda b, q: (0, 0))
    return pl.pallas_call(
        kern,
        grid=(batch, nq),
        in_specs=[pl.BlockSpec((seq, MLA_IN_W), lambda b, q: (row0 // seq + b, OFF_SMALL // MLA_IN_W)),
                  pl.BlockSpec((None, None, past, MLA_KV_RANK), lambda b, q: (b, l, 0, 0)),
                  pl.BlockSpec((None, None, past, LANES), lambda b, q: (b, l, 0, 0)),
                  tab, tab, tab]
        + _attn_weight_specs(l, 2),
        out_specs=pl.BlockSpec((tq, MLA_WIDTH), lambda b, q: (b * nq + q, 0)),
        out_shape=jax.ShapeDtypeStruct((ntok, MLA_WIDTH), f32),
        scratch_shapes=[pltpu.VMEM((past + seq, MLA_HEADS * HEAD_SLOT), bf16),
                        pltpu.VMEM((past + seq, MLA_WIDTH), bf16)],
        compiler_params=_cparams(2),
        name="attn_lat",
    )(proj, cache_ckv, cache_kr_p, *rope_tabs, *wts)


def _split3(x):
    hi = x.astype(bf16)
    r1 = x - hi.astype(f32)
    mid = r1.astype(bf16)
    lo = (r1 - mid.astype(f32)).astype(bf16)
    return hi, mid, lo


def _tri_cumsum(tri_b, x):
    hi, mid, lo = _split3(x)
    return _dot(tri_b, hi) + _dot(tri_b, mid) + _dot(tri_b, lo)


def _lane_bcast(x, c):
    return jnp.broadcast_to(x[:, c:c + 1], (x.shape[0], LANES))


def _gdn_kernel(*refs, seq, has_state, nprev):
    gin_ref, cw_ref, pk_ref = refs[:3]
    refs = refs[3:]
    s0_ref = sprev_ref = sout_ref = None
    if has_state:
        s0_ref, o_ref = refs[:2]
        refs = refs[2:]
    else:
        if nprev:
            sprev_ref = refs[0]
            refs = refs[1:]
        o_ref, sout_ref = refs[:2]
        refs = refs[2:]
    xpad, qkv_s, g_s, b_s, st_s, wq_s, ak_s, u_s, el_s, rhs_s = refs
    C = GDN_CHUNK
    nchunk = seq // C
    H = GDN_HEADS
    width = 2 * GDN_KW + GDN_VW
    halo = 8

    for j in range(width // LANES):
        xpad[j, 0:halo, :] = jnp.zeros((halo, LANES), f32)
        xpad[j, halo + seq:, :] = jnp.zeros((halo, LANES), f32)
        xpad[j, halo:halo + seq, :] = gin_ref[:, j * LANES:(j + 1) * LANES]
    if has_state:
        for d in range(2):
            for h in range(H):
                st_s[d * H + h] = s0_ref[d, h]
    else:
        st_s[...] = jnp.zeros((2 * H, GDN_DK, GDN_DV), f32)

    neg_a = -jnp.exp(_pk(pk_ref, "a_log"))
    dtb = _pk(pk_ref, "dt_bias")

    def conv_tile(j, l2norm):
        w = cw_ref[j]
        post = jnp.where(j < H, GDN_DK ** -0.5, 1.0)
        for c in range(nchunk):
            base = halo - CONV_W // 2 + c * C
            y = xpad[j, base:base + C, :] * w[0:1]
            for tap in range(1, CONV_W):
                y = y + xpad[j, base + tap:base + tap + C, :] * w[tap:tap + 1]
            y = y * jax.nn.sigmoid(y)
            if l2norm:
                y = y * (lax.rsqrt(jnp.sum(y * y, axis=-1, keepdims=True) + EPS) * post)
            qkv_s[j, c * C:(c + 1) * C, :] = y

    def conv_qk(j, carry):
        conv_tile(j, True)
        return carry

    def conv_v(j, carry):
        conv_tile(j, False)
        return carry

    tile_unroll = 4 if nchunk <= 4 else 1
    lax.fori_loop(0, 2 * H, conv_qk, 0, unroll=tile_unroll)
    lax.fori_loop(2 * H, 3 * H, conv_v, 0, unroll=tile_unroll)

    def prep(c, carry):
        r0 = pl.multiple_of(c * C, C)
        sm = gin_ref[pl.ds(r0, C), OFF_SMALL:OFF_SMALL + LANES]
        z = sm + dtb
        g_s[pl.ds(r0, C), :] = neg_a * (jnp.maximum(z, 0.0) + jnp.log1p(jnp.exp(-jnp.abs(z))))
        b_s[pl.ds(r0, C), :] = pltpu.roll(jax.nn.sigmoid(sm), LANES - 2 * H, 1)
        return carry

    lax.fori_loop(0, nchunk, prep, 0, unroll=4)

    ri = lax.broadcasted_iota(jnp.int32, (C, LANES), 0)
    cl = lax.broadcasted_iota(jnp.int32, (C, LANES), 1)
    fwd = cl < C
    cj = cl & (C - 1)
    eye2 = (ri == cj).astype(f32)
    incl2 = (fwd & (ri >= cj)) | (~fwd & (ri <= cj))
    strict2 = (fwd & (ri > cj)) | (~fwd & (ri < cj))
    xor = ri ^ cj
    level2 = sum((xor >= (1 << b)).astype(jnp.int32) for b in range(C.bit_length() - 1))
    lvl_top = jnp.where(fwd, level2, 0)
    lvl_bot = jnp.where(fwd, 0, level2)
    r2 = lax.broadcasted_iota(jnp.int32, (2 * C, C), 0)
    c2 = lax.broadcasted_iota(jnp.int32, (2 * C, C), 1)
    tri2 = (((r2 < C) & (r2 >= c2)) | ((r2 >= C) & (r2 - C <= c2))).astype(bf16)
    zrhs = jnp.zeros((C, 2 * LANES), bf16)
    zvn = jnp.zeros((C, LANES), bf16)
    fwd_row = fwd[0:1, :]
    cpi = min(GDN_SOLVE_CHUNKS, nchunk)

    def block_diag(x):
        return jnp.concatenate([jnp.where(fwd, x, 0.0), jnp.where(fwd, 0.0, x)], axis=0).astype(bf16)

    def solve_phase(i, carry):
        chains = []
        for cc in range(cpi):
            c = i * cpi + cc
            rows = pl.ds(pl.multiple_of(c * C, C), C)
            g2 = _tri_cumsum(tri2, g_s[rows, :])
            g2t = g2.T
            bt = b_s[rows, :]
            for h in range(H):
                chains.append((cc, c, h, rows, g2, g2t, bt))

        a2s, t2s = [], []
        for (cc, c, h, rows, g2, g2t, bt) in chains:
            q = qkv_s[h, rows, :]
            k = qkv_s[H + h, rows, :]
            v = qkv_s[2 * H + h, rows, :]
            kq = _dot_nt(jnp.concatenate([k, q], axis=0).astype(bf16),
                         jnp.concatenate([k, k], axis=0).astype(bf16))
            gcc_f = _lane_bcast(g2[:C], h)
            gcc_b = _lane_bcast(g2[C:], H + h)
            btc_f = _lane_bcast(bt, h)
            btc_b = _lane_bcast(bt, H + h)
            grow = jnp.where(fwd_row, g2t[h:h + 1, :], g2t[H + h:H + h + 1, :])
            diff = jnp.where(fwd, gcc_f, gcc_b) - grow
            dec = jnp.where(incl2, jnp.exp(jnp.where(incl2, diff, 0.0)), 0.0)
            a2 = jnp.where(strict2, jnp.where(fwd, btc_f, btc_b) * kq[:C] * dec, 0.0)
            a2s.append(a2)
            t2s.append(eye2 - jnp.where(level2 == 1, a2, 0.0))
            glast_f = gcc_f[C - 1:C, :]
            glast_b = gcc_b[0:1, :]
            e1_f = jnp.exp(gcc_f)
            e1_b = jnp.exp(gcc_b)
            ket = jnp.concatenate([k * jnp.exp(glast_f - gcc_f), k * jnp.exp(glast_b - gcc_b)], axis=0).T
            ak_s[pl.ds(pl.multiple_of((c * H + h) * 3 * C, 3 * C), 3 * C), :] = jnp.concatenate(
                [kq[C:] * dec, ket], axis=0).astype(bf16)
            for d, (btc, e1, glast) in enumerate(((btc_f, e1_f, glast_f), (btc_b, e1_b, glast_b))):
                ch = d * H + h
                j = (cc * H + h) * 2 + d
                rhs_s[j * C:(j + 1) * C, :] = jnp.concatenate([v * btc, k * (btc * e1)], axis=1).astype(bf16)
                wq_s[pl.ds(pl.multiple_of((c * 2 * H + ch) * 2 * C + C, C), C), :] = (q * e1).astype(bf16)
                el_s[pl.ds(pl.multiple_of((c * 2 * H + ch) * 8, 8), 8), :] = jnp.broadcast_to(
                    jnp.exp(glast), (8, LANES))

        for lv in range(2, C.bit_length()):
            rs = []
            for a2, t2 in zip(a2s, t2s):
                abd = jnp.concatenate([jnp.where(lvl_top == lv, a2, 0.0),
                                       jnp.where(lvl_bot == lv, a2, 0.0)], axis=0).astype(bf16)
                rs.append(_dot(t2.astype(bf16), abd))
            t2s = [t2 - _dot(r.astype(bf16), block_diag(t2)) for r, t2 in zip(rs, t2s)]

        for (cc, c, h, rows, g2, g2t, bt), t2 in zip(chains, t2s):
            t2b = t2.astype(bf16)
            for d in range(2):
                ch = d * H + h
                j = (cc * H + h) * 2 + d
                rhs = rhs_s[j * C:(j + 1) * C, :]
                rhs = jnp.concatenate([rhs, zrhs] if d == 0 else [zrhs, rhs], axis=0)
                uw = _dot(t2b, rhs)
                u_s[pl.ds(pl.multiple_of((c * 2 * H + ch) * C, C), C), :] = uw[:, :LANES]
                wq_s[pl.ds(pl.multiple_of((c * 2 * H + ch) * 2 * C, C), C), :] = uw[:, LANES:].astype(bf16)
        return carry

    lax.fori_loop(0, nchunk // cpi, solve_phase, 0)

    def scan_phase(i, carry):
        cs = [i if ch < H else nchunk - 1 - i for ch in range(2 * H)]
        s_old = [st_s[ch] for ch in range(2 * H)]
        r1 = [_dot(wq_s[pl.ds(pl.multiple_of((cs[ch] * 2 * H + ch) * 2 * C, 2 * C), 2 * C), :],
                   s_old[ch].astype(bf16)) for ch in range(2 * H)]
        r2s = []
        for ch in range(2 * H):
            u = u_s[pl.ds(pl.multiple_of((cs[ch] * 2 * H + ch) * C, C), C), :]
            vnb = (u - r1[ch][:C]).astype(bf16)
            rhs = jnp.concatenate([vnb, zvn] if ch < H else [zvn, vnb], axis=0)
            ak = ak_s[pl.ds(pl.multiple_of((cs[ch] * H + ch % H) * 3 * C, 3 * C), 3 * C), :]
            r2s.append(_dot(ak, rhs))
        for ch in range(2 * H):
            el = el_s[pl.ds(pl.multiple_of((cs[ch] * 2 * H + ch) * 8, 8), 8), :][0:1, :]
            st_s[ch] = s_old[ch] * el + r2s[ch][C:]
            xpad[ch, pl.ds(pl.multiple_of(cs[ch] * C, C), C), :] = r1[ch][C:] + r2s[ch][:C]
        return carry

    lax.fori_loop(0, nchunk, scan_phase, 0, unroll=4)

    onorm = _pk(pk_ref, "gdn_onorm")

    def fin(c, carry):
        rows = pl.ds(pl.multiple_of(c * C, C), C)
        for h in range(H):
            ls = slice(h * LANES, (h + 1) * LANES)
            o_ref[rows, ls] = _rms_rows(xpad[h, rows, :] + xpad[H + h, rows, :], onorm)
        return carry

    lax.fori_loop(0, nchunk, fin, 0, unroll=4)
    if sout_ref is not None:
        for i in range(nprev):
            sout_ref[i] = sprev_ref[i]
        for d in range(2):
            for h in range(H):
                sout_ref[nprev, d, h] = st_s[d * H + h]


def _gdn(proj, row0, conv_w, pack, state, prev_states, l, batch, seq):
    ntok = batch * seq
    width = 2 * GDN_KW + GDN_VW
    has_state = state is not None
    nprev = 0 if has_state else l
    nchunk = seq // GDN_CHUNK
    kern = functools.partial(_gdn_kernel, seq=seq, has_state=has_state, nprev=nprev)
    in_specs = [pl.BlockSpec((seq, GDN_IN_W), lambda b: (row0 // seq + b, OFF_QKV // GDN_IN_W)),
                pl.BlockSpec((None, width // LANES, CONV_W, LANES), lambda b: (l, 0, 0, 0)),
                _pk_spec(l)]
    args = [proj, conv_w, pack]
    o_spec = pl.BlockSpec((seq, GDN_VW), lambda b: (b, 0))
    o_shape = jax.ShapeDtypeStruct((ntok, GDN_VW), f32)
    st_block = (None, None, 2, GDN_HEADS, GDN_DK, GDN_DV)
    if has_state:
        in_specs.append(pl.BlockSpec(st_block, lambda b: (b, l, 0, 0, 0, 0)))
        args.append(state)
        out_specs, out_shape = [o_spec], [o_shape]
    else:
        def stacked(n):
            return pl.BlockSpec((None, n) + st_block[2:], lambda b: (b, 0, 0, 0, 0, 0))

        if nprev:
            in_specs.append(stacked(nprev))
            args.append(prev_states)
        out_specs = [o_spec, stacked(nprev + 1)]
        out_shape = [o_shape, jax.ShapeDtypeStruct((batch, nprev + 1, 2, GDN_HEADS, GDN_DK, GDN_DV), f32)]
    return _Part(
        kernel=kern,
        in_specs=in_specs,
        args=args,
        out_specs=out_specs,
        out_shape=out_shape,
        scratch_shapes=[pltpu.VMEM((width // LANES, seq + 16, LANES), f32),
                        pltpu.VMEM((width // LANES, seq, LANES), f32),
                        pltpu.VMEM((seq, LANES), f32),
                        pltpu.VMEM((seq, LANES), f32),
                        pltpu.VMEM((2 * GDN_HEADS, GDN_DK, GDN_DV), f32),
                        pltpu.VMEM((nchunk * 2 * GDN_HEADS * 2 * GDN_CHUNK, LANES), bf16),
                        pltpu.VMEM((nchunk * GDN_HEADS * 3 * GDN_CHUNK, LANES), bf16),
                        pltpu.VMEM((nchunk * 2 * GDN_HEADS * GDN_CHUNK, LANES), f32),
                        pltpu.VMEM((nchunk * 2 * GDN_HEADS * 8, LANES), f32),
                        pltpu.VMEM((min(GDN_SOLVE_CHUNKS, nchunk) * 2 * GDN_HEADS * GDN_CHUNK, 2 * LANES), bf16)])


def _merge_kernel(x_ref, mod_ref, oa_ref, ob_ref, p_ref, pk_ref, ws_ref, bs_ref, wbr_ref, wo_ref,
                  out_ref, sv_s):
    tm = x_ref.shape[0]

    def group(off, width=512):
        return p_ref[:, off:off + width].astype(f32)

    def gelu(x):
        c1 = math.sqrt(2.0 / math.pi)
        half = 0.5 * x
        return half + half * jnp.tanh(x * (c1 + (c1 * 0.044715) * (x * x)))

    u = gelu(group(OFF_CU))
    vf = gelu(group(OFF_CV))
    mu = jnp.mean(vf, axis=-1, keepdims=True)
    vc = vf - mu
    var = jnp.mean(vc * vc, axis=-1, keepdims=True)
    vn = (vc * lax.rsqrt(var + EPS) * _pk(pk_ref, "cm_ln_g") + _pk(pk_ref, "cm_ln_b")).astype(bf16)
    for ck in range(tm // CM_CHUNK):
        rs = slice(ck * CM_CHUNK, (ck + 1) * CM_CHUNK)
        for g in range(CM_GROUPS):
            ls = slice(g * LANES, (g + 1) * LANES)
            sv_s[rs, ls] = _dot(ws_ref[g], vn[rs, ls]) + bs_ref[:, ls]
    o_c = u * sv_s[...]

    def silu(z):
        return z * jax.nn.sigmoid(z)

    brs = (oa_ref[...] * silu(group(OFF_ZA)), ob_ref[...] * silu(group(OFF_ZB)), o_c * silu(group(OFF_ZC)))
    ysum = None
    for n in range(N_BRANCH):
        yb = _dot(brs[n].astype(bf16), wbr_ref[n])
        t = jax.nn.sigmoid(group(OFF_GL + n * D_MODEL, D_MODEL)) * yb
        ysum = t if ysum is None else ysum + t
    y = _dot(ysum.astype(bf16), wo_ref[...])
    gate = mod_ref[0][:, 2 * D_MODEL:]
    out_ref[...] = x_ref[...] + gate * y


def _merge(x, mod_l, proj, row0, o_a, o_b, pack, ws_b, bs_full, wbr_b, wo_b, l, latent, seq):
    ntok = x.shape[0]
    tm = 512

    def const(*idx):
        return lambda t: idx

    return pl.pallas_call(
        _merge_kernel,
        grid=(ntok // tm,),
        in_specs=[pl.BlockSpec((tm, D_MODEL), lambda t: (t, 0)),
                  pl.BlockSpec((None, 1, 1, 3 * D_MODEL), _mod_row(l, latent, tm, seq)),
                  pl.BlockSpec((tm, 512), lambda t: (t, 0)),
                  pl.BlockSpec((tm, 512), lambda t: (t, 0)),
                  pl.BlockSpec((tm, P16_W), lambda t: (row0 // tm + t, 0)),
                  _pk_spec(l),
                  pl.BlockSpec((None, CM_GROUPS, CM_CHUNK, CM_CHUNK), const(l, 0, 0, 0)),
                  pl.BlockSpec((None, CM_CHUNK, CM_WIDTH), const(l, 0, 0)),
                  pl.BlockSpec((None, N_BRANCH, BRANCH_W, D_MODEL), const(l, 0, 0, 0)),
                  pl.BlockSpec((None, D_MODEL, D_MODEL), const(l, 0, 0))],
        out_specs=pl.BlockSpec((tm, D_MODEL), lambda t: (t, 0)),
        out_shape=jax.ShapeDtypeStruct((ntok, D_MODEL), f32),
        scratch_shapes=[pltpu.VMEM((tm, CM_WIDTH), f32)],
        compiler_params=_cparams(1),
        name="merge",
    )(x, mod_l, o_a, o_b, proj, pack, ws_b, bs_full, wbr_b, wo_b)


def _pad_last(x, n):
    return jnp.pad(x, [(0, 0)] * (x.ndim - 1) + [(0, n - x.shape[-1])])


def _rope_tables(seq):
    t = np.arange(seq)
    row = (t // GRID_W).astype(np.float32)
    colp = (t % GRID_W).astype(np.float32)
    nf = MLA_ROPE // 4
    inv = (ROPE_THETA ** (-np.arange(nf, dtype=np.float32) / nf)).astype(np.float32)
    cos_t = np.ones((seq, LANES), np.float32)
    s_lo = np.zeros((seq, LANES), np.float32)
    s_hi = np.zeros((seq, LANES), np.float32)
    for i, pos in enumerate((row, colp)):
        ang = (pos[:, None] * inv[None, :]).astype(np.float32)
        cs, sn = np.cos(ang), np.sin(ang)
        lo = ROPE_LANE0 + 2 * nf * i
        cos_t[:, lo:lo + nf] = cs
        cos_t[:, lo + nf:lo + 2 * nf] = cs
        s_lo[:, lo:lo + nf] = -sn
        s_hi[:, lo + nf:lo + 2 * nf] = sn
    return tuple(jnp.asarray(p) for p in (cos_t, s_lo, s_hi))


def kernel(x_prompt, x_sample, cache_ckv, cache_krope, state_gdn, c, c_ctx, norm_g, w_mod, b_mod, w_in, q_a_norm, w_uq, kv_a_norm, w_ukv, q_norm, k_norm, conv_w, a_log, dt_bias, gdn_onorm, cm_ln_g, cm_ln_b, w_s, b_s, w_branch, w_o):
    L = DEPTH
    batch, seq, _ = x_prompt.shape
    dbatch, dseq, _ = x_sample.shape
    past = cache_ckv.shape[2]

    w_t = jnp.swapaxes(w_in, 1, 2)
    wuq_p = _pad_last(w_uq.reshape(L, MLA_Q_RANK, MLA_HEADS, MLA_QK), HEAD_SLOT)
    wuq_p = wuq_p.reshape(L, MLA_Q_RANK, MLA_HEADS * HEAD_SLOT).astype(bf16)
    wukv = w_ukv.reshape(L, MLA_KV_RANK, MLA_HEADS, MLA_NOPE + MLA_V)
    wk_p = _pad_last(wukv[..., :MLA_NOPE], HEAD_SLOT).reshape(L, MLA_KV_RANK, MLA_HEADS * HEAD_SLOT).astype(bf16)
    wv_p = wukv[..., MLA_NOPE:].reshape(L, MLA_KV_RANK, MLA_WIDTH).astype(bf16)
    cache_kr_p = jnp.pad(cache_krope, [(0, 0)] * 3 + [(ROPE_LANE0, LANES - ROPE_LANE0 - MLA_ROPE)])
    rope_tabs = _rope_tables(dseq)
    conv_w = jnp.swapaxes(conv_w.reshape(L, CONV_W, -1, LANES), 1, 2)
    ws_b = w_s.astype(bf16)
    bs_full = jnp.repeat(jnp.swapaxes(b_s, 1, 2), CM_WIDTH // CM_GROUPS, axis=2)
    gates8 = [_pad_last(v.reshape(L, 2 * GDN_HEADS), LANES) for v in (a_log, dt_bias)]
    pack = jnp.stack([norm_g,
                      jnp.concatenate([q_a_norm, kv_a_norm, _pad_last(q_norm, HEAD_SLOT),
                                       _pad_last(k_norm, HEAD_SLOT), gdn_onorm], axis=-1),
                      jnp.concatenate([cm_ln_g, cm_ln_b], axis=-1),
                      _pad_last(jnp.concatenate(gates8, axis=-1), D_MODEL)], axis=1)
    attn_w = (pack, wuq_p, wk_p, wv_p)

    c8 = jnp.concatenate([c, c_ctx[None, :], jnp.zeros((8 - dbatch - 1, D_MODEL), f32)], axis=0)
    wbr_b = w_branch.astype(bf16)
    wo_b = w_o.astype(bf16)
    mod = _modulation(c8, w_mod, b_mod)

    yp = x_prompt.reshape(batch * seq, D_MODEL)
    ys = x_sample.reshape(dbatch * dseq, D_MODEL)
    caches, states = (), None
    for l in range(L):
        mod_l = mod
        p16, p32 = _inproj(yp, ys, mod_l, pack, w_t, l, dseq)
        lat0 = batch * seq
        (o_a, new_ckv, new_kr), (o_b, states) = _run_parts(
            [_attn_ctx(p32, caches, attn_w, l, batch, seq), _gdn(p32, 0, conv_w, pack, None, states, l, batch, seq)],
            (batch,), "mix_ctx")
        caches = (new_ckv, new_kr)
        yp = _merge(yp, mod_l, p16, 0, o_a, o_b, pack, ws_b, bs_full, wbr_b, wo_b, l, False, seq)
        o_a = _attn_lat(p32, lat0, cache_ckv, cache_kr_p, rope_tabs, attn_w, l, dbatch, dseq, past)
        ((o_b,),) = _run_parts([_gdn(p32, lat0, conv_w, pack, state_gdn, None, l, dbatch, dseq)], (dbatch,), "gdn_lat")
        ys = _merge(ys, mod_l, p16, lat0, o_a, o_b, pack, ws_b, bs_full, wbr_b, wo_b, l, True, dseq)
    return (yp.reshape(batch, seq, D_MODEL), ys.reshape(dbatch, dseq, D_MODEL), caches[0], caches[1], states)
```

```python
import collections
import functools
import math

import numpy as np

import jax
import jax.numpy as jnp
from jax import lax
from jax.experimental import pallas as pl
from jax.experimental.pallas import tpu as pltpu

D_MODEL = 1024
DEPTH = 2
GRID_W = 64
EPS = 1e-6
MLA_HEADS = 8
MLA_NOPE = 64
MLA_ROPE = 32
MLA_QK = MLA_NOPE + MLA_ROPE
MLA_V = 64
MLA_Q_RANK = 384
MLA_KV_RANK = 256
MLA_WIDTH = MLA_HEADS * MLA_V
ROPE_THETA = 10000.0
GDN_HEADS = 4
GDN_DK = 128
GDN_DV = 128
GDN_KW = GDN_HEADS * GDN_DK
GDN_VW = GDN_HEADS * GDN_DV
GDN_CHUNK = 64
CONV_W = 5
GDN_SOLVE_CHUNKS = 8
GDN_LEVEL_CHAINS = 16
CM_GROUPS = 4
CM_CHUNK = 128
CM_WIDTH = 512
N_BRANCH = 3
BRANCH_W = 512
SPLIT_SIZES = (MLA_Q_RANK, MLA_KV_RANK, MLA_ROPE, MLA_WIDTH,
               2 * GDN_KW + GDN_VW, 2 * GDN_HEADS, 2 * GDN_HEADS, GDN_VW,
               CM_WIDTH, CM_WIDTH, CM_WIDTH, N_BRANCH * D_MODEL)

LANES = 128
SUBLANES = 8
HEAD_SLOT = LANES
ROPE_LANE0 = MLA_NOPE

OFF_GL = 0
OFF_ZA = OFF_GL + N_BRANCH * D_MODEL
OFF_ZB = OFF_ZA + 512
OFF_CU = OFF_ZB + 512
OFF_CV = OFF_CU + 512
OFF_ZC = OFF_CV + 512
P16_W = OFF_ZC + 512
OFF_QKV = 0
OFF_SMALL = OFF_QKV + 2 * GDN_KW + GDN_VW
OFF_CQ = OFF_SMALL + LANES
OFF_CKV = OFF_CQ + MLA_Q_RANK
P32_W = OFF_CKV + MLA_KV_RANK
PROJ_W = P16_W + P32_W
GDN_IN_W = OFF_CQ
MLA_IN_W = P32_W - OFF_SMALL
MLA_SMALL = slice(0, LANES)
MLA_CQ = slice(OFF_CQ - OFF_SMALL, OFF_CKV - OFF_SMALL)
MLA_CKV = slice(OFF_CKV - OFF_SMALL, MLA_IN_W)

VMEM_LIMIT = 56 * 1024 * 1024

PK_ROWS = 4
PK = {}
for _row, _fields in enumerate((
        (("norm_g", D_MODEL),),
        (("q_a_norm", MLA_Q_RANK), ("kv_a_norm", MLA_KV_RANK), ("q_norm", HEAD_SLOT), ("k_norm", HEAD_SLOT),
         ("gdn_onorm", GDN_DV)),
        (("cm_ln_g", CM_WIDTH), ("cm_ln_b", CM_WIDTH)),
        (("a_log", LANES), ("dt_bias", LANES)))):
    _lane = 0
    for _name, _width in _fields:
        PK[_name] = (_row, _lane, _width)
        _lane += _width
    assert _lane <= D_MODEL


def _pk(pk_ref, name):
    row, lane0, width = PK[name]
    return pk_ref[row:row + 1, lane0:lane0 + width]


def _pk_spec(l):
    return pl.BlockSpec((None, PK_ROWS, D_MODEL), lambda *g: (l, 0, 0))

f32 = jnp.float32
bf16 = jnp.bfloat16


def _cparams(n_axes):
    return pltpu.CompilerParams(dimension_semantics=("arbitrary",) * n_axes,
                                vmem_limit_bytes=VMEM_LIMIT)


_Part = collections.namedtuple("_Part", "kernel in_specs args out_specs out_shape scratch_shapes")


def _run_parts(parts, grid, name):
    n_in = [len(p.in_specs) for p in parts]
    n_out = [len(p.out_specs) for p in parts]
    n_scr = [len(p.scratch_shapes) for p in parts]

    def body(*refs):
        ins, outs, scr = refs[:sum(n_in)], refs[sum(n_in):sum(n_in) + sum(n_out)], refs[sum(n_in) + sum(n_out):]
        i = o = s = 0
        for p, ni, no, ns in zip(parts, n_in, n_out, n_scr):
            p.kernel(*ins[i:i + ni], *outs[o:o + no], *scr[s:s + ns])
            i, o, s = i + ni, o + no, s + ns

    res = pl.pallas_call(
        body,
        grid=grid,
        in_specs=[sp for p in parts for sp in p.in_specs],
        out_specs=[sp for p in parts for sp in p.out_specs],
        out_shape=[sh for p in parts for sh in p.out_shape],
        scratch_shapes=[sc for p in parts for sc in p.scratch_shapes],
        compiler_params=_cparams(len(grid)),
        name=name,
    )(*[a for p in parts for a in p.args])
    out, o = [], 0
    for no in n_out:
        out.append(res[o:o + no])
        o += no
    return out


def _dot(a, b):
    return jnp.dot(a, b, preferred_element_type=f32)


def _dot_nt(a, b):
    return lax.dot_general(a, b, (((1,), (1,)), ((), ())), preferred_element_type=f32)


def _rms_rows(x, g, n=None):
    n = x.shape[-1] if n is None else n
    ms = jnp.sum(x * x, axis=-1, keepdims=True) * (1.0 / n)
    return x * lax.rsqrt(ms + EPS) * g


def _mod_kernel(c_ref, w_ref, b_ref, o_ref):
    a = c_ref[...]
    a = (a * jax.nn.sigmoid(a)).astype(bf16)
    y = _dot(a, w_ref[0].astype(bf16)) + b_ref[0]
    for r in range(y.shape[0]):
        o_ref[0, r] = y[r:r + 1]


def _modulation(c8, w_mod, b_mod):
    tn = 1536
    return pl.pallas_call(
        _mod_kernel,
        grid=(DEPTH, 3 * D_MODEL // tn),
        in_specs=[pl.BlockSpec((8, D_MODEL), lambda l, n: (0, 0)),
                  pl.BlockSpec((1, D_MODEL, tn), lambda l, n: (l, 0, n)),
                  pl.BlockSpec((1, 1, tn), lambda l, n: (l, 0, n))],
        out_specs=pl.BlockSpec((1, 8, 1, tn), lambda l, n: (l, 0, 0, n)),
        out_shape=jax.ShapeDtypeStruct((DEPTH, 8, 1, 3 * D_MODEL), f32),
        compiler_params=_cparams(2),
        name="modulation",
    )(c8, w_mod, b_mod.reshape(DEPTH, 1, 3 * D_MODEL))


W_CHUNK = 512


def _w_in_moves():
    offs = [0]
    for s in SPLIT_SIZES:
        offs.append(offs[-1] + s)
    cq, ckv, krope, z_a, qkv, ga, gb, z_b, cu, cv, z_c, gl = offs[:-1]
    moves = [(gl, N_BRANCH * D_MODEL, OFF_GL), (z_a, 512, OFF_ZA), (z_b, 512, OFF_ZB), (cu, 512, OFF_CU),
             (cv, 512, OFF_CV), (z_c, 512, OFF_ZC), (qkv, 2 * GDN_KW + GDN_VW, P16_W + OFF_QKV),
             (cq, MLA_Q_RANK, P16_W + OFF_CQ), (ckv, MLA_KV_RANK, P16_W + OFF_CKV)]
    return moves, ga, krope


def _w_chunks():
    moves, ga, krope = _w_in_moves()
    chunks = []
    for (a, w, d) in moves:
        for o in range(0, w, W_CHUNK):
            chunks.append((a + o, min(W_CHUNK, w - o), d + o))
    small0 = P16_W + OFF_SMALL
    chunks.append((ga, 4 * GDN_HEADS, small0))
    chunks.append((krope, MLA_ROPE, small0 + ROPE_LANE0))
    return chunks


def _inproj_kernel(xc_ref, xl_ref, mod_ref, pk_ref, wt_hbm, o16_ref, o32_ref, w_s, stage, sem, *, layer, nctx):
    t = pl.program_id(0)

    @pl.when(t == 0)
    def _():
        chunks = _w_chunks()

        def copy(j):
            src, n, _ = chunks[j]
            return pltpu.make_async_copy(wt_hbm.at[layer, pl.ds(src, n), :], stage.at[j % 2, pl.ds(0, n), :],
                                         sem.at[j % 2])

        copy(0).start()
        for j, (_, n, dst) in enumerate(chunks):
            if j + 1 < len(chunks):
                copy(j + 1).start()
            copy(j).wait()
            w_s[dst:dst + n, :] = stage[j % 2, 0:n, :].astype(bf16)
        small0 = P16_W + OFF_SMALL
        for lo, hi in ((4 * GDN_HEADS, ROPE_LANE0), (ROPE_LANE0 + MLA_ROPE, LANES)):
            w_s[small0 + lo:small0 + hi, :] = jnp.zeros((hi - lo, D_MODEL), bf16)

    x = jnp.where(t < nctx, xc_ref[...], xl_ref[...])
    m = mod_ref[0]
    shift = m[:, :D_MODEL]
    scale = m[:, D_MODEL:2 * D_MODEL]
    h = (_rms_rows(x, _pk(pk_ref, "norm_g")) * (1.0 + scale) + shift).astype(bf16)
    for a in range(0, P16_W, 512):
        o16_ref[:, a:a + 512] = _dot_nt(h, w_s[a:a + 512, :]).astype(bf16)
    for a in range(0, P32_W, 512):
        b = min(a + 512, P32_W)
        o32_ref[:, a:b] = _dot_nt(h, w_s[P16_W + a:P16_W + b, :])


def _inproj(xc, xl, mod_l, norm_g, w_t, l, dseq):
    tm = 512
    nctx = xc.shape[0] // tm
    nlat = xl.shape[0] // tm
    ntok = xc.shape[0] + xl.shape[0]
    kern = functools.partial(_inproj_kernel, layer=l, nctx=nctx)
    return pl.pallas_call(
        kern,
        grid=(nctx + nlat,),
        in_specs=[pl.BlockSpec((tm, D_MODEL), lambda t: (jnp.minimum(t, nctx - 1), 0)),
                  pl.BlockSpec((tm, D_MODEL), lambda t: (jnp.maximum(t - nctx, 0), 0)),
                  pl.BlockSpec((None, 1, 1, 3 * D_MODEL),
                               lambda t: (l, jnp.where(t < nctx, 4, ((t - nctx) * tm) // dseq), 0, 0)),
                  _pk_spec(l),
                  pl.BlockSpec(memory_space=pl.ANY)],
        out_specs=[pl.BlockSpec((tm, P16_W), lambda t: (t, 0)),
                   pl.BlockSpec((tm, P32_W), lambda t: (t, 0))],
        out_shape=[jax.ShapeDtypeStruct((ntok, P16_W), bf16),
                   jax.ShapeDtypeStruct((ntok, P32_W), f32)],
        scratch_shapes=[pltpu.VMEM((PROJ_W, D_MODEL), bf16),
                        pltpu.VMEM((2, W_CHUNK, D_MODEL), f32),
                        pltpu.SemaphoreType.DMA((2,))],
        compiler_params=_cparams(1),
        name="inproj",
    )(xc, xl, mod_l, norm_g, w_t)


def _mod_row(l, latent, tm, seq):
    if latent:
        return lambda t: (l, (t * tm) // seq, 0, 0)
    return lambda t: (l, 4, 0, 0)


def _rope(x, cos_t, sin_lo, sin_hi):
    return x * cos_t + pltpu.roll(x, LANES - 8, 1) * sin_lo + pltpu.roll(x, 8, 1) * sin_hi


def _build_kv(ckvn_b, kr, wk_ref, wv_ref, knorm, rope, k_s, v_s, r0):
    n = ckvn_b.shape[0]
    kfull = _dot(ckvn_b, wk_ref[...])
    v_s[r0:r0 + n, :] = _dot(ckvn_b, wv_ref[...]).astype(bf16)
    krg = kr * knorm
    if rope is not None:
        krg = _rope(krg, *rope)
    kr_ss = jnp.sum(kr * kr, axis=-1, keepdims=True)
    for h in range(MLA_HEADS):
        sl = slice(h * HEAD_SLOT, (h + 1) * HEAD_SLOT)
        kn = kfull[:, sl]
        ms = (jnp.sum(kn * kn, axis=-1, keepdims=True) + kr_ss) * (1.0 / MLA_QK)
        k_s[r0:r0 + n, sl] = ((kn * knorm + krg) * lax.rsqrt(ms + EPS)).astype(bf16)


def _rope_lane_mask(shape):
    lane = lax.broadcasted_iota(jnp.int32, shape, 1)
    return (lane >= ROPE_LANE0) & (lane < ROPE_LANE0 + MLA_ROPE)


def _attend_block(qa_b, wuq_ref, qnorm, rope, k_s, v_s, o_ref):
    tq = qa_b.shape[0]
    qfull = _dot(qa_b, wuq_ref[...])
    qgain = qnorm * (math.log2(math.e) / math.sqrt(MLA_QK))
    lane = lax.broadcasted_iota(jnp.int32, (tq, LANES), 1)

    def scores(h):
        sl = slice(h * HEAD_SLOT, (h + 1) * HEAD_SLOT)
        qh = _rms_rows(qfull[:, sl], qgain, n=MLA_QK)
        if rope is not None:
            qh = _rope(qh, *rope)
        return _dot_nt(qh.astype(bf16), k_s[:, sl])

    s_next = scores(0)
    outs = []
    for h in range(MLA_HEADS):
        s = s_next
        if h + 1 < MLA_HEADS:
            s_next = scores(h + 1)
        p = jnp.exp2(s - jnp.max(s, axis=-1, keepdims=True))
        den = jnp.sum(p, axis=-1, keepdims=True)
        hp = h // 2
        outs.append(_dot(p.astype(bf16), v_s[:, hp * LANES:(hp + 1) * LANES]) / den)
        if h % 2 == 1:
            o_ref[:, hp * LANES:(hp + 1) * LANES] = jnp.where(lane < MLA_V, outs[h - 1], outs[h])


def _attn_ctx_kernel(*refs, layer):
    if layer:
        mla_ref, pckvn_ref, pkr_ref = refs[:3]
        refs = refs[3:]
    else:
        mla_ref = refs[0]
        refs = refs[1:]
    pk_ref, wuq_ref, wk_ref, wv_ref, o_ref, ckvn_ref, kr_ref, k_s, v_s = refs
    for i in range(layer):
        ckvn_ref[i] = pckvn_ref[i]
        kr_ref[i] = pkr_ref[i]
    small = mla_ref[:, MLA_SMALL]
    kr = jnp.where(_rope_lane_mask(small.shape), small, 0.0)
    kr_ref[layer] = small[:, ROPE_LANE0:ROPE_LANE0 + MLA_ROPE]
    ckvn = _rms_rows(mla_ref[:, MLA_CKV], _pk(pk_ref, "kv_a_norm"))
    ckvn_ref[layer] = ckvn
    _build_kv(ckvn.astype(bf16), kr, wk_ref, wv_ref, _pk(pk_ref, "k_norm"), None, k_s, v_s, 0)
    qa = _rms_rows(mla_ref[:, MLA_CQ], _pk(pk_ref, "q_a_norm")).astype(bf16)
    _attend_block(qa, wuq_ref, _pk(pk_ref, "q_norm"), None, k_s, v_s, o_ref)


def _attn_lat_kernel(mla_ref, cckv_ref, ckr_ref, cos_ref, slo_ref, shi_ref,
                     pk_ref, wuq_ref, wk_ref, wv_ref,
                     o_ref, k_s, v_s, *, seq, past, tq):
    qi = pl.program_id(1)
    rb = 256

    @pl.when(qi == 0)
    def _():
        _build_kv(cckv_ref[...].astype(bf16), ckr_ref[...], wk_ref, wv_ref, _pk(pk_ref, "k_norm"), None,
                  k_s, v_s, 0)
        for r in range(seq // rb):
            rs = slice(r * rb, (r + 1) * rb)
            small = mla_ref[rs, MLA_SMALL]
            kr = jnp.where(_rope_lane_mask(small.shape), small, 0.0)
            ckvn = _rms_rows(mla_ref[rs, MLA_CKV], _pk(pk_ref, "kv_a_norm"))
            rope = (cos_ref[rs, :], slo_ref[rs, :], shi_ref[rs, :])
            _build_kv(ckvn.astype(bf16), kr, wk_ref, wv_ref, _pk(pk_ref, "k_norm"), rope, k_s, v_s, past + r * rb)

    rows = pl.ds(pl.multiple_of(qi * tq, tq), tq)
    qa = _rms_rows(mla_ref[rows, MLA_CQ], _pk(pk_ref, "q_a_norm")).astype(bf16)
    rope = (cos_ref[rows, :], slo_ref[rows, :], shi_ref[rows, :])
    _attend_block(qa, wuq_ref, _pk(pk_ref, "q_norm"), rope, k_s, v_s, o_ref)


def _attn_weight_specs(l, nidx):
    z = (0,) * (nidx - 1)

    def const(*idx):
        return lambda *g: idx

    return [_pk_spec(l),
            pl.BlockSpec((None, MLA_Q_RANK, MLA_HEADS * HEAD_SLOT), const(l, 0, 0)),
            pl.BlockSpec((None, MLA_KV_RANK, MLA_HEADS * HEAD_SLOT), const(l, 0, 0)),
            pl.BlockSpec((None, MLA_KV_RANK, MLA_WIDTH), const(l, 0, 0))]


def _attn_ctx(proj, prev, wts, l, batch, seq):
    ntok = batch * seq

    def stacked(n, width):
        return pl.BlockSpec((None, n, seq, width), lambda b: (b, 0, 0, 0))

    prev_specs = [stacked(l, MLA_KV_RANK), stacked(l, MLA_ROPE)] if l else []
    return _Part(
        kernel=functools.partial(_attn_ctx_kernel, layer=l),
        in_specs=[pl.BlockSpec((seq, MLA_IN_W), lambda b: (b, OFF_SMALL // MLA_IN_W))]
        + prev_specs + _attn_weight_specs(l, 1),
        args=[proj, *prev, *wts],
        out_specs=[pl.BlockSpec((seq, MLA_WIDTH), lambda b: (b, 0)),
                   stacked(l + 1, MLA_KV_RANK), stacked(l + 1, MLA_ROPE)],
        out_shape=[jax.ShapeDtypeStruct((ntok, MLA_WIDTH), f32),
                   jax.ShapeDtypeStruct((batch, l + 1, seq, MLA_KV_RANK), f32),
                   jax.ShapeDtypeStruct((batch, l + 1, seq, MLA_ROPE), f32)],
        scratch_shapes=[pltpu.VMEM((seq, MLA_HEADS * HEAD_SLOT), bf16),
                        pltpu.VMEM((seq, MLA_WIDTH), bf16)])


def _attn_lat(proj, row0, cache_ckv, cache_kr_p, rope_tabs, wts, l, batch, seq, past):
    ntok = batch * seq
    tq = 256
    nq = seq // tq
    kern = functools.partial(_attn_lat_kernel, seq=seq, past=past, tq=tq)
    tab = pl.BlockSpec((seq, LANES), lambda b, q: (0, 0))
    return pl.pallas_call(
        kern,
        grid=(batch, nq),
        in_specs=[pl.BlockSpec((seq, MLA_IN_W), lambda b, q: (row0 // seq + b, OFF_SMALL // MLA_IN_W)),
                  pl.BlockSpec((None, None, past, MLA_KV_RANK), lambda b, q: (b, l, 0, 0)),
                  pl.BlockSpec((None, None, past, LANES), lambda b, q: (b, l, 0, 0)),
                  tab, tab, tab]
        + _attn_weight_specs(l, 2),
        out_specs=pl.BlockSpec((tq, MLA_WIDTH), lambda b, q: (b * nq + q, 0)),
        out_shape=jax.ShapeDtypeStruct((ntok, MLA_WIDTH), f32),
        scratch_shapes=[pltpu.VMEM((past + seq, MLA_HEADS * HEAD_SLOT), bf16),
                        pltpu.VMEM((past + seq, MLA_WIDTH), bf16)],
        compiler_params=_cparams(2),
        name="attn_lat",
    )(proj, cache_ckv, cache_kr_p, *rope_tabs, *wts)


def _split3(x):
    hi = x.astype(bf16)
    r1 = x - hi.astype(f32)
    mid = r1.astype(bf16)
    lo = (r1 - mid.astype(f32)).astype(bf16)
    return hi, mid, lo


def _tri_cumsum(tri_b, x):
    hi, mid, lo = _split3(x)
    return _dot(tri_b, hi) + _dot(tri_b, mid) + _dot(tri_b, lo)


def _lane_bcast(x, c):
    return jnp.broadcast_to(x[:, c:c + 1], (x.shape[0], LANES))


def _gdn_kernel(*refs, seq, has_state, nprev):
    gin_ref, cw_ref, pk_ref = refs[:3]
    refs = refs[3:]
    s0_ref = sprev_ref = sout_ref = None
    if has_state:
        s0_ref, o_ref = refs[:2]
        refs = refs[2:]
    else:
        if nprev:
            sprev_ref = refs[0]
            refs = refs[1:]
        o_ref, sout_ref = refs[:2]
        refs = refs[2:]
    xpad, qkv_s, g_s, b_s, st_s, wq_s, ak_s, u_s, el_s, rhs_s = refs
    C = GDN_CHUNK
    nchunk = seq // C
    H = GDN_HEADS
    width = 2 * GDN_KW + GDN_VW
    halo = 8

    for j in range(width // LANES):
        xpad[j, 0:halo, :] = jnp.zeros((halo, LANES), f32)
        xpad[j, halo + seq:, :] = jnp.zeros((halo, LANES), f32)
        xpad[j, halo:halo + seq, :] = gin_ref[:, j * LANES:(j + 1) * LANES]
    if has_state:
        for d in range(2):
            for h in range(H):
                st_s[d * H + h] = s0_ref[d, h]
    else:
        st_s[...] = jnp.zeros((2 * H, GDN_DK, GDN_DV), f32)

    neg_a = -jnp.exp(_pk(pk_ref, "a_log"))
    dtb = _pk(pk_ref, "dt_bias")

    def conv_tile(j, l2norm):
        w = cw_ref[j]
        post = jnp.where(j < H, GDN_DK ** -0.5, 1.0)
        for c in range(nchunk):
            base = halo - CONV_W // 2 + c * C
            y = xpad[j, base:base + C, :] * w[0:1]
            for tap in range(1, CONV_W):
                y = y + xpad[j, base + tap:base + tap + C, :] * w[tap:tap + 1]
            y = y * jax.nn.sigmoid(y)
            if l2norm:
                y = y * (lax.rsqrt(jnp.sum(y * y, axis=-1, keepdims=True) + EPS) * post)
            qkv_s[j, c * C:(c + 1) * C, :] = y

    def conv_qk(j, carry):
        conv_tile(j, True)
        return carry

    def conv_v(j, carry):
        conv_tile(j, False)
        return carry

    tile_unroll = 4 if nchunk <= 4 else 1
    lax.fori_loop(0, 2 * H, conv_qk, 0, unroll=tile_unroll)
    lax.fori_loop(2 * H, 3 * H, conv_v, 0, unroll=tile_unroll)

    def prep(c, carry):
        r0 = pl.multiple_of(c * C, C)
        sm = gin_ref[pl.ds(r0, C), OFF_SMALL:OFF_SMALL + LANES]
        z = sm + dtb
        g_s[pl.ds(r0, C), :] = neg_a * (jnp.maximum(z, 0.0) + jnp.log1p(jnp.exp(-jnp.abs(z))))
        b_s[pl.ds(r0, C), :] = pltpu.roll(jax.nn.sigmoid(sm), LANES - 2 * H, 1)
        return carry

    lax.fori_loop(0, nchunk, prep, 0, unroll=4)

    ri = lax.broadcasted_iota(jnp.int32, (C, LANES), 0)
    cl = lax.broadcasted_iota(jnp.int32, (C, LANES), 1)
    fwd = cl < C
    cj = cl & (C - 1)
    eye2 = (ri == cj).astype(f32)
    incl2 = (fwd & (ri >= cj)) | (~fwd & (ri <= cj))
    strict2 = (fwd & (ri > cj)) | (~fwd & (ri < cj))
    xor = ri ^ cj
    level2 = sum((xor >= (1 << b)).astype(jnp.int32) for b in range(C.bit_length() - 1))
    lvl_top = jnp.where(fwd, level2, 0)
    lvl_bot = jnp.where(fwd, 0, level2)
    r2 = lax.broadcasted_iota(jnp.int32, (2 * C, C), 0)
    c2 = lax.broadcasted_iota(jnp.int32, (2 * C, C), 1)
    tri2 = (((r2 < C) & (r2 >= c2)) | ((r2 >= C) & (r2 - C <= c2))).astype(bf16)
    zrhs = jnp.zeros((C, 2 * LANES), bf16)
    zvn = jnp.zeros((C, LANES), bf16)
    fwd_row = fwd[0:1, :]
    cpi = min(GDN_SOLVE_CHUNKS, nchunk)

    def block_diag(x):
        return jnp.concatenate([jnp.where(fwd, x, 0.0), jnp.where(fwd, 0.0, x)], axis=0).astype(bf16)

    def solve_phase(i, carry):
        chains = []
        for cc in range(cpi):
            c = i * cpi + cc
            rows = pl.ds(pl.multiple_of(c * C, C), C)
            g2 = _tri_cumsum(tri2, g_s[rows, :])
            g2t = g2.T
            bt = b_s[rows, :]
            for h in range(H):
                chains.append((cc, c, h, rows, g2, g2t, bt))

        a2s, t2s = [], []
        for (cc, c, h, rows, g2, g2t, bt) in chains:
            q = qkv_s[h, rows, :]
            k = qkv_s[H + h, rows, :]
            v = qkv_s[2 * H + h, rows, :]
            kq = _dot_nt(jnp.concatenate([k, q], axis=0).astype(bf16),
                         jnp.concatenate([k, k], axis=0).astype(bf16))
            gcc_f = _lane_bcast(g2[:C], h)
            gcc_b = _lane_bcast(g2[C:], H + h)
            btc_f = _lane_bcast(bt, h)
            btc_b = _lane_bcast(bt, H + h)
            grow = jnp.where(fwd_row, g2t[h:h + 1, :], g2t[H + h:H + h + 1, :])
            diff = jnp.where(fwd, gcc_f, gcc_b) - grow
            dec = jnp.where(incl2, jnp.exp(jnp.where(incl2, diff, 0.0)), 0.0)
            a2 = jnp.where(strict2, jnp.where(fwd, btc_f, btc_b) * kq[:C] * dec, 0.0)
            a2s.append(a2)
            t2s.append(eye2 - jnp.where(level2 == 1, a2, 0.0))
            glast_f = gcc_f[C - 1:C, :]
            glast_b = gcc_b[0:1, :]
            e1_f = jnp.exp(gcc_f)
            e1_b = jnp.exp(gcc_b)
            ket = jnp.concatenate([k * jnp.exp(glast_f - gcc_f), k * jnp.exp(glast_b - gcc_b)], axis=0).T
            ak_s[pl.ds(pl.multiple_of((c * H + h) * 3 * C, 3 * C), 3 * C), :] = jnp.concatenate(
                [kq[C:] * dec, ket], axis=0).astype(bf16)
            for d, (btc, e1, glast) in enumerate(((btc_f, e1_f, glast_f), (btc_b, e1_b, glast_b))):
                ch = d * H + h
                j = (cc * H + h) * 2 + d
                rhs_s[j * C:(j + 1) * C, :] = jnp.concatenate([v * btc, k * (btc * e1)], axis=1).astype(bf16)
                wq_s[pl.ds(pl.multiple_of((c * 2 * H + ch) * 2 * C + C, C), C), :] = (q * e1).astype(bf16)
                el_s[pl.ds(pl.multiple_of((c * 2 * H + ch) * 8, 8), 8), :] = jnp.broadcast_to(
                    jnp.exp(glast), (8, LANES))

        for g0 in range(0, len(chains), GDN_LEVEL_CHAINS):
            grp = slice(g0, g0 + GDN_LEVEL_CHAINS)
            ga2, gt2 = a2s[grp], t2s[grp]
            for lv in range(2, C.bit_length()):
                rs = []
                for a2, t2 in zip(ga2, gt2):
                    abd = jnp.concatenate([jnp.where(lvl_top == lv, a2, 0.0),
                                           jnp.where(lvl_bot == lv, a2, 0.0)], axis=0).astype(bf16)
                    rs.append(_dot(t2.astype(bf16), abd))
                gt2 = [t2 - _dot(r.astype(bf16), block_diag(t2)) for r, t2 in zip(rs, gt2)]

            for (cc, c, h, rows, g2, g2t, bt), t2 in zip(chains[grp], gt2):
                t2b = t2.astype(bf16)
                for d in range(2):
                    ch = d * H + h
                    j = (cc * H + h) * 2 + d
                    rhs = rhs_s[j * C:(j + 1) * C, :]
                    rhs = jnp.concatenate([rhs, zrhs] if d == 0 else [zrhs, rhs], axis=0)
                    uw = _dot(t2b, rhs)
                    u_s[pl.ds(pl.multiple_of((c * 2 * H + ch) * C, C), C), :] = uw[:, :LANES]
                    wq_s[pl.ds(pl.multiple_of((c * 2 * H + ch) * 2 * C, C), C), :] = uw[:, LANES:].astype(bf16)
        return carry

    lax.fori_loop(0, nchunk // cpi, solve_phase, 0)

    def scan_phase(i, carry):
        cs = [i if ch < H else nchunk - 1 - i for ch in range(2 * H)]
        s_old = [st_s[ch] for ch in range(2 * H)]
        r1 = [_dot(wq_s[pl.ds(pl.multiple_of((cs[ch] * 2 * H + ch) * 2 * C, 2 * C), 2 * C), :],
                   s_old[ch].astype(bf16)) for ch in range(2 * H)]
        r2s = []
        for ch in range(2 * H):
            u = u_s[pl.ds(pl.multiple_of((cs[ch] * 2 * H + ch) * C, C), C), :]
            vnb = (u - r1[ch][:C]).astype(bf16)
            rhs = jnp.concatenate([vnb, zvn] if ch < H else [zvn, vnb], axis=0)
            ak = ak_s[pl.ds(pl.multiple_of((cs[ch] * H + ch % H) * 3 * C, 3 * C), 3 * C), :]
            r2s.append(_dot(ak, rhs))
        for ch in range(2 * H):
            el = el_s[pl.ds(pl.multiple_of((cs[ch] * 2 * H + ch) * 8, 8), 8), :][0:1, :]
            st_s[ch] = s_old[ch] * el + r2s[ch][C:]
            xpad[ch, pl.ds(pl.multiple_of(cs[ch] * C, C), C), :] = r1[ch][C:] + r2s[ch][:C]
        return carry

    lax.fori_loop(0, nchunk, scan_phase, 0, unroll=4)

    onorm = _pk(pk_ref, "gdn_onorm")

    def fin(c, carry):
        rows = pl.ds(pl.multiple_of(c * C, C), C)
        for h in range(H):
            ls = slice(h * LANES, (h + 1) * LANES)
            o_ref[rows, ls] = _rms_rows(xpad[h, rows, :] + xpad[H + h, rows, :], onorm)
        return carry

    lax.fori_loop(0, nchunk, fin, 0, unroll=4)
    if sout_ref is not None:
        for i in range(nprev):
            sout_ref[i] = sprev_ref[i]
        for d in range(2):
            for h in range(H):
                sout_ref[nprev, d, h] = st_s[d * H + h]


def _gdn(proj, row0, conv_w, pack, state, prev_states, l, batch, seq):
    ntok = batch * seq
    width = 2 * GDN_KW + GDN_VW
    has_state = state is not None
    nprev = 0 if has_state else l
    nchunk = seq // GDN_CHUNK
    kern = functools.partial(_gdn_kernel, seq=seq, has_state=has_state, nprev=nprev)
    in_specs = [pl.BlockSpec((seq, GDN_IN_W), lambda b: (row0 // seq + b, OFF_QKV // GDN_IN_W)),
                pl.BlockSpec((None, width // LANES, CONV_W, LANES), lambda b: (l, 0, 0, 0)),
                _pk_spec(l)]
    args = [proj, conv_w, pack]
    o_spec = pl.BlockSpec((seq, GDN_VW), lambda b: (b, 0))
    o_shape = jax.ShapeDtypeStruct((ntok, GDN_VW), f32)
    st_block = (None, None, 2, GDN_HEADS, GDN_DK, GDN_DV)
    if has_state:
        in_specs.append(pl.BlockSpec(st_block, lambda b: (b, l, 0, 0, 0, 0)))
        args.append(state)
        out_specs, out_shape = [o_spec], [o_shape]
    else:
        def stacked(n):
            return pl.BlockSpec((None, n) + st_block[2:], lambda b: (b, 0, 0, 0, 0, 0))

        if nprev:
            in_specs.append(stacked(nprev))
            args.append(prev_states)
        out_specs = [o_spec, stacked(nprev + 1)]
        out_shape = [o_shape, jax.ShapeDtypeStruct((batch, nprev + 1, 2, GDN_HEADS, GDN_DK, GDN_DV), f32)]
    return _Part(
        kernel=kern,
        in_specs=in_specs,
        args=args,
        out_specs=out_specs,
        out_shape=out_shape,
        scratch_shapes=[pltpu.VMEM((width // LANES, seq + 16, LANES), f32),
                        pltpu.VMEM((width // LANES, seq, LANES), f32),
                        pltpu.VMEM((seq, LANES), f32),
                        pltpu.VMEM((seq, LANES), f32),
                        pltpu.VMEM((2 * GDN_HEADS, GDN_DK, GDN_DV), f32),
                        pltpu.VMEM((nchunk * 2 * GDN_HEADS * 2 * GDN_CHUNK, LANES), bf16),
                        pltpu.VMEM((nchunk * GDN_HEADS * 3 * GDN_CHUNK, LANES), bf16),
                        pltpu.VMEM((nchunk * 2 * GDN_HEADS * GDN_CHUNK, LANES), f32),
                        pltpu.VMEM((nchunk * 2 * GDN_HEADS * 8, LANES), f32),
                        pltpu.VMEM((min(GDN_SOLVE_CHUNKS, nchunk) * 2 * GDN_HEADS * GDN_CHUNK, 2 * LANES), bf16)])


def _merge_kernel(x_ref, mod_ref, oa_ref, ob_ref, p_ref, pk_ref, ws_ref, bs_ref, wbr_ref, wo_ref,
                  out_ref, sv_s):
    tm = x_ref.shape[0]

    def group(off, width=512):
        return p_ref[:, off:off + width].astype(f32)

    def gelu(x):
        c1 = math.sqrt(2.0 / math.pi)
        half = 0.5 * x
        return half + half * jnp.tanh(x * (c1 + (c1 * 0.044715) * (x * x)))

    u = gelu(group(OFF_CU))
    vf = gelu(group(OFF_CV))
    mu = jnp.mean(vf, axis=-1, keepdims=True)
    vc = vf - mu
    var = jnp.mean(vc * vc, axis=-1, keepdims=True)
    vn = (vc * lax.rsqrt(var + EPS) * _pk(pk_ref, "cm_ln_g") + _pk(pk_ref, "cm_ln_b")).astype(bf16)
    for ck in range(tm // CM_CHUNK):
        rs = slice(ck * CM_CHUNK, (ck + 1) * CM_CHUNK)
        for g in range(CM_GROUPS):
            ls = slice(g * LANES, (g + 1) * LANES)
            sv_s[rs, ls] = _dot(ws_ref[g], vn[rs, ls]) + bs_ref[:, ls]
    o_c = u * sv_s[...]

    def silu(z):
        return z * jax.nn.sigmoid(z)

    brs = (oa_ref[...] * silu(group(OFF_ZA)), ob_ref[...] * silu(group(OFF_ZB)), o_c * silu(group(OFF_ZC)))
    ysum = None
    for n in range(N_BRANCH):
        yb = _dot(brs[n].astype(bf16), wbr_ref[n])
        t = jax.nn.sigmoid(group(OFF_GL + n * D_MODEL, D_MODEL)) * yb
        ysum = t if ysum is None else ysum + t
    y = _dot(ysum.astype(bf16), wo_ref[...])
    gate = mod_ref[0][:, 2 * D_MODEL:]
    out_ref[...] = x_ref[...] + gate * y


def _merge(x, mod_l, proj, row0, o_a, o_b, pack, ws_b, bs_full, wbr_b, wo_b, l, latent, seq):
    ntok = x.shape[0]
    tm = 512

    def const(*idx):
        return lambda t: idx

    return pl.pallas_call(
        _merge_kernel,
        grid=(ntok // tm,),
        in_specs=[pl.BlockSpec((tm, D_MODEL), lambda t: (t, 0)),
                  pl.BlockSpec((None, 1, 1, 3 * D_MODEL), _mod_row(l, latent, tm, seq)),
                  pl.BlockSpec((tm, 512), lambda t: (t, 0)),
                  pl.BlockSpec((tm, 512), lambda t: (t, 0)),
                  pl.BlockSpec((tm, P16_W), lambda t: (row0 // tm + t, 0)),
                  _pk_spec(l),
                  pl.BlockSpec((None, CM_GROUPS, CM_CHUNK, CM_CHUNK), const(l, 0, 0, 0)),
                  pl.BlockSpec((None, CM_CHUNK, CM_WIDTH), const(l, 0, 0)),
                  pl.BlockSpec((None, N_BRANCH, BRANCH_W, D_MODEL), const(l, 0, 0, 0)),
                  pl.BlockSpec((None, D_MODEL, D_MODEL), const(l, 0, 0))],
        out_specs=pl.BlockSpec((tm, D_MODEL), lambda t: (t, 0)),
        out_shape=jax.ShapeDtypeStruct((ntok, D_MODEL), f32),
        scratch_shapes=[pltpu.VMEM((tm, CM_WIDTH), f32)],
        compiler_params=_cparams(1),
        name="merge",
    )(x, mod_l, o_a, o_b, proj, pack, ws_b, bs_full, wbr_b, wo_b)


def _pad_last(x, n):
    return jnp.pad(x, [(0, 0)] * (x.ndim - 1) + [(0, n - x.shape[-1])])


def _rope_tables(seq):
    t = np.arange(seq)
    row = (t // GRID_W).astype(np.float32)
    colp = (t % GRID_W).astype(np.float32)
    nf = MLA_ROPE // 4
    inv = (ROPE_THETA ** (-np.arange(nf, dtype=np.float32) / nf)).astype(np.float32)
    cos_t = np.ones((seq, LANES), np.float32)
    s_lo = np.zeros((seq, LANES), np.float32)
    s_hi = np.zeros((seq, LANES), np.float32)
    for i, pos in enumerate((row, colp)):
        ang = (pos[:, None] * inv[None, :]).astype(np.float32)
        cs, sn = np.cos(ang), np.sin(ang)
        lo = ROPE_LANE0 + 2 * nf * i
        cos_t[:, lo:lo + nf] = cs
        cos_t[:, lo + nf:lo + 2 * nf] = cs
        s_lo[:, lo:lo + nf] = -sn
        s_hi[:, lo + nf:lo + 2 * nf] = sn
    return tuple(jnp.asarray(p) for p in (cos_t, s_lo, s_hi))


def kernel(x_prompt, x_sample, cache_ckv, cache_krope, state_gdn, c, c_ctx, norm_g, w_mod, b_mod, w_in, q_a_norm, w_uq, kv_a_norm, w_ukv, q_norm, k_norm, conv_w, a_log, dt_bias, gdn_onorm, cm_ln_g, cm_ln_b, w_s, b_s, w_branch, w_o):
    L = DEPTH
    batch, seq, _ = x_prompt.shape
    dbatch, dseq, _ = x_sample.shape
    past = cache_ckv.shape[2]

    w_t = jnp.swapaxes(w_in, 1, 2)
    wuq_p = _pad_last(w_uq.reshape(L, MLA_Q_RANK, MLA_HEADS, MLA_QK), HEAD_SLOT)
    wuq_p = wuq_p.reshape(L, MLA_Q_RANK, MLA_HEADS * HEAD_SLOT).astype(bf16)
    wukv = w_ukv.reshape(L, MLA_KV_RANK, MLA_HEADS, MLA_NOPE + MLA_V)
    wk_p = _pad_last(wukv[..., :MLA_NOPE], HEAD_SLOT).reshape(L, MLA_KV_RANK, MLA_HEADS * HEAD_SLOT).astype(bf16)
    wv_p = wukv[..., MLA_NOPE:].reshape(L, MLA_KV_RANK, MLA_WIDTH).astype(bf16)
    cache_kr_p = jnp.pad(cache_krope, [(0, 0)] * 3 + [(ROPE_LANE0, LANES - ROPE_LANE0 - MLA_ROPE)])
    rope_tabs = _rope_tables(dseq)
    conv_w = jnp.swapaxes(conv_w.reshape(L, CONV_W, -1, LANES), 1, 2)
    ws_b = w_s.astype(bf16)
    bs_full = jnp.repeat(jnp.swapaxes(b_s, 1, 2), CM_WIDTH // CM_GROUPS, axis=2)
    gates8 = [_pad_last(v.reshape(L, 2 * GDN_HEADS), LANES) for v in (a_log, dt_bias)]
    pack = jnp.stack([norm_g,
                      jnp.concatenate([q_a_norm, kv_a_norm, _pad_last(q_norm, HEAD_SLOT),
                                       _pad_last(k_norm, HEAD_SLOT), gdn_onorm], axis=-1),
                      jnp.concatenate([cm_ln_g, cm_ln_b], axis=-1),
                      _pad_last(jnp.concatenate(gates8, axis=-1), D_MODEL)], axis=1)
    attn_w = (pack, wuq_p, wk_p, wv_p)

    c8 = jnp.concatenate([c, c_ctx[None, :], jnp.zeros((8 - dbatch - 1, D_MODEL), f32)], axis=0)
    wbr_b = w_branch.astype(bf16)
    wo_b = w_o.astype(bf16)
    mod = _modulation(c8, w_mod, b_mod)

    yp = x_prompt.reshape(batch * seq, D_MODEL)
    ys = x_sample.reshape(dbatch * dseq, D_MODEL)
    caches, states = (), None
    for l in range(L):
        mod_l = mod
        p16, p32 = _inproj(yp, ys, mod_l, pack, w_t, l, dseq)
        lat0 = batch * seq
        (o_a, new_ckv, new_kr), (o_b, states) = _run_parts(
            [_attn_ctx(p32, caches, attn_w, l, batch, seq), _gdn(p32, 0, conv_w, pack, None, states, l, batch, seq)],
            (batch,), "mix_ctx")
        caches = (new_ckv, new_kr)
        yp = _merge(yp, mod_l, p16, 0, o_a, o_b, pack, ws_b, bs_full, wbr_b, wo_b, l, False, seq)
        o_a = _attn_lat(p32, lat0, cache_ckv, cache_kr_p, rope_tabs, attn_w, l, dbatch, dseq, past)
        ((o_b,),) = _run_parts([_gdn(p32, lat0, conv_w, pack, state_gdn, None, l, dbatch, dseq)], (dbatch,), "gdn_lat")
        ys = _merge(ys, mod_l, p16, lat0, o_a, o_b, pack, ws_b, bs_full, wbr_b, wo_b, l, True, dseq)
    return (yp.reshape(batch, seq, D_MODEL), ys.reshape(dbatch, dseq, D_MODEL), caches[0], caches[1], states)
```

```python
import collections
import functools
import math

import numpy as np

import jax
import jax.numpy as jnp
from jax import lax
from jax.experimental import pallas as pl
from jax.experimental.pallas import tpu as pltpu

D_MODEL = 1024
DEPTH = 2
GRID_W = 64
EPS = 1e-6
MLA_HEADS = 8
MLA_NOPE = 64
MLA_ROPE = 32
MLA_QK = MLA_NOPE + MLA_ROPE
MLA_V = 64
MLA_Q_RANK = 384
MLA_KV_RANK = 256
MLA_WIDTH = MLA_HEADS * MLA_V
ROPE_THETA = 10000.0
GDN_HEADS = 4
GDN_DK = 128
GDN_DV = 128
GDN_KW = GDN_HEADS * GDN_DK
GDN_VW = GDN_HEADS * GDN_DV
GDN_CHUNK = 64
CONV_W = 5
GDN_SOLVE_CHUNKS = 16
GDN_LEVEL_CHAINS = 16
CM_GROUPS = 4
CM_CHUNK = 128
CM_WIDTH = 512
N_BRANCH = 3
BRANCH_W = 512
SPLIT_SIZES = (MLA_Q_RANK, MLA_KV_RANK, MLA_ROPE, MLA_WIDTH,
               2 * GDN_KW + GDN_VW, 2 * GDN_HEADS, 2 * GDN_HEADS, GDN_VW,
               CM_WIDTH, CM_WIDTH, CM_WIDTH, N_BRANCH * D_MODEL)

LANES = 128
SUBLANES = 8
HEAD_SLOT = LANES
ROPE_LANE0 = MLA_NOPE

OFF_GL = 0
OFF_ZA = OFF_GL + N_BRANCH * D_MODEL
OFF_ZB = OFF_ZA + 512
OFF_CU = OFF_ZB + 512
OFF_CV = OFF_CU + 512
OFF_ZC = OFF_CV + 512
P16_W = OFF_ZC + 512
OFF_QKV = 0
OFF_SMALL = OFF_QKV + 2 * GDN_KW + GDN_VW
OFF_CQ = OFF_SMALL + LANES
OFF_CKV = OFF_CQ + MLA_Q_RANK
P32_W = OFF_CKV + MLA_KV_RANK
PROJ_W = P16_W + P32_W
GDN_IN_W = OFF_CQ
MLA_IN_W = P32_W - OFF_SMALL
MLA_SMALL = slice(0, LANES)
MLA_CQ = slice(OFF_CQ - OFF_SMALL, OFF_CKV - OFF_SMALL)
MLA_CKV = slice(OFF_CKV - OFF_SMALL, MLA_IN_W)

VMEM_LIMIT = 56 * 1024 * 1024

PK_ROWS = 4
PK = {}
for _row, _fields in enumerate((
        (("norm_g", D_MODEL),),
        (("q_a_norm", MLA_Q_RANK), ("kv_a_norm", MLA_KV_RANK), ("q_norm", HEAD_SLOT), ("k_norm", HEAD_SLOT),
         ("gdn_onorm", GDN_DV)),
        (("cm_ln_g", CM_WIDTH), ("cm_ln_b", CM_WIDTH)),
        (("a_log", LANES), ("dt_bias", LANES)))):
    _lane = 0
    for _name, _width in _fields:
        PK[_name] = (_row, _lane, _width)
        _lane += _width
    assert _lane <= D_MODEL


def _pk(pk_ref, name):
    row, lane0, width = PK[name]
    return pk_ref[row:row + 1, lane0:lane0 + width]


def _pk_spec(l):
    return pl.BlockSpec((None, PK_ROWS, D_MODEL), lambda *g: (l, 0, 0))

f32 = jnp.float32
bf16 = jnp.bfloat16


def _cparams(n_axes):
    return pltpu.CompilerParams(dimension_semantics=("arbitrary",) * n_axes,
                                vmem_limit_bytes=VMEM_LIMIT)


_Part = collections.namedtuple("_Part", "kernel in_specs args out_specs out_shape scratch_shapes")


def _run_parts(parts, grid, name):
    n_in = [len(p.in_specs) for p in parts]
    n_out = [len(p.out_specs) for p in parts]
    n_scr = [len(p.scratch_shapes) for p in parts]

    def body(*refs):
        ins, outs, scr = refs[:sum(n_in)], refs[sum(n_in):sum(n_in) + sum(n_out)], refs[sum(n_in) + sum(n_out):]
        i = o = s = 0
        for p, ni, no, ns in zip(parts, n_in, n_out, n_scr):
            p.kernel(*ins[i:i + ni], *outs[o:o + no], *scr[s:s + ns])
            i, o, s = i + ni, o + no, s + ns

    res = pl.pallas_call(
        body,
        grid=grid,
        in_specs=[sp for p in parts for sp in p.in_specs],
        out_specs=[sp for p in parts for sp in p.out_specs],
        out_shape=[sh for p in parts for sh in p.out_shape],
        scratch_shapes=[sc for p in parts for sc in p.scratch_shapes],
        compiler_params=_cparams(len(grid)),
        name=name,
    )(*[a for p in parts for a in p.args])
    out, o = [], 0
    for no in n_out:
        out.append(res[o:o + no])
        o += no
    return out


def _dot(a, b):
    return jnp.dot(a, b, preferred_element_type=f32)


def _dot_nt(a, b):
    return lax.dot_general(a, b, (((1,), (1,)), ((), ())), preferred_element_type=f32)


def _rms_rows(x, g, n=None):
    n = x.shape[-1] if n is None else n
    ms = jnp.sum(x * x, axis=-1, keepdims=True) * (1.0 / n)
    return x * lax.rsqrt(ms + EPS) * g


def _mod_kernel(c_ref, w_ref, b_ref, o_ref):
    a = c_ref[...]
    a = (a * jax.nn.sigmoid(a)).astype(bf16)
    y = _dot(a, w_ref[0].astype(bf16)) + b_ref[0]
    for r in range(y.shape[0]):
        o_ref[0, r] = y[r:r + 1]


def _modulation(c8, w_mod, b_mod):
    tn = 1536
    return pl.pallas_call(
        _mod_kernel,
        grid=(DEPTH, 3 * D_MODEL // tn),
        in_specs=[pl.BlockSpec((8, D_MODEL), lambda l, n: (0, 0)),
                  pl.BlockSpec((1, D_MODEL, tn), lambda l, n: (l, 0, n)),
                  pl.BlockSpec((1, 1, tn), lambda l, n: (l, 0, n))],
        out_specs=pl.BlockSpec((1, 8, 1, tn), lambda l, n: (l, 0, 0, n)),
        out_shape=jax.ShapeDtypeStruct((DEPTH, 8, 1, 3 * D_MODEL), f32),
        compiler_params=_cparams(2),
        name="modulation",
    )(c8, w_mod, b_mod.reshape(DEPTH, 1, 3 * D_MODEL))


W_CHUNK = 512


def _w_in_moves():
    offs = [0]
    for s in SPLIT_SIZES:
        offs.append(offs[-1] + s)
    cq, ckv, krope, z_a, qkv, ga, gb, z_b, cu, cv, z_c, gl = offs[:-1]
    moves = [(gl, N_BRANCH * D_MODEL, OFF_GL), (z_a, 512, OFF_ZA), (z_b, 512, OFF_ZB), (cu, 512, OFF_CU),
             (cv, 512, OFF_CV), (z_c, 512, OFF_ZC), (qkv, 2 * GDN_KW + GDN_VW, P16_W + OFF_QKV),
             (cq, MLA_Q_RANK, P16_W + OFF_CQ), (ckv, MLA_KV_RANK, P16_W + OFF_CKV)]
    return moves, ga, krope


def _w_chunks():
    moves, ga, krope = _w_in_moves()
    chunks = []
    for (a, w, d) in moves:
        for o in range(0, w, W_CHUNK):
            chunks.append((a + o, min(W_CHUNK, w - o), d + o))
    small0 = P16_W + OFF_SMALL
    chunks.append((ga, 4 * GDN_HEADS, small0))
    chunks.append((krope, MLA_ROPE, small0 + ROPE_LANE0))
    return chunks


def _inproj_kernel(xc_ref, xl_ref, mod_ref, pk_ref, wt_hbm, o16_ref, o32_ref, w_s, stage, sem, *, layer, nctx):
    t = pl.program_id(0)

    @pl.when(t == 0)
    def _():
        chunks = _w_chunks()

        def copy(j):
            src, n, _ = chunks[j]
            return pltpu.make_async_copy(wt_hbm.at[layer, pl.ds(src, n), :], stage.at[j % 2, pl.ds(0, n), :],
                                         sem.at[j % 2])

        copy(0).start()
        for j, (_, n, dst) in enumerate(chunks):
            if j + 1 < len(chunks):
                copy(j + 1).start()
            copy(j).wait()
            w_s[dst:dst + n, :] = stage[j % 2, 0:n, :].astype(bf16)
        small0 = P16_W + OFF_SMALL
        for lo, hi in ((4 * GDN_HEADS, ROPE_LANE0), (ROPE_LANE0 + MLA_ROPE, LANES)):
            w_s[small0 + lo:small0 + hi, :] = jnp.zeros((hi - lo, D_MODEL), bf16)

    x = jnp.where(t < nctx, xc_ref[...], xl_ref[...])
    m = mod_ref[0]
    shift = m[:, :D_MODEL]
    scale = m[:, D_MODEL:2 * D_MODEL]
    h = (_rms_rows(x, _pk(pk_ref, "norm_g")) * (1.0 + scale) + shift).astype(bf16)
    for a in range(0, P16_W, 512):
        o16_ref[:, a:a + 512] = _dot_nt(h, w_s[a:a + 512, :]).astype(bf16)
    for a in range(0, P32_W, 512):
        b = min(a + 512, P32_W)
        o32_ref[:, a:b] = _dot_nt(h, w_s[P16_W + a:P16_W + b, :])


def _inproj(xc, xl, mod_l, norm_g, w_t, l, dseq):
    tm = 512
    nctx = xc.shape[0] // tm
    nlat = xl.shape[0] // tm
    ntok = xc.shape[0] + xl.shape[0]
    kern = functools.partial(_inproj_kernel, layer=l, nctx=nctx)
    return pl.pallas_call(
        kern,
        grid=(nctx + nlat,),
        in_specs=[pl.BlockSpec((tm, D_MODEL), lambda t: (jnp.minimum(t, nctx - 1), 0)),
                  pl.BlockSpec((tm, D_MODEL), lambda t: (jnp.maximum(t - nctx, 0), 0)),
                  pl.BlockSpec((None, 1, 1, 3 * D_MODEL),
                               lambda t: (l, jnp.where(t < nctx, 4, ((t - nctx) * tm) // dseq), 0, 0)),
                  _pk_spec(l),
                  pl.BlockSpec(memory_space=pl.ANY)],
        out_specs=[pl.BlockSpec((tm, P16_W), lambda t: (t, 0)),
                   pl.BlockSpec((tm, P32_W), lambda t: (t, 0))],
        out_shape=[jax.ShapeDtypeStruct((ntok, P16_W), bf16),
                   jax.ShapeDtypeStruct((ntok, P32_W), f32)],
        scratch_shapes=[pltpu.VMEM((PROJ_W, D_MODEL), bf16),
                        pltpu.VMEM((2, W_CHUNK, D_MODEL), f32),
                        pltpu.SemaphoreType.DMA((2,))],
        compiler_params=_cparams(1),
        name="inproj",
    )(xc, xl, mod_l, norm_g, w_t)


def _mod_row(l, latent, tm, seq):
    if latent:
        return lambda t: (l, (t * tm) // seq, 0, 0)
    return lambda t: (l, 4, 0, 0)


def _rope(x, cos_t, sin_lo, sin_hi):
    return x * cos_t + pltpu.roll(x, LANES - 8, 1) * sin_lo + pltpu.roll(x, 8, 1) * sin_hi


def _build_kv(ckvn_b, kr, wk_ref, wv_ref, knorm, rope, k_s, v_s, r0):
    n = ckvn_b.shape[0]
    kfull = _dot(ckvn_b, wk_ref[...])
    v_s[r0:r0 + n, :] = _dot(ckvn_b, wv_ref[...]).astype(bf16)
    krg = kr * knorm
    if rope is not None:
        krg = _rope(krg, *rope)
    kr_ss = jnp.sum(kr * kr, axis=-1, keepdims=True)
    for h in range(MLA_HEADS):
        sl = slice(h * HEAD_SLOT, (h + 1) * HEAD_SLOT)
        kn = kfull[:, sl]
        ms = (jnp.sum(kn * kn, axis=-1, keepdims=True) + kr_ss) * (1.0 / MLA_QK)
        k_s[r0:r0 + n, sl] = ((kn * knorm + krg) * lax.rsqrt(ms + EPS)).astype(bf16)


def _rope_lane_mask(shape):
    lane = lax.broadcasted_iota(jnp.int32, shape, 1)
    return (lane >= ROPE_LANE0) & (lane < ROPE_LANE0 + MLA_ROPE)


def _attend_block(qa_b, wuq_ref, qnorm, rope, k_s, v_s, o_ref):
    tq = qa_b.shape[0]
    qfull = _dot(qa_b, wuq_ref[...])
    qgain = qnorm * (math.log2(math.e) / math.sqrt(MLA_QK))
    lane = lax.broadcasted_iota(jnp.int32, (tq, LANES), 1)

    def scores(h):
        sl = slice(h * HEAD_SLOT, (h + 1) * HEAD_SLOT)
        qh = _rms_rows(qfull[:, sl], qgain, n=MLA_QK)
        if rope is not None:
            qh = _rope(qh, *rope)
        return _dot_nt(qh.astype(bf16), k_s[:, sl])

    s_next = scores(0)
    outs = []
    for h in range(MLA_HEADS):
        s = s_next
        if h + 1 < MLA_HEADS:
            s_next = scores(h + 1)
        p = jnp.exp2(s - jnp.max(s, axis=-1, keepdims=True))
        den = jnp.sum(p, axis=-1, keepdims=True)
        hp = h // 2
        outs.append(_dot(p.astype(bf16), v_s[:, hp * LANES:(hp + 1) * LANES]) / den)
        if h % 2 == 1:
            o_ref[:, hp * LANES:(hp + 1) * LANES] = jnp.where(lane < MLA_V, outs[h - 1], outs[h])


def _attn_ctx_kernel(*refs, layer):
    if layer:
        mla_ref, pckvn_ref, pkr_ref = refs[:3]
        refs = refs[3:]
    else:
        mla_ref = refs[0]
        refs = refs[1:]
    pk_ref, wuq_ref, wk_ref, wv_ref, o_ref, ckvn_ref, kr_ref, k_s, v_s = refs
    for i in range(layer):
        ckvn_ref[i] = pckvn_ref[i]
        kr_ref[i] = pkr_ref[i]
    small = mla_ref[:, MLA_SMALL]
    kr = jnp.where(_rope_lane_mask(small.shape), small, 0.0)
    kr_ref[layer] = small[:, ROPE_LANE0:ROPE_LANE0 + MLA_ROPE]
    ckvn = _rms_rows(mla_ref[:, MLA_CKV], _pk(pk_ref, "kv_a_norm"))
    ckvn_ref[layer] = ckvn
    _build_kv(ckvn.astype(bf16), kr, wk_ref, wv_ref, _pk(pk_ref, "k_norm"), None, k_s, v_s, 0)
    qa = _rms_rows(mla_ref[:, MLA_CQ], _pk(pk_ref, "q_a_norm")).astype(bf16)
    _attend_block(qa, wuq_ref, _pk(pk_ref, "q_norm"), None, k_s, v_s, o_ref)


def _attn_lat_kernel(mla_ref, cckv_ref, ckr_ref, cos_ref, slo_ref, shi_ref,
                     pk_ref, wuq_ref, wk_ref, wv_ref,
                     o_ref, k_s, v_s, *, seq, past, tq):
    qi = pl.program_id(1)
    rb = 256

    @pl.when(qi == 0)
    def _():
        _build_kv(cckv_ref[...].astype(bf16), ckr_ref[...], wk_ref, wv_ref, _pk(pk_ref, "k_norm"), None,
                  k_s, v_s, 0)
        for r in range(seq // rb):
            rs = slice(r * rb, (r + 1) * rb)
            small = mla_ref[rs, MLA_SMALL]
            kr = jnp.where(_rope_lane_mask(small.shape), small, 0.0)
            ckvn = _rms_rows(mla_ref[rs, MLA_CKV], _pk(pk_ref, "kv_a_norm"))
            rope = (cos_ref[rs, :], slo_ref[rs, :], shi_ref[rs, :])
            _build_kv(ckvn.astype(bf16), kr, wk_ref, wv_ref, _pk(pk_ref, "k_norm"), rope, k_s, v_s, past + r * rb)

    rows = pl.ds(pl.multiple_of(qi * tq, tq), tq)
    qa = _rms_rows(mla_ref[rows, MLA_CQ], _pk(pk_ref, "q_a_norm")).astype(bf16)
    rope = (cos_ref[rows, :], slo_ref[rows, :], shi_ref[rows, :])
    _attend_block(qa, wuq_ref, _pk(pk_ref, "q_norm"), rope, k_s, v_s, o_ref)


def _attn_weight_specs(l, nidx):
    z = (0,) * (nidx - 1)

    def const(*idx):
        return lambda *g: idx

    return [_pk_spec(l),
            pl.BlockSpec((None, MLA_Q_RANK, MLA_HEADS * HEAD_SLOT), const(l, 0, 0)),
            pl.BlockSpec((None, MLA_KV_RANK, MLA_HEADS * HEAD_SLOT), const(l, 0, 0)),
            pl.BlockSpec((None, MLA_KV_RANK, MLA_WIDTH), const(l, 0, 0))]


def _attn_ctx(proj, prev, wts, l, batch, seq):
    ntok = batch * seq

    def stacked(n, width):
        return pl.BlockSpec((None, n, seq, width), lambda b: (b, 0, 0, 0))

    prev_specs = [stacked(l, MLA_KV_RANK), stacked(l, MLA_ROPE)] if l else []
    return _Part(
        kernel=functools.partial(_attn_ctx_kernel, layer=l),
        in_specs=[pl.BlockSpec((seq, MLA_IN_W), lambda b: (b, OFF_SMALL // MLA_IN_W))]
        + prev_specs + _attn_weight_specs(l, 1),
        args=[proj, *prev, *wts],
        out_specs=[pl.BlockSpec((seq, MLA_WIDTH), lambda b: (b, 0)),
                   stacked(l + 1, MLA_KV_RANK), stacked(l + 1, MLA_ROPE)],
        out_shape=[jax.ShapeDtypeStruct((ntok, MLA_WIDTH), f32),
                   jax.ShapeDtypeStruct((batch, l + 1, seq, MLA_KV_RANK), f32),
                   jax.ShapeDtypeStruct((batch, l + 1, seq, MLA_ROPE), f32)],
        scratch_shapes=[pltpu.VMEM((seq, MLA_HEADS * HEAD_SLOT), bf16),
                        pltpu.VMEM((seq, MLA_WIDTH), bf16)])


def _attn_lat(proj, row0, cache_ckv, cache_kr_p, rope_tabs, wts, l, batch, seq, past):
    ntok = batch * seq
    tq = 256
    nq = seq // tq
    kern = functools.partial(_attn_lat_kernel, seq=seq, past=past, tq=tq)
    tab = pl.BlockSpec((seq, LANES), lambda b, q: (0, 0))
    return pl.pallas_call(
        kern,
        grid=(batch, nq),
        in_specs=[pl.BlockSpec((seq, MLA_IN_W), lambda b, q: (row0 // seq + b, OFF_SMALL // MLA_IN_W)),
                  pl.BlockSpec((None, None, past, MLA_KV_RANK), lambda b, q: (b, l, 0, 0)),
                  pl.BlockSpec((None, None, past, LANES), lambda b, q: (b, l, 0, 0)),
                  tab, tab, tab]
        + _attn_weight_specs(l, 2),
        out_specs=pl.BlockSpec((tq, MLA_WIDTH), lambda b, q: (b * nq + q, 0)),
        out_shape=jax.ShapeDtypeStruct((ntok, MLA_WIDTH), f32),
        scratch_shapes=[pltpu.VMEM((past + seq, MLA_HEADS * HEAD_SLOT), bf16),
                        pltpu.VMEM((past + seq, MLA_WIDTH), bf16)],
        compiler_params=_cparams(2),
        name="attn_lat",
    )(proj, cache_ckv, cache_kr_p, *rope_tabs, *wts)


def _split3(x):
    hi = x.astype(bf16)
    r1 = x - hi.astype(f32)
    mid = r1.astype(bf16)
    lo = (r1 - mid.astype(f32)).astype(bf16)
    return hi, mid, lo


def _tri_cumsum(tri_b, x):
    hi, mid, lo = _split3(x)
    return _dot(tri_b, hi) + _dot(tri_b, mid) + _dot(tri_b, lo)


def _lane_bcast(x, c):
    return jnp.broadcast_to(x[:, c:c + 1], (x.shape[0], LANES))


def _gdn_kernel(*refs, seq, has_state, nprev):
    gin_ref, cw_ref, pk_ref = refs[:3]
    refs = refs[3:]
    s0_ref = sprev_ref = sout_ref = None
    if has_state:
        s0_ref, o_ref = refs[:2]
        refs = refs[2:]
    else:
        if nprev:
            sprev_ref = refs[0]
            refs = refs[1:]
        o_ref, sout_ref = refs[:2]
        refs = refs[2:]
    xpad, qkv_s, g_s, b_s, st_s, wq_s, ak_s, u_s, el_s, rhs_s = refs
    C = GDN_CHUNK
    nchunk = seq // C
    H = GDN_HEADS
    width = 2 * GDN_KW + GDN_VW
    halo = 8

    for j in range(width // LANES):
        xpad[j, 0:halo, :] = jnp.zeros((halo, LANES), f32)
        xpad[j, halo + seq:, :] = jnp.zeros((halo, LANES), f32)
        xpad[j, halo:halo + seq, :] = gin_ref[:, j * LANES:(j + 1) * LANES]
    if has_state:
        for d in range(2):
            for h in range(H):
                st_s[d * H + h] = s0_ref[d, h]
    else:
        st_s[...] = jnp.zeros((2 * H, GDN_DK, GDN_DV), f32)

    neg_a = -jnp.exp(_pk(pk_ref, "a_log"))
    dtb = _pk(pk_ref, "dt_bias")

    def conv_tile(j, l2norm):
        w = cw_ref[j]
        post = jnp.where(j < H, GDN_DK ** -0.5, 1.0)
        for c in range(nchunk):
            base = halo - CONV_W // 2 + c * C
            y = xpad[j, base:base + C, :] * w[0:1]
            for tap in range(1, CONV_W):
                y = y + xpad[j, base + tap:base + tap + C, :] * w[tap:tap + 1]
            y = y * jax.nn.sigmoid(y)
            if l2norm:
                y = y * (lax.rsqrt(jnp.sum(y * y, axis=-1, keepdims=True) + EPS) * post)
            qkv_s[j, c * C:(c + 1) * C, :] = y

    def conv_qk(j, carry):
        conv_tile(j, True)
        return carry

    def conv_v(j, carry):
        conv_tile(j, False)
        return carry

    tile_unroll = 4 if nchunk <= 4 else 1
    lax.fori_loop(0, 2 * H, conv_qk, 0, unroll=tile_unroll)
    lax.fori_loop(2 * H, 3 * H, conv_v, 0, unroll=tile_unroll)

    def prep(c, carry):
        r0 = pl.multiple_of(c * C, C)
        sm = gin_ref[pl.ds(r0, C), OFF_SMALL:OFF_SMALL + LANES]
        z = sm + dtb
        g_s[pl.ds(r0, C), :] = neg_a * (jnp.maximum(z, 0.0) + jnp.log1p(jnp.exp(-jnp.abs(z))))
        b_s[pl.ds(r0, C), :] = pltpu.roll(jax.nn.sigmoid(sm), LANES - 2 * H, 1)
        return carry

    lax.fori_loop(0, nchunk, prep, 0, unroll=4)

    ri = lax.broadcasted_iota(jnp.int32, (C, LANES), 0)
    cl = lax.broadcasted_iota(jnp.int32, (C, LANES), 1)
    fwd = cl < C
    cj = cl & (C - 1)
    eye2 = (ri == cj).astype(f32)
    incl2 = (fwd & (ri >= cj)) | (~fwd & (ri <= cj))
    strict2 = (fwd & (ri > cj)) | (~fwd & (ri < cj))
    xor = ri ^ cj
    level2 = sum((xor >= (1 << b)).astype(jnp.int32) for b in range(C.bit_length() - 1))
    lvl_top = jnp.where(fwd, level2, 0)
    lvl_bot = jnp.where(fwd, 0, level2)
    r2 = lax.broadcasted_iota(jnp.int32, (2 * C, C), 0)
    c2 = lax.broadcasted_iota(jnp.int32, (2 * C, C), 1)
    tri2 = (((r2 < C) & (r2 >= c2)) | ((r2 >= C) & (r2 - C <= c2))).astype(bf16)
    zrhs = jnp.zeros((C, 2 * LANES), bf16)
    zvn = jnp.zeros((C, LANES), bf16)
    fwd_row = fwd[0:1, :]
    cpi = min(GDN_SOLVE_CHUNKS, nchunk)

    def block_diag(x):
        return jnp.concatenate([jnp.where(fwd, x, 0.0), jnp.where(fwd, 0.0, x)], axis=0).astype(bf16)

    def solve_phase(i, carry):
        chains = []
        for cc in range(cpi):
            c = i * cpi + cc
            rows = pl.ds(pl.multiple_of(c * C, C), C)
            g2 = _tri_cumsum(tri2, g_s[rows, :])
            g2t = g2.T
            bt = b_s[rows, :]
            for h in range(H):
                chains.append((cc, c, h, rows, g2, g2t, bt))

        a2s, t2s = [], []
        for (cc, c, h, rows, g2, g2t, bt) in chains:
            q = qkv_s[h, rows, :]
            k = qkv_s[H + h, rows, :]
            v = qkv_s[2 * H + h, rows, :]
            kq = _dot_nt(jnp.concatenate([k, q], axis=0).astype(bf16),
                         jnp.concatenate([k, k], axis=0).astype(bf16))
            gcc_f = _lane_bcast(g2[:C], h)
            gcc_b = _lane_bcast(g2[C:], H + h)
            btc_f = _lane_bcast(bt, h)
            btc_b = _lane_bcast(bt, H + h)
            grow = jnp.where(fwd_row, g2t[h:h + 1, :], g2t[H + h:H + h + 1, :])
            diff = jnp.where(fwd, gcc_f, gcc_b) - grow
            dec = jnp.where(incl2, jnp.exp(jnp.where(incl2, diff, 0.0)), 0.0)
            a2 = jnp.where(strict2, jnp.where(fwd, btc_f, btc_b) * kq[:C] * dec, 0.0)
            a2s.append(a2)
            t2s.append(eye2 - jnp.where(level2 == 1, a2, 0.0))
            glast_f = gcc_f[C - 1:C, :]
            glast_b = gcc_b[0:1, :]
            e1_f = jnp.exp(gcc_f)
            e1_b = jnp.exp(gcc_b)
            ket = jnp.concatenate([k * jnp.exp(glast_f - gcc_f), k * jnp.exp(glast_b - gcc_b)], axis=0).T
            ak_s[pl.ds(pl.multiple_of((c * H + h) * 3 * C, 3 * C), 3 * C), :] = jnp.concatenate(
                [kq[C:] * dec, ket], axis=0).astype(bf16)
            for d, (btc, e1, glast) in enumerate(((btc_f, e1_f, glast_f), (btc_b, e1_b, glast_b))):
                ch = d * H + h
                j = (cc * H + h) * 2 + d
                rhs_s[j * C:(j + 1) * C, :] = jnp.concatenate([v * btc, k * (btc * e1)], axis=1).astype(bf16)
                wq_s[pl.ds(pl.multiple_of((c * 2 * H + ch) * 2 * C + C, C), C), :] = (q * e1).astype(bf16)
                el_s[pl.ds(pl.multiple_of((c * 2 * H + ch) * 8, 8), 8), :] = jnp.broadcast_to(
                    jnp.exp(glast), (8, LANES))

        for g0 in range(0, len(chains), GDN_LEVEL_CHAINS):
            grp = slice(g0, g0 + GDN_LEVEL_CHAINS)
            ga2, gt2 = a2s[grp], t2s[grp]
            for lv in range(2, C.bit_length()):
                rs = []
                for a2, t2 in zip(ga2, gt2):
                    abd = jnp.concatenate([jnp.where(lvl_top == lv, a2, 0.0),
                                           jnp.where(lvl_bot == lv, a2, 0.0)], axis=0).astype(bf16)
                    rs.append(_dot(t2.astype(bf16), abd))
                gt2 = [t2 - _dot(r.astype(bf16), block_diag(t2)) for r, t2 in zip(rs, gt2)]

            for (cc, c, h, rows, g2, g2t, bt), t2 in zip(chains[grp], gt2):
                t2b = t2.astype(bf16)
                for d in range(2):
                    ch = d * H + h
                    j = (cc * H + h) * 2 + d
                    rhs = rhs_s[j * C:(j + 1) * C, :]
                    rhs = jnp.concatenate([rhs, zrhs] if d == 0 else [zrhs, rhs], axis=0)
                    uw = _dot(t2b, rhs)
                    u_s[pl.ds(pl.multiple_of((c * 2 * H + ch) * C, C), C), :] = uw[:, :LANES]
                    wq_s[pl.ds(pl.multiple_of((c * 2 * H + ch) * 2 * C, C), C), :] = uw[:, LANES:].astype(bf16)
        return carry

    lax.fori_loop(0, nchunk // cpi, solve_phase, 0)

    def scan_phase(i, carry):
        cs = [i if ch < H else nchunk - 1 - i for ch in range(2 * H)]
        s_old = [st_s[ch] for ch in range(2 * H)]
        r1 = [_dot(wq_s[pl.ds(pl.multiple_of((cs[ch] * 2 * H + ch) * 2 * C, 2 * C), 2 * C), :],
                   s_old[ch].astype(bf16)) for ch in range(2 * H)]
        r2s = []
        for ch in range(2 * H):
            u = u_s[pl.ds(pl.multiple_of((cs[ch] * 2 * H + ch) * C, C), C), :]
            vnb = (u - r1[ch][:C]).astype(bf16)
            rhs = jnp.concatenate([vnb, zvn] if ch < H else [zvn, vnb], axis=0)
            ak = ak_s[pl.ds(pl.multiple_of((cs[ch] * H + ch % H) * 3 * C, 3 * C), 3 * C), :]
            r2s.append(_dot(ak, rhs))
        for ch in range(2 * H):
            el = el_s[pl.ds(pl.multiple_of((cs[ch] * 2 * H + ch) * 8, 8), 8), :][0:1, :]
            st_s[ch] = s_old[ch] * el + r2s[ch][C:]
            xpad[ch, pl.ds(pl.multiple_of(cs[ch] * C, C), C), :] = r1[ch][C:] + r2s[ch][:C]
        return carry

    lax.fori_loop(0, nchunk, scan_phase, 0, unroll=4)

    onorm = _pk(pk_ref, "gdn_onorm")

    def fin(c, carry):
        rows = pl.ds(pl.multiple_of(c * C, C), C)
        for h in range(H):
            ls = slice(h * LANES, (h + 1) * LANES)
            o_ref[rows, ls] = _rms_rows(xpad[h, rows, :] + xpad[H + h, rows, :], onorm)
        return carry

    lax.fori_loop(0, nchunk, fin, 0, unroll=4)
    if sout_ref is not None:
        for i in range(nprev):
            sout_ref[i] = sprev_ref[i]
        for d in range(2):
            for h in range(H):
                sout_ref[nprev, d, h] = st_s[d * H + h]


def _gdn(proj, row0, conv_w, pack, state, prev_states, l, batch, seq):
    ntok = batch * seq
    width = 2 * GDN_KW + GDN_VW
    has_state = state is not None
    nprev = 0 if has_state else l
    nchunk = seq // GDN_CHUNK
    kern = functools.partial(_gdn_kernel, seq=seq, has_state=has_state, nprev=nprev)
    in_specs = [pl.BlockSpec((seq, GDN_IN_W), lambda b: (row0 // seq + b, OFF_QKV // GDN_IN_W)),
                pl.BlockSpec((None, width // LANES, CONV_W, LANES), lambda b: (l, 0, 0, 0)),
                _pk_spec(l)]
    args = [proj, conv_w, pack]
    o_spec = pl.BlockSpec((seq, GDN_VW), lambda b: (b, 0))
    o_shape = jax.ShapeDtypeStruct((ntok, GDN_VW), f32)
    st_block = (None, None, 2, GDN_HEADS, GDN_DK, GDN_DV)
    if has_state:
        in_specs.append(pl.BlockSpec(st_block, lambda b: (b, l, 0, 0, 0, 0)))
        args.append(state)
        out_specs, out_shape = [o_spec], [o_shape]
    else:
        def stacked(n):
            return pl.BlockSpec((None, n) + st_block[2:], lambda b: (b, 0, 0, 0, 0, 0))

        if nprev:
            in_specs.append(stacked(nprev))
            args.append(prev_states)
        out_specs = [o_spec, stacked(nprev + 1)]
        out_shape = [o_shape, jax.ShapeDtypeStruct((batch, nprev + 1, 2, GDN_HEADS, GDN_DK, GDN_DV), f32)]
    return _Part(
        kernel=kern,
        in_specs=in_specs,
        args=args,
        out_specs=out_specs,
        out_shape=out_shape,
        scratch_shapes=[pltpu.VMEM((width // LANES, seq + 16, LANES), f32),
                        pltpu.VMEM((width // LANES, seq, LANES), f32),
                        pltpu.VMEM((seq, LANES), f32),
                        pltpu.VMEM((seq, LANES), f32),
                        pltpu.VMEM((2 * GDN_HEADS, GDN_DK, GDN_DV), f32),
                        pltpu.VMEM((nchunk * 2 * GDN_HEADS * 2 * GDN_CHUNK, LANES), bf16),
                        pltpu.VMEM((nchunk * GDN_HEADS * 3 * GDN_CHUNK, LANES), bf16),
                        pltpu.VMEM((nchunk * 2 * GDN_HEADS * GDN_CHUNK, LANES), f32),
                        pltpu.VMEM((nchunk * 2 * GDN_HEADS * 8, LANES), f32),
                        pltpu.VMEM((min(GDN_SOLVE_CHUNKS, nchunk) * 2 * GDN_HEADS * GDN_CHUNK, 2 * LANES), bf16)])


def _merge_kernel(x_ref, mod_ref, oa_ref, ob_ref, p_ref, pk_ref, ws_ref, bs_ref, wbr_ref, wo_ref,
                  out_ref, sv_s):
    tm = x_ref.shape[0]

    def group(off, width=512):
        return p_ref[:, off:off + width].astype(f32)

    def gelu(x):
        c1 = math.sqrt(2.0 / math.pi)
        half = 0.5 * x
        return half + half * jnp.tanh(x * (c1 + (c1 * 0.044715) * (x * x)))

    u = gelu(group(OFF_CU))
    vf = gelu(group(OFF_CV))
    mu = jnp.mean(vf, axis=-1, keepdims=True)
    vc = vf - mu
    var = jnp.mean(vc * vc, axis=-1, keepdims=True)
    vn = (vc * lax.rsqrt(var + EPS) * _pk(pk_ref, "cm_ln_g") + _pk(pk_ref, "cm_ln_b")).astype(bf16)
    for ck in range(tm // CM_CHUNK):
        rs = slice(ck * CM_CHUNK, (ck + 1) * CM_CHUNK)
        for g in range(CM_GROUPS):
            ls = slice(g * LANES, (g + 1) * LANES)
            sv_s[rs, ls] = _dot(ws_ref[g], vn[rs, ls]) + bs_ref[:, ls]
    o_c = u * sv_s[...]

    def silu(z):
        return z * jax.nn.sigmoid(z)

    brs = (oa_ref[...] * silu(group(OFF_ZA)), ob_ref[...] * silu(group(OFF_ZB)), o_c * silu(group(OFF_ZC)))
    ysum = None
    for n in range(N_BRANCH):
        yb = _dot(brs[n].astype(bf16), wbr_ref[n])
        t = jax.nn.sigmoid(group(OFF_GL + n * D_MODEL, D_MODEL)) * yb
        ysum = t if ysum is None else ysum + t
    y = _dot(ysum.astype(bf16), wo_ref[...])
    gate = mod_ref[0][:, 2 * D_MODEL:]
    out_ref[...] = x_ref[...] + gate * y


def _merge(x, mod_l, proj, row0, o_a, o_b, pack, ws_b, bs_full, wbr_b, wo_b, l, latent, seq):
    ntok = x.shape[0]
    tm = 512

    def const(*idx):
        return lambda t: idx

    return pl.pallas_call(
        _merge_kernel,
        grid=(ntok // tm,),
        in_specs=[pl.BlockSpec((tm, D_MODEL), lambda t: (t, 0)),
                  pl.BlockSpec((None, 1, 1, 3 * D_MODEL), _mod_row(l, latent, tm, seq)),
                  pl.BlockSpec((tm, 512), lambda t: (t, 0)),
                  pl.BlockSpec((tm, 512), lambda t: (t, 0)),
                  pl.BlockSpec((tm, P16_W), lambda t: (row0 // tm + t, 0)),
                  _pk_spec(l),
                  pl.BlockSpec((None, CM_GROUPS, CM_CHUNK, CM_CHUNK), const(l, 0, 0, 0)),
                  pl.BlockSpec((None, CM_CHUNK, CM_WIDTH), const(l, 0, 0)),
                  pl.BlockSpec((None, N_BRANCH, BRANCH_W, D_MODEL), const(l, 0, 0, 0)),
                  pl.BlockSpec((None, D_MODEL, D_MODEL), const(l, 0, 0))],
        out_specs=pl.BlockSpec((tm, D_MODEL), lambda t: (t, 0)),
        out_shape=jax.ShapeDtypeStruct((ntok, D_MODEL), f32),
        scratch_shapes=[pltpu.VMEM((tm, CM_WIDTH), f32)],
        compiler_params=_cparams(1),
        name="merge",
    )(x, mod_l, o_a, o_b, proj, pack, ws_b, bs_full, wbr_b, wo_b)


def _pad_last(x, n):
    return jnp.pad(x, [(0, 0)] * (x.ndim - 1) + [(0, n - x.shape[-1])])


def _rope_tables(seq):
    t = np.arange(seq)
    row = (t // GRID_W).astype(np.float32)
    colp = (t % GRID_W).astype(np.float32)
    nf = MLA_ROPE // 4
    inv = (ROPE_THETA ** (-np.arange(nf, dtype=np.float32) / nf)).astype(np.float32)
    cos_t = np.ones((seq, LANES), np.float32)
    s_lo = np.zeros((seq, LANES), np.float32)
    s_hi = np.zeros((seq, LANES), np.float32)
    for i, pos in enumerate((row, colp)):
        ang = (pos[:, None] * inv[None, :]).astype(np.float32)
        cs, sn = np.cos(ang), np.sin(ang)
        lo = ROPE_LANE0 + 2 * nf * i
        cos_t[:, lo:lo + nf] = cs
        cos_t[:, lo + nf:lo + 2 * nf] = cs
        s_lo[:, lo:lo + nf] = -sn
        s_hi[:, lo + nf:lo + 2 * nf] = sn
    return tuple(jnp.asarray(p) for p in (cos_t, s_lo, s_hi))


def kernel(x_prompt, x_sample, cache_ckv, cache_krope, state_gdn, c, c_ctx, norm_g, w_mod, b_mod, w_in, q_a_norm, w_uq, kv_a_norm, w_ukv, q_norm, k_norm, conv_w, a_log, dt_bias, gdn_onorm, cm_ln_g, cm_ln_b, w_s, b_s, w_branch, w_o):
    L = DEPTH
    batch, seq, _ = x_prompt.shape
    dbatch, dseq, _ = x_sample.shape
    past = cache_ckv.shape[2]

    w_t = jnp.swapaxes(w_in, 1, 2)
    wuq_p = _pad_last(w_uq.reshape(L, MLA_Q_RANK, MLA_HEADS, MLA_QK), HEAD_SLOT)
    wuq_p = wuq_p.reshape(L, MLA_Q_RANK, MLA_HEADS * HEAD_SLOT).astype(bf16)
    wukv = w_ukv.reshape(L, MLA_KV_RANK, MLA_HEADS, MLA_NOPE + MLA_V)
    wk_p = _pad_last(wukv[..., :MLA_NOPE], HEAD_SLOT).reshape(L, MLA_KV_RANK, MLA_HEADS * HEAD_SLOT).astype(bf16)
    wv_p = wukv[..., MLA_NOPE:].reshape(L, MLA_KV_RANK, MLA_WIDTH).astype(bf16)
    cache_kr_p = jnp.pad(cache_krope, [(0, 0)] * 3 + [(ROPE_LANE0, LANES - ROPE_LANE0 - MLA_ROPE)])
    rope_tabs = _rope_tables(dseq)
    conv_w = jnp.swapaxes(conv_w.reshape(L, CONV_W, -1, LANES), 1, 2)
    ws_b = w_s.astype(bf16)
    bs_full = jnp.repeat(jnp.swapaxes(b_s, 1, 2), CM_WIDTH // CM_GROUPS, axis=2)
    gates8 = [_pad_last(v.reshape(L, 2 * GDN_HEADS), LANES) for v in (a_log, dt_bias)]
    pack = jnp.stack([norm_g,
                      jnp.concatenate([q_a_norm, kv_a_norm, _pad_last(q_norm, HEAD_SLOT),
                                       _pad_last(k_norm, HEAD_SLOT), gdn_onorm], axis=-1),
                      jnp.concatenate([cm_ln_g, cm_ln_b], axis=-1),
                      _pad_last(jnp.concatenate(gates8, axis=-1), D_MODEL)], axis=1)
    attn_w = (pack, wuq_p, wk_p, wv_p)

    c8 = jnp.concatenate([c, c_ctx[None, :], jnp.zeros((8 - dbatch - 1, D_MODEL), f32)], axis=0)
    wbr_b = w_branch.astype(bf16)
    wo_b = w_o.astype(bf16)
    mod = _modulation(c8, w_mod, b_mod)

    yp = x_prompt.reshape(batch * seq, D_MODEL)
    ys = x_sample.reshape(dbatch * dseq, D_MODEL)
    caches, states = (), None
    for l in range(L):
        mod_l = mod
        p16, p32 = _inproj(yp, ys, mod_l, pack, w_t, l, dseq)
        lat0 = batch * seq
        (o_a, new_ckv, new_kr), (o_b, states) = _run_parts(
            [_attn_ctx(p32, caches, attn_w, l, batch, seq), _gdn(p32, 0, conv_w, pack, None, states, l, batch, seq)],
            (batch,), "mix_ctx")
        caches = (new_ckv, new_kr)
        yp = _merge(yp, mod_l, p16, 0, o_a, o_b, pack, ws_b, bs_full, wbr_b, wo_b, l, False, seq)
        o_a = _attn_lat(p32, lat0, cache_ckv, cache_kr_p, rope_tabs, attn_w, l, dbatch, dseq, past)
        ((o_b,),) = _run_parts([_gdn(p32, lat0, conv_w, pack, state_gdn, None, l, dbatch, dseq)], (dbatch,), "gdn_lat")
        ys = _merge(ys, mod_l, p16, lat0, o_a, o_b, pack, ws_b, bs_full, wbr_b, wo_b, l, True, dseq)
    return (yp.reshape(batch, seq, D_MODEL), ys.reshape(dbatch, dseq, D_MODEL), caches[0], caches[1], states)
```

```python
import collections
import functools
import math

import numpy as np

import jax
import jax.numpy as jnp
from jax import lax
from jax.experimental import pallas as pl
from jax.experimental.pallas import tpu as pltpu

D_MODEL = 1024
DEPTH = 2
GRID_W = 64
EPS = 1e-6
MLA_HEADS = 8
MLA_NOPE = 64
MLA_ROPE = 32
MLA_QK = MLA_NOPE + MLA_ROPE
MLA_V = 64
MLA_Q_RANK = 384
MLA_KV_RANK = 256
MLA_WIDTH = MLA_HEADS * MLA_V
ROPE_THETA = 10000.0
GDN_HEADS = 4
GDN_DK = 128
GDN_DV = 128
GDN_KW = GDN_HEADS * GDN_DK
GDN_VW = GDN_HEADS * GDN_DV
GDN_CHUNK = 64
CONV_W = 5
GDN_SOLVE_CHUNKS = 16
GDN_LEVEL_CHAINS = 16
CTX_SEQS_PER_STEP = 2
CM_GROUPS = 4
CM_CHUNK = 128
CM_WIDTH = 512
N_BRANCH = 3
BRANCH_W = 512
SPLIT_SIZES = (MLA_Q_RANK, MLA_KV_RANK, MLA_ROPE, MLA_WIDTH,
               2 * GDN_KW + GDN_VW, 2 * GDN_HEADS, 2 * GDN_HEADS, GDN_VW,
               CM_WIDTH, CM_WIDTH, CM_WIDTH, N_BRANCH * D_MODEL)

LANES = 128
SUBLANES = 8
HEAD_SLOT = LANES
ROPE_LANE0 = MLA_NOPE

OFF_GL = 0
OFF_ZA = OFF_GL + N_BRANCH * D_MODEL
OFF_ZB = OFF_ZA + 512
OFF_CU = OFF_ZB + 512
OFF_CV = OFF_CU + 512
OFF_ZC = OFF_CV + 512
P16_W = OFF_ZC + 512
OFF_QKV = 0
OFF_SMALL = OFF_QKV + 2 * GDN_KW + GDN_VW
OFF_CQ = OFF_SMALL + LANES
OFF_CKV = OFF_CQ + MLA_Q_RANK
P32_W = OFF_CKV + MLA_KV_RANK
PROJ_W = P16_W + P32_W
GDN_IN_W = OFF_CQ
MLA_IN_W = P32_W - OFF_SMALL
MLA_SMALL = slice(0, LANES)
MLA_CQ = slice(OFF_CQ - OFF_SMALL, OFF_CKV - OFF_SMALL)
MLA_CKV = slice(OFF_CKV - OFF_SMALL, MLA_IN_W)

VMEM_LIMIT = 56 * 1024 * 1024

PK_ROWS = 4
PK = {}
for _row, _fields in enumerate((
        (("norm_g", D_MODEL),),
        (("q_a_norm", MLA_Q_RANK), ("kv_a_norm", MLA_KV_RANK), ("q_norm", HEAD_SLOT), ("k_norm", HEAD_SLOT),
         ("gdn_onorm", GDN_DV)),
        (("cm_ln_g", CM_WIDTH), ("cm_ln_b", CM_WIDTH)),
        (("a_log", LANES), ("dt_bias", LANES)))):
    _lane = 0
    for _name, _width in _fields:
        PK[_name] = (_row, _lane, _width)
        _lane += _width
    assert _lane <= D_MODEL


def _pk(pk_ref, name):
    row, lane0, width = PK[name]
    return pk_ref[row:row + 1, lane0:lane0 + width]


def _pk_spec(l):
    return pl.BlockSpec((None, PK_ROWS, D_MODEL), lambda *g: (l, 0, 0))

f32 = jnp.float32
bf16 = jnp.bfloat16


def _cparams(n_axes):
    return pltpu.CompilerParams(dimension_semantics=("arbitrary",) * n_axes,
                                vmem_limit_bytes=VMEM_LIMIT)


_Part = collections.namedtuple("_Part", "kernel in_specs args out_specs out_shape scratch_shapes")


def _run_parts(parts, grid, name):
    n_in = [len(p.in_specs) for p in parts]
    n_out = [len(p.out_specs) for p in parts]
    n_scr = [len(p.scratch_shapes) for p in parts]

    def body(*refs):
        ins, outs, scr = refs[:sum(n_in)], refs[sum(n_in):sum(n_in) + sum(n_out)], refs[sum(n_in) + sum(n_out):]
        i = o = s = 0
        for p, ni, no, ns in zip(parts, n_in, n_out, n_scr):
            p.kernel(*ins[i:i + ni], *outs[o:o + no], *scr[s:s + ns])
            i, o, s = i + ni, o + no, s + ns

    res = pl.pallas_call(
        body,
        grid=grid,
        in_specs=[sp for p in parts for sp in p.in_specs],
        out_specs=[sp for p in parts for sp in p.out_specs],
        out_shape=[sh for p in parts for sh in p.out_shape],
        scratch_shapes=[sc for p in parts for sc in p.scratch_shapes],
        compiler_params=_cparams(len(grid)),
        name=name,
    )(*[a for p in parts for a in p.args])
    out, o = [], 0
    for no in n_out:
        out.append(res[o:o + no])
        o += no
    return out


def _dot(a, b):
    return jnp.dot(a, b, preferred_element_type=f32)


def _dot_nt(a, b):
    return lax.dot_general(a, b, (((1,), (1,)), ((), ())), preferred_element_type=f32)


def _rms_rows(x, g, n=None):
    n = x.shape[-1] if n is None else n
    ms = jnp.sum(x * x, axis=-1, keepdims=True) * (1.0 / n)
    return x * lax.rsqrt(ms + EPS) * g


def _mod_kernel(c_ref, w_ref, b_ref, o_ref):
    a = c_ref[...]
    a = (a * jax.nn.sigmoid(a)).astype(bf16)
    y = _dot(a, w_ref[0].astype(bf16)) + b_ref[0]
    for r in range(y.shape[0]):
        o_ref[0, r] = y[r:r + 1]


def _modulation(c8, w_mod, b_mod):
    tn = 1536
    return pl.pallas_call(
        _mod_kernel,
        grid=(DEPTH, 3 * D_MODEL // tn),
        in_specs=[pl.BlockSpec((8, D_MODEL), lambda l, n: (0, 0)),
                  pl.BlockSpec((1, D_MODEL, tn), lambda l, n: (l, 0, n)),
                  pl.BlockSpec((1, 1, tn), lambda l, n: (l, 0, n))],
        out_specs=pl.BlockSpec((1, 8, 1, tn), lambda l, n: (l, 0, 0, n)),
        out_shape=jax.ShapeDtypeStruct((DEPTH, 8, 1, 3 * D_MODEL), f32),
        compiler_params=_cparams(2),
        name="modulation",
    )(c8, w_mod, b_mod.reshape(DEPTH, 1, 3 * D_MODEL))


W_CHUNK = 512


def _w_in_moves():
    offs = [0]
    for s in SPLIT_SIZES:
        offs.append(offs[-1] + s)
    cq, ckv, krope, z_a, qkv, ga, gb, z_b, cu, cv, z_c, gl = offs[:-1]
    moves = [(gl, N_BRANCH * D_MODEL, OFF_GL), (z_a, 512, OFF_ZA), (z_b, 512, OFF_ZB), (cu, 512, OFF_CU),
             (cv, 512, OFF_CV), (z_c, 512, OFF_ZC), (qkv, 2 * GDN_KW + GDN_VW, P16_W + OFF_QKV),
             (cq, MLA_Q_RANK, P16_W + OFF_CQ), (ckv, MLA_KV_RANK, P16_W + OFF_CKV)]
    return moves, ga, krope


def _w_chunks():
    moves, ga, krope = _w_in_moves()
    chunks = []
    for (a, w, d) in moves:
        for o in range(0, w, W_CHUNK):
            chunks.append((a + o, min(W_CHUNK, w - o), d + o))
    small0 = P16_W + OFF_SMALL
    chunks.append((ga, 4 * GDN_HEADS, small0))
    chunks.append((krope, MLA_ROPE, small0 + ROPE_LANE0))
    return chunks


def _inproj_kernel(xc_ref, xl_ref, mod_ref, pk_ref, wt_hbm, o16_ref, o32_ref, w_s, stage, sem, *, layer, nctx):
    t = pl.program_id(0)

    @pl.when(t == 0)
    def _():
        chunks = _w_chunks()

        def copy(j):
            src, n, _ = chunks[j]
            return pltpu.make_async_copy(wt_hbm.at[layer, pl.ds(src, n), :], stage.at[j % 2, pl.ds(0, n), :],
                                         sem.at[j % 2])

        copy(0).start()
        for j, (_, n, dst) in enumerate(chunks):
            if j + 1 < len(chunks):
                copy(j + 1).start()
            copy(j).wait()
            w_s[dst:dst + n, :] = stage[j % 2, 0:n, :].astype(bf16)
        small0 = P16_W + OFF_SMALL
        for lo, hi in ((4 * GDN_HEADS, ROPE_LANE0), (ROPE_LANE0 + MLA_ROPE, LANES)):
            w_s[small0 + lo:small0 + hi, :] = jnp.zeros((hi - lo, D_MODEL), bf16)

    x = jnp.where(t < nctx, xc_ref[...], xl_ref[...])
    m = mod_ref[0]
    shift = m[:, :D_MODEL]
    scale = m[:, D_MODEL:2 * D_MODEL]
    h = (_rms_rows(x, _pk(pk_ref, "norm_g")) * (1.0 + scale) + shift).astype(bf16)
    for a in range(0, P16_W, 512):
        o16_ref[:, a:a + 512] = _dot_nt(h, w_s[a:a + 512, :]).astype(bf16)
    for a in range(0, P32_W, 512):
        b = min(a + 512, P32_W)
        o32_ref[:, a:b] = _dot_nt(h, w_s[P16_W + a:P16_W + b, :])


def _inproj(xc, xl, mod_l, norm_g, w_t, l, dseq):
    tm = 512
    nctx = xc.shape[0] // tm
    nlat = xl.shape[0] // tm
    ntok = xc.shape[0] + xl.shape[0]
    kern = functools.partial(_inproj_kernel, layer=l, nctx=nctx)
    return pl.pallas_call(
        kern,
        grid=(nctx + nlat,),
        in_specs=[pl.BlockSpec((tm, D_MODEL), lambda t: (jnp.minimum(t, nctx - 1), 0)),
                  pl.BlockSpec((tm, D_MODEL), lambda t: (jnp.maximum(t - nctx, 0), 0)),
                  pl.BlockSpec((None, 1, 1, 3 * D_MODEL),
                               lambda t: (l, jnp.where(t < nctx, 4, ((t - nctx) * tm) // dseq), 0, 0)),
                  _pk_spec(l),
                  pl.BlockSpec(memory_space=pl.ANY)],
        out_specs=[pl.BlockSpec((tm, P16_W), lambda t: (t, 0)),
                   pl.BlockSpec((tm, P32_W), lambda t: (t, 0))],
        out_shape=[jax.ShapeDtypeStruct((ntok, P16_W), bf16),
                   jax.ShapeDtypeStruct((ntok, P32_W), f32)],
        scratch_shapes=[pltpu.VMEM((PROJ_W, D_MODEL), bf16),
                        pltpu.VMEM((2, W_CHUNK, D_MODEL), f32),
                        pltpu.SemaphoreType.DMA((2,))],
        compiler_params=_cparams(1),
        name="inproj",
    )(xc, xl, mod_l, norm_g, w_t)


def _mod_row(l, latent, tm, seq):
    if latent:
        return lambda t: (l, (t * tm) // seq, 0, 0)
    return lambda t: (l, 4, 0, 0)


def _rope(x, cos_t, sin_lo, sin_hi):
    return x * cos_t + pltpu.roll(x, LANES - 8, 1) * sin_lo + pltpu.roll(x, 8, 1) * sin_hi


def _build_kv(ckvn_b, kr, wk_ref, wv_ref, knorm, rope, k_s, v_s, r0):
    n = ckvn_b.shape[0]
    kfull = _dot(ckvn_b, wk_ref[...])
    v_s[r0:r0 + n, :] = _dot(ckvn_b, wv_ref[...]).astype(bf16)
    krg = kr * knorm
    if rope is not None:
        krg = _rope(krg, *rope)
    kr_ss = jnp.sum(kr * kr, axis=-1, keepdims=True)
    for h in range(MLA_HEADS):
        sl = slice(h * HEAD_SLOT, (h + 1) * HEAD_SLOT)
        kn = kfull[:, sl]
        ms = (jnp.sum(kn * kn, axis=-1, keepdims=True) + kr_ss) * (1.0 / MLA_QK)
        k_s[r0:r0 + n, sl] = ((kn * knorm + krg) * lax.rsqrt(ms + EPS)).astype(bf16)


def _rope_lane_mask(shape):
    lane = lax.broadcasted_iota(jnp.int32, shape, 1)
    return (lane >= ROPE_LANE0) & (lane < ROPE_LANE0 + MLA_ROPE)


def _attend_block(qa_b, wuq_ref, qnorm, rope, k_s, v_s, o_ref, rows=slice(None)):
    tq = qa_b.shape[0]
    qfull = _dot(qa_b, wuq_ref[...])
    qgain = qnorm * (math.log2(math.e) / math.sqrt(MLA_QK))
    lane = lax.broadcasted_iota(jnp.int32, (tq, LANES), 1)

    def scores(h):
        sl = slice(h * HEAD_SLOT, (h + 1) * HEAD_SLOT)
        qh = _rms_rows(qfull[:, sl], qgain, n=MLA_QK)
        if rope is not None:
            qh = _rope(qh, *rope)
        return _dot_nt(qh.astype(bf16), k_s[rows, sl])

    s_next = scores(0)
    outs = []
    for h in range(MLA_HEADS):
        s = s_next
        if h + 1 < MLA_HEADS:
            s_next = scores(h + 1)
        p = jnp.exp2(s - jnp.max(s, axis=-1, keepdims=True))
        den = jnp.sum(p, axis=-1, keepdims=True)
        hp = h // 2
        outs.append(_dot(p.astype(bf16), v_s[rows, hp * LANES:(hp + 1) * LANES]) / den)
        if h % 2 == 1:
            o_ref[rows, hp * LANES:(hp + 1) * LANES] = jnp.where(lane < MLA_V, outs[h - 1], outs[h])


def _attn_ctx_kernel(*refs, layer):
    if layer:
        mla_ref, pckvn_ref, pkr_ref = refs[:3]
        refs = refs[3:]
    else:
        mla_ref = refs[0]
        refs = refs[1:]
    pk_ref, wuq_ref, wk_ref, wv_ref, o_ref, ckvn_ref, kr_ref, k_s, v_s = refs
    nb, _, seq, _ = ckvn_ref.shape
    for b in range(nb):
        rows = slice(b * seq, (b + 1) * seq)
        for i in range(layer):
            ckvn_ref[b, i] = pckvn_ref[b, i]
            kr_ref[b, i] = pkr_ref[b, i]
        small = mla_ref[rows, MLA_SMALL]
        kr = jnp.where(_rope_lane_mask(small.shape), small, 0.0)
        kr_ref[b, layer] = small[:, ROPE_LANE0:ROPE_LANE0 + MLA_ROPE]
        ckvn = _rms_rows(mla_ref[rows, MLA_CKV], _pk(pk_ref, "kv_a_norm"))
        ckvn_ref[b, layer] = ckvn
        _build_kv(ckvn.astype(bf16), kr, wk_ref, wv_ref, _pk(pk_ref, "k_norm"), None, k_s, v_s, b * seq)
        qa = _rms_rows(mla_ref[rows, MLA_CQ], _pk(pk_ref, "q_a_norm")).astype(bf16)
        _attend_block(qa, wuq_ref, _pk(pk_ref, "q_norm"), None, k_s, v_s, o_ref, rows)


def _attn_lat_kernel(mla_ref, cckv_ref, ckr_ref, cos_ref, slo_ref, shi_ref,
                     pk_ref, wuq_ref, wk_ref, wv_ref,
                     o_ref, k_s, v_s, *, seq, past, tq):
    qi = pl.program_id(1)
    rb = 256

    @pl.when(qi == 0)
    def _():
        _build_kv(cckv_ref[...].astype(bf16), ckr_ref[...], wk_ref, wv_ref, _pk(pk_ref, "k_norm"), None,
                  k_s, v_s, 0)
        for r in range(seq // rb):
            rs = slice(r * rb, (r + 1) * rb)
            small = mla_ref[rs, MLA_SMALL]
            kr = jnp.where(_rope_lane_mask(small.shape), small, 0.0)
            ckvn = _rms_rows(mla_ref[rs, MLA_CKV], _pk(pk_ref, "kv_a_norm"))
            rope = (cos_ref[rs, :], slo_ref[rs, :], shi_ref[rs, :])
            _build_kv(ckvn.astype(bf16), kr, wk_ref, wv_ref, _pk(pk_ref, "k_norm"), rope, k_s, v_s, past + r * rb)

    rows = pl.ds(pl.multiple_of(qi * tq, tq), tq)
    qa = _rms_rows(mla_ref[rows, MLA_CQ], _pk(pk_ref, "q_a_norm")).astype(bf16)
    rope = (cos_ref[rows, :], slo_ref[rows, :], shi_ref[rows, :])
    _attend_block(qa, wuq_ref, _pk(pk_ref, "q_norm"), rope, k_s, v_s, o_ref)


def _attn_weight_specs(l, nidx):
    z = (0,) * (nidx - 1)

    def const(*idx):
        return lambda *g: idx

    return [_pk_spec(l),
            pl.BlockSpec((None, MLA_Q_RANK, MLA_HEADS * HEAD_SLOT), const(l, 0, 0)),
            pl.BlockSpec((None, MLA_KV_RANK, MLA_HEADS * HEAD_SLOT), const(l, 0, 0)),
            pl.BlockSpec((None, MLA_KV_RANK, MLA_WIDTH), const(l, 0, 0))]


def _attn_ctx(proj, prev, wts, l, batch, seq, nseq=1):
    ntok = batch * seq
    rows = nseq * seq

    def stacked(n, width):
        return pl.BlockSpec((nseq, n, seq, width), lambda b: (b, 0, 0, 0))

    prev_specs = [stacked(l, MLA_KV_RANK), stacked(l, MLA_ROPE)] if l else []
    return _Part(
        kernel=functools.partial(_attn_ctx_kernel, layer=l),
        in_specs=[pl.BlockSpec((rows, MLA_IN_W), lambda b: (b, OFF_SMALL // MLA_IN_W))]
        + prev_specs + _attn_weight_specs(l, 1),
        args=[proj, *prev, *wts],
        out_specs=[pl.BlockSpec((rows, MLA_WIDTH), lambda b: (b, 0)),
                   stacked(l + 1, MLA_KV_RANK), stacked(l + 1, MLA_ROPE)],
        out_shape=[jax.ShapeDtypeStruct((ntok, MLA_WIDTH), f32),
                   jax.ShapeDtypeStruct((batch, l + 1, seq, MLA_KV_RANK), f32),
                   jax.ShapeDtypeStruct((batch, l + 1, seq, MLA_ROPE), f32)],
        scratch_shapes=[pltpu.VMEM((rows, MLA_HEADS * HEAD_SLOT), bf16),
                        pltpu.VMEM((rows, MLA_WIDTH), bf16)])


def _attn_lat(proj, row0, cache_ckv, cache_kr_p, rope_tabs, wts, l, batch, seq, past):
    ntok = batch * seq
    tq = 256
    nq = seq // tq
    kern = functools.partial(_attn_lat_kernel, seq=seq, past=past, tq=tq)
    tab = pl.BlockSpec((seq, LANES), lambda b, q: (0, 0))
    return pl.pallas_call(
        kern,
        grid=(batch, nq),
        in_specs=[pl.BlockSpec((seq, MLA_IN_W), lambda b, q: (row0 // seq + b, OFF_SMALL // MLA_IN_W)),
                  pl.BlockSpec((None, None, past, MLA_KV_RANK), lambda b, q: (b, l, 0, 0)),
                  pl.BlockSpec((None, None, past, LANES), lambda b, q: (b, l, 0, 0)),
                  tab, tab, tab]
        + _attn_weight_specs(l, 2),
        out_specs=pl.BlockSpec((tq, MLA_WIDTH), lambda b, q: (b * nq + q, 0)),
        out_shape=jax.ShapeDtypeStruct((ntok, MLA_WIDTH), f32),
        scratch_shapes=[pltpu.VMEM((past + seq, MLA_HEADS * HEAD_SLOT), bf16),
                        pltpu.VMEM((past + seq, MLA_WIDTH), bf16)],
        compiler_params=_cparams(2),
        name="attn_lat",
    )(proj, cache_ckv, cache_kr_p, *rope_tabs, *wts)


def _split3(x):
    hi = x.astype(bf16)
    r1 = x - hi.astype(f32)
    mid = r1.astype(bf16)
    lo = (r1 - mid.astype(f32)).astype(bf16)
    return hi, mid, lo


def _tri_cumsum(tri_b, x):
    hi, mid, lo = _split3(x)
    return _dot(tri_b, hi) + _dot(tri_b, mid) + _dot(tri_b, lo)


def _lane_bcast(x, c):
    return jnp.broadcast_to(x[:, c:c + 1], (x.shape[0], LANES))


def _gdn_kernel(*refs, seq, nseq, has_state, nprev):
    gin_ref, cw_ref, pk_ref = refs[:3]
    refs = refs[3:]
    s0_ref = sprev_ref = sout_ref = None
    if has_state:
        s0_ref, o_ref = refs[:2]
        refs = refs[2:]
    else:
        if nprev:
            sprev_ref = refs[0]
            refs = refs[1:]
        o_ref, sout_ref = refs[:2]
        refs = refs[2:]
    xpad, qkv_s, g_s, b_s, st_s, wq_s, ak_s, u_s, el_s, rhs_s = refs
    C = GDN_CHUNK
    ncs = seq // C
    nchunk = nseq * ncs
    H = GDN_HEADS
    width = 2 * GDN_KW + GDN_VW
    halo = 8
    pseq = seq + 2 * halo

    for j in range(width // LANES):
        for sb in range(nseq):
            xpad[j, sb * pseq:sb * pseq + halo, :] = jnp.zeros((halo, LANES), f32)
            xpad[j, sb * pseq + halo + seq:(sb + 1) * pseq, :] = jnp.zeros((halo, LANES), f32)
            xpad[j, sb * pseq + halo:sb * pseq + halo + seq, :] = gin_ref[sb * seq:(sb + 1) * seq,
                                                                          j * LANES:(j + 1) * LANES]
    if has_state:
        for d in range(2):
            for h in range(H):
                st_s[d * H + h] = s0_ref[d, h]
    else:
        st_s[...] = jnp.zeros((nseq * 2 * H, GDN_DK, GDN_DV), f32)

    neg_a = -jnp.exp(_pk(pk_ref, "a_log"))
    dtb = _pk(pk_ref, "dt_bias")

    def conv_tile(j, l2norm):
        w = cw_ref[j]
        post = jnp.where(j < H, GDN_DK ** -0.5, 1.0)
        for c in range(nchunk):
            base = (c // ncs) * pseq + halo - CONV_W // 2 + (c % ncs) * C
            y = xpad[j, base:base + C, :] * w[0:1]
            for tap in range(1, CONV_W):
                y = y + xpad[j, base + tap:base + tap + C, :] * w[tap:tap + 1]
            y = y * jax.nn.sigmoid(y)
            if l2norm:
                y = y * (lax.rsqrt(jnp.sum(y * y, axis=-1, keepdims=True) + EPS) * post)
            qkv_s[j, c * C:(c + 1) * C, :] = y

    def conv_qk(j, carry):
        conv_tile(j, True)
        return carry

    def conv_v(j, carry):
        conv_tile(j, False)
        return carry

    tile_unroll = 4 if nchunk <= 4 else (2 if nchunk <= 8 else 1)
    lax.fori_loop(0, 2 * H, conv_qk, 0, unroll=tile_unroll)
    lax.fori_loop(2 * H, 3 * H, conv_v, 0, unroll=tile_unroll)

    def prep(c, carry):
        r0 = pl.multiple_of(c * C, C)
        sm = gin_ref[pl.ds(r0, C), OFF_SMALL:OFF_SMALL + LANES]
        z = sm + dtb
        g_s[pl.ds(r0, C), :] = neg_a * (jnp.maximum(z, 0.0) + jnp.log1p(jnp.exp(-jnp.abs(z))))
        b_s[pl.ds(r0, C), :] = pltpu.roll(jax.nn.sigmoid(sm), LANES - 2 * H, 1)
        return carry

    lax.fori_loop(0, nchunk, prep, 0, unroll=4)

    ri = lax.broadcasted_iota(jnp.int32, (C, LANES), 0)
    cl = lax.broadcasted_iota(jnp.int32, (C, LANES), 1)
    fwd = cl < C
    cj = cl & (C - 1)
    eye2 = (ri == cj).astype(f32)
    incl2 = (fwd & (ri >= cj)) | (~fwd & (ri <= cj))
    strict2 = (fwd & (ri > cj)) | (~fwd & (ri < cj))
    xor = ri ^ cj
    level2 = sum((xor >= (1 << b)).astype(jnp.int32) for b in range(C.bit_length() - 1))
    lvl_top = jnp.where(fwd, level2, 0)
    lvl_bot = jnp.where(fwd, 0, level2)
    r2 = lax.broadcasted_iota(jnp.int32, (2 * C, C), 0)
    c2 = lax.broadcasted_iota(jnp.int32, (2 * C, C), 1)
    tri2 = (((r2 < C) & (r2 >= c2)) | ((r2 >= C) & (r2 - C <= c2))).astype(bf16)
    zrhs = jnp.zeros((C, 2 * LANES), bf16)
    zvn = jnp.zeros((C, LANES), bf16)
    fwd_row = fwd[0:1, :]
    cpi = min(GDN_SOLVE_CHUNKS, nchunk)

    def block_diag(x):
        return jnp.concatenate([jnp.where(fwd, x, 0.0), jnp.where(fwd, 0.0, x)], axis=0).astype(bf16)

    def solve_phase(i, carry):
        chains = []
        for cc in range(cpi):
            c = i * cpi + cc
            rows = pl.ds(pl.multiple_of(c * C, C), C)
            g2 = _tri_cumsum(tri2, g_s[rows, :])
            g2t = g2.T
            bt = b_s[rows, :]
            for h in range(H):
                chains.append((cc, c, h, rows, g2, g2t, bt))

        a2s, t2s = [], []
        for (cc, c, h, rows, g2, g2t, bt) in chains:
            q = qkv_s[h, rows, :]
            k = qkv_s[H + h, rows, :]
            v = qkv_s[2 * H + h, rows, :]
            kq = _dot_nt(jnp.concatenate([k, q], axis=0).astype(bf16),
                         jnp.concatenate([k, k], axis=0).astype(bf16))
            gcc_f = _lane_bcast(g2[:C], h)
            gcc_b = _lane_bcast(g2[C:], H + h)
            btc_f = _lane_bcast(bt, h)
            btc_b = _lane_bcast(bt, H + h)
            grow = jnp.where(fwd_row, g2t[h:h + 1, :], g2t[H + h:H + h + 1, :])
            diff = jnp.where(fwd, gcc_f, gcc_b) - grow
            dec = jnp.where(incl2, jnp.exp(jnp.where(incl2, diff, 0.0)), 0.0)
            a2 = jnp.where(strict2, jnp.where(fwd, btc_f, btc_b) * kq[:C] * dec, 0.0)
            a2s.append(a2)
            t2s.append(eye2 - jnp.where(level2 == 1, a2, 0.0))
            glast_f = gcc_f[C - 1:C, :]
            glast_b = gcc_b[0:1, :]
            e1_f = jnp.exp(gcc_f)
            e1_b = jnp.exp(gcc_b)
            ket = jnp.concatenate([k * jnp.exp(glast_f - gcc_f), k * jnp.exp(glast_b - gcc_b)], axis=0).T
            ak_s[pl.ds(pl.multiple_of((c * H + h) * 3 * C, 3 * C), 3 * C), :] = jnp.concatenate(
                [kq[C:] * dec, ket], axis=0).astype(bf16)
            for d, (btc, e1, glast) in enumerate(((btc_f, e1_f, glast_f), (btc_b, e1_b, glast_b))):
                ch = d * H + h
                j = (cc * H + h) * 2 + d
                rhs_s[j * C:(j + 1) * C, :] = jnp.concatenate([v * btc, k * (btc * e1)], axis=1).astype(bf16)
                wq_s[pl.ds(pl.multiple_of((c * 2 * H + ch) * 2 * C + C, C), C), :] = (q * e1).astype(bf16)
                el_s[pl.ds(pl.multiple_of((c * 2 * H + ch) * 8, 8), 8), :] = jnp.broadcast_to(
                    jnp.exp(glast), (8, LANES))

        for g0 in range(0, len(chains), GDN_LEVEL_CHAINS):
            grp = slice(g0, g0 + GDN_LEVEL_CHAINS)
            ga2, gt2 = a2s[grp], t2s[grp]
            for lv in range(2, C.bit_length()):
                rs = []
                for a2, t2 in zip(ga2, gt2):
                    abd = jnp.concatenate([jnp.where(lvl_top == lv, a2, 0.0),
                                           jnp.where(lvl_bot == lv, a2, 0.0)], axis=0).astype(bf16)
                    rs.append(_dot(t2.astype(bf16), abd))
                gt2 = [t2 - _dot(r.astype(bf16), block_diag(t2)) for r, t2 in zip(rs, gt2)]

            for (cc, c, h, rows, g2, g2t, bt), t2 in zip(chains[grp], gt2):
                t2b = t2.astype(bf16)
                for d in range(2):
                    ch = d * H + h
                    j = (cc * H + h) * 2 + d
                    rhs = rhs_s[j * C:(j + 1) * C, :]
                    rhs = jnp.concatenate([rhs, zrhs] if d == 0 else [zrhs, rhs], axis=0)
                    uw = _dot(t2b, rhs)
                    u_s[pl.ds(pl.multiple_of((c * 2 * H + ch) * C, C), C), :] = uw[:, :LANES]
                    wq_s[pl.ds(pl.multiple_of((c * 2 * H + ch) * 2 * C, C), C), :] = uw[:, LANES:].astype(bf16)
        return carry

    lax.fori_loop(0, nchunk // cpi, solve_phase, 0)

    def scan_phase(i, carry):
        chs = [ch for _ in range(nseq) for ch in range(2 * H)]
        cs = [sb * ncs + (i if ch < H else ncs - 1 - i) for sb in range(nseq) for ch in range(2 * H)]
        nk = len(chs)
        s_old = [st_s[k] for k in range(nk)]
        r1 = [_dot(wq_s[pl.ds(pl.multiple_of((cs[k] * 2 * H + chs[k]) * 2 * C, 2 * C), 2 * C), :],
                   s_old[k].astype(bf16)) for k in range(nk)]
        r2s = []
        for k in range(nk):
            u = u_s[pl.ds(pl.multiple_of((cs[k] * 2 * H + chs[k]) * C, C), C), :]
            vnb = (u - r1[k][:C]).astype(bf16)
            rhs = jnp.concatenate([vnb, zvn] if chs[k] < H else [zvn, vnb], axis=0)
            ak = ak_s[pl.ds(pl.multiple_of((cs[k] * H + chs[k] % H) * 3 * C, 3 * C), 3 * C), :]
            r2s.append(_dot(ak, rhs))
        for k in range(nk):
            el = el_s[pl.ds(pl.multiple_of((cs[k] * 2 * H + chs[k]) * 8, 8), 8), :][0:1, :]
            st_s[k] = s_old[k] * el + r2s[k][C:]
            xpad[chs[k], pl.ds(pl.multiple_of(cs[k] * C, C), C), :] = r1[k][C:] + r2s[k][:C]
        return carry

    lax.fori_loop(0, ncs, scan_phase, 0, unroll=4)

    onorm = _pk(pk_ref, "gdn_onorm")

    def fin(c, carry):
        rows = pl.ds(pl.multiple_of(c * C, C), C)
        for h in range(H):
            ls = slice(h * LANES, (h + 1) * LANES)
            o_ref[rows, ls] = _rms_rows(xpad[h, rows, :] + xpad[H + h, rows, :], onorm)
        return carry

    lax.fori_loop(0, nchunk, fin, 0, unroll=4)
    if sout_ref is not None:
        for sb in range(nseq):
            for i in range(nprev):
                sout_ref[sb, i] = sprev_ref[sb, i]
            for d in range(2):
                for h in range(H):
                    sout_ref[sb, nprev, d, h] = st_s[sb * 2 * H + d * H + h]


def _gdn(proj, row0, conv_w, pack, state, prev_states, l, batch, seq, nseq=1):
    ntok = batch * seq
    width = 2 * GDN_KW + GDN_VW
    has_state = state is not None
    nprev = 0 if has_state else l
    nchunk = nseq * (seq // GDN_CHUNK)
    rows = nseq * seq
    kern = functools.partial(_gdn_kernel, seq=seq, nseq=nseq, has_state=has_state, nprev=nprev)
    in_specs = [pl.BlockSpec((rows, GDN_IN_W), lambda b: (row0 // rows + b, OFF_QKV // GDN_IN_W)),
                pl.BlockSpec((None, width // LANES, CONV_W, LANES), lambda b: (l, 0, 0, 0)),
                _pk_spec(l)]
    args = [proj, conv_w, pack]
    o_spec = pl.BlockSpec((rows, GDN_VW), lambda b: (b, 0))
    o_shape = jax.ShapeDtypeStruct((ntok, GDN_VW), f32)
    st_block = (None, None, 2, GDN_HEADS, GDN_DK, GDN_DV)
    if has_state:
        assert nseq == 1
        in_specs.append(pl.BlockSpec(st_block, lambda b: (b, l, 0, 0, 0, 0)))
        args.append(state)
        out_specs, out_shape = [o_spec], [o_shape]
    else:
        def stacked(n):
            return pl.BlockSpec((nseq, n) + st_block[2:], lambda b: (b, 0, 0, 0, 0, 0))

        if nprev:
            in_specs.append(stacked(nprev))
            args.append(prev_states)
        out_specs = [o_spec, stacked(nprev + 1)]
        out_shape = [o_shape, jax.ShapeDtypeStruct((batch, nprev + 1, 2, GDN_HEADS, GDN_DK, GDN_DV), f32)]
    return _Part(
        kernel=kern,
        in_specs=in_specs,
        args=args,
        out_specs=out_specs,
        out_shape=out_shape,
        scratch_shapes=[pltpu.VMEM((width // LANES, nseq * (seq + 16), LANES), f32),
                        pltpu.VMEM((width // LANES, rows, LANES), f32),
                        pltpu.VMEM((rows, LANES), f32),
                        pltpu.VMEM((rows, LANES), f32),
                        pltpu.VMEM((nseq * 2 * GDN_HEADS, GDN_DK, GDN_DV), f32),
                        pltpu.VMEM((nchunk * 2 * GDN_HEADS * 2 * GDN_CHUNK, LANES), bf16),
                        pltpu.VMEM((nchunk * GDN_HEADS * 3 * GDN_CHUNK, LANES), bf16),
                        pltpu.VMEM((nchunk * 2 * GDN_HEADS * GDN_CHUNK, LANES), f32),
                        pltpu.VMEM((nchunk * 2 * GDN_HEADS * 8, LANES), f32),
                        pltpu.VMEM((min(GDN_SOLVE_CHUNKS, nchunk) * 2 * GDN_HEADS * GDN_CHUNK, 2 * LANES), bf16)])


def _merge_kernel(x_ref, mod_ref, oa_ref, ob_ref, p_ref, pk_ref, ws_ref, bs_ref, wbr_ref, wo_ref,
                  out_ref, sv_s):
    tm = x_ref.shape[0]

    def group(off, width=512):
        return p_ref[:, off:off + width].astype(f32)

    def gelu(x):
        c1 = math.sqrt(2.0 / math.pi)
        half = 0.5 * x
        return half + half * jnp.tanh(x * (c1 + (c1 * 0.044715) * (x * x)))

    u = gelu(group(OFF_CU))
    vf = gelu(group(OFF_CV))
    mu = jnp.mean(vf, axis=-1, keepdims=True)
    vc = vf - mu
    var = jnp.mean(vc * vc, axis=-1, keepdims=True)
    vn = (vc * lax.rsqrt(var + EPS) * _pk(pk_ref, "cm_ln_g") + _pk(pk_ref, "cm_ln_b")).astype(bf16)
    for ck in range(tm // CM_CHUNK):
        rs = slice(ck * CM_CHUNK, (ck + 1) * CM_CHUNK)
        for g in range(CM_GROUPS):
            ls = slice(g * LANES, (g + 1) * LANES)
            sv_s[rs, ls] = _dot(ws_ref[g], vn[rs, ls]) + bs_ref[:, ls]
    o_c = u * sv_s[...]

    def silu(z):
        return z * jax.nn.sigmoid(z)

    brs = (oa_ref[...] * silu(group(OFF_ZA)), ob_ref[...] * silu(group(OFF_ZB)), o_c * silu(group(OFF_ZC)))
    ysum = None
    for n in range(N_BRANCH):
        yb = _dot(brs[n].astype(bf16), wbr_ref[n])
        t = jax.nn.sigmoid(group(OFF_GL + n * D_MODEL, D_MODEL)) * yb
        ysum = t if ysum is None else ysum + t
    y = _dot(ysum.astype(bf16), wo_ref[...])
    gate = mod_ref[0][:, 2 * D_MODEL:]
    out_ref[...] = x_ref[...] + gate * y


def _merge(x, mod_l, proj, row0, o_a, o_b, pack, ws_b, bs_full, wbr_b, wo_b, l, latent, seq):
    ntok = x.shape[0]
    tm = 512

    def const(*idx):
        return lambda t: idx

    return pl.pallas_call(
        _merge_kernel,
        grid=(ntok // tm,),
        in_specs=[pl.BlockSpec((tm, D_MODEL), lambda t: (t, 0)),
                  pl.BlockSpec((None, 1, 1, 3 * D_MODEL), _mod_row(l, latent, tm, seq)),
                  pl.BlockSpec((tm, 512), lambda t: (t, 0)),
                  pl.BlockSpec((tm, 512), lambda t: (t, 0)),
                  pl.BlockSpec((tm, P16_W), lambda t: (row0 // tm + t, 0)),
                  _pk_spec(l),
                  pl.BlockSpec((None, CM_GROUPS, CM_CHUNK, CM_CHUNK), const(l, 0, 0, 0)),
                  pl.BlockSpec((None, CM_CHUNK, CM_WIDTH), const(l, 0, 0)),
                  pl.BlockSpec((None, N_BRANCH, BRANCH_W, D_MODEL), const(l, 0, 0, 0)),
                  pl.BlockSpec((None, D_MODEL, D_MODEL), const(l, 0, 0))],
        out_specs=pl.BlockSpec((tm, D_MODEL), lambda t: (t, 0)),
        out_shape=jax.ShapeDtypeStruct((ntok, D_MODEL), f32),
        scratch_shapes=[pltpu.VMEM((tm, CM_WIDTH), f32)],
        compiler_params=_cparams(1),
        name="merge",
    )(x, mod_l, o_a, o_b, proj, pack, ws_b, bs_full, wbr_b, wo_b)


def _pad_last(x, n):
    return jnp.pad(x, [(0, 0)] * (x.ndim - 1) + [(0, n - x.shape[-1])])


def _rope_tables(seq):
    t = np.arange(seq)
    row = (t // GRID_W).astype(np.float32)
    colp = (t % GRID_W).astype(np.float32)
    nf = MLA_ROPE // 4
    inv = (ROPE_THETA ** (-np.arange(nf, dtype=np.float32) / nf)).astype(np.float32)
    cos_t = np.ones((seq, LANES), np.float32)
    s_lo = np.zeros((seq, LANES), np.float32)
    s_hi = np.zeros((seq, LANES), np.float32)
    for i, pos in enumerate((row, colp)):
        ang = (pos[:, None] * inv[None, :]).astype(np.float32)
        cs, sn = np.cos(ang), np.sin(ang)
        lo = ROPE_LANE0 + 2 * nf * i
        cos_t[:, lo:lo + nf] = cs
        cos_t[:, lo + nf:lo + 2 * nf] = cs
        s_lo[:, lo:lo + nf] = -sn
        s_hi[:, lo + nf:lo + 2 * nf] = sn
    return tuple(jnp.asarray(p) for p in (cos_t, s_lo, s_hi))


def kernel(x_prompt, x_sample, cache_ckv, cache_krope, state_gdn, c, c_ctx, norm_g, w_mod, b_mod, w_in, q_a_norm, w_uq, kv_a_norm, w_ukv, q_norm, k_norm, conv_w, a_log, dt_bias, gdn_onorm, cm_ln_g, cm_ln_b, w_s, b_s, w_branch, w_o):
    L = DEPTH
    batch, seq, _ = x_prompt.shape
    dbatch, dseq, _ = x_sample.shape
    past = cache_ckv.shape[2]

    w_t = jnp.swapaxes(w_in, 1, 2)
    wuq_p = _pad_last(w_uq.reshape(L, MLA_Q_RANK, MLA_HEADS, MLA_QK), HEAD_SLOT)
    wuq_p = wuq_p.reshape(L, MLA_Q_RANK, MLA_HEADS * HEAD_SLOT).astype(bf16)
    wukv = w_ukv.reshape(L, MLA_KV_RANK, MLA_HEADS, MLA_NOPE + MLA_V)
    wk_p = _pad_last(wukv[..., :MLA_NOPE], HEAD_SLOT).reshape(L, MLA_KV_RANK, MLA_HEADS * HEAD_SLOT).astype(bf16)
    wv_p = wukv[..., MLA_NOPE:].reshape(L, MLA_KV_RANK, MLA_WIDTH).astype(bf16)
    cache_kr_p = jnp.pad(cache_krope, [(0, 0)] * 3 + [(ROPE_LANE0, LANES - ROPE_LANE0 - MLA_ROPE)])
    rope_tabs = _rope_tables(dseq)
    conv_w = jnp.swapaxes(conv_w.reshape(L, CONV_W, -1, LANES), 1, 2)
    ws_b = w_s.astype(bf16)
    bs_full = jnp.repeat(jnp.swapaxes(b_s, 1, 2), CM_WIDTH // CM_GROUPS, axis=2)
    gates8 = [_pad_last(v.reshape(L, 2 * GDN_HEADS), LANES) for v in (a_log, dt_bias)]
    pack = jnp.stack([norm_g,
                      jnp.concatenate([q_a_norm, kv_a_norm, _pad_last(q_norm, HEAD_SLOT),
                                       _pad_last(k_norm, HEAD_SLOT), gdn_onorm], axis=-1),
                      jnp.concatenate([cm_ln_g, cm_ln_b], axis=-1),
                      _pad_last(jnp.concatenate(gates8, axis=-1), D_MODEL)], axis=1)
    attn_w = (pack, wuq_p, wk_p, wv_p)

    c8 = jnp.concatenate([c, c_ctx[None, :], jnp.zeros((8 - dbatch - 1, D_MODEL), f32)], axis=0)
    wbr_b = w_branch.astype(bf16)
    wo_b = w_o.astype(bf16)
    mod = _modulation(c8, w_mod, b_mod)

    yp = x_prompt.reshape(batch * seq, D_MODEL)
    ys = x_sample.reshape(dbatch * dseq, D_MODEL)
    caches, states = (), None
    for l in range(L):
        mod_l = mod
        p16, p32 = _inproj(yp, ys, mod_l, pack, w_t, l, dseq)
        lat0 = batch * seq
        (o_a, new_ckv, new_kr), (o_b, states) = _run_parts(
            [_attn_ctx(p32, caches, attn_w, l, batch, seq, CTX_SEQS_PER_STEP),
             _gdn(p32, 0, conv_w, pack, None, states, l, batch, seq, CTX_SEQS_PER_STEP)],
            (batch // CTX_SEQS_PER_STEP,), "mix_ctx")
        caches = (new_ckv, new_kr)
        yp = _merge(yp, mod_l, p16, 0, o_a, o_b, pack, ws_b, bs_full, wbr_b, wo_b, l, False, seq)
        o_a = _attn_lat(p32, lat0, cache_ckv, cache_kr_p, rope_tabs, attn_w, l, dbatch, dseq, past)
        ((o_b,),) = _run_parts([_gdn(p32, lat0, conv_w, pack, state_gdn, None, l, dbatch, dseq)], (dbatch,), "gdn_lat")
        ys = _merge(ys, mod_l, p16, lat0, o_a, o_b, pack, ws_b, bs_full, wbr_b, wo_b, l, True, dseq)
    return (yp.reshape(batch, seq, D_MODEL), ys.reshape(dbatch, dseq, D_MODEL), caches[0], caches[1], states)
```

```python
import collections
import functools
import math

import numpy as np

import jax
import jax.numpy as jnp
from jax import lax
from jax.experimental import pallas as pl
from jax.experimental.pallas import tpu as pltpu

D_MODEL = 1024
DEPTH = 2
GRID_W = 64
EPS = 1e-6
MLA_HEADS = 8
MLA_NOPE = 64
MLA_ROPE = 32
MLA_QK = MLA_NOPE + MLA_ROPE
MLA_V = 64
MLA_Q_RANK = 384
MLA_KV_RANK = 256
MLA_WIDTH = MLA_HEADS * MLA_V
ROPE_THETA = 10000.0
GDN_HEADS = 4
GDN_DK = 128
GDN_DV = 128
GDN_KW = GDN_HEADS * GDN_DK
GDN_VW = GDN_HEADS * GDN_DV
GDN_CHUNK = 64
CONV_W = 5
GDN_SOLVE_CHUNKS = 16
GDN_LEVEL_CHAINS = 16
CTX_SEQS_PER_STEP = 2
CM_GROUPS = 4
CM_CHUNK = 128
CM_WIDTH = 512
N_BRANCH = 3
BRANCH_W = 512
SPLIT_SIZES = (MLA_Q_RANK, MLA_KV_RANK, MLA_ROPE, MLA_WIDTH,
               2 * GDN_KW + GDN_VW, 2 * GDN_HEADS, 2 * GDN_HEADS, GDN_VW,
               CM_WIDTH, CM_WIDTH, CM_WIDTH, N_BRANCH * D_MODEL)

LANES = 128
SUBLANES = 8
HEAD_SLOT = LANES
ROPE_LANE0 = MLA_NOPE

OFF_GL = 0
OFF_ZA = OFF_GL + N_BRANCH * D_MODEL
OFF_ZB = OFF_ZA + 512
OFF_CU = OFF_ZB + 512
OFF_CV = OFF_CU + 512
OFF_ZC = OFF_CV + 512
P16_W = OFF_ZC + 512
OFF_QKV = 0
OFF_SMALL = OFF_QKV + 2 * GDN_KW + GDN_VW
OFF_CQ = OFF_SMALL + LANES
OFF_CKV = OFF_CQ + MLA_Q_RANK
P32_W = OFF_CKV + MLA_KV_RANK
PROJ_W = P16_W + P32_W
GDN_IN_W = OFF_CQ
MLA_IN_W = P32_W - OFF_SMALL
MLA_SMALL = slice(0, LANES)
MLA_CQ = slice(OFF_CQ - OFF_SMALL, OFF_CKV - OFF_SMALL)
MLA_CKV = slice(OFF_CKV - OFF_SMALL, MLA_IN_W)

VMEM_LIMIT = 56 * 1024 * 1024

PK_ROWS = 4
PK = {}
for _row, _fields in enumerate((
        (("norm_g", D_MODEL),),
        (("q_a_norm", MLA_Q_RANK), ("kv_a_norm", MLA_KV_RANK), ("q_norm", HEAD_SLOT), ("k_norm", HEAD_SLOT),
         ("gdn_onorm", GDN_DV)),
        (("cm_ln_g", CM_WIDTH), ("cm_ln_b", CM_WIDTH)),
        (("a_log", LANES), ("dt_bias", LANES)))):
    _lane = 0
    for _name, _width in _fields:
        PK[_name] = (_row, _lane, _width)
        _lane += _width
    assert _lane <= D_MODEL


def _pk(pk_ref, name):
    row, lane0, width = PK[name]
    return pk_ref[row:row + 1, lane0:lane0 + width]


def _pk_spec(l):
    return pl.BlockSpec((None, PK_ROWS, D_MODEL), lambda *g: (l, 0, 0))

f32 = jnp.float32
bf16 = jnp.bfloat16


def _cparams(n_axes):
    return pltpu.CompilerParams(dimension_semantics=("arbitrary",) * n_axes,
                                vmem_limit_bytes=VMEM_LIMIT)


_Part = collections.namedtuple("_Part", "kernel in_specs args out_specs out_shape scratch_shapes")


def _run_parts(parts, grid, name):
    n_in = [len(p.in_specs) for p in parts]
    n_out = [len(p.out_specs) for p in parts]
    n_scr = [len(p.scratch_shapes) for p in parts]

    def body(*refs):
        ins, outs, scr = refs[:sum(n_in)], refs[sum(n_in):sum(n_in) + sum(n_out)], refs[sum(n_in) + sum(n_out):]
        i = o = s = 0
        for p, ni, no, ns in zip(parts, n_in, n_out, n_scr):
            p.kernel(*ins[i:i + ni], *outs[o:o + no], *scr[s:s + ns])
            i, o, s = i + ni, o + no, s + ns

    res = pl.pallas_call(
        body,
        grid=grid,
        in_specs=[sp for p in parts for sp in p.in_specs],
        out_specs=[sp for p in parts for sp in p.out_specs],
        out_shape=[sh for p in parts for sh in p.out_shape],
        scratch_shapes=[sc for p in parts for sc in p.scratch_shapes],
        compiler_params=_cparams(len(grid)),
        name=name,
    )(*[a for p in parts for a in p.args])
    out, o = [], 0
    for no in n_out:
        out.append(res[o:o + no])
        o += no
    return out


def _dot(a, b):
    return jnp.dot(a, b, preferred_element_type=f32)


def _dot_nt(a, b):
    return lax.dot_general(a, b, (((1,), (1,)), ((), ())), preferred_element_type=f32)


def _rms_rows(x, g, n=None):
    n = x.shape[-1] if n is None else n
    ms = jnp.sum(x * x, axis=-1, keepdims=True) * (1.0 / n)
    return x * lax.rsqrt(ms + EPS) * g


def _mod_kernel(c_ref, w_ref, b_ref, o_ref):
    a = c_ref[...]
    a = (a * jax.nn.sigmoid(a)).astype(bf16)
    y = _dot(a, w_ref[0].astype(bf16)) + b_ref[0]
    for r in range(y.shape[0]):
        o_ref[0, r] = y[r:r + 1]


def _modulation(c8, w_mod, b_mod):
    tn = 1536
    return pl.pallas_call(
        _mod_kernel,
        grid=(DEPTH, 3 * D_MODEL // tn),
        in_specs=[pl.BlockSpec((8, D_MODEL), lambda l, n: (0, 0)),
                  pl.BlockSpec((1, D_MODEL, tn), lambda l, n: (l, 0, n)),
                  pl.BlockSpec((1, 1, tn), lambda l, n: (l, 0, n))],
        out_specs=pl.BlockSpec((1, 8, 1, tn), lambda l, n: (l, 0, 0, n)),
        out_shape=jax.ShapeDtypeStruct((DEPTH, 8, 1, 3 * D_MODEL), f32),
        compiler_params=_cparams(2),
        name="modulation",
    )(c8, w_mod, b_mod.reshape(DEPTH, 1, 3 * D_MODEL))


W_CHUNK = 512


def _w_in_moves():
    offs = [0]
    for s in SPLIT_SIZES:
        offs.append(offs[-1] + s)
    cq, ckv, krope, z_a, qkv, ga, gb, z_b, cu, cv, z_c, gl = offs[:-1]
    moves = [(gl, N_BRANCH * D_MODEL, OFF_GL), (z_a, 512, OFF_ZA), (z_b, 512, OFF_ZB), (cu, 512, OFF_CU),
             (cv, 512, OFF_CV), (z_c, 512, OFF_ZC), (qkv, 2 * GDN_KW + GDN_VW, P16_W + OFF_QKV),
             (cq, MLA_Q_RANK, P16_W + OFF_CQ), (ckv, MLA_KV_RANK, P16_W + OFF_CKV)]
    return moves, ga, krope


def _w_chunks():
    moves, ga, krope = _w_in_moves()
    chunks = []
    for (a, w, d) in moves:
        for o in range(0, w, W_CHUNK):
            chunks.append((a + o, min(W_CHUNK, w - o), d + o))
    small0 = P16_W + OFF_SMALL
    chunks.append((ga, 4 * GDN_HEADS, small0))
    chunks.append((krope, MLA_ROPE, small0 + ROPE_LANE0))
    return chunks


def _inproj_kernel(xc_ref, xl_ref, mod_ref, pk_ref, wt_hbm, o16_ref, o32_ref, w_s, stage, sem, *, layer, nctx):
    t = pl.program_id(0)

    @pl.when(t == 0)
    def _():
        chunks = _w_chunks()

        def copy(j):
            src, n, _ = chunks[j]
            return pltpu.make_async_copy(wt_hbm.at[layer, pl.ds(src, n), :], stage.at[j % 2, pl.ds(0, n), :],
                                         sem.at[j % 2])

        copy(0).start()
        for j, (_, n, dst) in enumerate(chunks):
            if j + 1 < len(chunks):
                copy(j + 1).start()
            copy(j).wait()
            w_s[dst:dst + n, :] = stage[j % 2, 0:n, :].astype(bf16)
        small0 = P16_W + OFF_SMALL
        for lo, hi in ((4 * GDN_HEADS, ROPE_LANE0), (ROPE_LANE0 + MLA_ROPE, LANES)):
            w_s[small0 + lo:small0 + hi, :] = jnp.zeros((hi - lo, D_MODEL), bf16)

    m = mod_ref[0]
    shift = m[:, :D_MODEL]
    scale = m[:, D_MODEL:2 * D_MODEL]
    half = xc_ref.shape[0] // 2
    for r0 in (0, half):
        rs = slice(r0, r0 + half)
        x = jnp.where(t < nctx, xc_ref[rs, :], xl_ref[rs, :])
        h = (_rms_rows(x, _pk(pk_ref, "norm_g")) * (1.0 + scale) + shift).astype(bf16)
        for a in range(0, P16_W, 512):
            o16_ref[rs, a:a + 512] = _dot_nt(h, w_s[a:a + 512, :]).astype(bf16)
        for a in range(0, P32_W, 512):
            b = min(a + 512, P32_W)
            o32_ref[rs, a:b] = _dot_nt(h, w_s[P16_W + a:P16_W + b, :])


def _inproj(xc, xl, mod_l, norm_g, w_t, l, dseq):
    tm = 512
    nctx = xc.shape[0] // tm
    nlat = xl.shape[0] // tm
    ntok = xc.shape[0] + xl.shape[0]
    kern = functools.partial(_inproj_kernel, layer=l, nctx=nctx)
    return pl.pallas_call(
        kern,
        grid=(nctx + nlat,),
        in_specs=[pl.BlockSpec((tm, D_MODEL), lambda t: (jnp.minimum(t, nctx - 1), 0)),
                  pl.BlockSpec((tm, D_MODEL), lambda t: (jnp.maximum(t - nctx, 0), 0)),
                  pl.BlockSpec((None, 1, 1, 3 * D_MODEL),
                               lambda t: (l, jnp.where(t < nctx, 4, ((t - nctx) * tm) // dseq), 0, 0)),
                  _pk_spec(l),
                  pl.BlockSpec(memory_space=pl.ANY)],
        out_specs=[pl.BlockSpec((tm, P16_W), lambda t: (t, 0)),
                   pl.BlockSpec((tm, P32_W), lambda t: (t, 0))],
        out_shape=[jax.ShapeDtypeStruct((ntok, P16_W), bf16),
                   jax.ShapeDtypeStruct((ntok, P32_W), f32)],
        scratch_shapes=[pltpu.VMEM((PROJ_W, D_MODEL), bf16),
                        pltpu.VMEM((2, W_CHUNK, D_MODEL), f32),
                        pltpu.SemaphoreType.DMA((2,))],
        compiler_params=_cparams(1),
        name="inproj",
    )(xc, xl, mod_l, norm_g, w_t)


def _mod_row(l, latent, tm, seq):
    if latent:
        return lambda t: (l, (t * tm) // seq, 0, 0)
    return lambda t: (l, 4, 0, 0)


def _rope(x, cos_t, sin_lo, sin_hi):
    return x * cos_t + pltpu.roll(x, LANES - 8, 1) * sin_lo + pltpu.roll(x, 8, 1) * sin_hi


def _build_kv(ckvn_b, kr, wk_ref, wv_ref, knorm, rope, k_s, v_s, r0):
    n = ckvn_b.shape[0]
    kfull = _dot(ckvn_b, wk_ref[...])
    v_s[r0:r0 + n, :] = _dot(ckvn_b, wv_ref[...]).astype(bf16)
    krg = kr * knorm
    if rope is not None:
        krg = _rope(krg, *rope)
    kr_ss = jnp.sum(kr * kr, axis=-1, keepdims=True)
    for h in range(MLA_HEADS):
        sl = slice(h * HEAD_SLOT, (h + 1) * HEAD_SLOT)
        kn = kfull[:, sl]
        ms = (jnp.sum(kn * kn, axis=-1, keepdims=True) + kr_ss) * (1.0 / MLA_QK)
        k_s[r0:r0 + n, sl] = ((kn * knorm + krg) * lax.rsqrt(ms + EPS)).astype(bf16)


def _rope_lane_mask(shape):
    lane = lax.broadcasted_iota(jnp.int32, shape, 1)
    return (lane >= ROPE_LANE0) & (lane < ROPE_LANE0 + MLA_ROPE)


def _attend_block(qa_b, wuq_ref, qnorm, rope, k_s, v_s, o_ref, rows=slice(None)):
    tq = qa_b.shape[0]
    qfull = _dot(qa_b, wuq_ref[...])
    qgain = qnorm * (math.log2(math.e) / math.sqrt(MLA_QK))
    lane = lax.broadcasted_iota(jnp.int32, (tq, LANES), 1)

    def scores(h):
        sl = slice(h * HEAD_SLOT, (h + 1) * HEAD_SLOT)
        qh = _rms_rows(qfull[:, sl], qgain, n=MLA_QK)
        if rope is not None:
            qh = _rope(qh, *rope)
        return _dot_nt(qh.astype(bf16), k_s[rows, sl])

    s_next = scores(0)
    outs = []
    for h in range(MLA_HEADS):
        s = s_next
        if h + 1 < MLA_HEADS:
            s_next = scores(h + 1)
        p = jnp.exp2(s - jnp.max(s, axis=-1, keepdims=True))
        den = jnp.sum(p, axis=-1, keepdims=True)
        hp = h // 2
        outs.append(_dot(p.astype(bf16), v_s[rows, hp * LANES:(hp + 1) * LANES]) / den)
        if h % 2 == 1:
            o_ref[rows, hp * LANES:(hp + 1) * LANES] = jnp.where(lane < MLA_V, outs[h - 1], outs[h])


def _attn_ctx_kernel(*refs, layer):
    if layer:
        mla_ref, pckvn_ref, pkr_ref = refs[:3]
        refs = refs[3:]
    else:
        mla_ref = refs[0]
        refs = refs[1:]
    pk_ref, wuq_ref, wk_ref, wv_ref, o_ref, ckvn_ref, kr_ref, k_s, v_s = refs
    nb, _, seq, _ = ckvn_ref.shape
    for b in range(nb):
        rows = slice(b * seq, (b + 1) * seq)
        for i in range(layer):
            ckvn_ref[b, i] = pckvn_ref[b, i]
            kr_ref[b, i] = pkr_ref[b, i]
        small = mla_ref[rows, MLA_SMALL]
        kr = jnp.where(_rope_lane_mask(small.shape), small, 0.0)
        kr_ref[b, layer] = small[:, ROPE_LANE0:ROPE_LANE0 + MLA_ROPE]
        ckvn = _rms_rows(mla_ref[rows, MLA_CKV], _pk(pk_ref, "kv_a_norm"))
        ckvn_ref[b, layer] = ckvn
        _build_kv(ckvn.astype(bf16), kr, wk_ref, wv_ref, _pk(pk_ref, "k_norm"), None, k_s, v_s, b * seq)
        qa = _rms_rows(mla_ref[rows, MLA_CQ], _pk(pk_ref, "q_a_norm")).astype(bf16)
        _attend_block(qa, wuq_ref, _pk(pk_ref, "q_norm"), None, k_s, v_s, o_ref, rows)


def _attn_lat_kernel(mla_ref, cckv_ref, ckr_ref, cos_ref, slo_ref, shi_ref,
                     pk_ref, wuq_ref, wk_ref, wv_ref,
                     o_ref, k_s, v_s, *, seq, past, tq):
    qi = pl.program_id(1)
    rb = 256

    @pl.when(qi == 0)
    def _():
        _build_kv(cckv_ref[...].astype(bf16), ckr_ref[...], wk_ref, wv_ref, _pk(pk_ref, "k_norm"), None,
                  k_s, v_s, 0)
        for r in range(seq // rb):
            rs = slice(r * rb, (r + 1) * rb)
            small = mla_ref[rs, MLA_SMALL]
            kr = jnp.where(_rope_lane_mask(small.shape), small, 0.0)
            ckvn = _rms_rows(mla_ref[rs, MLA_CKV], _pk(pk_ref, "kv_a_norm"))
            rope = (cos_ref[rs, :], slo_ref[rs, :], shi_ref[rs, :])
            _build_kv(ckvn.astype(bf16), kr, wk_ref, wv_ref, _pk(pk_ref, "k_norm"), rope, k_s, v_s, past + r * rb)

    rows = pl.ds(pl.multiple_of(qi * tq, tq), tq)
    qa = _rms_rows(mla_ref[rows, MLA_CQ], _pk(pk_ref, "q_a_norm")).astype(bf16)
    rope = (cos_ref[rows, :], slo_ref[rows, :], shi_ref[rows, :])
    _attend_block(qa, wuq_ref, _pk(pk_ref, "q_norm"), rope, k_s, v_s, o_ref)


def _attn_weight_specs(l, nidx):
    z = (0,) * (nidx - 1)

    def const(*idx):
        return lambda *g: idx

    return [_pk_spec(l),
            pl.BlockSpec((None, MLA_Q_RANK, MLA_HEADS * HEAD_SLOT), const(l, 0, 0)),
            pl.BlockSpec((None, MLA_KV_RANK, MLA_HEADS * HEAD_SLOT), const(l, 0, 0)),
            pl.BlockSpec((None, MLA_KV_RANK, MLA_WIDTH), const(l, 0, 0))]


def _attn_ctx(proj, prev, wts, l, batch, seq, nseq=1):
    ntok = batch * seq
    rows = nseq * seq

    def stacked(n, width):
        return pl.BlockSpec((nseq, n, seq, width), lambda b: (b, 0, 0, 0))

    prev_specs = [stacked(l, MLA_KV_RANK), stacked(l, MLA_ROPE)] if l else []
    return _Part(
        kernel=functools.partial(_attn_ctx_kernel, layer=l),
        in_specs=[pl.BlockSpec((rows, MLA_IN_W), lambda b: (b, OFF_SMALL // MLA_IN_W))]
        + prev_specs + _attn_weight_specs(l, 1),
        args=[proj, *prev, *wts],
        out_specs=[pl.BlockSpec((rows, MLA_WIDTH), lambda b: (b, 0)),
                   stacked(l + 1, MLA_KV_RANK), stacked(l + 1, MLA_ROPE)],
        out_shape=[jax.ShapeDtypeStruct((ntok, MLA_WIDTH), f32),
                   jax.ShapeDtypeStruct((batch, l + 1, seq, MLA_KV_RANK), f32),
                   jax.ShapeDtypeStruct((batch, l + 1, seq, MLA_ROPE), f32)],
        scratch_shapes=[pltpu.VMEM((rows, MLA_HEADS * HEAD_SLOT), bf16),
                        pltpu.VMEM((rows, MLA_WIDTH), bf16)])


def _attn_lat(proj, row0, cache_ckv, cache_kr_p, rope_tabs, wts, l, batch, seq, past):
    ntok = batch * seq
    tq = 256
    nq = seq // tq
    kern = functools.partial(_attn_lat_kernel, seq=seq, past=past, tq=tq)
    tab = pl.BlockSpec((seq, LANES), lambda b, q: (0, 0))
    return pl.pallas_call(
        kern,
        grid=(batch, nq),
        in_specs=[pl.BlockSpec((seq, MLA_IN_W), lambda b, q: (row0 // seq + b, OFF_SMALL // MLA_IN_W)),
                  pl.BlockSpec((None, None, past, MLA_KV_RANK), lambda b, q: (b, l, 0, 0)),
                  pl.BlockSpec((None, None, past, LANES), lambda b, q: (b, l, 0, 0)),
                  tab, tab, tab]
        + _attn_weight_specs(l, 2),
        out_specs=pl.BlockSpec((tq, MLA_WIDTH), lambda b, q: (b * nq + q, 0)),
        out_shape=jax.ShapeDtypeStruct((ntok, MLA_WIDTH), f32),
        scratch_shapes=[pltpu.VMEM((past + seq, MLA_HEADS * HEAD_SLOT), bf16),
                        pltpu.VMEM((past + seq, MLA_WIDTH), bf16)],
        compiler_params=_cparams(2),
        name="attn_lat",
    )(proj, cache_ckv, cache_kr_p, *rope_tabs, *wts)


def _split3(x):
    hi = x.astype(bf16)
    r1 = x - hi.astype(f32)
    mid = r1.astype(bf16)
    lo = (r1 - mid.astype(f32)).astype(bf16)
    return hi, mid, lo


def _tri_cumsum(tri_b, x):
    hi, mid, lo = _split3(x)
    return _dot(tri_b, hi) + _dot(tri_b, mid) + _dot(tri_b, lo)


def _lane_bcast(x, c):
    return jnp.broadcast_to(x[:, c:c + 1], (x.shape[0], LANES))


def _gdn_kernel(*refs, seq, nseq, has_state, nprev):
    gin_ref, cw_ref, pk_ref = refs[:3]
    refs = refs[3:]
    s0_ref = sprev_ref = sout_ref = None
    if has_state:
        s0_ref, o_ref = refs[:2]
        refs = refs[2:]
    else:
        if nprev:
            sprev_ref = refs[0]
            refs = refs[1:]
        o_ref, sout_ref = refs[:2]
        refs = refs[2:]
    xpad, qkv_s, g_s, b_s, st_s, wq_s, ak_s, u_s, el_s, rhs_s = refs
    C = GDN_CHUNK
    ncs = seq // C
    nchunk = nseq * ncs
    H = GDN_HEADS
    width = 2 * GDN_KW + GDN_VW
    halo = 8
    pseq = seq + 2 * halo

    for j in range(width // LANES):
        for sb in range(nseq):
            xpad[j, sb * pseq:sb * pseq + halo, :] = jnp.zeros((halo, LANES), f32)
            xpad[j, sb * pseq + halo + seq:(sb + 1) * pseq, :] = jnp.zeros((halo, LANES), f32)
            xpad[j, sb * pseq + halo:sb * pseq + halo + seq, :] = gin_ref[sb * seq:(sb + 1) * seq,
                                                                          j * LANES:(j + 1) * LANES]
    if has_state:
        for d in range(2):
            for h in range(H):
                st_s[d * H + h] = s0_ref[d, h]
    else:
        st_s[...] = jnp.zeros((nseq * 2 * H, GDN_DK, GDN_DV), f32)

    neg_a = -jnp.exp(_pk(pk_ref, "a_log"))
    dtb = _pk(pk_ref, "dt_bias")

    def conv_tile(j, l2norm):
        w = cw_ref[j]
        post = jnp.where(j < H, GDN_DK ** -0.5, 1.0)
        for c in range(nchunk):
            base = (c // ncs) * pseq + halo - CONV_W // 2 + (c % ncs) * C
            y = xpad[j, base:base + C, :] * w[0:1]
            for tap in range(1, CONV_W):
                y = y + xpad[j, base + tap:base + tap + C, :] * w[tap:tap + 1]
            y = y * jax.nn.sigmoid(y)
            if l2norm:
                y = y * (lax.rsqrt(jnp.sum(y * y, axis=-1, keepdims=True) + EPS) * post)
            qkv_s[j, c * C:(c + 1) * C, :] = y

    def conv_qk(j, carry):
        conv_tile(j, True)
        return carry

    def conv_v(j, carry):
        conv_tile(j, False)
        return carry

    tile_unroll = 4 if nchunk <= 4 else (2 if nchunk <= 8 else 1)
    lax.fori_loop(0, 2 * H, conv_qk, 0, unroll=tile_unroll)
    lax.fori_loop(2 * H, 3 * H, conv_v, 0, unroll=tile_unroll)

    def prep(c, carry):
        r0 = pl.multiple_of(c * C, C)
        sm = gin_ref[pl.ds(r0, C), OFF_SMALL:OFF_SMALL + LANES]
        z = sm + dtb
        g_s[pl.ds(r0, C), :] = neg_a * (jnp.maximum(z, 0.0) + jnp.log1p(jnp.exp(-jnp.abs(z))))
        b_s[pl.ds(r0, C), :] = pltpu.roll(jax.nn.sigmoid(sm), LANES - 2 * H, 1)
        return carry

    lax.fori_loop(0, nchunk, prep, 0, unroll=4)

    ri = lax.broadcasted_iota(jnp.int32, (C, LANES), 0)
    cl = lax.broadcasted_iota(jnp.int32, (C, LANES), 1)
    fwd = cl < C
    cj = cl & (C - 1)
    eye2 = (ri == cj).astype(f32)
    incl2 = (fwd & (ri >= cj)) | (~fwd & (ri <= cj))
    strict2 = (fwd & (ri > cj)) | (~fwd & (ri < cj))
    xor = ri ^ cj
    level2 = sum((xor >= (1 << b)).astype(jnp.int32) for b in range(C.bit_length() - 1))
    lvl_top = jnp.where(fwd, level2, 0)
    lvl_bot = jnp.where(fwd, 0, level2)
    r2 = lax.broadcasted_iota(jnp.int32, (2 * C, C), 0)
    c2 = lax.broadcasted_iota(jnp.int32, (2 * C, C), 1)
    tri2 = (((r2 < C) & (r2 >= c2)) | ((r2 >= C) & (r2 - C <= c2))).astype(bf16)
    zrhs = jnp.zeros((C, 2 * LANES), bf16)
    zvn = jnp.zeros((C, LANES), bf16)
    fwd_row = fwd[0:1, :]
    cpi = min(GDN_SOLVE_CHUNKS, nchunk)

    def block_diag(x):
        return jnp.concatenate([jnp.where(fwd, x, 0.0), jnp.where(fwd, 0.0, x)], axis=0).astype(bf16)

    def solve_phase(i, carry):
        chains = []
        for cc in range(cpi):
            c = i * cpi + cc
            rows = pl.ds(pl.multiple_of(c * C, C), C)
            g2 = _tri_cumsum(tri2, g_s[rows, :])
            g2t = g2.T
            bt = b_s[rows, :]
            for h in range(H):
                chains.append((cc, c, h, rows, g2, g2t, bt))

        a2s, t2s = [], []
        for (cc, c, h, rows, g2, g2t, bt) in chains:
            q = qkv_s[h, rows, :]
            k = qkv_s[H + h, rows, :]
            v = qkv_s[2 * H + h, rows, :]
            kq = _dot_nt(jnp.concatenate([k, q], axis=0).astype(bf16),
                         jnp.concatenate([k, k], axis=0).astype(bf16))
            gcc_f = _lane_bcast(g2[:C], h)
            gcc_b = _lane_bcast(g2[C:], H + h)
            btc_f = _lane_bcast(bt, h)
            btc_b = _lane_bcast(bt, H + h)
            grow = jnp.where(fwd_row, g2t[h:h + 1, :], g2t[H + h:H + h + 1, :])
            diff = jnp.where(fwd, gcc_f, gcc_b) - grow
            dec = jnp.where(incl2, jnp.exp(jnp.where(incl2, diff, 0.0)), 0.0)
            a2 = jnp.where(strict2, jnp.where(fwd, btc_f, btc_b) * kq[:C] * dec, 0.0)
            a2s.append(a2)
            t2s.append(eye2 - jnp.where(level2 == 1, a2, 0.0))
            glast_f = gcc_f[C - 1:C, :]
            glast_b = gcc_b[0:1, :]
            e1_f = jnp.exp(gcc_f)
            e1_b = jnp.exp(gcc_b)
            ket = jnp.concatenate([k * jnp.exp(glast_f - gcc_f), k * jnp.exp(glast_b - gcc_b)], axis=0).T
            ak_s[pl.ds(pl.multiple_of((c * H + h) * 3 * C, 3 * C), 3 * C), :] = jnp.concatenate(
                [kq[C:] * dec, ket], axis=0).astype(bf16)
            for d, (btc, e1, glast) in enumerate(((btc_f, e1_f, glast_f), (btc_b, e1_b, glast_b))):
                ch = d * H + h
                j = (cc * H + h) * 2 + d
                rhs_s[j * C:(j + 1) * C, :] = jnp.concatenate([v * btc, k * (btc * e1)], axis=1).astype(bf16)
                wq_s[pl.ds(pl.multiple_of((c * 2 * H + ch) * 2 * C + C, C), C), :] = (q * e1).astype(bf16)
                el_s[pl.ds(pl.multiple_of((c * 2 * H + ch) * 8, 8), 8), :] = jnp.broadcast_to(
                    jnp.exp(glast), (8, LANES))

        for g0 in range(0, len(chains), GDN_LEVEL_CHAINS):
            grp = slice(g0, g0 + GDN_LEVEL_CHAINS)
            ga2, gt2 = a2s[grp], t2s[grp]
            for lv in range(2, C.bit_length()):
                rs = []
                for a2, t2 in zip(ga2, gt2):
                    abd = jnp.concatenate([jnp.where(lvl_top == lv, a2, 0.0),
                                           jnp.where(lvl_bot == lv, a2, 0.0)], axis=0).astype(bf16)
                    rs.append(_dot(t2.astype(bf16), abd))
                gt2 = [t2 - _dot(r.astype(bf16), block_diag(t2)) for r, t2 in zip(rs, gt2)]

            for (cc, c, h, rows, g2, g2t, bt), t2 in zip(chains[grp], gt2):
                t2b = t2.astype(bf16)
                for d in range(2):
                    ch = d * H + h
                    j = (cc * H + h) * 2 + d
                    rhs = rhs_s[j * C:(j + 1) * C, :]
                    rhs = jnp.concatenate([rhs, zrhs] if d == 0 else [zrhs, rhs], axis=0)
                    uw = _dot(t2b, rhs)
                    u_s[pl.ds(pl.multiple_of((c * 2 * H + ch) * C, C), C), :] = uw[:, :LANES]
                    wq_s[pl.ds(pl.multiple_of((c * 2 * H + ch) * 2 * C, C), C), :] = uw[:, LANES:].astype(bf16)
        return carry

    lax.fori_loop(0, nchunk // cpi, solve_phase, 0)

    def scan_phase(i, carry):
        chs = [ch for _ in range(nseq) for ch in range(2 * H)]
        cs = [sb * ncs + (i if ch < H else ncs - 1 - i) for sb in range(nseq) for ch in range(2 * H)]
        nk = len(chs)
        s_old = [st_s[k] for k in range(nk)]
        r1 = [_dot(wq_s[pl.ds(pl.multiple_of((cs[k] * 2 * H + chs[k]) * 2 * C, 2 * C), 2 * C), :],
                   s_old[k].astype(bf16)) for k in range(nk)]
        r2s = []
        for k in range(nk):
            u = u_s[pl.ds(pl.multiple_of((cs[k] * 2 * H + chs[k]) * C, C), C), :]
            vnb = (u - r1[k][:C]).astype(bf16)
            rhs = jnp.concatenate([vnb, zvn] if chs[k] < H else [zvn, vnb], axis=0)
            ak = ak_s[pl.ds(pl.multiple_of((cs[k] * H + chs[k] % H) * 3 * C, 3 * C), 3 * C), :]
            r2s.append(_dot(ak, rhs))
        for k in range(nk):
            el = el_s[pl.ds(pl.multiple_of((cs[k] * 2 * H + chs[k]) * 8, 8), 8), :][0:1, :]
            st_s[k] = s_old[k] * el + r2s[k][C:]
            xpad[chs[k], pl.ds(pl.multiple_of(cs[k] * C, C), C), :] = r1[k][C:] + r2s[k][:C]
        return carry

    lax.fori_loop(0, ncs, scan_phase, 0, unroll=4)

    onorm = _pk(pk_ref, "gdn_onorm")

    def fin(c, carry):
        rows = pl.ds(pl.multiple_of(c * C, C), C)
        for h in range(H):
            ls = slice(h * LANES, (h + 1) * LANES)
            o_ref[rows, ls] = _rms_rows(xpad[h, rows, :] + xpad[H + h, rows, :], onorm)
        return carry

    lax.fori_loop(0, nchunk, fin, 0, unroll=4)
    if sout_ref is not None:
        for sb in range(nseq):
            for i in range(nprev):
                sout_ref[sb, i] = sprev_ref[sb, i]
            for d in range(2):
                for h in range(H):
                    sout_ref[sb, nprev, d, h] = st_s[sb * 2 * H + d * H + h]


def _gdn(proj, row0, conv_w, pack, state, prev_states, l, batch, seq, nseq=1):
    ntok = batch * seq
    width = 2 * GDN_KW + GDN_VW
    has_state = state is not None
    nprev = 0 if has_state else l
    nchunk = nseq * (seq // GDN_CHUNK)
    rows = nseq * seq
    kern = functools.partial(_gdn_kernel, seq=seq, nseq=nseq, has_state=has_state, nprev=nprev)
    in_specs = [pl.BlockSpec((rows, GDN_IN_W), lambda b: (row0 // rows + b, OFF_QKV // GDN_IN_W)),
                pl.BlockSpec((None, width // LANES, CONV_W, LANES), lambda b: (l, 0, 0, 0)),
                _pk_spec(l)]
    args = [proj, conv_w, pack]
    o_spec = pl.BlockSpec((rows, GDN_VW), lambda b: (b, 0))
    o_shape = jax.ShapeDtypeStruct((ntok, GDN_VW), f32)
    st_block = (None, None, 2, GDN_HEADS, GDN_DK, GDN_DV)
    if has_state:
        assert nseq == 1
        in_specs.append(pl.BlockSpec(st_block, lambda b: (b, l, 0, 0, 0, 0)))
        args.append(state)
        out_specs, out_shape = [o_spec], [o_shape]
    else:
        def stacked(n):
            return pl.BlockSpec((nseq, n) + st_block[2:], lambda b: (b, 0, 0, 0, 0, 0))

        if nprev:
            in_specs.append(stacked(nprev))
            args.append(prev_states)
        out_specs = [o_spec, stacked(nprev + 1)]
        out_shape = [o_shape, jax.ShapeDtypeStruct((batch, nprev + 1, 2, GDN_HEADS, GDN_DK, GDN_DV), f32)]
    return _Part(
        kernel=kern,
        in_specs=in_specs,
        args=args,
        out_specs=out_specs,
        out_shape=out_shape,
        scratch_shapes=[pltpu.VMEM((width // LANES, nseq * (seq + 16), LANES), f32),
                        pltpu.VMEM((width // LANES, rows, LANES), f32),
                        pltpu.VMEM((rows, LANES), f32),
                        pltpu.VMEM((rows, LANES), f32),
                        pltpu.VMEM((nseq * 2 * GDN_HEADS, GDN_DK, GDN_DV), f32),
                        pltpu.VMEM((nchunk * 2 * GDN_HEADS * 2 * GDN_CHUNK, LANES), bf16),
                        pltpu.VMEM((nchunk * GDN_HEADS * 3 * GDN_CHUNK, LANES), bf16),
                        pltpu.VMEM((nchunk * 2 * GDN_HEADS * GDN_CHUNK, LANES), f32),
                        pltpu.VMEM((nchunk * 2 * GDN_HEADS * 8, LANES), f32),
                        pltpu.VMEM((min(GDN_SOLVE_CHUNKS, nchunk) * 2 * GDN_HEADS * GDN_CHUNK, 2 * LANES), bf16)])


def _merge_kernel(x_ref, mod_ref, oa_ref, ob_ref, p_ref, pk_ref, ws_ref, bs_ref, wbr_ref, wo_ref,
                  out_ref, sv_s):
    tm = x_ref.shape[0]

    def group(off, width=512):
        return p_ref[:, off:off + width].astype(f32)

    def gelu(x):
        c1 = math.sqrt(2.0 / math.pi)
        half = 0.5 * x
        return half + half * jnp.tanh(x * (c1 + (c1 * 0.044715) * (x * x)))

    u = gelu(group(OFF_CU))
    vf = gelu(group(OFF_CV))
    mu = jnp.mean(vf, axis=-1, keepdims=True)
    vc = vf - mu
    var = jnp.mean(vc * vc, axis=-1, keepdims=True)
    vn = (vc * lax.rsqrt(var + EPS) * _pk(pk_ref, "cm_ln_g") + _pk(pk_ref, "cm_ln_b")).astype(bf16)
    for ck in range(tm // CM_CHUNK):
        rs = slice(ck * CM_CHUNK, (ck + 1) * CM_CHUNK)
        for g in range(CM_GROUPS):
            ls = slice(g * LANES, (g + 1) * LANES)
            sv_s[rs, ls] = _dot(ws_ref[g], vn[rs, ls]) + bs_ref[:, ls]
    o_c = u * sv_s[...]

    def silu(z):
        return z * jax.nn.sigmoid(z)

    brs = (oa_ref[...] * silu(group(OFF_ZA)), ob_ref[...] * silu(group(OFF_ZB)), o_c * silu(group(OFF_ZC)))
    ysum = None
    for n in range(N_BRANCH):
        yb = _dot(brs[n].astype(bf16), wbr_ref[n])
        t = jax.nn.sigmoid(group(OFF_GL + n * D_MODEL, D_MODEL)) * yb
        ysum = t if ysum is None else ysum + t
    y = _dot(ysum.astype(bf16), wo_ref[...])
    gate = mod_ref[0][:, 2 * D_MODEL:]
    out_ref[...] = x_ref[...] + gate * y


def _merge(x, mod_l, proj, row0, o_a, o_b, pack, ws_b, bs_full, wbr_b, wo_b, l, latent, seq):
    ntok = x.shape[0]
    tm = 512

    def const(*idx):
        return lambda t: idx

    return pl.pallas_call(
        _merge_kernel,
        grid=(ntok // tm,),
        in_specs=[pl.BlockSpec((tm, D_MODEL), lambda t: (t, 0)),
                  pl.BlockSpec((None, 1, 1, 3 * D_MODEL), _mod_row(l, latent, tm, seq)),
                  pl.BlockSpec((tm, 512), lambda t: (t, 0)),
                  pl.BlockSpec((tm, 512), lambda t: (t, 0)),
                  pl.BlockSpec((tm, P16_W), lambda t: (row0 // tm + t, 0)),
                  _pk_spec(l),
                  pl.BlockSpec((None, CM_GROUPS, CM_CHUNK, CM_CHUNK), const(l, 0, 0, 0)),
                  pl.BlockSpec((None, CM_CHUNK, CM_WIDTH), const(l, 0, 0)),
                  pl.BlockSpec((None, N_BRANCH, BRANCH_W, D_MODEL), const(l, 0, 0, 0)),
                  pl.BlockSpec((None, D_MODEL, D_MODEL), const(l, 0, 0))],
        out_specs=pl.BlockSpec((tm, D_MODEL), lambda t: (t, 0)),
        out_shape=jax.ShapeDtypeStruct((ntok, D_MODEL), f32),
        scratch_shapes=[pltpu.VMEM((tm, CM_WIDTH), f32)],
        compiler_params=_cparams(1),
        name="merge",
    )(x, mod_l, o_a, o_b, proj, pack, ws_b, bs_full, wbr_b, wo_b)


def _pad_last(x, n):
    return jnp.pad(x, [(0, 0)] * (x.ndim - 1) + [(0, n - x.shape[-1])])


def _rope_tables(seq):
    t = np.arange(seq)
    row = (t // GRID_W).astype(np.float32)
    colp = (t % GRID_W).astype(np.float32)
    nf = MLA_ROPE // 4
    inv = (ROPE_THETA ** (-np.arange(nf, dtype=np.float32) / nf)).astype(np.float32)
    cos_t = np.ones((seq, LANES), np.float32)
    s_lo = np.zeros((seq, LANES), np.float32)
    s_hi = np.zeros((seq, LANES), np.float32)
    for i, pos in enumerate((row, colp)):
        ang = (pos[:, None] * inv[None, :]).astype(np.float32)
        cs, sn = np.cos(ang), np.sin(ang)
        lo = ROPE_LANE0 + 2 * nf * i
        cos_t[:, lo:lo + nf] = cs
        cos_t[:, lo + nf:lo + 2 * nf] = cs
        s_lo[:, lo:lo + nf] = -sn
        s_hi[:, lo + nf:lo + 2 * nf] = sn
    return tuple(jnp.asarray(p) for p in (cos_t, s_lo, s_hi))


def kernel(x_prompt, x_sample, cache_ckv, cache_krope, state_gdn, c, c_ctx, norm_g, w_mod, b_mod, w_in, q_a_norm, w_uq, kv_a_norm, w_ukv, q_norm, k_norm, conv_w, a_log, dt_bias, gdn_onorm, cm_ln_g, cm_ln_b, w_s, b_s, w_branch, w_o):
    L = DEPTH
    batch, seq, _ = x_prompt.shape
    dbatch, dseq, _ = x_sample.shape
    past = cache_ckv.shape[2]

    w_t = jnp.swapaxes(w_in, 1, 2)
    wuq_p = _pad_last(w_uq.reshape(L, MLA_Q_RANK, MLA_HEADS, MLA_QK), HEAD_SLOT)
    wuq_p = wuq_p.reshape(L, MLA_Q_RANK, MLA_HEADS * HEAD_SLOT).astype(bf16)
    wukv = w_ukv.reshape(L, MLA_KV_RANK, MLA_HEADS, MLA_NOPE + MLA_V)
    wk_p = _pad_last(wukv[..., :MLA_NOPE], HEAD_SLOT).reshape(L, MLA_KV_RANK, MLA_HEADS * HEAD_SLOT).astype(bf16)
    wv_p = wukv[..., MLA_NOPE:].reshape(L, MLA_KV_RANK, MLA_WIDTH).astype(bf16)
    cache_kr_p = jnp.pad(cache_krope, [(0, 0)] * 3 + [(ROPE_LANE0, LANES - ROPE_LANE0 - MLA_ROPE)])
    rope_tabs = _rope_tables(dseq)
    conv_w = jnp.swapaxes(conv_w.reshape(L, CONV_W, -1, LANES), 1, 2)
    ws_b = w_s.astype(bf16)
    bs_full = jnp.repeat(jnp.swapaxes(b_s, 1, 2), CM_WIDTH // CM_GROUPS, axis=2)
    gates8 = [_pad_last(v.reshape(L, 2 * GDN_HEADS), LANES) for v in (a_log, dt_bias)]
    pack = jnp.stack([norm_g,
                      jnp.concatenate([q_a_norm, kv_a_norm, _pad_last(q_norm, HEAD_SLOT),
                                       _pad_last(k_norm, HEAD_SLOT), gdn_onorm], axis=-1),
                      jnp.concatenate([cm_ln_g, cm_ln_b], axis=-1),
                      _pad_last(jnp.concatenate(gates8, axis=-1), D_MODEL)], axis=1)
    attn_w = (pack, wuq_p, wk_p, wv_p)

    c8 = jnp.concatenate([c, c_ctx[None, :], jnp.zeros((8 - dbatch - 1, D_MODEL), f32)], axis=0)
    wbr_b = w_branch.astype(bf16)
    wo_b = w_o.astype(bf16)
    mod = _modulation(c8, w_mod, b_mod)

    yp = x_prompt.reshape(batch * seq, D_MODEL)
    ys = x_sample.reshape(dbatch * dseq, D_MODEL)
    caches, states = (), None
    for l in range(L):
        mod_l = mod
        p16, p32 = _inproj(yp, ys, mod_l, pack, w_t, l, dseq)
        lat0 = batch * seq
        (o_a, new_ckv, new_kr), (o_b, states) = _run_parts(
            [_attn_ctx(p32, caches, attn_w, l, batch, seq, CTX_SEQS_PER_STEP),
             _gdn(p32, 0, conv_w, pack, None, states, l, batch, seq, CTX_SEQS_PER_STEP)],
            (batch // CTX_SEQS_PER_STEP,), "mix_ctx")
        caches = (new_ckv, new_kr)
        yp = _merge(yp, mod_l, p16, 0, o_a, o_b, pack, ws_b, bs_full, wbr_b, wo_b, l, False, seq)
        o_a = _attn_lat(p32, lat0, cache_ckv, cache_kr_p, rope_tabs, attn_w, l, dbatch, dseq, past)
        ((o_b,),) = _run_parts([_gdn(p32, lat0, conv_w, pack, state_gdn, None, l, dbatch, dseq)], (dbatch,), "gdn_lat")
        ys = _merge(ys, mod_l, p16, lat0, o_a, o_b, pack, ws_b, bs_full, wbr_b, wo_b, l, True, dseq)
    return (yp.reshape(batch, seq, D_MODEL), ys.reshape(dbatch, dseq, D_MODEL), caches[0], caches[1], states)
```
